```python
import math
import jax, jax.numpy as jnp
from jax import lax
import numpy as np

D_MODEL = 2048
BATCH = 8
SEQ = 2048
DEPTH = 2

CHUNK = 64
Q_BLOCK = 128
N_MIXERS = 2
N_RWKV_LAYERS = (DEPTH + N_MIXERS - 1) // N_MIXERS
N_MLA_LAYERS = DEPTH // N_MIXERS
D_FF = 5504
RMS_EPS = 1e-6
RWKV_HEAD = 64
RWKV_HEADS = D_MODEL // RWKV_HEAD
DECAY_LORA = 96
AAA_LORA = 96
GATE_LORA = 256
GN_EPS = 64e-5
MLA_HEADS = 16
Q_LORA = 512
KV_LORA = 512
NOPE_DIM = 128
ROPE_DIM = 64
QK_DIM = NOPE_DIM + ROPE_DIM
V_DIM = 128
ROPE_THETA = 10000.0

kernel_name = 'hybrid_rwkv7_mla_macaron_trunk'


def rmsnorm(x, g, eps=RMS_EPS):
    xf = x.astype(jnp.float32)
    y = xf * lax.rsqrt(jnp.mean(xf * xf, axis=-1, keepdims=True) + eps)
    return (y * g.astype(jnp.float32)).astype(x.dtype)


def swiglu(h, w13, w2):
    gate, up = jnp.split(h @ w13, 2, axis=-1)
    return (jax.nn.silu(gate) * up) @ w2


def rwkv7_scan(r, w, k, v, a, b):
    B, T, H, N = r.shape

    def step(S, inp):
        r_t, w_t, k_t, v_t, a_t, b_t = inp
        sa = jnp.einsum('bhvk,bhk->bhv', S, a_t)
        S = (S * w_t[:, :, None, :]
             + sa[..., None] * b_t[:, :, None, :]
             + v_t[..., None] * k_t[:, :, None, :])
        y_t = jnp.einsum('bhvk,bhk->bhv', S, r_t)
        return S, y_t

    xs = tuple(jnp.moveaxis(t, 1, 0) for t in (r, w, k, v, a, b))
    S0 = jnp.zeros((B, H, N, N), jnp.float32)
    _, y = lax.scan(step, S0, xs)
    return jnp.moveaxis(y, 0, 1)


def rwkv7_time_mix(h, mu, w_rkv, w0, w1, w2, a0, a1, a2, g1, g2,
                   k_k, k_a, r_k, ln_w, ln_b, w_o):
    B, T, C = h.shape
    H, N = RWKV_HEADS, RWKV_HEAD
    f32 = jnp.float32
    x_prev = jnp.pad(h, ((0, 0), (1, 0), (0, 0)))[:, :T]
    xx = x_prev - h
    xr, xw, xk, xv, xa, xg = (h + xx * mu[j] for j in range(6))
    r, k, v = jnp.einsum('cbtd,cde->cbte', jnp.stack([xr, xk, xv]), w_rkv)
    w_log = -jax.nn.softplus(-(w0 + jnp.tanh(xw @ w1) @ w2)) - 0.5
    decay = jnp.exp(-jnp.exp(w_log.astype(f32)))
    a = jax.nn.sigmoid(a0 + (xa @ a1) @ a2)
    g = jax.nn.sigmoid(xg @ g1) @ g2
    heads = lambda t: t.reshape(B, T, H, N).astype(f32)
    kk = heads(k * k_k)
    kk = kk / jnp.maximum(jnp.linalg.norm(kk, axis=-1, keepdims=True), 1e-12)
    k = k * (1.0 + (a - 1.0) * k_a)
    r_h, k_h, v_h, a_h = heads(r), heads(k), heads(v), heads(a)
    y = rwkv7_scan(r_h, heads(decay), k_h, v_h, -kk, kk * a_h)
    mean = jnp.mean(y, axis=-1, keepdims=True)
    var = jnp.mean(jnp.square(y - mean), axis=-1, keepdims=True)
    y = ((y - mean) * lax.rsqrt(var + GN_EPS)).reshape(B, T, C)
    y = y * ln_w.astype(f32) + ln_b.astype(f32)
    bonus = jnp.sum(r_h * k_h * r_k.astype(f32), axis=-1, keepdims=True) * v_h
    y = y + bonus.reshape(B, T, C)
    return (y.astype(h.dtype) * g) @ w_o


def rope(x, cos, sin):
    x1, x2 = jnp.split(x, 2, axis=-1)
    return jnp.concatenate([x1 * cos - x2 * sin, x2 * cos + x1 * sin], axis=-1)


def chunk_causal_attention(q, k, v):
    B, T, H, Dqk = q.shape
    Dv = v.shape[-1]
    nb = T // Q_BLOCK
    scale = 1.0 / math.sqrt(Dqk)
    qb = jnp.moveaxis(q.reshape(B, nb, Q_BLOCK, H, Dqk), 1, 0)
    key_chunk = jnp.arange(T) // CHUNK

    def one_block(args):
        q_i, blk = args
        q_chunk = (blk * Q_BLOCK + jnp.arange(Q_BLOCK)) // CHUNK
        s = jnp.einsum('bqhd,bkhd->bhqk', q_i, k,
                       preferred_element_type=jnp.float32) * scale
        mask = key_chunk[None, :] <= q_chunk[:, None]
        s = jnp.where(mask[None, None], s, jnp.finfo(jnp.float32).min)
        p = jax.nn.softmax(s, axis=-1)
        return jnp.einsum('bhqk,bkhd->bqhd', p.astype(v.dtype), v)

    o = lax.map(one_block, (qb, jnp.arange(nb)))
    return jnp.moveaxis(o, 0, 1).reshape(B, T, H, Dv)


def mla_mix(h, cos, sin, w_down, q_a_norm, kv_a_norm, w_uq, w_ukv,
            q_norm, k_norm, w_o):
    B, T, _ = h.shape
    c = h @ w_down
    c_q, c_kv, k_pe = jnp.split(c, [Q_LORA, Q_LORA + KV_LORA], axis=-1)
    c_q = rmsnorm(c_q, q_a_norm)
    c_kv = rmsnorm(c_kv, kv_a_norm)
    q = jnp.einsum('btl,lhd->bthd', c_q, w_uq)
    kv = jnp.einsum('btl,lhd->bthd', c_kv, w_ukv)
    k_nope, v = jnp.split(kv, [NOPE_DIM], axis=-1)
    k_pe = jnp.broadcast_to(k_pe[:, :, None, :], (B, T, MLA_HEADS, ROPE_DIM))
    k = jnp.concatenate([k_nope, k_pe], axis=-1)
    q = rmsnorm(q, q_norm)
    k = rmsnorm(k, k_norm)
    q = jnp.concatenate([q[..., :NOPE_DIM], rope(q[..., NOPE_DIM:], cos, sin)], axis=-1)
    k = jnp.concatenate([k[..., :NOPE_DIM], rope(k[..., NOPE_DIM:], cos, sin)], axis=-1)
    o = chunk_causal_attention(q, k, v)
    return jnp.einsum('bthd,hde->bte', o, w_o)


def setup_inputs(seed: int = 0) -> dict:
    key = jax.random.key(seed)
    ks = jax.random.split(key, 40)
    f32 = jnp.float32
    D, F, NR, NM = D_MODEL, D_FF, N_RWKV_LAYERS, N_MLA_LAYERS
    H, N = RWKV_HEADS, RWKV_HEAD

    def nrm(k, shape, scale):
        return jax.random.normal(k, shape, f32) * scale

    def gain(k, shape):
        return 1.0 + 0.02 * jax.random.normal(k, shape, f32)

    x = nrm(ks[0], (BATCH, SEQ, D), 1.0)
    start = jax.random.randint(ks[1], (BATCH, 1), 0, 4096, dtype=jnp.int32)
    positions = start + jnp.arange(SEQ, dtype=jnp.int32)[None, :]
    return {
        'x': x,
        'positions': positions,
        'ffn_norm': gain(ks[2], (DEPTH, 2, D)),
        'ffn_w13': nrm(ks[3], (DEPTH, 2, D, 2 * F), D ** -0.5),
        'ffn_w2': nrm(ks[4], (DEPTH, 2, F, D), F ** -0.5),
        'mix_norm': gain(ks[5], (DEPTH, D)),
        'rwkv_mu': jax.random.uniform(ks[6], (NR, 6, D), f32),
        'rwkv_w_rkv': nrm(ks[7], (NR, 3, D, D), D ** -0.5),
        'rwkv_w0': -6.5 + 5.0 * jax.random.uniform(ks[8], (NR, D), f32),
        'rwkv_w1': nrm(ks[9], (NR, D, DECAY_LORA), D ** -0.5),
        'rwkv_w2': nrm(ks[10], (NR, DECAY_LORA, D), 0.5 * DECAY_LORA ** -0.5),
        'rwkv_a0': nrm(ks[11], (NR, D), 0.1),
        'rwkv_a1': nrm(ks[12], (NR, D, AAA_LORA), D ** -0.5),
        'rwkv_a2': nrm(ks[13], (NR, AAA_LORA, D), 0.5 * AAA_LORA ** -0.5),
        'rwkv_g1': nrm(ks[14], (NR, D, GATE_LORA), D ** -0.5),
        'rwkv_g2': nrm(ks[15], (NR, GATE_LORA, D), GATE_LORA ** -0.5),
        'rwkv_k_k': 0.85 + 0.05 * jax.random.normal(ks[16], (NR, D), f32),
        'rwkv_k_a': gain(ks[17], (NR, D)),
        'rwkv_r_k': nrm(ks[18], (NR, H, N), 0.1),
        'rwkv_ln_w': gain(ks[19], (NR, D)),
        'rwkv_ln_b': nrm(ks[20], (NR, D), 0.01),
        'rwkv_w_o': nrm(ks[21], (NR, D, D), D ** -0.5),
        'mla_w_down': nrm(ks[22], (NM, D, Q_LORA + KV_LORA + ROPE_DIM), D ** -0.5),
        'mla_q_a_norm': gain(ks[23], (NM, Q_LORA)),
        'mla_kv_a_norm': gain(ks[24], (NM, KV_LORA)),
        'mla_w_uq': nrm(ks[25], (NM, Q_LORA, MLA_HEADS, QK_DIM), Q_LORA ** -0.5),
        'mla_w_ukv': nrm(ks[26], (NM, KV_LORA, MLA_HEADS, NOPE_DIM + V_DIM), KV_LORA ** -0.5),
        'mla_q_norm': gain(ks[27], (NM, QK_DIM)),
        'mla_k_norm': gain(ks[28], (NM, QK_DIM)),
        'mla_w_o': nrm(ks[29], (NM, MLA_HEADS, V_DIM, D), (MLA_HEADS * V_DIM) ** -0.5),
    }


def reference(x, positions, ffn_norm, ffn_w13, ffn_w2, mix_norm,
              rwkv_mu, rwkv_w_rkv, rwkv_w0, rwkv_w1, rwkv_w2, rwkv_a0, rwkv_a1, rwkv_a2,
              rwkv_g1, rwkv_g2, rwkv_k_k, rwkv_k_a, rwkv_r_k, rwkv_ln_w, rwkv_ln_b, rwkv_w_o,
              mla_w_down, mla_q_a_norm, mla_kv_a_norm, mla_w_uq, mla_w_ukv,
              mla_q_norm, mla_k_norm, mla_w_o):
    inv_freq = ROPE_THETA ** (-jnp.arange(0, ROPE_DIM, 2, dtype=jnp.float32) / ROPE_DIM)
    ang = positions.astype(jnp.float32)[..., None] * inv_freq
    cos = jnp.cos(ang)[:, :, None, :].astype(x.dtype)
    sin = jnp.sin(ang)[:, :, None, :].astype(x.dtype)

    for i in range(DEPTH):
        j = i // N_MIXERS
        x = x + 0.5 * swiglu(rmsnorm(x, ffn_norm[i, 0]), ffn_w13[i, 0], ffn_w2[i, 0])
        h = rmsnorm(x, mix_norm[i])
        if i % N_MIXERS == 0:
            y = rwkv7_time_mix(h, rwkv_mu[j], rwkv_w_rkv[j], rwkv_w0[j], rwkv_w1[j],
                               rwkv_w2[j], rwkv_a0[j], rwkv_a1[j], rwkv_a2[j],
                               rwkv_g1[j], rwkv_g2[j], rwkv_k_k[j], rwkv_k_a[j],
                               rwkv_r_k[j], rwkv_ln_w[j], rwkv_ln_b[j], rwkv_w_o[j])
        else:
            y = mla_mix(h, cos, sin, mla_w_down[j], mla_q_a_norm[j], mla_kv_a_norm[j],
                        mla_w_uq[j], mla_w_ukv[j], mla_q_norm[j], mla_k_norm[j],
                        mla_w_o[j])
        x = x + y
        x = x + 0.5 * swiglu(rmsnorm(x, ffn_norm[i, 1]), ffn_w13[i, 1], ffn_w2[i, 1])
    return x
```

```python
import functools
import math

import jax
import jax.numpy as jnp
from jax import lax
from jax.experimental import pallas as pl
from jax.experimental.pallas import tpu as pltpu

F32 = jnp.float32
BF16 = jnp.bfloat16

RMS_EPS = 1e-6
GN_EPS = 64e-5
ROPE_THETA = 10000.0
ATTN_CHUNK = 64
RWKV_HEAD = 64
SCAN_CHUNK = 64
HEADS_PER_GROUP = 4
LANE = 128
VMEM_LIMIT_BYTES = 56 * 1024 * 1024


def _params(semantics):
    return pltpu.CompilerParams(dimension_semantics=semantics,
                                vmem_limit_bytes=VMEM_LIMIT_BYTES)


def _pick_tile(n, target, quantum):
    if n <= target:
        return n
    t = (target // quantum) * quantum
    while t > quantum and n % t:
        t -= quantum
    assert n % t == 0, (n, target, quantum)
    return t


def _rms(x, g):
    return x * lax.rsqrt(jnp.mean(x * x, axis=-1, keepdims=True) + RMS_EPS) * g


def _ffn_body(x_ref, g_ref, wg_ref, wu_ref, w2_ref, o_ref, xn_ref, acc_ref):
    j = pl.program_id(1)

    @pl.when(j == 0)
    def _():
        xn_ref[...] = _rms(x_ref[...], g_ref[...]).astype(BF16)
        acc_ref[...] = jnp.zeros_like(acc_ref)

    xn = xn_ref[...]
    gate = jnp.dot(xn, wg_ref[...], preferred_element_type=F32)
    up = jnp.dot(xn, wu_ref[...], preferred_element_type=F32)
    act = (gate * jax.nn.sigmoid(gate) * up).astype(BF16)
    acc_ref[...] += jnp.dot(act, w2_ref[...], preferred_element_type=F32)

    @pl.when(j == pl.num_programs(1) - 1)
    def _():
        o_ref[...] = x_ref[...] + 0.5 * acc_ref[...]


def _ffn(x, g, w13, w2):
    m, d = x.shape
    f = w2.shape[0]
    tf = 512 if f > 512 else LANE
    fp = -(-f // tf) * tf
    wg = jnp.pad(w13[:, :f], ((0, 0), (0, fp - f))).astype(BF16)
    wu = jnp.pad(w13[:, f:], ((0, 0), (0, fp - f))).astype(BF16)
    w2p = jnp.pad(w2, ((0, fp - f), (0, 0))).astype(BF16)
    tm = _pick_tile(m, 512, 8)
    return pl.pallas_call(
        _ffn_body,
        grid=(m // tm, fp // tf),
        in_specs=[
            pl.BlockSpec((tm, d), lambda i, j: (i, 0)),
            pl.BlockSpec((1, d), lambda i, j: (0, 0)),
            pl.BlockSpec((d, tf), lambda i, j: (0, j)),
            pl.BlockSpec((d, tf), lambda i, j: (0, j)),
            pl.BlockSpec((tf, d), lambda i, j: (j, 0)),
        ],
        out_specs=pl.BlockSpec((tm, d), lambda i, j: (i, 0)),
        out_shape=jax.ShapeDtypeStruct((m, d), F32),
        scratch_shapes=[pltpu.VMEM((tm, d), BF16), pltpu.VMEM((tm, d), F32)],
        compiler_params=_params(("parallel", "arbitrary")),
        name="ffn",
    )(x, g.reshape(1, d), wg, wu, w2p)


def _mm_body(n_lhs, n_epi, n_out, lhs_fn, epi_fn, *refs):
    lhs_refs = refs[:n_lhs]
    w_ref = refs[n_lhs]
    epi_refs = refs[n_lhs + 1:n_lhs + 1 + n_epi]
    out_refs = refs[n_lhs + 1 + n_epi:n_lhs + 1 + n_epi + n_out]
    lhs_scratch = refs[-1]

    @pl.when(pl.program_id(1) == 0)
    def _():
        lhs_scratch[...] = lhs_fn(*[r[...] for r in lhs_refs]).astype(BF16)

    acc = jnp.dot(lhs_scratch[...], w_ref[...], preferred_element_type=F32)
    outs = epi_fn(acc, *[r[...] for r in epi_refs])
    for o_ref, o in zip(out_refs, outs):
        o_ref[...] = o.astype(o_ref.dtype)


def _mm(name, m, lhs, lhs_fn, w, epi, epi_fn, outs, *, tm, tn):
    k, n = w.shape
    body = functools.partial(_mm_body, len(lhs), len(epi), len(outs), lhs_fn, epi_fn)
    return pl.pallas_call(
        body,
        grid=(m // tm, n // tn),
        in_specs=[s for _, s in lhs] + [pl.BlockSpec((k, tn), lambda i, j: (0, j))]
        + [s for _, s in epi],
        out_specs=[s for _, s in outs],
        out_shape=[s for s, _ in outs],
        scratch_shapes=[pltpu.VMEM((tm, k), BF16)],
        compiler_params=_params(("parallel", "arbitrary")),
        name=name,
    )(*[a for a, _ in lhs], w, *[a for a, _ in epi])


def _row_spec(tm, width, col_block=0):
    return pl.BlockSpec((tm, width), lambda i, j: (i, col_block))


def _tile_spec(tm, tn):
    return pl.BlockSpec((tm, tn), lambda i, j: (i, j))


def _vec_spec(width):
    return pl.BlockSpec((1, width), lambda i, j: (0, 0))


def _colvec_spec(tn):
    return pl.BlockSpec((1, tn), lambda i, j: (0, j))


def _simple_mm(name, lhs, lhs_fn, w, epi, epi_fn, out_dtype=F32, tm_target=512, tn_target=1024):
    m = lhs[0][0].shape[0]
    n = w.shape[1]
    tm = _pick_tile(m, tm_target, 8)
    tn = _pick_tile(n, tn_target, LANE)
    lhs = [(a, spec(tm)) for a, spec in lhs]
    epi = [(a, spec(tm, tn)) for a, spec in epi]
    out = (jax.ShapeDtypeStruct((m, n), out_dtype), _tile_spec(tm, tn))
    return _mm(name, m, lhs, lhs_fn, w, epi, epi_fn, [out], tm=tm, tn=tn)[0]


def _rows(a):
    return a, lambda tm: _row_spec(tm, a.shape[1])


def _vec(a):
    a = a.reshape(1, -1)
    return a, lambda tm: _vec_spec(a.shape[1])


def _epi_tile(a):
    return a, lambda tm, tn: _tile_spec(tm, tn)


def _epi_colvec(a):
    return a.reshape(1, -1), lambda tm, tn: _colvec_spec(tn)


def _rmsnorm_body(x_ref, g_ref, o_ref):
    o_ref[...] = _rms(x_ref[...], g_ref[...])


def _rmsnorm(x, g):
    m, d = x.shape
    tm = _pick_tile(m, 1024, 8)
    return pl.pallas_call(
        _rmsnorm_body,
        grid=(m // tm,),
        in_specs=[pl.BlockSpec((tm, d), lambda i: (i, 0)), pl.BlockSpec((1, d), lambda i: (0, 0))],
        out_specs=pl.BlockSpec((tm, d), lambda i: (i, 0)),
        out_shape=jax.ShapeDtypeStruct((m, d), F32),
        compiler_params=_params(("parallel",)),
        name="rmsnorm",
    )(x, g.reshape(1, d))


def _bf(x):
    return x.astype(BF16)


def _dot(a, b):
    return jnp.dot(_bf(a), _bf(b), preferred_element_type=F32)


def _dot_nt(a, b):
    return lax.dot_general(_bf(a), _bf(b), (((1,), (1,)), ((), ())), preferred_element_type=F32)


def _dot_tn(a, b):
    return lax.dot_general(_bf(a), _bf(b), (((0,), (0,)), ((), ())), preferred_element_type=F32)


def _scan_body(r_ref, ld_ref, k_ref, v_ref, al_ref, g_ref, kk_w_ref, ka_w_ref, rk_w_ref,
               lnw_ref, lnb_ref, o_ref, s_ref):
    c = SCAN_CHUNK
    w = HEADS_PER_GROUP * RWKV_HEAD
    wc = HEADS_PER_GROUP * c

    @pl.when(pl.program_id(2) == 0)
    def _():
        s_ref[...] = jnp.zeros_like(s_ref)

    r = r_ref[...]
    ld = ld_ref[...]
    k = k_ref[...]
    v = v_ref[...]
    al = al_ref[...]

    row = lax.broadcasted_iota(jnp.int32, (c, w), 0)
    cum = ld
    shift = 1
    while shift < c:
        cum = cum + jnp.where(row >= shift, pltpu.roll(cum, shift, 0), 0.0)
        shift *= 2
    p_incl = jnp.exp(cum)
    p_prev = jnp.exp(cum - ld)
    p_inv = jnp.exp(-cum)
    p_last = p_incl[c - 1:c, :]

    bd_r = lax.broadcasted_iota(jnp.int32, (w, w), 0) // RWKV_HEAD
    bd_c = lax.broadcasted_iota(jnp.int32, (w, w), 1) // RWKV_HEAD
    head_mask = bd_r == bd_c
    head_ones = jnp.where(head_mask, 1.0, 0.0).astype(BF16)

    def head_sum(x):
        hi = _bf(x)
        lo = _bf(x - hi.astype(F32))
        return (jnp.dot(hi, head_ones, preferred_element_type=F32)
                + jnp.dot(lo, head_ones, preferred_element_type=F32))

    kk = k * kk_w_ref[...]
    kk = kk / jnp.maximum(jnp.sqrt(head_sum(kk * kk)), 1e-12)
    kmod = k * (1.0 + (al - 1.0) * ka_w_ref[...])

    a_t = -kk * p_prev
    b_t = kk * al * p_inv
    k_t = kmod * p_inv
    r_t = r * p_incl

    def block_diag(x):
        xb = _bf(x)
        return jnp.where(head_mask, jnp.concatenate([xb] * HEADS_PER_GROUP, axis=0),
                         jnp.zeros((), BF16))

    ar = jnp.concatenate([a_t, r_t], axis=0)
    m_b = _dot_nt(ar, block_diag(b_t))
    m_k = _dot_nt(ar, block_diag(k_t))
    t_idx = lax.broadcasted_iota(jnp.int32, (c, wc), 0)
    s_idx = lax.broadcasted_iota(jnp.int32, (c, wc), 1) % c
    strict = s_idx < t_idx
    incl = s_idx <= t_idx
    a_ab = jnp.where(strict, m_b[:c], 0.0)
    a_rb = jnp.where(incl, m_b[c:], 0.0)
    a_ak = jnp.where(strict, m_k[:c], 0.0)
    a_rk = jnp.where(incl, m_k[c:], 0.0)

    n_bd = block_diag(a_ab)
    t_inv = jnp.where(s_idx == t_idx, 1.0, 0.0)
    level = 0
    while (1 << level) < c:
        tb = t_idx >> level
        sb = s_idx >> level
        sel = ((tb & 1) == 1) & (sb == tb - 1)
        x = _dot(_dot(t_inv, n_bd), block_diag(t_inv))
        t_inv = t_inv + jnp.where(sel, x, 0.0)
        level += 1

    s0 = s_ref[...]
    w1 = _dot_nt(ar, s0)
    w2 = _dot(jnp.concatenate([a_ak, a_rk], axis=0), block_diag(v))
    u = _dot(t_inv, block_diag(w1[:c] + w2[:c]))
    y = w1[c:] + w2[c:] + _dot(a_rb, block_diag(u))

    uv = jnp.concatenate([u, v], axis=0)
    bk = jnp.concatenate([b_t, k_t], axis=0) * p_last
    s_ref[...] = jnp.where(head_mask, s0 * p_last + _dot_tn(uv, bk), 0.0)

    inv_n = 1.0 / RWKV_HEAD
    mean = head_sum(y) * inv_n
    dlt = y - mean
    var = head_sum(dlt * dlt) * inv_n
    yn = dlt * lax.rsqrt(var + GN_EPS) * lnw_ref[...] + lnb_ref[...]
    bonus = head_sum(r * kmod * rk_w_ref[...]) * v
    o_ref[...] = ((yn + bonus) * g_ref[...]).astype(o_ref.dtype)


def _rwkv_scan(batch, r, ld, k, v, al, g, k_k, k_a, r_k, ln_w, ln_b):
    m, d = r.shape
    t = m // batch
    c = SCAN_CHUNK
    w = HEADS_PER_GROUP * RWKV_HEAD
    assert d % w == 0 and t % c == 0
    nc = t // c
    tok = pl.BlockSpec((c, w), lambda b, gi, ci: (b * nc + ci, gi))
    par = pl.BlockSpec((1, w), lambda b, gi, ci: (0, gi))
    return pl.pallas_call(
        _scan_body,
        grid=(batch, d // w, nc),
        in_specs=[tok] * 6 + [par] * 5,
        out_specs=tok,
        out_shape=jax.ShapeDtypeStruct((m, d), BF16),
        scratch_shapes=[pltpu.VMEM((w, w), F32)],
        compiler_params=_params(("parallel", "parallel", "arbitrary")),
        name="rwkv_scan",
    )(r, ld, k, v, al, g, *[p.reshape(1, d) for p in (k_k, k_a, r_k, ln_w, ln_b)])


def _lerp(h, hp, mu):
    return h + (hp - h) * mu


def _rwkv_mix(batch, x, norm_g, mu, w_rkv, w0, w1, w2, a0, a1, a2, g1, g2,
              k_k, k_a, r_k, ln_w, ln_b, w_o):
    m, d = x.shape
    h = _rmsnorm(x, norm_g)
    hp = jnp.pad(h.reshape(batch, m // batch, d), ((0, 0), (1, 0), (0, 0)))[:, :-1].reshape(m, d)

    def lerp_mm(name, j, w, epi=(), epi_fn=lambda acc: (acc,)):
        return _simple_mm(name, [_rows(h), _rows(hp), _vec(mu[j])], _lerp, w.astype(BF16),
                          list(epi), epi_fn)

    def pad_lora(wa, wb):
        rank = wa.shape[1]
        rp = -(-rank // LANE) * LANE
        return (jnp.pad(wa, ((0, 0), (0, rp - rank))), jnp.pad(wb, ((0, rp - rank), (0, 0))))

    ident = lambda t: t
    r = lerp_mm("rwkv_r", 0, w_rkv[0])
    k = lerp_mm("rwkv_k", 2, w_rkv[1])
    v = lerp_mm("rwkv_v", 3, w_rkv[2])

    w1p, w2p = pad_lora(w1, w2)
    tw = lerp_mm("rwkv_w1", 1, w1p, epi_fn=lambda acc: (jnp.tanh(acc),))

    def decay_epi(acc, w0_t):
        z = -(w0_t + acc)
        softplus = jnp.maximum(z, 0.0) + jnp.log1p(jnp.exp(-jnp.abs(z)))
        return (-jnp.exp(-softplus - 0.5),)

    ld = _simple_mm("rwkv_w2", [_rows(tw)], ident, w2p.astype(BF16), [_epi_colvec(w0)], decay_epi)

    a1p, a2p = pad_lora(a1, a2)
    ta = lerp_mm("rwkv_a1", 4, a1p)
    al = _simple_mm("rwkv_a2", [_rows(ta)], ident, a2p.astype(BF16), [_epi_colvec(a0)],
                    lambda acc, a0_t: (jax.nn.sigmoid(a0_t + acc),))

    g1p, g2p = pad_lora(g1, g2)
    tg = lerp_mm("rwkv_g1", 5, g1p, epi_fn=lambda acc: (jax.nn.sigmoid(acc),))
    g = _simple_mm("rwkv_g2", [_rows(tg)], ident, g2p.astype(BF16), [], lambda acc: (acc,))

    yg = _rwkv_scan(batch, r, ld, k, v, al, g, k_k, k_a, r_k.reshape(-1), ln_w, ln_b)
    return _simple_mm("rwkv_out", [_rows(yg)], ident, w_o.astype(BF16), [_epi_tile(x)],
                      lambda acc, x_t: (x_t + acc,))


def _attn_body(tq, tk, q_ref, k_ref, v_ref, o_ref):
    qi = pl.program_id(2)
    q = q_ref[...]
    dv = v_ref.shape[-1]

    def step(j, carry, masked):
        m_i, l_i, acc = carry
        start = pl.multiple_of(j * tk, tk)
        kb = k_ref[pl.ds(start, tk), :]
        vb = v_ref[pl.ds(start, tk), :]
        s = lax.dot_general(q, kb, (((1,), (1,)), ((), ())), preferred_element_type=F32)
        if masked:
            q_chunk = (qi * tq + lax.broadcasted_iota(jnp.int32, (tq, tk), 0)) // ATTN_CHUNK
            k_chunk = (j * tk + lax.broadcasted_iota(jnp.int32, (tq, tk), 1)) // ATTN_CHUNK
            s = jnp.where(k_chunk <= q_chunk, s, jnp.finfo(F32).min)
        m_new = jnp.maximum(m_i, jnp.max(s, axis=-1, keepdims=True))
        p = jnp.exp(s - m_new)
        alpha = jnp.exp(m_i - m_new)
        l_new = alpha * l_i + jnp.sum(p, axis=-1, keepdims=True)
        acc = alpha * acc + jnp.dot(p.astype(BF16), vb, preferred_element_type=F32)
        return m_new, l_new, acc

    init = (jnp.full((tq, 1), -jnp.inf, F32), jnp.zeros((tq, 1), F32), jnp.zeros((tq, dv), F32))
    carry = lax.fori_loop(0, qi, lambda j, cr: step(j, cr, False), init)
    _, l_i, acc = step(qi, carry, True)
    o_ref[...] = (acc / l_i).astype(o_ref.dtype)


def _attention(batch, q, k, v):
    heads, m, dqk = q.shape
    dv = v.shape[-1]
    t = m // batch
    tq = _pick_tile(t, 512, ATTN_CHUNK)
    nq = t // tq
    return pl.pallas_call(
        functools.partial(_attn_body, tq, tq),
        grid=(batch, heads, nq),
        in_specs=[
            pl.BlockSpec((None, tq, dqk), lambda b, h, i: (h, b * nq + i, 0)),
            pl.BlockSpec((None, t, dqk), lambda b, h, i: (h, b, 0)),
            pl.BlockSpec((None, t, dv), lambda b, h, i: (h, b, 0)),
        ],
        out_specs=pl.BlockSpec((tq, dv), lambda b, h, i: (b * nq + i, h)),
        out_shape=jax.ShapeDtypeStruct((m, heads * dv), BF16),
        compiler_params=_params(("parallel", "parallel", "arbitrary")),
        name="mla_attention",
    )(q, k, v)


def _mla_mix(batch, x, positions, norm_g, w_down, q_a_norm, kv_a_norm, w_uq, w_ukv,
             q_norm, k_norm, w_o):
    m, d = x.shape
    q_lora = q_a_norm.shape[0]
    kv_lora = kv_a_norm.shape[0]
    heads, qk_dim = w_uq.shape[1], w_uq.shape[2]
    rope = w_down.shape[1] - q_lora - kv_lora
    nope = qk_dim - rope
    dv = w_ukv.shape[2] - nope
    half = rope // 2
    assert nope == LANE and dv == LANE and rope <= LANE and q_lora == kv_lora and q_lora % LANE == 0
    scale = 1.0 / math.sqrt(qk_dim)

    perm = jnp.concatenate([jnp.arange(half, rope), jnp.arange(0, half)])
    lane_pad = lambda a: jnp.pad(a, [(0, 0)] * (a.ndim - 1) + [(0, LANE - a.shape[-1])])

    inv_freq = ROPE_THETA ** (-jnp.arange(0, rope, 2, dtype=F32) / rope)
    ang = positions.reshape(m, 1).astype(F32) * inv_freq
    cos, sin = jnp.cos(ang), jnp.sin(ang)
    cos_t = lane_pad(jnp.concatenate([cos, cos], axis=-1))
    sin_t = lane_pad(jnp.concatenate([-sin, sin], axis=-1))

    w_kpe = w_down[:, q_lora + kv_lora:]
    w_down_x = jnp.concatenate(
        [w_down[:, :q_lora + kv_lora], lane_pad(w_kpe), lane_pad(w_kpe[:, perm])], axis=1).astype(BF16)
    c = _simple_mm("mla_down", [_rows(x), _vec(norm_g)], _rms, w_down_x, [], lambda acc: (acc,),
                   tn_target=w_down_x.shape[1])
    qb = q_lora // q_lora
    tm = _pick_tile(m, 512, 8)

    def norm_rope(body, pe, rot, g_body, g_pe, g_rot, cos_b, sin_b, extra_scale):
        ss = jnp.sum(body * body, axis=-1, keepdims=True) + jnp.sum(pe * pe, axis=-1, keepdims=True)
        s = lax.rsqrt(ss / qk_dim + RMS_EPS) * extra_scale
        return jnp.concatenate([body * s * g_body, (pe * g_pe * cos_b + rot * g_rot * sin_b) * s], axis=-1)

    w_q_pe = w_uq[:, :, nope:]
    w_q_x = jnp.concatenate([w_uq[:, :, :nope], lane_pad(w_q_pe), lane_pad(w_q_pe[:, :, perm])],
                            axis=-1).reshape(q_lora, heads * 3 * LANE).astype(BF16)
    gq = jnp.concatenate([q_norm[:nope], lane_pad(q_norm[nope:]), lane_pad(q_norm[nope:][perm])]).reshape(1, -1)

    def q_epi(acc, g_t, cos_b, sin_b):
        return (norm_rope(acc[:, :LANE], acc[:, LANE:2 * LANE], acc[:, 2 * LANE:], g_t[:, :LANE],
                          g_t[:, LANE:2 * LANE], g_t[:, 2 * LANE:], cos_b, sin_b, scale),)

    tn_q = 3 * LANE
    q = _mm("mla_q", m,
            [(c, _row_spec(tm, q_lora, 0)), (q_a_norm.reshape(1, -1), _vec_spec(q_lora))], _rms, w_q_x,
            [(gq, _vec_spec(tn_q)), (cos_t, _row_spec(tm, LANE)), (sin_t, _row_spec(tm, LANE))], q_epi,
            [(jax.ShapeDtypeStruct((heads, m, 2 * LANE), BF16),
              pl.BlockSpec((None, tm, 2 * LANE), lambda i, j: (j, i, 0)))],
            tm=tm, tn=tn_q)[0]

    w_kv_x = w_ukv.reshape(kv_lora, heads * (nope + dv)).astype(BF16)
    gk = jnp.concatenate([k_norm[:nope], lane_pad(k_norm[nope:]), lane_pad(k_norm[nope:][perm])]).reshape(1, -1)
    pe_block = (q_lora + kv_lora) // LANE

    def kv_epi(acc, g_t, pe, rot, cos_b, sin_b):
        k_out = norm_rope(acc[:, :LANE], pe, rot, g_t[:, :LANE], g_t[:, LANE:2 * LANE],
                          g_t[:, 2 * LANE:], cos_b, sin_b, 1.0)
        return k_out, acc[:, LANE:]

    tn_kv = nope + dv
    k, v = _mm("mla_kv", m,
               [(c, _row_spec(tm, kv_lora, qb)), (kv_a_norm.reshape(1, -1), _vec_spec(kv_lora))], _rms, w_kv_x,
               [(gk, _vec_spec(3 * LANE)), (c, _row_spec(tm, LANE, pe_block)),
                (c, _row_spec(tm, LANE, pe_block + 1)), (cos_t, _row_spec(tm, LANE)),
                (sin_t, _row_spec(tm, LANE))], kv_epi,
               [(jax.ShapeDtypeStruct((heads, m, 2 * LANE), BF16),
                 pl.BlockSpec((None, tm, 2 * LANE), lambda i, j: (j, i, 0))),
                (jax.ShapeDtypeStruct((heads, m, dv), BF16),
                 pl.BlockSpec((None, tm, dv), lambda i, j: (j, i, 0)))],
               tm=tm, tn=tn_kv)

    o = _attention(batch, q, k, v)
    return _simple_mm("mla_out", [_rows(o)], lambda t: t, w_o.reshape(heads * dv, d).astype(BF16),
                      [_epi_tile(x)], lambda acc, x_t: (x_t + acc,))


def kernel(x, positions, ffn_norm, ffn_w13, ffn_w2, mix_norm, rwkv_mu, rwkv_w_rkv, rwkv_w0, rwkv_w1, rwkv_w2, rwkv_a0, rwkv_a1, rwkv_a2, rwkv_g1, rwkv_g2, rwkv_k_k, rwkv_k_a, rwkv_r_k, rwkv_ln_w, rwkv_ln_b, rwkv_w_o, mla_w_down, mla_q_a_norm, mla_kv_a_norm, mla_w_uq, mla_w_ukv, mla_q_norm, mla_k_norm, mla_w_o):
    batch, seq, d = x.shape
    depth = ffn_norm.shape[0]
    n_mixers = 2
    xf = x.reshape(batch * seq, d)
    for i in range(depth):
        j = i // n_mixers
        xf = _ffn(xf, ffn_norm[i, 0], ffn_w13[i, 0], ffn_w2[i, 0])
        if i % n_mixers == 0:
            xf = _rwkv_mix(batch, xf, mix_norm[i], rwkv_mu[j], rwkv_w_rkv[j], rwkv_w0[j], rwkv_w1[j],
                           rwkv_w2[j], rwkv_a0[j], rwkv_a1[j], rwkv_a2[j], rwkv_g1[j], rwkv_g2[j],
                           rwkv_k_k[j], rwkv_k_a[j], rwkv_r_k[j], rwkv_ln_w[j], rwkv_ln_b[j],
                           rwkv_w_o[j])
        else:
            xf = _mla_mix(batch, xf, positions, mix_norm[i], mla_w_down[j], mla_q_a_norm[j],
                          mla_kv_a_norm[j], mla_w_uq[j], mla_w_ukv[j], mla_q_norm[j], mla_k_norm[j],
                          mla_w_o[j])
        xf = _ffn(xf, ffn_norm[i, 1], ffn_w13[i, 1], ffn_w2[i, 1])
    return xf.reshape(batch, seq, d)
```

```python
import functools
import math

import jax
import jax.numpy as jnp
from jax import lax
from jax.experimental import pallas as pl
from jax.experimental.pallas import tpu as pltpu

F32 = jnp.float32
BF16 = jnp.bfloat16

RMS_EPS = 1e-6
GN_EPS = 64e-5
ROPE_THETA = 10000.0
ATTN_CHUNK = 64
RWKV_HEAD = 64
SCAN_CHUNK = 64
HEADS_PER_GROUP = 4
SCAN_STREAMS = 8
LANE = 128
VMEM_LIMIT_BYTES = 56 * 1024 * 1024


def _params(semantics):
    return pltpu.CompilerParams(dimension_semantics=semantics,
                                vmem_limit_bytes=VMEM_LIMIT_BYTES)


def _pick_tile(n, target, quantum):
    if n <= target:
        return n
    t = (target // quantum) * quantum
    while t > quantum and n % t:
        t -= quantum
    assert n % t == 0, (n, target, quantum)
    return t


def _rms(x, g):
    return x * lax.rsqrt(jnp.mean(x * x, axis=-1, keepdims=True) + RMS_EPS) * g


def _ffn_body(x_ref, g_ref, wg_ref, wu_ref, w2_ref, o_ref, xn_ref, acc_ref):
    j = pl.program_id(1)

    @pl.when(j == 0)
    def _():
        xn_ref[...] = _rms(x_ref[...], g_ref[...]).astype(BF16)
        acc_ref[...] = jnp.zeros_like(acc_ref)

    xn = xn_ref[...]
    gate = jnp.dot(xn, wg_ref[...], preferred_element_type=F32)
    up = jnp.dot(xn, wu_ref[...], preferred_element_type=F32)
    act = (gate * jax.nn.sigmoid(gate) * up).astype(BF16)
    acc_ref[...] += jnp.dot(act, w2_ref[...], preferred_element_type=F32)

    @pl.when(j == pl.num_programs(1) - 1)
    def _():
        o_ref[...] = x_ref[...] + 0.5 * acc_ref[...]


def _ffn(x, g, w13, w2):
    m, d = x.shape
    f = w2.shape[0]
    tf = 512 if f > 512 else LANE
    fp = -(-f // tf) * tf
    wg = jnp.pad(w13[:, :f], ((0, 0), (0, fp - f))).astype(BF16)
    wu = jnp.pad(w13[:, f:], ((0, 0), (0, fp - f))).astype(BF16)
    w2p = jnp.pad(w2, ((0, fp - f), (0, 0))).astype(BF16)
    tm = _pick_tile(m, 512, 8)
    return pl.pallas_call(
        _ffn_body,
        grid=(m // tm, fp // tf),
        in_specs=[
            pl.BlockSpec((tm, d), lambda i, j: (i, 0)),
            pl.BlockSpec((1, d), lambda i, j: (0, 0)),
            pl.BlockSpec((d, tf), lambda i, j: (0, j)),
            pl.BlockSpec((d, tf), lambda i, j: (0, j)),
            pl.BlockSpec((tf, d), lambda i, j: (j, 0)),
        ],
        out_specs=pl.BlockSpec((tm, d), lambda i, j: (i, 0)),
        out_shape=jax.ShapeDtypeStruct((m, d), F32),
        scratch_shapes=[pltpu.VMEM((tm, d), BF16), pltpu.VMEM((tm, d), F32)],
        compiler_params=_params(("parallel", "arbitrary")),
        name="ffn",
    )(x, g.reshape(1, d), wg, wu, w2p)


def _mm_body(n_lhs, n_epi, n_out, lhs_fn, epi_fn, *refs):
    lhs_refs = refs[:n_lhs]
    w_ref = refs[n_lhs]
    epi_refs = refs[n_lhs + 1:n_lhs + 1 + n_epi]
    out_refs = refs[n_lhs + 1 + n_epi:n_lhs + 1 + n_epi + n_out]
    lhs_scratch = refs[-1]

    @pl.when(pl.program_id(1) == 0)
    def _():
        lhs_scratch[...] = lhs_fn(*[r[...] for r in lhs_refs]).astype(BF16)

    acc = jnp.dot(lhs_scratch[...], w_ref[...], preferred_element_type=F32)
    outs = epi_fn(acc, *[r[...] for r in epi_refs])
    for o_ref, o in zip(out_refs, outs):
        o_ref[...] = o.astype(o_ref.dtype)


def _mm(name, m, lhs, lhs_fn, w, epi, epi_fn, outs, *, tm, tn):
    k, n = w.shape
    body = functools.partial(_mm_body, len(lhs), len(epi), len(outs), lhs_fn, epi_fn)
    return pl.pallas_call(
        body,
        grid=(m // tm, n // tn),
        in_specs=[s for _, s in lhs] + [pl.BlockSpec((k, tn), lambda i, j: (0, j))]
        + [s for _, s in epi],
        out_specs=[s for _, s in outs],
        out_shape=[s for s, _ in outs],
        scratch_shapes=[pltpu.VMEM((tm, k), BF16)],
        compiler_params=_params(("parallel", "arbitrary")),
        name=name,
    )(*[a for a, _ in lhs], w, *[a for a, _ in epi])


def _row_spec(tm, width, col_block=0):
    return pl.BlockSpec((tm, width), lambda i, j: (i, col_block))


def _tile_spec(tm, tn):
    return pl.BlockSpec((tm, tn), lambda i, j: (i, j))


def _vec_spec(width):
    return pl.BlockSpec((1, width), lambda i, j: (0, 0))


def _colvec_spec(tn):
    return pl.BlockSpec((1, tn), lambda i, j: (0, j))


def _simple_mm(name, lhs, lhs_fn, w, epi, epi_fn, out_dtype=F32, tm_target=512, tn_target=1024):
    m = lhs[0][0].shape[0]
    n = w.shape[1]
    tm = _pick_tile(m, tm_target, 8)
    tn = _pick_tile(n, tn_target, LANE)
    lhs = [(a, spec(tm)) for a, spec in lhs]
    epi = [(a, spec(tm, tn)) for a, spec in epi]
    out = (jax.ShapeDtypeStruct((m, n), out_dtype), _tile_spec(tm, tn))
    return _mm(name, m, lhs, lhs_fn, w, epi, epi_fn, [out], tm=tm, tn=tn)[0]


def _rows(a):
    return a, lambda tm: _row_spec(tm, a.shape[1])


def _vec(a):
    a = a.reshape(1, -1)
    return a, lambda tm: _vec_spec(a.shape[1])


def _epi_tile(a):
    return a, lambda tm, tn: _tile_spec(tm, tn)


def _epi_colvec(a):
    return a.reshape(1, -1), lambda tm, tn: _colvec_spec(tn)


def _rmsnorm_body(x_ref, g_ref, o_ref):
    o_ref[...] = _rms(x_ref[...], g_ref[...])


def _rmsnorm(x, g):
    m, d = x.shape
    tm = _pick_tile(m, 1024, 8)
    return pl.pallas_call(
        _rmsnorm_body,
        grid=(m // tm,),
        in_specs=[pl.BlockSpec((tm, d), lambda i: (i, 0)), pl.BlockSpec((1, d), lambda i: (0, 0))],
        out_specs=pl.BlockSpec((tm, d), lambda i: (i, 0)),
        out_shape=jax.ShapeDtypeStruct((m, d), F32),
        compiler_params=_params(("parallel",)),
        name="rmsnorm",
    )(x, g.reshape(1, d))


def _bf(x):
    return x.astype(BF16)


def _dot(a, b):
    return jnp.dot(_bf(a), _bf(b), preferred_element_type=F32)


def _dot_nt(a, b):
    return lax.dot_general(_bf(a), _bf(b), (((1,), (1,)), ((), ())), preferred_element_type=F32)


def _dot_tn(a, b):
    return lax.dot_general(_bf(a), _bf(b), (((0,), (0,)), ((), ())), preferred_element_type=F32)


def _scan_body(n_streams, r_ref, ld_ref, k_ref, v_ref, al_ref, g_ref, kk_w_ref, ka_w_ref, rk_w_ref,
               lnw_ref, lnb_ref, o_ref, s_ref):
    c = SCAN_CHUNK
    w = HEADS_PER_GROUP * RWKV_HEAD
    wc = HEADS_PER_GROUP * c

    @pl.when(pl.program_id(2) == 0)
    def _():
        s_ref[...] = jnp.zeros_like(s_ref)

    row = lax.broadcasted_iota(jnp.int32, (c, w), 0)
    bd_r = lax.broadcasted_iota(jnp.int32, (w, w), 0) // RWKV_HEAD
    bd_c = lax.broadcasted_iota(jnp.int32, (w, w), 1) // RWKV_HEAD
    head_mask = bd_r == bd_c
    head_ones = jnp.where(head_mask, 1.0, 0.0).astype(BF16)
    t_idx = lax.broadcasted_iota(jnp.int32, (c, wc), 0)
    s_idx = lax.broadcasted_iota(jnp.int32, (c, wc), 1) % c
    strict = s_idx < t_idx
    incl = s_idx <= t_idx
    eye = jnp.where(s_idx == t_idx, 1.0, 0.0)

    def head_sum(x):
        hi = _bf(x)
        lo = _bf(x - hi.astype(F32))
        return (jnp.dot(hi, head_ones, preferred_element_type=F32)
                + jnp.dot(lo, head_ones, preferred_element_type=F32))

    def block_diag(x):
        xb = _bf(x)
        return jnp.where(head_mask, jnp.concatenate([xb] * HEADS_PER_GROUP, axis=0),
                         jnp.zeros((), BF16))

    def stream(si):
        ln = pl.ds(si * w, w)
        r = r_ref[:, ln]
        ld = ld_ref[:, ln]
        k = k_ref[:, ln]
        v = v_ref[:, ln]
        al = al_ref[:, ln]

        cum = ld
        shift = 1
        while shift < c:
            cum = cum + jnp.where(row >= shift, pltpu.roll(cum, shift, 0), 0.0)
            shift *= 2
        p_incl = jnp.exp(cum)
        p_prev = jnp.exp(cum - ld)
        p_inv = jnp.exp(-cum)
        p_last = p_incl[c - 1:c, :]

        kk = k * kk_w_ref[:, ln]
        ss = head_sum(kk * kk)
        yield
        kk = kk / jnp.maximum(jnp.sqrt(ss), 1e-12)
        kmod = k * (1.0 + (al - 1.0) * ka_w_ref[:, ln])

        a_t = -kk * p_prev
        b_t = kk * al * p_inv
        k_t = kmod * p_inv
        r_t = r * p_incl

        ar = jnp.concatenate([a_t, r_t], axis=0)
        m_b = _dot_nt(ar, block_diag(b_t))
        m_k = _dot_nt(ar, block_diag(k_t))
        yield
        a_ab = jnp.where(strict, m_b[:c], 0.0)
        a_rb = jnp.where(incl, m_b[c:], 0.0)
        a_ak = jnp.where(strict, m_k[:c], 0.0)
        a_rk = jnp.where(incl, m_k[c:], 0.0)

        n_bd = block_diag(a_ab)
        t_inv = eye
        level = 0
        while (1 << level) < c:
            tb = t_idx >> level
            sb = s_idx >> level
            sel = ((tb & 1) == 1) & (sb == tb - 1)
            tn = _dot(t_inv, n_bd)
            yield
            x = _dot(tn, block_diag(t_inv))
            yield
            t_inv = t_inv + jnp.where(sel, x, 0.0)
            level += 1

        s0 = s_ref[si]
        w1 = _dot_nt(ar, s0)
        w2 = _dot(jnp.concatenate([a_ak, a_rk], axis=0), block_diag(v))
        yield
        u = _dot(t_inv, block_diag(w1[:c] + w2[:c]))
        yield
        y = w1[c:] + w2[c:] + _dot(a_rb, block_diag(u))

        uv = jnp.concatenate([u, v], axis=0)
        bk = jnp.concatenate([b_t, k_t], axis=0) * p_last
        s_new = _dot_tn(uv, bk)
        yield
        s_ref[si] = jnp.where(head_mask, s0 * p_last + s_new, 0.0)

        inv_n = 1.0 / RWKV_HEAD
        mean = head_sum(y) * inv_n
        yield
        dlt = y - mean
        var = head_sum(dlt * dlt) * inv_n
        bsum = head_sum(r * kmod * rk_w_ref[:, ln])
        yield
        yn = dlt * lax.rsqrt(var + GN_EPS) * lnw_ref[:, ln] + lnb_ref[:, ln]
        bonus = bsum * v
        o_ref[:, ln] = ((yn + bonus) * g_ref[:, ln]).astype(o_ref.dtype)

    live = [stream(si) for si in range(n_streams)]
    while live:
        live = [g for g in live if next(g, True) is None]


def _rwkv_scan(batch, r, ld, k, v, al, g, k_k, k_a, r_k, ln_w, ln_b):
    m, d = r.shape
    t = m // batch
    c = SCAN_CHUNK
    w = HEADS_PER_GROUP * RWKV_HEAD
    assert d % w == 0 and t % c == 0 and c == RWKV_HEAD
    n_streams = _pick_tile(d // w, SCAN_STREAMS, 1)
    wb = n_streams * w
    nc = t // c
    tok = pl.BlockSpec((c, wb), lambda b, gi, ci: (b * nc + ci, gi))
    par = pl.BlockSpec((1, wb), lambda b, gi, ci: (0, gi))
    return pl.pallas_call(
        functools.partial(_scan_body, n_streams),
        grid=(batch, d // wb, nc),
        in_specs=[tok] * 6 + [par] * 5,
        out_specs=tok,
        out_shape=jax.ShapeDtypeStruct((m, d), BF16),
        scratch_shapes=[pltpu.VMEM((n_streams, w, w), F32)],
        compiler_params=_params(("parallel", "parallel", "arbitrary")),
        name="rwkv_scan",
    )(r, ld, k, v, al, g, *[p.reshape(1, d) for p in (k_k, k_a, r_k, ln_w, ln_b)])


def _lerp(h, hp, mu):
    return h + (hp - h) * mu


def _rwkv_mix(batch, x, norm_g, mu, w_rkv, w0, w1, w2, a0, a1, a2, g1, g2,
              k_k, k_a, r_k, ln_w, ln_b, w_o):
    m, d = x.shape
    h = _rmsnorm(x, norm_g)
    hp = jnp.pad(h.reshape(batch, m // batch, d), ((0, 0), (1, 0), (0, 0)))[:, :-1].reshape(m, d)

    def lerp_mm(name, j, w, epi=(), epi_fn=lambda acc: (acc,)):
        return _simple_mm(name, [_rows(h), _rows(hp), _vec(mu[j])], _lerp, w.astype(BF16),
                          list(epi), epi_fn)

    def pad_lora(wa, wb):
        rank = wa.shape[1]
        rp = -(-rank // LANE) * LANE
        return (jnp.pad(wa, ((0, 0), (0, rp - rank))), jnp.pad(wb, ((0, rp - rank), (0, 0))))

    ident = lambda t: t
    r = lerp_mm("rwkv_r", 0, w_rkv[0])
    k = lerp_mm("rwkv_k", 2, w_rkv[1])
    v = lerp_mm("rwkv_v", 3, w_rkv[2])

    w1p, w2p = pad_lora(w1, w2)
    tw = lerp_mm("rwkv_w1", 1, w1p, epi_fn=lambda acc: (jnp.tanh(acc),))

    def decay_epi(acc, w0_t):
        z = -(w0_t + acc)
        softplus = jnp.maximum(z, 0.0) + jnp.log1p(jnp.exp(-jnp.abs(z)))
        return (-jnp.exp(-softplus - 0.5),)

    ld = _simple_mm("rwkv_w2", [_rows(tw)], ident, w2p.astype(BF16), [_epi_colvec(w0)], decay_epi)

    a1p, a2p = pad_lora(a1, a2)
    ta = lerp_mm("rwkv_a1", 4, a1p)
    al = _simple_mm("rwkv_a2", [_rows(ta)], ident, a2p.astype(BF16), [_epi_colvec(a0)],
                    lambda acc, a0_t: (jax.nn.sigmoid(a0_t + acc),))

    g1p, g2p = pad_lora(g1, g2)
    tg = lerp_mm("rwkv_g1", 5, g1p, epi_fn=lambda acc: (jax.nn.sigmoid(acc),))
    g = _simple_mm("rwkv_g2", [_rows(tg)], ident, g2p.astype(BF16), [], lambda acc: (acc,))

    yg = _rwkv_scan(batch, r, ld, k, v, al, g, k_k, k_a, r_k.reshape(-1), ln_w, ln_b)
    return _simple_mm("rwkv_out", [_rows(yg)], ident, w_o.astype(BF16), [_epi_tile(x)],
                      lambda acc, x_t: (x_t + acc,))


def _attn_body(tq, tk, q_ref, k_ref, v_ref, o_ref):
    qi = pl.program_id(2)
    q = q_ref[...]
    dv = v_ref.shape[-1]

    def step(j, carry, masked):
        m_i, l_i, acc = carry
        start = pl.multiple_of(j * tk, tk)
        kb = k_ref[pl.ds(start, tk), :]
        vb = v_ref[pl.ds(start, tk), :]
        s = lax.dot_general(q, kb, (((1,), (1,)), ((), ())), preferred_element_type=F32)
        if masked:
            q_chunk = (qi * tq + lax.broadcasted_iota(jnp.int32, (tq, tk), 0)) // ATTN_CHUNK
            k_chunk = (j * tk + lax.broadcasted_iota(jnp.int32, (tq, tk), 1)) // ATTN_CHUNK
            s = jnp.where(k_chunk <= q_chunk, s, jnp.finfo(F32).min)
        m_new = jnp.maximum(m_i, jnp.max(s, axis=-1, keepdims=True))
        p = jnp.exp(s - m_new)
        alpha = jnp.exp(m_i - m_new)
        l_new = alpha * l_i + jnp.sum(p, axis=-1, keepdims=True)
        acc = alpha * acc + jnp.dot(p.astype(BF16), vb, preferred_element_type=F32)
        return m_new, l_new, acc

    init = (jnp.full((tq, 1), -jnp.inf, F32), jnp.zeros((tq, 1), F32), jnp.zeros((tq, dv), F32))
    carry = lax.fori_loop(0, qi, lambda j, cr: step(j, cr, False), init)
    _, l_i, acc = step(qi, carry, True)
    o_ref[...] = (acc / l_i).astype(o_ref.dtype)


def _attention(batch, q, k, v):
    heads, m, dqk = q.shape
    dv = v.shape[-1]
    t = m // batch
    tq = _pick_tile(t, 512, ATTN_CHUNK)
    nq = t // tq
    return pl.pallas_call(
        functools.partial(_attn_body, tq, tq),
        grid=(batch, heads, nq),
        in_specs=[
            pl.BlockSpec((None, tq, dqk), lambda b, h, i: (h, b * nq + i, 0)),
            pl.BlockSpec((None, t, dqk), lambda b, h, i: (h, b, 0)),
            pl.BlockSpec((None, t, dv), lambda b, h, i: (h, b, 0)),
        ],
        out_specs=pl.BlockSpec((tq, dv), lambda b, h, i: (b * nq + i, h)),
        out_shape=jax.ShapeDtypeStruct((m, heads * dv), BF16),
        compiler_params=_params(("parallel", "parallel", "arbitrary")),
        name="mla_attention",
    )(q, k, v)


def _mla_mix(batch, x, positions, norm_g, w_down, q_a_norm, kv_a_norm, w_uq, w_ukv,
             q_norm, k_norm, w_o):
    m, d = x.shape
    q_lora = q_a_norm.shape[0]
    kv_lora = kv_a_norm.shape[0]
    heads, qk_dim = w_uq.shape[1], w_uq.shape[2]
    rope = w_down.shape[1] - q_lora - kv_lora
    nope = qk_dim - rope
    dv = w_ukv.shape[2] - nope
    half = rope // 2
    assert nope == LANE and dv == LANE and rope <= LANE and q_lora == kv_lora and q_lora % LANE == 0
    scale = 1.0 / math.sqrt(qk_dim)

    perm = jnp.concatenate([jnp.arange(half, rope), jnp.arange(0, half)])
    lane_pad = lambda a: jnp.pad(a, [(0, 0)] * (a.ndim - 1) + [(0, LANE - a.shape[-1])])

    inv_freq = ROPE_THETA ** (-jnp.arange(0, rope, 2, dtype=F32) / rope)
    ang = positions.reshape(m, 1).astype(F32) * inv_freq
    cos, sin = jnp.cos(ang), jnp.sin(ang)
    cos_t = lane_pad(jnp.concatenate([cos, cos], axis=-1))
    sin_t = lane_pad(jnp.concatenate([-sin, sin], axis=-1))

    w_kpe = w_down[:, q_lora + kv_lora:]
    w_down_x = jnp.concatenate(
        [w_down[:, :q_lora + kv_lora], lane_pad(w_kpe), lane_pad(w_kpe[:, perm])], axis=1).astype(BF16)
    c = _simple_mm("mla_down", [_rows(x), _vec(norm_g)], _rms, w_down_x, [], lambda acc: (acc,),
                   tn_target=w_down_x.shape[1])
    qb = q_lora // q_lora
    tm = _pick_tile(m, 512, 8)

    def norm_rope(body, pe, rot, g_body, g_pe, g_rot, cos_b, sin_b, extra_scale):
        ss = jnp.sum(body * body, axis=-1, keepdims=True) + jnp.sum(pe * pe, axis=-1, keepdims=True)
        s = lax.rsqrt(ss / qk_dim + RMS_EPS) * extra_scale
        return jnp.concatenate([body * s * g_body, (pe * g_pe * cos_b + rot * g_rot * sin_b) * s], axis=-1)

    w_q_pe = w_uq[:, :, nope:]
    w_q_x = jnp.concatenate([w_uq[:, :, :nope], lane_pad(w_q_pe), lane_pad(w_q_pe[:, :, perm])],
                            axis=-1).reshape(q_lora, heads * 3 * LANE).astype(BF16)
    gq = jnp.concatenate([q_norm[:nope], lane_pad(q_norm[nope:]), lane_pad(q_norm[nope:][perm])]).reshape(1, -1)

    def q_epi(acc, g_t, cos_b, sin_b):
        return (norm_rope(acc[:, :LANE], acc[:, LANE:2 * LANE], acc[:, 2 * LANE:], g_t[:, :LANE],
                          g_t[:, LANE:2 * LANE], g_t[:, 2 * LANE:], cos_b, sin_b, scale),)

    tn_q = 3 * LANE
    q = _mm("mla_q", m,
            [(c, _row_spec(tm, q_lora, 0)), (q_a_norm.reshape(1, -1), _vec_spec(q_lora))], _rms, w_q_x,
            [(gq, _vec_spec(tn_q)), (cos_t, _row_spec(tm, LANE)), (sin_t, _row_spec(tm, LANE))], q_epi,
            [(jax.ShapeDtypeStruct((heads, m, 2 * LANE), BF16),
              pl.BlockSpec((None, tm, 2 * LANE), lambda i, j: (j, i, 0)))],
            tm=tm, tn=tn_q)[0]

    w_kv_x = w_ukv.reshape(kv_lora, heads * (nope + dv)).astype(BF16)
    gk = jnp.concatenate([k_norm[:nope], lane_pad(k_norm[nope:]), lane_pad(k_norm[nope:][perm])]).reshape(1, -1)
    pe_block = (q_lora + kv_lora) // LANE

    def kv_epi(acc, g_t, pe, rot, cos_b, sin_b):
        k_out = norm_rope(acc[:, :LANE], pe, rot, g_t[:, :LANE], g_t[:, LANE:2 * LANE],
                          g_t[:, 2 * LANE:], cos_b, sin_b, 1.0)
        return k_out, acc[:, LANE:]

    tn_kv = nope + dv
    k, v = _mm("mla_kv", m,
               [(c, _row_spec(tm, kv_lora, qb)), (kv_a_norm.reshape(1, -1), _vec_spec(kv_lora))], _rms, w_kv_x,
               [(gk, _vec_spec(3 * LANE)), (c, _row_spec(tm, LANE, pe_block)),
                (c, _row_spec(tm, LANE, pe_block + 1)), (cos_t, _row_spec(tm, LANE)),
                (sin_t, _row_spec(tm, LANE))], kv_epi,
               [(jax.ShapeDtypeStruct((heads, m, 2 * LANE), BF16),
                 pl.BlockSpec((None, tm, 2 * LANE), lambda i, j: (j, i, 0))),
                (jax.ShapeDtypeStruct((heads, m, dv), BF16),
                 pl.BlockSpec((None, tm, dv), lambda i, j: (j, i, 0)))],
               tm=tm, tn=tn_kv)

    o = _attention(batch, q, k, v)
    return _simple_mm("mla_out", [_rows(o)], lambda t: t, w_o.reshape(heads * dv, d).astype(BF16),
                      [_epi_tile(x)], lambda acc, x_t: (x_t + acc,))


def kernel(x, positions, ffn_norm, ffn_w13, ffn_w2, mix_norm, rwkv_mu, rwkv_w_rkv, rwkv_w0, rwkv_w1, rwkv_w2, rwkv_a0, rwkv_a1, rwkv_a2, rwkv_g1, rwkv_g2, rwkv_k_k, rwkv_k_a, rwkv_r_k, rwkv_ln_w, rwkv_ln_b, rwkv_w_o, mla_w_down, mla_q_a_norm, mla_kv_a_norm, mla_w_uq, mla_w_ukv, mla_q_norm, mla_k_norm, mla_w_o):
    batch, seq, d = x.shape
    depth = ffn_norm.shape[0]
    n_mixers = 2
    xf = x.reshape(batch * seq, d)
    for i in range(depth):
        j = i // n_mixers
        xf = _ffn(xf, ffn_norm[i, 0], ffn_w13[i, 0], ffn_w2[i, 0])
        if i % n_mixers == 0:
            xf = _rwkv_mix(batch, xf, mix_norm[i], rwkv_mu[j], rwkv_w_rkv[j], rwkv_w0[j], rwkv_w1[j],
                           rwkv_w2[j], rwkv_a0[j], rwkv_a1[j], rwkv_a2[j], rwkv_g1[j], rwkv_g2[j],
                           rwkv_k_k[j], rwkv_k_a[j], rwkv_r_k[j], rwkv_ln_w[j], rwkv_ln_b[j],
                           rwkv_w_o[j])
        else:
            xf = _mla_mix(batch, xf, positions, mix_norm[i], mla_w_down[j], mla_q_a_norm[j],
                          mla_kv_a_norm[j], mla_w_uq[j], mla_w_ukv[j], mla_q_norm[j], mla_k_norm[j],
                          mla_w_o[j])
        xf = _ffn(xf, ffn_norm[i, 1], ffn_w13[i, 1], ffn_w2[i, 1])
    return xf.reshape(batch, seq, d)
```

```python
import functools
import math

import jax
import jax.numpy as jnp
from jax import lax
from jax.experimental import pallas as pl
from jax.experimental.pallas import tpu as pltpu

F32 = jnp.float32
BF16 = jnp.bfloat16

RMS_EPS = 1e-6
GN_EPS = 64e-5
ROPE_THETA = 10000.0
ATTN_CHUNK = 64
RWKV_HEAD = 64
SCAN_CHUNK = 64
HEADS_PER_GROUP = 4
MLA_HEADS_PER_STEP = 4
SCAN_STREAMS = 8
LANE = 128
VMEM_LIMIT_BYTES = 56 * 1024 * 1024


def _params(semantics):
    return pltpu.CompilerParams(dimension_semantics=semantics,
                                vmem_limit_bytes=VMEM_LIMIT_BYTES)


def _pick_tile(n, target, quantum):
    if n <= target:
        return n
    t = (target // quantum) * quantum
    while t > quantum and n % t:
        t -= quantum
    assert n % t == 0, (n, target, quantum)
    return t


def _rms(x, g):
    return x * lax.rsqrt(jnp.mean(x * x, axis=-1, keepdims=True) + RMS_EPS) * g


def _ffn_body(x_ref, g_ref, wg_ref, wu_ref, w2_ref, o_ref, xn_ref, acc_ref):
    j = pl.program_id(1)

    @pl.when(j == 0)
    def _():
        xn_ref[...] = _rms(x_ref[...], g_ref[...]).astype(BF16)
        acc_ref[...] = jnp.zeros_like(acc_ref)

    xn = xn_ref[...]
    gate = jnp.dot(xn, wg_ref[...], preferred_element_type=F32)
    up = jnp.dot(xn, wu_ref[...], preferred_element_type=F32)
    act = (gate * jax.nn.sigmoid(gate) * up).astype(BF16)
    acc_ref[...] += jnp.dot(act, w2_ref[...], preferred_element_type=F32)

    @pl.when(j == pl.num_programs(1) - 1)
    def _():
        o_ref[...] = x_ref[...] + 0.5 * acc_ref[...]


def _ffn_tile(f):
    return 512 if f > 512 else LANE


def _ffn_weights(w13, w2):
    n, d, f2 = w13.shape
    f = f2 // 2
    tf = _ffn_tile(f)
    fp = -(-f // tf) * tf
    w13p = jnp.pad(w13.reshape(n, d, 2, f), ((0, 0), (0, 0), (0, 0), (0, fp - f))).astype(BF16)
    w2p = jnp.pad(w2, ((0, 0), (0, fp - f), (0, 0))).astype(BF16)
    return w13p.reshape(n, d, 2 * fp), w2p


def _ffn(x, g, w13p, w2p):
    m, d = x.shape
    fp = w2p.shape[0]
    tf = _ffn_tile(fp)
    nj = fp // tf
    tm = _pick_tile(m, 512, 8)
    return pl.pallas_call(
        _ffn_body,
        grid=(m // tm, nj),
        in_specs=[
            pl.BlockSpec((tm, d), lambda i, j: (i, 0)),
            pl.BlockSpec((1, d), lambda i, j: (0, 0)),
            pl.BlockSpec((d, tf), lambda i, j: (0, j)),
            pl.BlockSpec((d, tf), lambda i, j: (0, j + nj)),
            pl.BlockSpec((tf, d), lambda i, j: (j, 0)),
        ],
        out_specs=pl.BlockSpec((tm, d), lambda i, j: (i, 0)),
        out_shape=jax.ShapeDtypeStruct((m, d), F32),
        scratch_shapes=[pltpu.VMEM((tm, d), BF16), pltpu.VMEM((tm, d), F32)],
        compiler_params=_params(("parallel", "arbitrary")),
        name="ffn",
    )(x, g.reshape(1, d), w13p, w13p, w2p)


def _mm_body(n_lhs, n_epi, n_out, lhs_fn, epi_fn, *refs):
    lhs_refs = refs[:n_lhs]
    w_ref = refs[n_lhs]
    epi_refs = refs[n_lhs + 1:n_lhs + 1 + n_epi]
    out_refs = refs[n_lhs + 1 + n_epi:n_lhs + 1 + n_epi + n_out]
    lhs_scratch = refs[-1]

    @pl.when(pl.program_id(1) == 0)
    def _():
        lhs_scratch[...] = lhs_fn(*[r[...] for r in lhs_refs]).astype(BF16)

    acc = jnp.dot(lhs_scratch[...], w_ref[...], preferred_element_type=F32)
    outs = epi_fn(acc, *[r[...] for r in epi_refs])
    for o_ref, o in zip(out_refs, outs):
        o_ref[...] = o.astype(o_ref.dtype)


def _mm(name, m, lhs, lhs_fn, w, epi, epi_fn, outs, *, tm, tn):
    k, n = w.shape
    body = functools.partial(_mm_body, len(lhs), len(epi), len(outs), lhs_fn, epi_fn)
    return pl.pallas_call(
        body,
        grid=(m // tm, n // tn),
        in_specs=[s for _, s in lhs] + [pl.BlockSpec((k, tn), lambda i, j: (0, j))]
        + [s for _, s in epi],
        out_specs=[s for _, s in outs],
        out_shape=[s for s, _ in outs],
        scratch_shapes=[pltpu.VMEM((tm, k), BF16)],
        compiler_params=_params(("parallel", "arbitrary")),
        name=name,
    )(*[a for a, _ in lhs], w, *[a for a, _ in epi])


def _row_spec(tm, width, col_block=0):
    return pl.BlockSpec((tm, width), lambda i, j: (i, col_block))


def _tile_spec(tm, tn):
    return pl.BlockSpec((tm, tn), lambda i, j: (i, j))


def _vec_spec(width):
    return pl.BlockSpec((1, width), lambda i, j: (0, 0))


def _colvec_spec(tn):
    return pl.BlockSpec((1, tn), lambda i, j: (0, j))


def _simple_mm(name, lhs, lhs_fn, w, epi, epi_fn, out_dtype=F32, tm_target=512, tn_target=1024):
    m = lhs[0][0].shape[0]
    n = w.shape[1]
    tm = _pick_tile(m, tm_target, 8)
    tn = _pick_tile(n, tn_target, LANE)
    lhs = [(a, spec(tm)) for a, spec in lhs]
    epi = [(a, spec(tm, tn)) for a, spec in epi]
    out = (jax.ShapeDtypeStruct((m, n), out_dtype), _tile_spec(tm, tn))
    return _mm(name, m, lhs, lhs_fn, w, epi, epi_fn, [out], tm=tm, tn=tn)[0]


def _rows(a):
    return a, lambda tm: _row_spec(tm, a.shape[1])


def _vec(a):
    a = a.reshape(1, -1)
    return a, lambda tm: _vec_spec(a.shape[1])


def _epi_tile(a):
    return a, lambda tm, tn: _tile_spec(tm, tn)


def _epi_colvec(a):
    return a.reshape(1, -1), lambda tm, tn: _colvec_spec(tn)


SUBLANE = 8


def _rwkv_proj_body(seq_tiles, ranks, x_ref, xprev_ref, g_ref, mu_ref, wr_ref, wk_ref, wv_ref, l1_ref,
                    w2_ref, a2_ref, g2_ref, w0_ref, a0_ref, rkv_ref, lag_ref,
                    lerp_scr, tw_scr, ta_scr, tg_scr):
    i = pl.program_id(0)
    rw, ra = ranks

    @pl.when(pl.program_id(1) == 0)
    def _():
        h = _rms(x_ref[...], g_ref[...])
        prev = _rms(xprev_ref[...], g_ref[...])[SUBLANE - 1:SUBLANE, :]
        prev = jnp.where(i % seq_tiles == 0, 0.0, prev)
        row = lax.broadcasted_iota(jnp.int32, h.shape, 0)
        dx = jnp.where(row == 0, prev, pltpu.roll(h, 1, 0)) - h
        for idx in range(6):
            lerp_scr[idx] = (h + dx * mu_ref[idx:idx + 1, :]).astype(BF16)
        tw_scr[...] = jnp.tanh(jnp.dot(lerp_scr[3], l1_ref[:, :rw],
                                       preferred_element_type=F32)).astype(BF16)
        ta_scr[...] = jnp.dot(lerp_scr[4], l1_ref[:, rw:rw + ra],
                              preferred_element_type=F32).astype(BF16)
        tg_scr[...] = jax.nn.sigmoid(jnp.dot(lerp_scr[5], l1_ref[:, rw + ra:],
                                             preferred_element_type=F32)).astype(BF16)

    rkv_ref[0] = jnp.dot(lerp_scr[0], wr_ref[...], preferred_element_type=F32)
    rkv_ref[1] = jnp.dot(lerp_scr[1], wk_ref[...], preferred_element_type=F32)
    rkv_ref[2] = jnp.dot(lerp_scr[2], wv_ref[...], preferred_element_type=F32)
    z = -(w0_ref[...] + jnp.dot(tw_scr[...], w2_ref[...], preferred_element_type=F32))
    softplus = jnp.maximum(z, 0.0) + jnp.log1p(jnp.exp(-jnp.abs(z)))
    lag_ref[0] = -jnp.exp(-softplus - 0.5)
    lag_ref[1] = jax.nn.sigmoid(
        a0_ref[...] + jnp.dot(ta_scr[...], a2_ref[...], preferred_element_type=F32))
    lag_ref[2] = jnp.dot(tg_scr[...], g2_ref[...], preferred_element_type=F32)


def _pad_lora(wa, wb):
    rank = wa.shape[1]
    rp = -(-rank // LANE) * LANE
    return (jnp.pad(wa, ((0, 0), (0, rp - rank))).astype(BF16),
            jnp.pad(wb, ((0, rp - rank), (0, 0))).astype(BF16))


def _rwkv_proj(batch, x, norm_g, mu, w_rkv, w0, w1, w2, a0, a1, a2, g1, g2):
    m, d = x.shape
    tm = _pick_tile(m // batch, 512, SUBLANE)
    tn = _pick_tile(d, 256, LANE)
    w1p, w2p = _pad_lora(w1, w2)
    a1p, a2p = _pad_lora(a1, a2)
    g1p, g2p = _pad_lora(g1, g2)
    rw, ra, rg = w1p.shape[1], a1p.shape[1], g1p.shape[1]
    l1 = jnp.concatenate([w1p, a1p, g1p], axis=1)
    w_bf = w_rkv.astype(BF16)
    mu_ord = mu[jnp.array([0, 2, 3, 1, 4, 5])]
    col = lambda i, j: (0, j)

    def w_plane(p):
        return pl.BlockSpec((None, d, tn), lambda i, j: (p, 0, j))

    planes = pl.BlockSpec((3, tm, tn), lambda i, j: (0, i, j))
    body = functools.partial(_rwkv_proj_body, (m // batch) // tm, (rw, ra))
    return pl.pallas_call(
        body,
        grid=(m // tm, d // tn),
        in_specs=[
            pl.BlockSpec((tm, d), lambda i, j: (i, 0)),
            pl.BlockSpec((SUBLANE, d), lambda i, j: (jnp.maximum(i * (tm // SUBLANE) - 1, 0), 0)),
            pl.BlockSpec((1, d), lambda i, j: (0, 0)),
            pl.BlockSpec((6, d), lambda i, j: (0, 0)),
            w_plane(0), w_plane(1), w_plane(2),
            pl.BlockSpec((d, rw + ra + rg), lambda i, j: (0, 0)),
            pl.BlockSpec((rw, tn), col),
            pl.BlockSpec((ra, tn), col),
            pl.BlockSpec((rg, tn), col),
            pl.BlockSpec((1, tn), col),
            pl.BlockSpec((1, tn), col),
        ],
        out_specs=[planes, planes],
        out_shape=[jax.ShapeDtypeStruct((3, m, d), F32), jax.ShapeDtypeStruct((3, m, d), F32)],
        scratch_shapes=[pltpu.VMEM((6, tm, d), BF16), pltpu.VMEM((tm, rw), BF16),
                        pltpu.VMEM((tm, ra), BF16), pltpu.VMEM((tm, rg), BF16)],
        compiler_params=_params(("parallel", "arbitrary")),
        name="rwkv_proj",
    )(x, x, norm_g.reshape(1, d), mu_ord, w_bf, w_bf, w_bf, l1, w2p, a2p, g2p,
      w0.reshape(1, d), a0.reshape(1, d))


def _bf(x):
    return x.astype(BF16)


def _dot(a, b):
    return jnp.dot(_bf(a), _bf(b), preferred_element_type=F32)


def _dot_nt(a, b):
    return lax.dot_general(_bf(a), _bf(b), (((1,), (1,)), ((), ())), preferred_element_type=F32)


def _dot_tn(a, b):
    return lax.dot_general(_bf(a), _bf(b), (((0,), (0,)), ((), ())), preferred_element_type=F32)


def _scan_body(n_streams, r_ref, ld_ref, k_ref, v_ref, al_ref, g_ref, kk_w_ref, ka_w_ref, rk_w_ref,
               lnw_ref, lnb_ref, o_ref, s_ref):
    c = SCAN_CHUNK
    w = HEADS_PER_GROUP * RWKV_HEAD
    wc = HEADS_PER_GROUP * c

    @pl.when(pl.program_id(2) == 0)
    def _():
        s_ref[...] = jnp.zeros_like(s_ref)

    row = lax.broadcasted_iota(jnp.int32, (c, w), 0)
    bd_r = lax.broadcasted_iota(jnp.int32, (w, w), 0) // RWKV_HEAD
    bd_c = lax.broadcasted_iota(jnp.int32, (w, w), 1) // RWKV_HEAD
    head_mask = bd_r == bd_c
    head_ones = jnp.where(head_mask, 1.0, 0.0).astype(BF16)
    t_idx = lax.broadcasted_iota(jnp.int32, (c, wc), 0)
    s_idx = lax.broadcasted_iota(jnp.int32, (c, wc), 1) % c
    strict = s_idx < t_idx
    incl = s_idx <= t_idx
    eye = jnp.where(s_idx == t_idx, 1.0, 0.0)

    def head_sum(x):
        hi = _bf(x)
        lo = _bf(x - hi.astype(F32))
        return (jnp.dot(hi, head_ones, preferred_element_type=F32)
                + jnp.dot(lo, head_ones, preferred_element_type=F32))

    def block_diag(x):
        xb = _bf(x)
        return jnp.where(head_mask, jnp.concatenate([xb] * HEADS_PER_GROUP, axis=0),
                         jnp.zeros((), BF16))

    def stream(si):
        ln = pl.ds(si * w, w)
        r = r_ref[:, ln]
        ld = ld_ref[:, ln]
        k = k_ref[:, ln]
        v = v_ref[:, ln]
        al = al_ref[:, ln]

        cum = ld
        shift = 1
        while shift < c:
            cum = cum + jnp.where(row >= shift, pltpu.roll(cum, shift, 0), 0.0)
            shift *= 2
        p_incl = jnp.exp(cum)
        p_prev = jnp.exp(cum - ld)
        p_inv = jnp.exp(-cum)
        p_last = p_incl[c - 1:c, :]

        kk = k * kk_w_ref[:, ln]
        ss = head_sum(kk * kk)
        yield
        kk = kk / jnp.maximum(jnp.sqrt(ss), 1e-12)
        kmod = k * (1.0 + (al - 1.0) * ka_w_ref[:, ln])

        a_t = -kk * p_prev
        b_t = kk * al * p_inv
        k_t = kmod * p_inv
        r_t = r * p_incl

        ar = jnp.concatenate([a_t, r_t], axis=0)
        m_b = _dot_nt(ar, block_diag(b_t))
        m_k = _dot_nt(ar, block_diag(k_t))
        yield
        a_ab = jnp.where(strict, m_b[:c], 0.0)
        a_rb = jnp.where(incl, m_b[c:], 0.0)
        a_ak = jnp.where(strict, m_k[:c], 0.0)
        a_rk = jnp.where(incl, m_k[c:], 0.0)

        n_bd = block_diag(a_ab)
        t_inv = eye
        level = 0
        while (1 << level) < c:
            tb = t_idx >> level
            sb = s_idx >> level
            sel = ((tb & 1) == 1) & (sb == tb - 1)
            tn = _dot(t_inv, n_bd)
            yield
            x = _dot(tn, block_diag(t_inv))
            yield
            t_inv = t_inv + jnp.where(sel, x, 0.0)
            level += 1

        s0 = s_ref[si]
        w1 = _dot_nt(ar, s0)
        w2 = _dot(jnp.concatenate([a_ak, a_rk], axis=0), block_diag(v))
        yield
        u = _dot(t_inv, block_diag(w1[:c] + w2[:c]))
        yield
        y = w1[c:] + w2[c:] + _dot(a_rb, block_diag(u))

        uv = jnp.concatenate([u, v], axis=0)
        bk = jnp.concatenate([b_t, k_t], axis=0) * p_last
        s_new = _dot_tn(uv, bk)
        yield
        s_ref[si] = jnp.where(head_mask, s0 * p_last + s_new, 0.0)

        inv_n = 1.0 / RWKV_HEAD
        mean = head_sum(y) * inv_n
        yield
        dlt = y - mean
        var = head_sum(dlt * dlt) * inv_n
        bsum = head_sum(r * kmod * rk_w_ref[:, ln])
        yield
        yn = dlt * lax.rsqrt(var + GN_EPS) * lnw_ref[:, ln] + lnb_ref[:, ln]
        bonus = bsum * v
        o_ref[:, ln] = ((yn + bonus) * g_ref[:, ln]).astype(o_ref.dtype)

    live = [stream(si) for si in range(n_streams)]
    while live:
        live = [g for g in live if next(g, True) is None]


def _rwkv_scan(batch, rkv, lag, k_k, k_a, r_k, ln_w, ln_b):
    _, m, d = rkv.shape
    t = m // batch
    c = SCAN_CHUNK
    w = HEADS_PER_GROUP * RWKV_HEAD
    assert d % w == 0 and t % c == 0 and c == RWKV_HEAD
    n_streams = _pick_tile(d // w, SCAN_STREAMS, 1)
    wb = n_streams * w
    nc = t // c
    tok = pl.BlockSpec((c, wb), lambda b, gi, ci: (b * nc + ci, gi))
    par = pl.BlockSpec((1, wb), lambda b, gi, ci: (0, gi))

    def plane(p):
        return pl.BlockSpec((None, c, wb), lambda b, gi, ci: (p, b * nc + ci, gi))

    return pl.pallas_call(
        functools.partial(_scan_body, n_streams),
        grid=(batch, d // wb, nc),
        in_specs=[plane(0), plane(0), plane(1), plane(2), plane(1), plane(2)] + [par] * 5,
        out_specs=tok,
        out_shape=jax.ShapeDtypeStruct((m, d), BF16),
        scratch_shapes=[pltpu.VMEM((n_streams, w, w), F32)],
        compiler_params=_params(("parallel", "parallel", "arbitrary")),
        name="rwkv_scan",
    )(rkv, lag, rkv, rkv, lag, lag, *[p.reshape(1, d) for p in (k_k, k_a, r_k, ln_w, ln_b)])


def _rwkv_mix(batch, x, norm_g, mu, w_rkv, w0, w1, w2, a0, a1, a2, g1, g2,
              k_k, k_a, r_k, ln_w, ln_b, w_o):
    rkv, lag = _rwkv_proj(batch, x, norm_g, mu, w_rkv, w0, w1, w2, a0, a1, a2, g1, g2)
    yg = _rwkv_scan(batch, rkv, lag, k_k, k_a, r_k.reshape(-1), ln_w, ln_b)
    return _simple_mm("rwkv_out", [_rows(yg)], lambda t: t, w_o.astype(BF16), [_epi_tile(x)],
                      lambda acc, x_t: (x_t + acc,))


def _attn_body(tq, tk, q_ref, k_ref, v_ref, o_ref):
    qi = pl.program_id(2)
    q = q_ref[...]
    dv = v_ref.shape[-1]

    def step(j, carry, masked):
        m_i, l_i, acc = carry
        start = pl.multiple_of(j * tk, tk)
        kb = k_ref[pl.ds(start, tk), :]
        vb = v_ref[pl.ds(start, tk), :]
        s = lax.dot_general(q, kb, (((1,), (1,)), ((), ())), preferred_element_type=F32)
        if masked:
            q_chunk = (qi * tq + lax.broadcasted_iota(jnp.int32, (tq, tk), 0)) // ATTN_CHUNK
            k_chunk = (j * tk + lax.broadcasted_iota(jnp.int32, (tq, tk), 1)) // ATTN_CHUNK
            s = jnp.where(k_chunk <= q_chunk, s, jnp.finfo(F32).min)
        m_new = jnp.maximum(m_i, jnp.max(s, axis=-1, keepdims=True))
        p = jnp.exp(s - m_new)
        alpha = jnp.exp(m_i - m_new)
        l_new = alpha * l_i + jnp.sum(p, axis=-1, keepdims=True)
        acc = alpha * acc + jnp.dot(p.astype(BF16), vb, preferred_element_type=F32)
        return m_new, l_new, acc

    init = (jnp.full((tq, 1), -jnp.inf, F32), jnp.zeros((tq, 1), F32), jnp.zeros((tq, dv), F32))
    carry = lax.fori_loop(0, qi, lambda j, cr: step(j, cr, False), init)
    _, l_i, acc = step(qi, carry, True)
    o_ref[...] = (acc / l_i).astype(o_ref.dtype)


def _attention(batch, q, k, v):
    heads, m, dqk = q.shape
    dv = v.shape[-1]
    t = m // batch
    tq = _pick_tile(t, 512, ATTN_CHUNK)
    nq = t // tq
    return pl.pallas_call(
        functools.partial(_attn_body, tq, tq),
        grid=(batch, heads, nq),
        in_specs=[
            pl.BlockSpec((None, tq, dqk), lambda b, h, i: (h, b * nq + i, 0)),
            pl.BlockSpec((None, t, dqk), lambda b, h, i: (h, b, 0)),
            pl.BlockSpec((None, t, dv), lambda b, h, i: (h, b, 0)),
        ],
        out_specs=pl.BlockSpec((tq, dv), lambda b, h, i: (b * nq + i, h)),
        out_shape=jax.ShapeDtypeStruct((m, heads * dv), BF16),
        compiler_params=_params(("parallel", "parallel", "arbitrary")),
        name="mla_attention",
    )(q, k, v)


def _mla_mix(batch, x, positions, norm_g, w_down, q_a_norm, kv_a_norm, w_uq, w_ukv,
             q_norm, k_norm, w_o):
    m, d = x.shape
    q_lora = q_a_norm.shape[0]
    kv_lora = kv_a_norm.shape[0]
    heads, qk_dim = w_uq.shape[1], w_uq.shape[2]
    rope = w_down.shape[1] - q_lora - kv_lora
    nope = qk_dim - rope
    dv = w_ukv.shape[2] - nope
    half = rope // 2
    assert nope == LANE and dv == LANE and rope <= LANE and q_lora == kv_lora and q_lora % LANE == 0
    scale = 1.0 / math.sqrt(qk_dim)

    perm = jnp.concatenate([jnp.arange(half, rope), jnp.arange(0, half)])
    lane_pad = lambda a: jnp.pad(a, [(0, 0)] * (a.ndim - 1) + [(0, LANE - a.shape[-1])])

    inv_freq = ROPE_THETA ** (-jnp.arange(0, rope, 2, dtype=F32) / rope)
    ang = positions.reshape(m, 1).astype(F32) * inv_freq
    cos, sin = jnp.cos(ang), jnp.sin(ang)
    cos_t = lane_pad(jnp.concatenate([cos, cos], axis=-1))
    sin_t = lane_pad(jnp.concatenate([-sin, sin], axis=-1))

    w_kpe = w_down[:, q_lora + kv_lora:]
    w_down_x = jnp.concatenate(
        [w_down[:, :q_lora + kv_lora], lane_pad(w_kpe), lane_pad(w_kpe[:, perm])], axis=1).astype(BF16)
    c = _simple_mm("mla_down", [_rows(x), _vec(norm_g)], _rms, w_down_x, [], lambda acc: (acc,),
                   tn_target=w_down_x.shape[1])
    qb = q_lora // q_lora
    tm = _pick_tile(m, 512, 8)

    def norm_rope(body, pe, rot, g_body, g_pe, g_rot, cos_b, sin_b, extra_scale):
        ss = jnp.sum(body * body, axis=-1, keepdims=True) + jnp.sum(pe * pe, axis=-1, keepdims=True)
        s = lax.rsqrt(ss / qk_dim + RMS_EPS) * extra_scale
        return jnp.concatenate([body * s * g_body, (pe * g_pe * cos_b + rot * g_rot * sin_b) * s], axis=-1)

    w_q_pe = w_uq[:, :, nope:]
    w_q_x = jnp.concatenate([w_uq[:, :, :nope], lane_pad(w_q_pe), lane_pad(w_q_pe[:, :, perm])],
                            axis=-1).reshape(q_lora, heads * 3 * LANE).astype(BF16)
    gq = jnp.concatenate([q_norm[:nope], lane_pad(q_norm[nope:]), lane_pad(q_norm[nope:][perm])]).reshape(1, -1)

    hb = _pick_tile(heads, MLA_HEADS_PER_STEP, 1)

    def q_epi(acc, g_t, cos_b, sin_b):
        per_head = []
        for hh in range(hb):
            a = acc[:, hh * 3 * LANE:(hh + 1) * 3 * LANE]
            per_head.append(norm_rope(a[:, :LANE], a[:, LANE:2 * LANE], a[:, 2 * LANE:], g_t[:, :LANE],
                                      g_t[:, LANE:2 * LANE], g_t[:, 2 * LANE:], cos_b, sin_b, scale))
        return (jnp.stack(per_head, axis=0),)

    q = _mm("mla_q", m,
            [(c, _row_spec(tm, q_lora, 0)), (q_a_norm.reshape(1, -1), _vec_spec(q_lora))], _rms, w_q_x,
            [(gq, _vec_spec(3 * LANE)), (cos_t, _row_spec(tm, LANE)), (sin_t, _row_spec(tm, LANE))], q_epi,
            [(jax.ShapeDtypeStruct((heads, m, 2 * LANE), BF16),
              pl.BlockSpec((hb, tm, 2 * LANE), lambda i, j: (j, i, 0)))],
            tm=tm, tn=hb * 3 * LANE)[0]

    w_kv_x = w_ukv.reshape(kv_lora, heads * (nope + dv)).astype(BF16)
    gk = jnp.concatenate([k_norm[:nope], lane_pad(k_norm[nope:]), lane_pad(k_norm[nope:][perm])]).reshape(1, -1)
    pe_block = (q_lora + kv_lora) // LANE

    def kv_epi(acc, g_t, pe, rot, cos_b, sin_b):
        k_out, v_out = [], []
        for hh in range(hb):
            a = acc[:, hh * (nope + dv):(hh + 1) * (nope + dv)]
            k_out.append(norm_rope(a[:, :nope], pe, rot, g_t[:, :LANE], g_t[:, LANE:2 * LANE],
                                   g_t[:, 2 * LANE:], cos_b, sin_b, 1.0))
            v_out.append(a[:, nope:])
        return jnp.stack(k_out, axis=0), jnp.stack(v_out, axis=0)

    k, v = _mm("mla_kv", m,
               [(c, _row_spec(tm, kv_lora, qb)), (kv_a_norm.reshape(1, -1), _vec_spec(kv_lora))], _rms, w_kv_x,
               [(gk, _vec_spec(3 * LANE)), (c, _row_spec(tm, LANE, pe_block)),
                (c, _row_spec(tm, LANE, pe_block + 1)), (cos_t, _row_spec(tm, LANE)),
                (sin_t, _row_spec(tm, LANE))], kv_epi,
               [(jax.ShapeDtypeStruct((heads, m, 2 * LANE), BF16),
                 pl.BlockSpec((hb, tm, 2 * LANE), lambda i, j: (j, i, 0))),
                (jax.ShapeDtypeStruct((heads, m, dv), BF16),
                 pl.BlockSpec((hb, tm, dv), lambda i, j: (j, i, 0)))],
               tm=tm, tn=hb * (nope + dv))

    o = _attention(batch, q, k, v)
    return _simple_mm("mla_out", [_rows(o)], lambda t: t, w_o.reshape(heads * dv, d).astype(BF16),
                      [_epi_tile(x)], lambda acc, x_t: (x_t + acc,))


def kernel(x, positions, ffn_norm, ffn_w13, ffn_w2, mix_norm, rwkv_mu, rwkv_w_rkv, rwkv_w0, rwkv_w1, rwkv_w2, rwkv_a0, rwkv_a1, rwkv_a2, rwkv_g1, rwkv_g2, rwkv_k_k, rwkv_k_a, rwkv_r_k, rwkv_ln_w, rwkv_ln_b, rwkv_w_o, mla_w_down, mla_q_a_norm, mla_kv_a_norm, mla_w_uq, mla_w_ukv, mla_q_norm, mla_k_norm, mla_w_o):
    batch, seq, d = x.shape
    depth = ffn_norm.shape[0]
    n_mixers = 2
    xf = x.reshape(batch * seq, d)
    w13p, w2p = _ffn_weights(ffn_w13.reshape((2 * depth,) + ffn_w13.shape[2:]),
                             ffn_w2.reshape((2 * depth,) + ffn_w2.shape[2:]))
    for i in range(depth):
        j = i // n_mixers
        xf = _ffn(xf, ffn_norm[i, 0], w13p[2 * i], w2p[2 * i])
        if i % n_mixers == 0:
            xf = _rwkv_mix(batch, xf, mix_norm[i], rwkv_mu[j], rwkv_w_rkv[j], rwkv_w0[j], rwkv_w1[j],
                           rwkv_w2[j], rwkv_a0[j], rwkv_a1[j], rwkv_a2[j], rwkv_g1[j], rwkv_g2[j],
                           rwkv_k_k[j], rwkv_k_a[j], rwkv_r_k[j], rwkv_ln_w[j], rwkv_ln_b[j],
                           rwkv_w_o[j])
        else:
            xf = _mla_mix(batch, xf, positions, mix_norm[i], mla_w_down[j], mla_q_a_norm[j],
                          mla_kv_a_norm[j], mla_w_uq[j], mla_w_ukv[j], mla_q_norm[j], mla_k_norm[j],
                          mla_w_o[j])
        xf = _ffn(xf, ffn_norm[i, 1], w13p[2 * i + 1], w2p[2 * i + 1])
    return xf.reshape(batch, seq, d)
```

```python
import functools
import math

import jax
import jax.numpy as jnp
from jax import lax
from jax.experimental import pallas as pl
from jax.experimental.pallas import tpu as pltpu

F32 = jnp.float32
BF16 = jnp.bfloat16

RMS_EPS = 1e-6
GN_EPS = 64e-5
ROPE_THETA = 10000.0
ATTN_CHUNK = 64
RWKV_HEAD = 64
SCAN_CHUNK = 64
HEADS_PER_GROUP = 4
ATTN_Q_TILE = 512
ATTN_HEADS_PER_STEP = 2
MLA_HEADS_PER_STEP = 4
SCAN_STREAMS = 8
LANE = 128
SUBLANE = 8
VMEM_LIMIT_BYTES = 56 * 1024 * 1024


def _params(semantics):
    return pltpu.CompilerParams(dimension_semantics=semantics,
                                vmem_limit_bytes=VMEM_LIMIT_BYTES)


def _pick_tile(n, target, quantum):
    if n <= target:
        return n
    t = (target // quantum) * quantum
    while t > quantum and n % t:
        t -= quantum
    assert n % t == 0, (n, target, quantum)
    return t


def _round_robin(gens):
    live = list(gens)
    while live:
        live = [g for g in live if next(g, True) is None]


def _rms(x, g):
    return x * lax.rsqrt(jnp.mean(x * x, axis=-1, keepdims=True) + RMS_EPS) * g


def _ffn_body(x_ref, g_ref, wg_ref, wu_ref, w2_ref, o_ref, xn_ref, acc_ref):
    j = pl.program_id(1)

    @pl.when(j == 0)
    def _():
        xn_ref[...] = _rms(x_ref[...], g_ref[...]).astype(BF16)
        acc_ref[...] = jnp.zeros_like(acc_ref)

    xn = xn_ref[...]
    gate = jnp.dot(xn, wg_ref[...], preferred_element_type=F32)
    up = jnp.dot(xn, wu_ref[...], preferred_element_type=F32)
    act = (gate * jax.nn.sigmoid(gate) * up).astype(BF16)
    acc_ref[...] += jnp.dot(act, w2_ref[...], preferred_element_type=F32)

    @pl.when(j == pl.num_programs(1) - 1)
    def _():
        o_ref[...] = x_ref[...] + 0.5 * acc_ref[...]


def _ffn_tile(f):
    return 512 if f > 512 else LANE


def _cast_w13_body(f, fp, x_ref, o_ref):
    o_ref[:, :f] = x_ref[:, :f].astype(BF16)
    o_ref[:, fp:fp + f] = x_ref[:, f:].astype(BF16)
    if fp > f:
        zeros = jnp.zeros((o_ref.shape[0], fp - f), BF16)
        o_ref[:, f:fp] = zeros
        o_ref[:, fp + f:] = zeros


def _cast_w2_body(f, x_ref, o_ref):
    tr = o_ref.shape[0]
    row = pl.program_id(1) * tr + lax.broadcasted_iota(jnp.int32, o_ref.shape, 0)
    o_ref[...] = jnp.where(row < f, x_ref[...], 0.0).astype(BF16)


def _ffn_weights(w13, w2):
    n, d, f2 = w13.shape
    f = f2 // 2
    assert f % LANE == 0
    tf = _ffn_tile(f)
    fp = -(-f // tf) * tf
    td = _pick_tile(d, 128, SUBLANE)
    w13p = pl.pallas_call(
        functools.partial(_cast_w13_body, f, fp),
        grid=(n, d // td),
        in_specs=[pl.BlockSpec((None, td, 2 * f), lambda l, i: (l, i, 0))],
        out_specs=pl.BlockSpec((None, td, 2 * fp), lambda l, i: (l, i, 0)),
        out_shape=jax.ShapeDtypeStruct((n, d, 2 * fp), BF16),
        compiler_params=_params(("parallel", "parallel")),
        name="cast_w13",
    )(w13)
    w2p = pl.pallas_call(
        functools.partial(_cast_w2_body, f),
        grid=(n, fp // tf),
        in_specs=[pl.BlockSpec((None, tf, d), lambda l, i: (l, i, 0))],
        out_specs=pl.BlockSpec((None, tf, d), lambda l, i: (l, i, 0)),
        out_shape=jax.ShapeDtypeStruct((n, fp, d), BF16),
        compiler_params=_params(("parallel", "parallel")),
        name="cast_w2",
    )(w2)
    return w13p, w2p


def _ffn(x, g, w13p, w2p):
    m, d = x.shape
    fp = w2p.shape[0]
    tf = _ffn_tile(fp)
    nj = fp // tf
    tm = _pick_tile(m, 512, 8)
    return pl.pallas_call(
        _ffn_body,
        grid=(m // tm, nj),
        in_specs=[
            pl.BlockSpec((tm, d), lambda i, j: (i, 0)),
            pl.BlockSpec((1, d), lambda i, j: (0, 0)),
            pl.BlockSpec((d, tf), lambda i, j: (0, j)),
            pl.BlockSpec((d, tf), lambda i, j: (0, j + nj)),
            pl.BlockSpec((tf, d), lambda i, j: (j, 0)),
        ],
        out_specs=pl.BlockSpec((tm, d), lambda i, j: (i, 0)),
        out_shape=jax.ShapeDtypeStruct((m, d), F32),
        scratch_shapes=[pltpu.VMEM((tm, d), BF16), pltpu.VMEM((tm, d), F32)],
        compiler_params=_params(("parallel", "arbitrary")),
        name="ffn",
    )(x, g.reshape(1, d), w13p, w13p, w2p)


def _mm_body(n_lhs, n_epi, n_out, lhs_fn, epi_fn, *refs):
    lhs_refs = refs[:n_lhs]
    w_ref = refs[n_lhs]
    epi_refs = refs[n_lhs + 1:n_lhs + 1 + n_epi]
    out_refs = refs[n_lhs + 1 + n_epi:n_lhs + 1 + n_epi + n_out]
    lhs_scratch = refs[-1]

    @pl.when(pl.program_id(1) == 0)
    def _():
        lhs_scratch[...] = lhs_fn(*[r[...] for r in lhs_refs]).astype(BF16)

    acc = jnp.dot(lhs_scratch[...], w_ref[...], preferred_element_type=F32)
    outs = epi_fn(acc, *[r[...] for r in epi_refs])
    for o_ref, o in zip(out_refs, outs):
        o_ref[...] = o.astype(o_ref.dtype)


def _mm(name, m, lhs, lhs_fn, w, epi, epi_fn, outs, *, tm, tn):
    k, n = w.shape
    body = functools.partial(_mm_body, len(lhs), len(epi), len(outs), lhs_fn, epi_fn)
    return pl.pallas_call(
        body,
        grid=(m // tm, n // tn),
        in_specs=[s for _, s in lhs] + [pl.BlockSpec((k, tn), lambda i, j: (0, j))]
        + [s for _, s in epi],
        out_specs=[s for _, s in outs],
        out_shape=[s for s, _ in outs],
        scratch_shapes=[pltpu.VMEM((tm, k), BF16)],
        compiler_params=_params(("parallel", "arbitrary")),
        name=name,
    )(*[a for a, _ in lhs], w, *[a for a, _ in epi])


def _row_spec(tm, width, col_block=0):
    return pl.BlockSpec((tm, width), lambda i, j: (i, col_block))


def _tile_spec(tm, tn):
    return pl.BlockSpec((tm, tn), lambda i, j: (i, j))


def _vec_spec(width):
    return pl.BlockSpec((1, width), lambda i, j: (0, 0))


def _colvec_spec(tn):
    return pl.BlockSpec((1, tn), lambda i, j: (0, j))


def _simple_mm(name, lhs, lhs_fn, w, epi, epi_fn, out_dtype=F32, tm_target=512, tn_target=1024):
    m = lhs[0][0].shape[0]
    n = w.shape[1]
    tm = _pick_tile(m, tm_target, 8)
    tn = _pick_tile(n, tn_target, LANE)
    lhs = [(a, spec(tm)) for a, spec in lhs]
    epi = [(a, spec(tm, tn)) for a, spec in epi]
    out = (jax.ShapeDtypeStruct((m, n), out_dtype), _tile_spec(tm, tn))
    return _mm(name, m, lhs, lhs_fn, w, epi, epi_fn, [out], tm=tm, tn=tn)[0]


def _rows(a):
    return a, lambda tm: _row_spec(tm, a.shape[1])


def _vec(a):
    a = a.reshape(1, -1)
    return a, lambda tm: _vec_spec(a.shape[1])


def _epi_tile(a):
    return a, lambda tm, tn: _tile_spec(tm, tn)


def _epi_colvec(a):
    return a.reshape(1, -1), lambda tm, tn: _colvec_spec(tn)


def _rwkv_proj_body(seq_tiles, ranks, x_ref, xprev_ref, g_ref, mu_ref, wr_ref, wk_ref, wv_ref, l1_ref,
                    w2_ref, a2_ref, g2_ref, w0_ref, a0_ref, rkv_ref, lag_ref,
                    lerp_scr, tw_scr, ta_scr, tg_scr):
    i = pl.program_id(0)
    rw, ra = ranks

    @pl.when(pl.program_id(1) == 0)
    def _():
        h = _rms(x_ref[...], g_ref[...])
        prev = _rms(xprev_ref[...], g_ref[...])[SUBLANE - 1:SUBLANE, :]
        prev = jnp.where(i % seq_tiles == 0, 0.0, prev)
        row = lax.broadcasted_iota(jnp.int32, h.shape, 0)
        dx = jnp.where(row == 0, prev, pltpu.roll(h, 1, 0)) - h
        for idx in range(6):
            lerp_scr[idx] = (h + dx * mu_ref[idx:idx + 1, :]).astype(BF16)
        tw_scr[...] = jnp.tanh(jnp.dot(lerp_scr[3], l1_ref[:, :rw],
                                       preferred_element_type=F32)).astype(BF16)
        ta_scr[...] = jnp.dot(lerp_scr[4], l1_ref[:, rw:rw + ra],
                              preferred_element_type=F32).astype(BF16)
        tg_scr[...] = jax.nn.sigmoid(jnp.dot(lerp_scr[5], l1_ref[:, rw + ra:],
                                             preferred_element_type=F32)).astype(BF16)

    rkv_ref[0] = jnp.dot(lerp_scr[0], wr_ref[...], preferred_element_type=F32)
    rkv_ref[1] = jnp.dot(lerp_scr[1], wk_ref[...], preferred_element_type=F32)
    rkv_ref[2] = jnp.dot(lerp_scr[2], wv_ref[...], preferred_element_type=F32)
    z = -(w0_ref[...] + jnp.dot(tw_scr[...], w2_ref[...], preferred_element_type=F32))
    softplus = jnp.maximum(z, 0.0) + jnp.log1p(jnp.exp(-jnp.abs(z)))
    lag_ref[0] = -jnp.exp(-softplus - 0.5)
    lag_ref[1] = jax.nn.sigmoid(
        a0_ref[...] + jnp.dot(ta_scr[...], a2_ref[...], preferred_element_type=F32))
    lag_ref[2] = jnp.dot(tg_scr[...], g2_ref[...], preferred_element_type=F32)


def _pad_lora(wa, wb):
    rank = wa.shape[1]
    rp = -(-rank // LANE) * LANE
    return (jnp.pad(wa, ((0, 0), (0, rp - rank))).astype(BF16),
            jnp.pad(wb, ((0, rp - rank), (0, 0))).astype(BF16))


def _rwkv_proj(batch, x, norm_g, mu, w_rkv, w0, w1, w2, a0, a1, a2, g1, g2):
    m, d = x.shape
    tm = _pick_tile(m // batch, 512, SUBLANE)
    tn = _pick_tile(d, 256, LANE)
    w1p, w2p = _pad_lora(w1, w2)
    a1p, a2p = _pad_lora(a1, a2)
    g1p, g2p = _pad_lora(g1, g2)
    rw, ra, rg = w1p.shape[1], a1p.shape[1], g1p.shape[1]
    l1 = jnp.concatenate([w1p, a1p, g1p], axis=1)
    w_bf = w_rkv.astype(BF16)
    mu_ord = mu[jnp.array([0, 2, 3, 1, 4, 5])]
    col = lambda i, j: (0, j)

    def w_plane(p):
        return pl.BlockSpec((None, d, tn), lambda i, j: (p, 0, j))

    planes = pl.BlockSpec((3, tm, tn), lambda i, j: (0, i, j))
    body = functools.partial(_rwkv_proj_body, (m // batch) // tm, (rw, ra))
    return pl.pallas_call(
        body,
        grid=(m // tm, d // tn),
        in_specs=[
            pl.BlockSpec((tm, d), lambda i, j: (i, 0)),
            pl.BlockSpec((SUBLANE, d), lambda i, j: (jnp.maximum(i * (tm // SUBLANE) - 1, 0), 0)),
            pl.BlockSpec((1, d), lambda i, j: (0, 0)),
            pl.BlockSpec((6, d), lambda i, j: (0, 0)),
            w_plane(0), w_plane(1), w_plane(2),
            pl.BlockSpec((d, rw + ra + rg), lambda i, j: (0, 0)),
            pl.BlockSpec((rw, tn), col),
            pl.BlockSpec((ra, tn), col),
            pl.BlockSpec((rg, tn), col),
            pl.BlockSpec((1, tn), col),
            pl.BlockSpec((1, tn), col),
        ],
        out_specs=[planes, planes],
        out_shape=[jax.ShapeDtypeStruct((3, m, d), F32), jax.ShapeDtypeStruct((3, m, d), F32)],
        scratch_shapes=[pltpu.VMEM((6, tm, d), BF16), pltpu.VMEM((tm, rw), BF16),
                        pltpu.VMEM((tm, ra), BF16), pltpu.VMEM((tm, rg), BF16)],
        compiler_params=_params(("parallel", "arbitrary")),
        name="rwkv_proj",
    )(x, x, norm_g.reshape(1, d), mu_ord, w_bf, w_bf, w_bf, l1, w2p, a2p, g2p,
      w0.reshape(1, d), a0.reshape(1, d))


def _bf(x):
    return x.astype(BF16)


def _dot(a, b):
    return jnp.dot(_bf(a), _bf(b), preferred_element_type=F32)


def _dot_nt(a, b):
    return lax.dot_general(_bf(a), _bf(b), (((1,), (1,)), ((), ())), preferred_element_type=F32)


def _dot_tn(a, b):
    return lax.dot_general(_bf(a), _bf(b), (((0,), (0,)), ((), ())), preferred_element_type=F32)


def _scan_body(n_streams, r_ref, ld_ref, k_ref, v_ref, al_ref, g_ref, kk_w_ref, ka_w_ref, rk_w_ref,
               lnw_ref, lnb_ref, o_ref, s_ref):
    c = SCAN_CHUNK
    w = HEADS_PER_GROUP * RWKV_HEAD
    wc = HEADS_PER_GROUP * c

    @pl.when(pl.program_id(2) == 0)
    def _():
        s_ref[...] = jnp.zeros_like(s_ref)

    row = lax.broadcasted_iota(jnp.int32, (c, w), 0)
    bd_r = lax.broadcasted_iota(jnp.int32, (w, w), 0) // RWKV_HEAD
    bd_c = lax.broadcasted_iota(jnp.int32, (w, w), 1) // RWKV_HEAD
    head_mask = bd_r == bd_c
    head_ones = jnp.where(head_mask, 1.0, 0.0).astype(BF16)
    t_idx = lax.broadcasted_iota(jnp.int32, (c, wc), 0)
    s_idx = lax.broadcasted_iota(jnp.int32, (c, wc), 1) % c
    strict = s_idx < t_idx
    incl = s_idx <= t_idx
    eye = jnp.where(s_idx == t_idx, 1.0, 0.0)

    def head_sum(x):
        hi = _bf(x)
        lo = _bf(x - hi.astype(F32))
        return (jnp.dot(hi, head_ones, preferred_element_type=F32)
                + jnp.dot(lo, head_ones, preferred_element_type=F32))

    def block_diag(x):
        xb = _bf(x)
        return jnp.where(head_mask, jnp.concatenate([xb] * HEADS_PER_GROUP, axis=0),
                         jnp.zeros((), BF16))

    def stream(si):
        ln = pl.ds(si * w, w)
        r = r_ref[:, ln]
        ld = ld_ref[:, ln]
        k = k_ref[:, ln]
        v = v_ref[:, ln]
        al = al_ref[:, ln]

        cum = ld
        shift = 1
        while shift < c:
            cum = cum + jnp.where(row >= shift, pltpu.roll(cum, shift, 0), 0.0)
            shift *= 2
        p_incl = jnp.exp(cum)
        p_prev = jnp.exp(cum - ld)
        p_inv = jnp.exp(-cum)
        p_last = p_incl[c - 1:c, :]

        kk = k * kk_w_ref[:, ln]
        ss = head_sum(kk * kk)
        yield
        kk = kk / jnp.maximum(jnp.sqrt(ss), 1e-12)
        kmod = k * (1.0 + (al - 1.0) * ka_w_ref[:, ln])

        a_t = -kk * p_prev
        b_t = kk * al * p_inv
        k_t = kmod * p_inv
        r_t = r * p_incl

        ar = jnp.concatenate([a_t, r_t], axis=0)
        m_b = _dot_nt(ar, block_diag(b_t))
        m_k = _dot_nt(ar, block_diag(k_t))
        yield
        a_ab = jnp.where(strict, m_b[:c], 0.0)
        a_rb = jnp.where(incl, m_b[c:], 0.0)
        a_ak = jnp.where(strict, m_k[:c], 0.0)
        a_rk = jnp.where(incl, m_k[c:], 0.0)

        n_bd = block_diag(a_ab)
        t_inv = eye
        level = 0
        while (1 << level) < c:
            tb = t_idx >> level
            sb = s_idx >> level
            sel = ((tb & 1) == 1) & (sb == tb - 1)
            tn = _dot(t_inv, n_bd)
            yield
            x = _dot(tn, block_diag(t_inv))
            yield
            t_inv = t_inv + jnp.where(sel, x, 0.0)
            level += 1

        s0 = s_ref[si]
        w1 = _dot_nt(ar, s0)
        w2 = _dot(jnp.concatenate([a_ak, a_rk], axis=0), block_diag(v))
        yield
        u = _dot(t_inv, block_diag(w1[:c] + w2[:c]))
        yield
        y = w1[c:] + w2[c:] + _dot(a_rb, block_diag(u))

        uv = jnp.concatenate([u, v], axis=0)
        bk = jnp.concatenate([b_t, k_t], axis=0) * p_last
        s_new = _dot_tn(uv, bk)
        yield
        s_ref[si] = jnp.where(head_mask, s0 * p_last + s_new, 0.0)

        inv_n = 1.0 / RWKV_HEAD
        mean = head_sum(y) * inv_n
        yield
        dlt = y - mean
        var = head_sum(dlt * dlt) * inv_n
        bsum = head_sum(r * kmod * rk_w_ref[:, ln])
        yield
        yn = dlt * lax.rsqrt(var + GN_EPS) * lnw_ref[:, ln] + lnb_ref[:, ln]
        bonus = bsum * v
        o_ref[:, ln] = ((yn + bonus) * g_ref[:, ln]).astype(o_ref.dtype)

    _round_robin([stream(si) for si in range(n_streams)])


def _rwkv_scan(batch, rkv, lag, k_k, k_a, r_k, ln_w, ln_b):
    _, m, d = rkv.shape
    t = m // batch
    c = SCAN_CHUNK
    w = HEADS_PER_GROUP * RWKV_HEAD
    assert d % w == 0 and t % c == 0 and c == RWKV_HEAD
    n_streams = _pick_tile(d // w, SCAN_STREAMS, 1)
    wb = n_streams * w
    nc = t // c
    tok = pl.BlockSpec((c, wb), lambda b, gi, ci: (b * nc + ci, gi))
    par = pl.BlockSpec((1, wb), lambda b, gi, ci: (0, gi))

    def plane(p):
        return pl.BlockSpec((None, c, wb), lambda b, gi, ci: (p, b * nc + ci, gi))

    return pl.pallas_call(
        functools.partial(_scan_body, n_streams),
        grid=(batch, d // wb, nc),
        in_specs=[plane(0), plane(0), plane(1), plane(2), plane(1), plane(2)] + [par] * 5,
        out_specs=tok,
        out_shape=jax.ShapeDtypeStruct((m, d), BF16),
        scratch_shapes=[pltpu.VMEM((n_streams, w, w), F32)],
        compiler_params=_params(("parallel", "parallel", "arbitrary")),
        name="rwkv_scan",
    )(rkv, lag, rkv, rkv, lag, lag, *[p.reshape(1, d) for p in (k_k, k_a, r_k, ln_w, ln_b)])


def _rwkv_mix(batch, x, norm_g, mu, w_rkv, w0, w1, w2, a0, a1, a2, g1, g2,
              k_k, k_a, r_k, ln_w, ln_b, w_o):
    rkv, lag = _rwkv_proj(batch, x, norm_g, mu, w_rkv, w0, w1, w2, a0, a1, a2, g1, g2)
    yg = _rwkv_scan(batch, rkv, lag, k_k, k_a, r_k.reshape(-1), ln_w, ln_b)
    return _simple_mm("rwkv_out", [_rows(yg)], lambda t: t, w_o.astype(BF16), [_epi_tile(x)],
                      lambda acc, x_t: (x_t + acc,))


def _attn_body(tq, tk, nh, q_ref, k_ref, v_ref, o_ref):
    qi = pl.program_id(2)
    dv = v_ref.shape[-1]

    def head_step(h, j, carry, masked, out):
        m_i, l_i, acc = carry
        start = pl.multiple_of(j * tk, tk)
        kb = k_ref[h, pl.ds(start, tk), :]
        s = lax.dot_general(q_ref[h], kb, (((1,), (1,)), ((), ())), preferred_element_type=F32)
        yield
        if masked:
            q_chunk = (qi * tq + lax.broadcasted_iota(jnp.int32, (tq, tk), 0)) // ATTN_CHUNK
            k_chunk = (j * tk + lax.broadcasted_iota(jnp.int32, (tq, tk), 1)) // ATTN_CHUNK
            s = jnp.where(k_chunk <= q_chunk, s, jnp.finfo(F32).min)
        m_new = jnp.maximum(m_i, jnp.max(s, axis=-1, keepdims=True))
        p = jnp.exp(s - m_new)
        alpha = jnp.exp(m_i - m_new)
        l_new = alpha * l_i + jnp.sum(p, axis=-1, keepdims=True)
        pv = jnp.dot(p.astype(BF16), v_ref[h, pl.ds(start, tk), :], preferred_element_type=F32)
        yield
        out.append((m_new, l_new, alpha * acc + pv))

    def step(j, carries, masked):
        outs = [[] for _ in range(nh)]
        _round_robin([head_step(h, j, carries[h], masked, outs[h]) for h in range(nh)])
        return tuple(o[0] for o in outs)

    init = tuple((jnp.full((tq, 1), -jnp.inf, F32), jnp.zeros((tq, 1), F32),
                  jnp.zeros((tq, dv), F32)) for _ in range(nh))
    carries = lax.fori_loop(0, qi, lambda j, cr: step(j, cr, False), init)
    carries = step(qi, carries, True)
    for h, (_, l_i, acc) in enumerate(carries):
        o_ref[:, h * dv:(h + 1) * dv] = (acc / l_i).astype(o_ref.dtype)


def _attention(batch, q, k, v):
    heads, m, dqk = q.shape
    dv = v.shape[-1]
    t = m // batch
    tq = _pick_tile(t, ATTN_Q_TILE, ATTN_CHUNK)
    nq = t // tq
    nh = _pick_tile(heads, ATTN_HEADS_PER_STEP, 1)
    return pl.pallas_call(
        functools.partial(_attn_body, tq, tq, nh),
        grid=(batch, heads // nh, nq),
        in_specs=[
            pl.BlockSpec((nh, tq, dqk), lambda b, h, i: (h, b * nq + i, 0)),
            pl.BlockSpec((nh, t, dqk), lambda b, h, i: (h, b, 0)),
            pl.BlockSpec((nh, t, dv), lambda b, h, i: (h, b, 0)),
        ],
        out_specs=pl.BlockSpec((tq, nh * dv), lambda b, h, i: (b * nq + i, h)),
        out_shape=jax.ShapeDtypeStruct((m, heads * dv), BF16),
        compiler_params=_params(("parallel", "parallel", "arbitrary")),
        name="mla_attention",
    )(q, k, v)


def _mla_mix(batch, x, positions, norm_g, w_down, q_a_norm, kv_a_norm, w_uq, w_ukv,
             q_norm, k_norm, w_o):
    m, d = x.shape
    q_lora = q_a_norm.shape[0]
    kv_lora = kv_a_norm.shape[0]
    heads, qk_dim = w_uq.shape[1], w_uq.shape[2]
    rope = w_down.shape[1] - q_lora - kv_lora
    nope = qk_dim - rope
    dv = w_ukv.shape[2] - nope
    half = rope // 2
    assert nope == LANE and dv == LANE and rope <= LANE and q_lora == kv_lora and q_lora % LANE == 0
    scale = 1.0 / math.sqrt(qk_dim)

    perm = jnp.concatenate([jnp.arange(half, rope), jnp.arange(0, half)])
    lane_pad = lambda a: jnp.pad(a, [(0, 0)] * (a.ndim - 1) + [(0, LANE - a.shape[-1])])

    inv_freq = ROPE_THETA ** (-jnp.arange(0, rope, 2, dtype=F32) / rope)
    ang = positions.reshape(m, 1).astype(F32) * inv_freq
    cos, sin = jnp.cos(ang), jnp.sin(ang)
    cos_t = lane_pad(jnp.concatenate([cos, cos], axis=-1))
    sin_t = lane_pad(jnp.concatenate([-sin, sin], axis=-1))

    w_kpe = w_down[:, q_lora + kv_lora:]
    w_down_x = jnp.concatenate(
        [w_down[:, :q_lora + kv_lora], lane_pad(w_kpe), lane_pad(w_kpe[:, perm])], axis=1).astype(BF16)
    c = _simple_mm("mla_down", [_rows(x), _vec(norm_g)], _rms, w_down_x, [], lambda acc: (acc,),
                   tn_target=w_down_x.shape[1])
    qb = q_lora // q_lora
    tm = _pick_tile(m, 512, 8)

    def norm_rope(body, pe, rot, g_body, g_pe, g_rot, cos_b, sin_b, extra_scale):
        ss = jnp.sum(body * body, axis=-1, keepdims=True) + jnp.sum(pe * pe, axis=-1, keepdims=True)
        s = lax.rsqrt(ss / qk_dim + RMS_EPS) * extra_scale
        return jnp.concatenate([body * s * g_body, (pe * g_pe * cos_b + rot * g_rot * sin_b) * s], axis=-1)

    w_q_pe = w_uq[:, :, nope:]
    w_q_x = jnp.concatenate([w_uq[:, :, :nope], lane_pad(w_q_pe), lane_pad(w_q_pe[:, :, perm])],
                            axis=-1).reshape(q_lora, heads * 3 * LANE).astype(BF16)
    gq = jnp.concatenate([q_norm[:nope], lane_pad(q_norm[nope:]), lane_pad(q_norm[nope:][perm])]).reshape(1, -1)

    hb = _pick_tile(heads, MLA_HEADS_PER_STEP, 1)

    def q_epi(acc, g_t, cos_b, sin_b):
        per_head = []
        for hh in range(hb):
            a = acc[:, hh * 3 * LANE:(hh + 1) * 3 * LANE]
            per_head.append(norm_rope(a[:, :LANE], a[:, LANE:2 * LANE], a[:, 2 * LANE:], g_t[:, :LANE],
                                      g_t[:, LANE:2 * LANE], g_t[:, 2 * LANE:], cos_b, sin_b, scale))
        return (jnp.stack(per_head, axis=0),)

    q = _mm("mla_q", m,
            [(c, _row_spec(tm, q_lora, 0)), (q_a_norm.reshape(1, -1), _vec_spec(q_lora))], _rms, w_q_x,
            [(gq, _vec_spec(3 * LANE)), (cos_t, _row_spec(tm, LANE)), (sin_t, _row_spec(tm, LANE))], q_epi,
            [(jax.ShapeDtypeStruct((heads, m, 2 * LANE), BF16),
              pl.BlockSpec((hb, tm, 2 * LANE), lambda i, j: (j, i, 0)))],
            tm=tm, tn=hb * 3 * LANE)[0]

    w_kv_x = w_ukv.reshape(kv_lora, heads * (nope + dv)).astype(BF16)
    gk = jnp.concatenate([k_norm[:nope], lane_pad(k_norm[nope:]), lane_pad(k_norm[nope:][perm])]).reshape(1, -1)
    pe_block = (q_lora + kv_lora) // LANE

    def kv_epi(acc, g_t, pe, rot, cos_b, sin_b):
        k_out, v_out = [], []
        for hh in range(hb):
            a = acc[:, hh * (nope + dv):(hh + 1) * (nope + dv)]
            k_out.append(norm_rope(a[:, :nope], pe, rot, g_t[:, :LANE], g_t[:, LANE:2 * LANE],
                                   g_t[:, 2 * LANE:], cos_b, sin_b, 1.0))
            v_out.append(a[:, nope:])
        return jnp.stack(k_out, axis=0), jnp.stack(v_out, axis=0)

    k, v = _mm("mla_kv", m,
               [(c, _row_spec(tm, kv_lora, qb)), (kv_a_norm.reshape(1, -1), _vec_spec(kv_lora))], _rms, w_kv_x,
               [(gk, _vec_spec(3 * LANE)), (c, _row_spec(tm, LANE, pe_block)),
                (c, _row_spec(tm, LANE, pe_block + 1)), (cos_t, _row_spec(tm, LANE)),
                (sin_t, _row_spec(tm, LANE))], kv_epi,
               [(jax.ShapeDtypeStruct((heads, m, 2 * LANE), BF16),
                 pl.BlockSpec((hb, tm, 2 * LANE), lambda i, j: (j, i, 0))),
                (jax.ShapeDtypeStruct((heads, m, dv), BF16),
                 pl.BlockSpec((hb, tm, dv), lambda i, j: (j, i, 0)))],
               tm=tm, tn=hb * (nope + dv))

    o = _attention(batch, q, k, v)
    return _simple_mm("mla_out", [_rows(o)], lambda t: t, w_o.reshape(heads * dv, d).astype(BF16),
                      [_epi_tile(x)], lambda acc, x_t: (x_t + acc,))


def kernel(x, positions, ffn_norm, ffn_w13, ffn_w2, mix_norm, rwkv_mu, rwkv_w_rkv, rwkv_w0, rwkv_w1, rwkv_w2, rwkv_a0, rwkv_a1, rwkv_a2, rwkv_g1, rwkv_g2, rwkv_k_k, rwkv_k_a, rwkv_r_k, rwkv_ln_w, rwkv_ln_b, rwkv_w_o, mla_w_down, mla_q_a_norm, mla_kv_a_norm, mla_w_uq, mla_w_ukv, mla_q_norm, mla_k_norm, mla_w_o):
    batch, seq, d = x.shape
    depth = ffn_norm.shape[0]
    n_mixers = 2
    xf = x.reshape(batch * seq, d)
    w13p, w2p = _ffn_weights(ffn_w13.reshape((2 * depth,) + ffn_w13.shape[2:]),
                             ffn_w2.reshape((2 * depth,) + ffn_w2.shape[2:]))
    for i in range(depth):
        j = i // n_mixers
        xf = _ffn(xf, ffn_norm[i, 0], w13p[2 * i], w2p[2 * i])
        if i % n_mixers == 0:
            xf = _rwkv_mix(batch, xf, mix_norm[i], rwkv_mu[j], rwkv_w_rkv[j], rwkv_w0[j], rwkv_w1[j],
                           rwkv_w2[j], rwkv_a0[j], rwkv_a1[j], rwkv_a2[j], rwkv_g1[j], rwkv_g2[j],
                           rwkv_k_k[j], rwkv_k_a[j], rwkv_r_k[j], rwkv_ln_w[j], rwkv_ln_b[j],
                           rwkv_w_o[j])
        else:
            xf = _mla_mix(batch, xf, positions, mix_norm[i], mla_w_down[j], mla_q_a_norm[j],
                          mla_kv_a_norm[j], mla_w_uq[j], mla_w_ukv[j], mla_q_norm[j], mla_k_norm[j],
                          mla_w_o[j])
        xf = _ffn(xf, ffn_norm[i, 1], w13p[2 * i + 1], w2p[2 * i + 1])
    return xf.reshape(batch, seq, d)
```

```python
import functools
import math

import jax
import jax.numpy as jnp
from jax import lax
from jax.experimental import pallas as pl
from jax.experimental.pallas import tpu as pltpu

F32 = jnp.float32
BF16 = jnp.bfloat16

RMS_EPS = 1e-6
GN_EPS = 64e-5
ROPE_THETA = 10000.0
ATTN_CHUNK = 64
RWKV_HEAD = 64
SCAN_CHUNK = 64
HEADS_PER_GROUP = 4
ATTN_Q_TILE = 512
ATTN_HEADS_PER_STEP = 2
FFN_TILE = 1024
MLA_HEADS_PER_STEP = 4
SCAN_STREAMS = 8
LANE = 128
SUBLANE = 8
VMEM_LIMIT_BYTES = 56 * 1024 * 1024


def _params(semantics):
    return pltpu.CompilerParams(dimension_semantics=semantics,
                                vmem_limit_bytes=VMEM_LIMIT_BYTES)


def _pick_tile(n, target, quantum):
    if n <= target:
        return n
    t = (target // quantum) * quantum
    while t > quantum and n % t:
        t -= quantum
    assert n % t == 0, (n, target, quantum)
    return t


def _round_robin(gens):
    live = list(gens)
    while live:
        live = [g for g in live if next(g, True) is None]


def _rms(x, g):
    return x * lax.rsqrt(jnp.mean(x * x, axis=-1, keepdims=True) + RMS_EPS) * g


def _ffn_body(f_last, x_ref, g_ref, wg_ref, wu_ref, w2_ref, o_ref, xn_ref):
    j = pl.program_id(1)
    last = pl.num_programs(1) - 1
    tf = wg_ref.shape[1]

    @pl.when(j == 0)
    def _():
        xn_ref[...] = _rms(x_ref[...], g_ref[...]).astype(BF16)
        o_ref[...] = jnp.zeros_like(o_ref)

    def add_tile(width):
        xn = xn_ref[...]
        gate = jnp.dot(xn, wg_ref[:, :width], preferred_element_type=F32)
        up = jnp.dot(xn, wu_ref[:, :width], preferred_element_type=F32)
        act = (gate * jax.nn.sigmoid(gate) * up).astype(BF16)
        o_ref[...] += jnp.dot(act, w2_ref[:width, :], preferred_element_type=F32)

    if f_last == tf:
        add_tile(tf)
    else:
        pl.when(j < last)(functools.partial(add_tile, tf))
        pl.when(j == last)(functools.partial(add_tile, f_last))

    @pl.when(j == last)
    def _():
        o_ref[...] = x_ref[...] + 0.5 * o_ref[...]


def _ffn_tile(f):
    return min(FFN_TILE, -(-f // LANE) * LANE)


def _cast_w13_body(f, fp, x_ref, o_ref):
    o_ref[:, :f] = x_ref[:, :f].astype(BF16)
    o_ref[:, fp:fp + f] = x_ref[:, f:].astype(BF16)
    if fp > f:
        zeros = jnp.zeros((o_ref.shape[0], fp - f), BF16)
        o_ref[:, f:fp] = zeros
        o_ref[:, fp + f:] = zeros


def _cast_w2_body(f, x_ref, o_ref):
    tr = o_ref.shape[0]
    row = pl.program_id(1) * tr + lax.broadcasted_iota(jnp.int32, o_ref.shape, 0)
    o_ref[...] = jnp.where(row < f, x_ref[...], 0.0).astype(BF16)


def _ffn_weights(w13, w2):
    n, d, f2 = w13.shape
    f = f2 // 2
    assert f % LANE == 0
    tf = _ffn_tile(f)
    fp = -(-f // tf) * tf
    td = _pick_tile(d, 128, SUBLANE)
    w13p = pl.pallas_call(
        functools.partial(_cast_w13_body, f, fp),
        grid=(n, d // td),
        in_specs=[pl.BlockSpec((None, td, 2 * f), lambda l, i: (l, i, 0))],
        out_specs=pl.BlockSpec((None, td, 2 * fp), lambda l, i: (l, i, 0)),
        out_shape=jax.ShapeDtypeStruct((n, d, 2 * fp), BF16),
        compiler_params=_params(("parallel", "parallel")),
        name="cast_w13",
    )(w13)
    w2p = pl.pallas_call(
        functools.partial(_cast_w2_body, f),
        grid=(n, fp // tf),
        in_specs=[pl.BlockSpec((None, tf, d), lambda l, i: (l, i, 0))],
        out_specs=pl.BlockSpec((None, tf, d), lambda l, i: (l, i, 0)),
        out_shape=jax.ShapeDtypeStruct((n, fp, d), BF16),
        compiler_params=_params(("parallel", "parallel")),
        name="cast_w2",
    )(w2)
    return w13p, w2p


def _ffn(x, g, w13p, w2p, f):
    m, d = x.shape
    fp = w2p.shape[0]
    tf = _ffn_tile(f)
    nj = fp // tf
    tm = _pick_tile(m, 512, 8)
    return pl.pallas_call(
        functools.partial(_ffn_body, f - (nj - 1) * tf),
        grid=(m // tm, nj),
        in_specs=[
            pl.BlockSpec((tm, d), lambda i, j: (i, 0)),
            pl.BlockSpec((1, d), lambda i, j: (0, 0)),
            pl.BlockSpec((d, tf), lambda i, j: (0, j)),
            pl.BlockSpec((d, tf), lambda i, j: (0, j + nj)),
            pl.BlockSpec((tf, d), lambda i, j: (j, 0)),
        ],
        out_specs=pl.BlockSpec((tm, d), lambda i, j: (i, 0)),
        out_shape=jax.ShapeDtypeStruct((m, d), F32),
        scratch_shapes=[pltpu.VMEM((tm, d), BF16)],
        compiler_params=_params(("parallel", "arbitrary")),
        name="ffn",
    )(x, g.reshape(1, d), w13p, w13p, w2p)


def _mm_body(n_lhs, n_epi, n_out, lhs_fn, epi_fn, *refs):
    lhs_refs = refs[:n_lhs]
    w_ref = refs[n_lhs]
    epi_refs = refs[n_lhs + 1:n_lhs + 1 + n_epi]
    out_refs = refs[n_lhs + 1 + n_epi:n_lhs + 1 + n_epi + n_out]
    lhs_scratch = refs[-1]

    @pl.when(pl.program_id(1) == 0)
    def _():
        lhs_scratch[...] = lhs_fn(*[r[...] for r in lhs_refs]).astype(BF16)

    acc = jnp.dot(lhs_scratch[...], w_ref[...], preferred_element_type=F32)
    outs = epi_fn(acc, *[r[...] for r in epi_refs])
    for o_ref, o in zip(out_refs, outs):
        o_ref[...] = o.astype(o_ref.dtype)


def _mm(name, m, lhs, lhs_fn, w, epi, epi_fn, outs, *, tm, tn):
    k, n = w.shape
    body = functools.partial(_mm_body, len(lhs), len(epi), len(outs), lhs_fn, epi_fn)
    return pl.pallas_call(
        body,
        grid=(m // tm, n // tn),
        in_specs=[s for _, s in lhs] + [pl.BlockSpec((k, tn), lambda i, j: (0, j))]
        + [s for _, s in epi],
        out_specs=[s for _, s in outs],
        out_shape=[s for s, _ in outs],
        scratch_shapes=[pltpu.VMEM((tm, k), BF16)],
        compiler_params=_params(("parallel", "arbitrary")),
        name=name,
    )(*[a for a, _ in lhs], w, *[a for a, _ in epi])


def _row_spec(tm, width, col_block=0):
    return pl.BlockSpec((tm, width), lambda i, j: (i, col_block))


def _tile_spec(tm, tn):
    return pl.BlockSpec((tm, tn), lambda i, j: (i, j))


def _vec_spec(width):
    return pl.BlockSpec((1, width), lambda i, j: (0, 0))


def _colvec_spec(tn):
    return pl.BlockSpec((1, tn), lambda i, j: (0, j))


def _simple_mm(name, lhs, lhs_fn, w, epi, epi_fn, out_dtype=F32, tm_target=512, tn_target=1024):
    m = lhs[0][0].shape[0]
    n = w.shape[1]
    tm = _pick_tile(m, tm_target, 8)
    tn = _pick_tile(n, tn_target, LANE)
    lhs = [(a, spec(tm)) for a, spec in lhs]
    epi = [(a, spec(tm, tn)) for a, spec in epi]
    out = (jax.ShapeDtypeStruct((m, n), out_dtype), _tile_spec(tm, tn))
    return _mm(name, m, lhs, lhs_fn, w, epi, epi_fn, [out], tm=tm, tn=tn)[0]


def _rows(a):
    return a, lambda tm: _row_spec(tm, a.shape[1])


def _vec(a):
    a = a.reshape(1, -1)
    return a, lambda tm: _vec_spec(a.shape[1])


def _epi_tile(a):
    return a, lambda tm, tn: _tile_spec(tm, tn)


def _epi_colvec(a):
    return a.reshape(1, -1), lambda tm, tn: _colvec_spec(tn)


def _rwkv_proj_body(seq_tiles, ranks, x_ref, xprev_ref, g_ref, mu_ref, wr_ref, wk_ref, wv_ref, l1_ref,
                    w2_ref, a2_ref, g2_ref, w0_ref, a0_ref, rkv_ref, lag_ref,
                    lerp_scr, tw_scr, ta_scr, tg_scr):
    i = pl.program_id(0)
    rw, ra = ranks

    @pl.when(pl.program_id(1) == 0)
    def _():
        h = _rms(x_ref[...], g_ref[...])
        prev = _rms(xprev_ref[...], g_ref[...])[SUBLANE - 1:SUBLANE, :]
        prev = jnp.where(i % seq_tiles == 0, 0.0, prev)
        row = lax.broadcasted_iota(jnp.int32, h.shape, 0)
        dx = jnp.where(row == 0, prev, pltpu.roll(h, 1, 0)) - h
        for idx in range(6):
            lerp_scr[idx] = (h + dx * mu_ref[idx:idx + 1, :]).astype(BF16)
        tw_scr[...] = jnp.tanh(jnp.dot(lerp_scr[3], l1_ref[:, :rw],
                                       preferred_element_type=F32)).astype(BF16)
        ta_scr[...] = jnp.dot(lerp_scr[4], l1_ref[:, rw:rw + ra],
                              preferred_element_type=F32).astype(BF16)
        tg_scr[...] = jax.nn.sigmoid(jnp.dot(lerp_scr[5], l1_ref[:, rw + ra:],
                                             preferred_element_type=F32)).astype(BF16)

    rkv_ref[0] = jnp.dot(lerp_scr[0], wr_ref[...], preferred_element_type=F32)
    rkv_ref[1] = jnp.dot(lerp_scr[1], wk_ref[...], preferred_element_type=F32)
    rkv_ref[2] = jnp.dot(lerp_scr[2], wv_ref[...], preferred_element_type=F32)
    z = -(w0_ref[...] + jnp.dot(tw_scr[...], w2_ref[...], preferred_element_type=F32))
    softplus = jnp.maximum(z, 0.0) + jnp.log1p(jnp.exp(-jnp.abs(z)))
    lag_ref[0] = -jnp.exp(-softplus - 0.5)
    lag_ref[1] = jax.nn.sigmoid(
        a0_ref[...] + jnp.dot(ta_scr[...], a2_ref[...], preferred_element_type=F32))
    lag_ref[2] = jnp.dot(tg_scr[...], g2_ref[...], preferred_element_type=F32)


def _pad_lora(wa, wb):
    rank = wa.shape[1]
    rp = -(-rank // LANE) * LANE
    return (jnp.pad(wa, ((0, 0), (0, rp - rank))).astype(BF16),
            jnp.pad(wb, ((0, rp - rank), (0, 0))).astype(BF16))


def _rwkv_proj(batch, x, norm_g, mu, w_rkv, w0, w1, w2, a0, a1, a2, g1, g2):
    m, d = x.shape
    tm = _pick_tile(m // batch, 512, SUBLANE)
    tn = _pick_tile(d, 256, LANE)
    w1p, w2p = _pad_lora(w1, w2)
    a1p, a2p = _pad_lora(a1, a2)
    g1p, g2p = _pad_lora(g1, g2)
    rw, ra, rg = w1p.shape[1], a1p.shape[1], g1p.shape[1]
    l1 = jnp.concatenate([w1p, a1p, g1p], axis=1)
    w_bf = w_rkv.astype(BF16)
    mu_ord = mu[jnp.array([0, 2, 3, 1, 4, 5])]
    col = lambda i, j: (0, j)

    def w_plane(p):
        return pl.BlockSpec((None, d, tn), lambda i, j: (p, 0, j))

    planes = pl.BlockSpec((3, tm, tn), lambda i, j: (0, i, j))
    body = functools.partial(_rwkv_proj_body, (m // batch) // tm, (rw, ra))
    return pl.pallas_call(
        body,
        grid=(m // tm, d // tn),
        in_specs=[
            pl.BlockSpec((tm, d), lambda i, j: (i, 0)),
            pl.BlockSpec((SUBLANE, d), lambda i, j: (jnp.maximum(i * (tm // SUBLANE) - 1, 0), 0)),
            pl.BlockSpec((1, d), lambda i, j: (0, 0)),
            pl.BlockSpec((6, d), lambda i, j: (0, 0)),
            w_plane(0), w_plane(1), w_plane(2),
            pl.BlockSpec((d, rw + ra + rg), lambda i, j: (0, 0)),
            pl.BlockSpec((rw, tn), col),
            pl.BlockSpec((ra, tn), col),
            pl.BlockSpec((rg, tn), col),
            pl.BlockSpec((1, tn), col),
            pl.BlockSpec((1, tn), col),
        ],
        out_specs=[planes, planes],
        out_shape=[jax.ShapeDtypeStruct((3, m, d), F32), jax.ShapeDtypeStruct((3, m, d), F32)],
        scratch_shapes=[pltpu.VMEM((6, tm, d), BF16), pltpu.VMEM((tm, rw), BF16),
                        pltpu.VMEM((tm, ra), BF16), pltpu.VMEM((tm, rg), BF16)],
        compiler_params=_params(("parallel", "arbitrary")),
        name="rwkv_proj",
    )(x, x, norm_g.reshape(1, d), mu_ord, w_bf, w_bf, w_bf, l1, w2p, a2p, g2p,
      w0.reshape(1, d), a0.reshape(1, d))


def _bf(x):
    return x.astype(BF16)


def _dot(a, b):
    return jnp.dot(_bf(a), _bf(b), preferred_element_type=F32)


def _dot_nt(a, b):
    return lax.dot_general(_bf(a), _bf(b), (((1,), (1,)), ((), ())), preferred_element_type=F32)


def _dot_tn(a, b):
    return lax.dot_general(_bf(a), _bf(b), (((0,), (0,)), ((), ())), preferred_element_type=F32)


def _scan_body(n_streams, r_ref, ld_ref, k_ref, v_ref, al_ref, g_ref, kk_w_ref, ka_w_ref, rk_w_ref,
               lnw_ref, lnb_ref, o_ref, s_ref):
    c = SCAN_CHUNK
    w = HEADS_PER_GROUP * RWKV_HEAD
    wc = HEADS_PER_GROUP * c

    @pl.when(pl.program_id(2) == 0)
    def _():
        s_ref[...] = jnp.zeros_like(s_ref)

    row = lax.broadcasted_iota(jnp.int32, (c, w), 0)
    bd_r = lax.broadcasted_iota(jnp.int32, (w, w), 0) // RWKV_HEAD
    bd_c = lax.broadcasted_iota(jnp.int32, (w, w), 1) // RWKV_HEAD
    head_mask = bd_r == bd_c
    head_ones = jnp.where(head_mask, 1.0, 0.0).astype(BF16)
    t_idx = lax.broadcasted_iota(jnp.int32, (c, wc), 0)
    s_idx = lax.broadcasted_iota(jnp.int32, (c, wc), 1) % c
    strict = s_idx < t_idx
    incl = s_idx <= t_idx
    eye = jnp.where(s_idx == t_idx, 1.0, 0.0)

    def head_sum(x):
        return jnp.dot(_bf(x), head_ones, preferred_element_type=F32)

    def block_diag(x):
        xb = _bf(x)
        return jnp.where(head_mask, jnp.concatenate([xb] * HEADS_PER_GROUP, axis=0),
                         jnp.zeros((), BF16))

    def stream(si):
        ln = pl.ds(si * w, w)
        r = r_ref[:, ln]
        ld = ld_ref[:, ln]
        k = k_ref[:, ln]
        v = v_ref[:, ln]
        al = al_ref[:, ln]

        cum = ld
        shift = 1
        while shift < c:
            cum = cum + jnp.where(row >= shift, pltpu.roll(cum, shift, 0), 0.0)
            shift *= 2
        p_incl = jnp.exp(cum)
        p_prev = jnp.exp(cum - ld)
        p_inv = jnp.exp(-cum)
        p_last = p_incl[c - 1:c, :]

        kk = k * kk_w_ref[:, ln]
        ss = head_sum(kk * kk)
        yield
        kk = kk / jnp.maximum(jnp.sqrt(ss), 1e-12)
        kmod = k * (1.0 + (al - 1.0) * ka_w_ref[:, ln])

        a_t = -kk * p_prev
        b_t = kk * al * p_inv
        k_t = kmod * p_inv
        r_t = r * p_incl

        ar = jnp.concatenate([a_t, r_t], axis=0)
        m_b = _dot_nt(ar, block_diag(b_t))
        m_k = _dot_nt(ar, block_diag(k_t))
        yield
        a_ab = jnp.where(strict, m_b[:c], 0.0)
        a_rb = jnp.where(incl, m_b[c:], 0.0)
        a_ak = jnp.where(strict, m_k[:c], 0.0)
        a_rk = jnp.where(incl, m_k[c:], 0.0)

        base_level = 3
        pw = jnp.where((t_idx >> base_level) == (s_idx >> base_level), a_ab, 0.0)
        t_inv = eye
        for step in range(base_level):
            if step < base_level - 1:
                z = _dot(jnp.concatenate([pw, t_inv], axis=0), block_diag(pw))
                yield
                pw, t_inv = z[:c], t_inv + z[c:]
            else:
                z = _dot(t_inv, block_diag(pw))
                yield
                t_inv = t_inv + z
        n_bd = block_diag(a_ab)
        level = base_level
        while (1 << level) < c:
            tb = t_idx >> level
            sb = s_idx >> level
            sel = ((tb & 1) == 1) & (sb == tb - 1)
            tn = _dot(t_inv, n_bd)
            yield
            x = _dot(tn, block_diag(t_inv))
            yield
            t_inv = t_inv + jnp.where(sel, x, 0.0)
            level += 1

        s0 = s_ref[si]
        w1 = _dot_nt(ar, s0)
        w2 = _dot(jnp.concatenate([a_ak, a_rk], axis=0), block_diag(v))
        yield
        u = _dot(t_inv, block_diag(w1[:c] + w2[:c]))
        yield
        y = w1[c:] + w2[c:] + _dot(a_rb, block_diag(u))

        uv = jnp.concatenate([u, v], axis=0)
        bk = jnp.concatenate([b_t, k_t], axis=0) * p_last
        s_new = _dot_tn(uv, bk)
        yield
        s_ref[si] = jnp.where(head_mask, s0 * p_last + s_new, 0.0)

        inv_n = 1.0 / RWKV_HEAD
        sums = head_sum(jnp.concatenate([y, r * kmod * rk_w_ref[:, ln]], axis=0))
        yield
        dlt = y - sums[:c] * inv_n
        bsum = sums[c:]
        var = head_sum(dlt * dlt) * inv_n
        yield
        yn = dlt * lax.rsqrt(var + GN_EPS) * lnw_ref[:, ln] + lnb_ref[:, ln]
        bonus = bsum * v
        o_ref[:, ln] = ((yn + bonus) * g_ref[:, ln]).astype(o_ref.dtype)

    _round_robin([stream(si) for si in range(n_streams)])


def _rwkv_scan(batch, rkv, lag, k_k, k_a, r_k, ln_w, ln_b):
    _, m, d = rkv.shape
    t = m // batch
    c = SCAN_CHUNK
    w = HEADS_PER_GROUP * RWKV_HEAD
    assert d % w == 0 and t % c == 0 and c == RWKV_HEAD
    n_streams = _pick_tile(d // w, SCAN_STREAMS, 1)
    wb = n_streams * w
    nc = t // c
    tok = pl.BlockSpec((c, wb), lambda b, gi, ci: (b * nc + ci, gi))
    par = pl.BlockSpec((1, wb), lambda b, gi, ci: (0, gi))

    def plane(p):
        return pl.BlockSpec((None, c, wb), lambda b, gi, ci: (p, b * nc + ci, gi))

    return pl.pallas_call(
        functools.partial(_scan_body, n_streams),
        grid=(batch, d // wb, nc),
        in_specs=[plane(0), plane(0), plane(1), plane(2), plane(1), plane(2)] + [par] * 5,
        out_specs=tok,
        out_shape=jax.ShapeDtypeStruct((m, d), BF16),
        scratch_shapes=[pltpu.VMEM((n_streams, w, w), F32)],
        compiler_params=_params(("parallel", "parallel", "arbitrary")),
        name="rwkv_scan",
    )(rkv, lag, rkv, rkv, lag, lag, *[p.reshape(1, d) for p in (k_k, k_a, r_k, ln_w, ln_b)])


def _rwkv_mix(batch, x, norm_g, mu, w_rkv, w0, w1, w2, a0, a1, a2, g1, g2,
              k_k, k_a, r_k, ln_w, ln_b, w_o):
    rkv, lag = _rwkv_proj(batch, x, norm_g, mu, w_rkv, w0, w1, w2, a0, a1, a2, g1, g2)
    yg = _rwkv_scan(batch, rkv, lag, k_k, k_a, r_k.reshape(-1), ln_w, ln_b)
    return _simple_mm("rwkv_out", [_rows(yg)], lambda t: t, w_o.astype(BF16), [_epi_tile(x)],
                      lambda acc, x_t: (x_t + acc,), tn_target=x.shape[1])


def _attn_body(tq, tk, nh, q_ref, k_ref, v_ref, o_ref):
    qi = pl.program_id(2)
    dv = v_ref.shape[-1]

    def head_step(h, j, carry, masked, out):
        m_i, l_i, acc = carry
        start = pl.multiple_of(j * tk, tk)
        kb = k_ref[h, pl.ds(start, tk), :]
        s = lax.dot_general(q_ref[h], kb, (((1,), (1,)), ((), ())), preferred_element_type=F32)
        yield
        if masked:
            q_chunk = (qi * tq + lax.broadcasted_iota(jnp.int32, (tq, tk), 0)) // ATTN_CHUNK
            k_chunk = (j * tk + lax.broadcasted_iota(jnp.int32, (tq, tk), 1)) // ATTN_CHUNK
            s = jnp.where(k_chunk <= q_chunk, s, jnp.finfo(F32).min)
        m_new = jnp.maximum(m_i, jnp.max(s, axis=-1, keepdims=True))
        p = jnp.exp(s - m_new)
        alpha = jnp.exp(m_i - m_new)
        l_new = alpha * l_i + jnp.sum(p, axis=-1, keepdims=True)
        pv = jnp.dot(p.astype(BF16), v_ref[h, pl.ds(start, tk), :], preferred_element_type=F32)
        yield
        out.append((m_new, l_new, alpha * acc + pv))

    def step(j, carries, masked):
        outs = [[] for _ in range(nh)]
        _round_robin([head_step(h, j, carries[h], masked, outs[h]) for h in range(nh)])
        return tuple(o[0] for o in outs)

    init = tuple((jnp.full((tq, 1), -jnp.inf, F32), jnp.zeros((tq, 1), F32),
                  jnp.zeros((tq, dv), F32)) for _ in range(nh))
    carries = lax.fori_loop(0, qi, lambda j, cr: step(j, cr, False), init)
    carries = step(qi, carries, True)
    for h, (_, l_i, acc) in enumerate(carries):
        o_ref[:, h * dv:(h + 1) * dv] = (acc / l_i).astype(o_ref.dtype)


def _attention(batch, q, k, v):
    heads, m, dqk = q.shape
    dv = v.shape[-1]
    t = m // batch
    tq = _pick_tile(t, ATTN_Q_TILE, ATTN_CHUNK)
    nq = t // tq
    nh = _pick_tile(heads, ATTN_HEADS_PER_STEP, 1)
    return pl.pallas_call(
        functools.partial(_attn_body, tq, tq, nh),
        grid=(batch, heads // nh, nq),
        in_specs=[
            pl.BlockSpec((nh, tq, dqk), lambda b, h, i: (h, b * nq + i, 0)),
            pl.BlockSpec((nh, t, dqk), lambda b, h, i: (h, b, 0)),
            pl.BlockSpec((nh, t, dv), lambda b, h, i: (h, b, 0)),
        ],
        out_specs=pl.BlockSpec((tq, nh * dv), lambda b, h, i: (b * nq + i, h)),
        out_shape=jax.ShapeDtypeStruct((m, heads * dv), BF16),
        compiler_params=_params(("parallel", "parallel", "arbitrary")),
        name="mla_attention",
    )(q, k, v)


def _mla_mix(batch, x, positions, norm_g, w_down, q_a_norm, kv_a_norm, w_uq, w_ukv,
             q_norm, k_norm, w_o):
    m, d = x.shape
    q_lora = q_a_norm.shape[0]
    kv_lora = kv_a_norm.shape[0]
    heads, qk_dim = w_uq.shape[1], w_uq.shape[2]
    rope = w_down.shape[1] - q_lora - kv_lora
    nope = qk_dim - rope
    dv = w_ukv.shape[2] - nope
    half = rope // 2
    assert nope == LANE and dv == LANE and rope <= LANE and q_lora == kv_lora and q_lora % LANE == 0
    scale = 1.0 / math.sqrt(qk_dim)

    perm = jnp.concatenate([jnp.arange(half, rope), jnp.arange(0, half)])
    lane_pad = lambda a: jnp.pad(a, [(0, 0)] * (a.ndim - 1) + [(0, LANE - a.shape[-1])])

    inv_freq = ROPE_THETA ** (-jnp.arange(0, rope, 2, dtype=F32) / rope)
    ang = positions.reshape(m, 1).astype(F32) * inv_freq
    cos, sin = jnp.cos(ang), jnp.sin(ang)
    cos_t = lane_pad(jnp.concatenate([cos, cos], axis=-1))
    sin_t = lane_pad(jnp.concatenate([-sin, sin], axis=-1))

    w_kpe = w_down[:, q_lora + kv_lora:]
    w_down_x = jnp.concatenate(
        [w_down[:, :q_lora + kv_lora], lane_pad(w_kpe), lane_pad(w_kpe[:, perm])], axis=1).astype(BF16)
    c = _simple_mm("mla_down", [_rows(x), _vec(norm_g)], _rms, w_down_x, [], lambda acc: (acc,),
                   tn_target=w_down_x.shape[1])
    qb = q_lora // q_lora
    tm = _pick_tile(m, 512, 8)

    def norm_rope(body, pe, rot, g_body, g_pe, g_rot, cos_b, sin_b, extra_scale):
        ss = jnp.sum(body * body, axis=-1, keepdims=True) + jnp.sum(pe * pe, axis=-1, keepdims=True)
        s = lax.rsqrt(ss / qk_dim + RMS_EPS) * extra_scale
        return jnp.concatenate([body * s * g_body, (pe * g_pe * cos_b + rot * g_rot * sin_b) * s], axis=-1)

    w_q_pe = w_uq[:, :, nope:]
    w_q_x = jnp.concatenate([w_uq[:, :, :nope], lane_pad(w_q_pe), lane_pad(w_q_pe[:, :, perm])],
                            axis=-1).reshape(q_lora, heads * 3 * LANE).astype(BF16)
    gq = jnp.concatenate([q_norm[:nope], lane_pad(q_norm[nope:]), lane_pad(q_norm[nope:][perm])]).reshape(1, -1)

    hb = _pick_tile(heads, MLA_HEADS_PER_STEP, 1)

    def q_epi(acc, g_t, cos_b, sin_b):
        per_head = []
        for hh in range(hb):
            a = acc[:, hh * 3 * LANE:(hh + 1) * 3 * LANE]
            per_head.append(norm_rope(a[:, :LANE], a[:, LANE:2 * LANE], a[:, 2 * LANE:], g_t[:, :LANE],
                                      g_t[:, LANE:2 * LANE], g_t[:, 2 * LANE:], cos_b, sin_b, scale))
        return (jnp.stack(per_head, axis=0),)

    q = _mm("mla_q", m,
            [(c, _row_spec(tm, q_lora, 0)), (q_a_norm.reshape(1, -1), _vec_spec(q_lora))], _rms, w_q_x,
            [(gq, _vec_spec(3 * LANE)), (cos_t, _row_spec(tm, LANE)), (sin_t, _row_spec(tm, LANE))], q_epi,
            [(jax.ShapeDtypeStruct((heads, m, 2 * LANE), BF16),
              pl.BlockSpec((hb, tm, 2 * LANE), lambda i, j: (j, i, 0)))],
            tm=tm, tn=hb * 3 * LANE)[0]

    w_kv_x = w_ukv.reshape(kv_lora, heads * (nope + dv)).astype(BF16)
    gk = jnp.concatenate([k_norm[:nope], lane_pad(k_norm[nope:]), lane_pad(k_norm[nope:][perm])]).reshape(1, -1)
    pe_block = (q_lora + kv_lora) // LANE

    def kv_epi(acc, g_t, pe, rot, cos_b, sin_b):
        k_out, v_out = [], []
        for hh in range(hb):
            a = acc[:, hh * (nope + dv):(hh + 1) * (nope + dv)]
            k_out.append(norm_rope(a[:, :nope], pe, rot, g_t[:, :LANE], g_t[:, LANE:2 * LANE],
                                   g_t[:, 2 * LANE:], cos_b, sin_b, 1.0))
            v_out.append(a[:, nope:])
        return jnp.stack(k_out, axis=0), jnp.stack(v_out, axis=0)

    k, v = _mm("mla_kv", m,
               [(c, _row_spec(tm, kv_lora, qb)), (kv_a_norm.reshape(1, -1), _vec_spec(kv_lora))], _rms, w_kv_x,
               [(gk, _vec_spec(3 * LANE)), (c, _row_spec(tm, LANE, pe_block)),
                (c, _row_spec(tm, LANE, pe_block + 1)), (cos_t, _row_spec(tm, LANE)),
                (sin_t, _row_spec(tm, LANE))], kv_epi,
               [(jax.ShapeDtypeStruct((heads, m, 2 * LANE), BF16),
                 pl.BlockSpec((hb, tm, 2 * LANE), lambda i, j: (j, i, 0))),
                (jax.ShapeDtypeStruct((heads, m, dv), BF16),
                 pl.BlockSpec((hb, tm, dv), lambda i, j: (j, i, 0)))],
               tm=tm, tn=hb * (nope + dv))

    o = _attention(batch, q, k, v)
    return _simple_mm("mla_out", [_rows(o)], lambda t: t, w_o.reshape(heads * dv, d).astype(BF16),
                      [_epi_tile(x)], lambda acc, x_t: (x_t + acc,), tn_target=d)


def kernel(x, positions, ffn_norm, ffn_w13, ffn_w2, mix_norm, rwkv_mu, rwkv_w_rkv, rwkv_w0, rwkv_w1, rwkv_w2, rwkv_a0, rwkv_a1, rwkv_a2, rwkv_g1, rwkv_g2, rwkv_k_k, rwkv_k_a, rwkv_r_k, rwkv_ln_w, rwkv_ln_b, rwkv_w_o, mla_w_down, mla_q_a_norm, mla_kv_a_norm, mla_w_uq, mla_w_ukv, mla_q_norm, mla_k_norm, mla_w_o):
    batch, seq, d = x.shape
    depth = ffn_norm.shape[0]
    n_mixers = 2
    xf = x.reshape(batch * seq, d)
    d_ff = ffn_w2.shape[2]
    w13p, w2p = _ffn_weights(ffn_w13.reshape((2 * depth,) + ffn_w13.shape[2:]),
                             ffn_w2.reshape((2 * depth,) + ffn_w2.shape[2:]))
    for i in range(depth):
        j = i // n_mixers
        xf = _ffn(xf, ffn_norm[i, 0], w13p[2 * i], w2p[2 * i], d_ff)
        if i % n_mixers == 0:
            xf = _rwkv_mix(batch, xf, mix_norm[i], rwkv_mu[j], rwkv_w_rkv[j], rwkv_w0[j], rwkv_w1[j],
                           rwkv_w2[j], rwkv_a0[j], rwkv_a1[j], rwkv_a2[j], rwkv_g1[j], rwkv_g2[j],
                           rwkv_k_k[j], rwkv_k_a[j], rwkv_r_k[j], rwkv_ln_w[j], rwkv_ln_b[j],
                           rwkv_w_o[j])
        else:
            xf = _mla_mix(batch, xf, positions, mix_norm[i], mla_w_down[j], mla_q_a_norm[j],
                          mla_kv_a_norm[j], mla_w_uq[j], mla_w_ukv[j], mla_q_norm[j], mla_k_norm[j],
                          mla_w_o[j])
        xf = _ffn(xf, ffn_norm[i, 1], w13p[2 * i + 1], w2p[2 * i + 1], d_ff)
    return xf.reshape(batch, seq, d)
```

```python
import functools
import math

import jax
import jax.numpy as jnp
from jax import lax
from jax.experimental import pallas as pl
from jax.experimental.pallas import tpu as pltpu

F32 = jnp.float32
BF16 = jnp.bfloat16

RMS_EPS = 1e-6
GN_EPS = 64e-5
ROPE_THETA = 10000.0
ATTN_CHUNK = 64
RWKV_HEAD = 64
SCAN_CHUNK = 64
HEADS_PER_GROUP = 4
ATTN_Q_TILE = 512
ATTN_HEADS_PER_STEP = 2
FFN_TILE = 512
MLA_HEADS_PER_STEP = 4
SCAN_STREAMS = 8
LANE = 128
SUBLANE = 8
VMEM_LIMIT_BYTES = 56 * 1024 * 1024


def _params(semantics):
    return pltpu.CompilerParams(dimension_semantics=semantics,
                                vmem_limit_bytes=VMEM_LIMIT_BYTES)


def _pick_tile(n, target, quantum):
    if n <= target:
        return n
    t = (target // quantum) * quantum
    while t > quantum and n % t:
        t -= quantum
    assert n % t == 0, (n, target, quantum)
    return t


def _round_robin(gens):
    live = list(gens)
    while live:
        live = [g for g in live if next(g, True) is None]


def _rms(x, g):
    return x * lax.rsqrt(jnp.mean(x * x, axis=-1, keepdims=True) + RMS_EPS) * g


def _ffn_body(f_last, x_ref, g_ref, wg_ref, wu_ref, w2_ref, o_ref, xn_ref):
    j = pl.program_id(1)
    last = pl.num_programs(1) - 1
    tf = wg_ref.shape[1]

    @pl.when(j == 0)
    def _():
        xn_ref[...] = _rms(x_ref[...], g_ref[...]).astype(BF16)
        o_ref[...] = jnp.zeros_like(o_ref)

    def add_tile(width):
        xn = xn_ref[...]
        gate = jnp.dot(xn, wg_ref[:, :width], preferred_element_type=F32)
        up = jnp.dot(xn, wu_ref[:, :width], preferred_element_type=F32)
        act = (gate * jax.nn.sigmoid(gate) * up).astype(BF16)
        o_ref[...] += jnp.dot(act, w2_ref[:width, :], preferred_element_type=F32)

    if f_last == tf:
        add_tile(tf)
    else:
        pl.when(j < last)(functools.partial(add_tile, tf))
        pl.when(j == last)(functools.partial(add_tile, f_last))

    @pl.when(j == last)
    def _():
        o_ref[...] = x_ref[...] + 0.5 * o_ref[...]


def _ffn_tile(f):
    return min(FFN_TILE, -(-f // LANE) * LANE)


def _cast_w13_body(f, fp, x_ref, o_ref):
    o_ref[:, :f] = x_ref[:, :f].astype(BF16)
    o_ref[:, fp:fp + f] = x_ref[:, f:].astype(BF16)
    if fp > f:
        zeros = jnp.zeros((o_ref.shape[0], fp - f), BF16)
        o_ref[:, f:fp] = zeros
        o_ref[:, fp + f:] = zeros


def _cast_w2_body(f, x_ref, o_ref):
    tr = o_ref.shape[0]
    row = pl.program_id(1) * tr + lax.broadcasted_iota(jnp.int32, o_ref.shape, 0)
    o_ref[...] = jnp.where(row < f, x_ref[...], 0.0).astype(BF16)


def _ffn_weights(w13, w2):
    n, d, f2 = w13.shape
    f = f2 // 2
    assert f % LANE == 0
    tf = _ffn_tile(f)
    fp = -(-f // tf) * tf
    td = _pick_tile(d, 128, SUBLANE)
    w13p = pl.pallas_call(
        functools.partial(_cast_w13_body, f, fp),
        grid=(n, d // td),
        in_specs=[pl.BlockSpec((None, td, 2 * f), lambda l, i: (l, i, 0))],
        out_specs=pl.BlockSpec((None, td, 2 * fp), lambda l, i: (l, i, 0)),
        out_shape=jax.ShapeDtypeStruct((n, d, 2 * fp), BF16),
        compiler_params=_params(("parallel", "parallel")),
        name="cast_w13",
    )(w13)
    w2p = pl.pallas_call(
        functools.partial(_cast_w2_body, f),
        grid=(n, fp // tf),
        in_specs=[pl.BlockSpec((None, tf, d), lambda l, i: (l, i, 0))],
        out_specs=pl.BlockSpec((None, tf, d), lambda l, i: (l, i, 0)),
        out_shape=jax.ShapeDtypeStruct((n, fp, d), BF16),
        compiler_params=_params(("parallel", "parallel")),
        name="cast_w2",
    )(w2)
    return w13p, w2p


def _ffn(x, g, w13p, w2p, layer, f):
    m, d = x.shape
    fp = w2p.shape[1]
    tf = _ffn_tile(f)
    nj = fp // tf
    tm = _pick_tile(m, 512, 8)
    return pl.pallas_call(
        functools.partial(_ffn_body, f - (nj - 1) * tf),
        grid=(m // tm, nj),
        in_specs=[
            pl.BlockSpec((tm, d), lambda i, j: (i, 0)),
            pl.BlockSpec((1, d), lambda i, j: (0, 0)),
            pl.BlockSpec((None, d, tf), lambda i, j: (layer, 0, j)),
            pl.BlockSpec((None, d, tf), lambda i, j: (layer, 0, j + nj)),
            pl.BlockSpec((None, tf, d), lambda i, j: (layer, j, 0)),
        ],
        out_specs=pl.BlockSpec((tm, d), lambda i, j: (i, 0)),
        out_shape=jax.ShapeDtypeStruct((m, d), F32),
        scratch_shapes=[pltpu.VMEM((tm, d), BF16)],
        compiler_params=_params(("parallel", "arbitrary")),
        name="ffn",
    )(x, g.reshape(1, d), w13p, w13p, w2p)


def _mm_body(n_lhs, n_epi, n_out, lhs_fn, epi_fn, *refs):
    lhs_refs = refs[:n_lhs]
    w_ref = refs[n_lhs]
    epi_refs = refs[n_lhs + 1:n_lhs + 1 + n_epi]
    out_refs = refs[n_lhs + 1 + n_epi:n_lhs + 1 + n_epi + n_out]
    lhs_scratch = refs[-1]

    @pl.when(pl.program_id(1) == 0)
    def _():
        lhs_scratch[...] = lhs_fn(*[r[...] for r in lhs_refs]).astype(BF16)

    acc = jnp.dot(lhs_scratch[...], w_ref[...], preferred_element_type=F32)
    outs = epi_fn(acc, *[r[...] for r in epi_refs])
    for o_ref, o in zip(out_refs, outs):
        o_ref[...] = o.astype(o_ref.dtype)


def _mm(name, m, lhs, lhs_fn, w, epi, epi_fn, outs, *, tm, tn):
    k, n = w.shape
    body = functools.partial(_mm_body, len(lhs), len(epi), len(outs), lhs_fn, epi_fn)
    return pl.pallas_call(
        body,
        grid=(m // tm, n // tn),
        in_specs=[s for _, s in lhs] + [pl.BlockSpec((k, tn), lambda i, j: (0, j))]
        + [s for _, s in epi],
        out_specs=[s for _, s in outs],
        out_shape=[s for s, _ in outs],
        scratch_shapes=[pltpu.VMEM((tm, k), BF16)],
        compiler_params=_params(("parallel", "arbitrary")),
        name=name,
    )(*[a for a, _ in lhs], w, *[a for a, _ in epi])


def _row_spec(tm, width, col_block=0):
    return pl.BlockSpec((tm, width), lambda i, j: (i, col_block))


def _tile_spec(tm, tn):
    return pl.BlockSpec((tm, tn), lambda i, j: (i, j))


def _vec_spec(width):
    return pl.BlockSpec((1, width), lambda i, j: (0, 0))


def _colvec_spec(tn):
    return pl.BlockSpec((1, tn), lambda i, j: (0, j))


def _simple_mm(name, lhs, lhs_fn, w, epi, epi_fn, out_dtype=F32, tm_target=512, tn_target=1024):
    m = lhs[0][0].shape[0]
    n = w.shape[1]
    tm = _pick_tile(m, tm_target, 8)
    tn = _pick_tile(n, tn_target, LANE)
    lhs = [(a, spec(tm)) for a, spec in lhs]
    epi = [(a, spec(tm, tn)) for a, spec in epi]
    out = (jax.ShapeDtypeStruct((m, n), out_dtype), _tile_spec(tm, tn))
    return _mm(name, m, lhs, lhs_fn, w, epi, epi_fn, [out], tm=tm, tn=tn)[0]


def _rows(a):
    return a, lambda tm: _row_spec(tm, a.shape[1])


def _vec(a):
    a = a.reshape(1, -1)
    return a, lambda tm: _vec_spec(a.shape[1])


def _epi_tile(a):
    return a, lambda tm, tn: _tile_spec(tm, tn)


def _epi_colvec(a):
    return a.reshape(1, -1), lambda tm, tn: _colvec_spec(tn)


def _rwkv_proj_body(seq_tiles, ranks, x_ref, xprev_ref, g_ref, mu_ref, wr_ref, wk_ref, wv_ref, l1_ref,
                    w2_ref, a2_ref, g2_ref, w0_ref, a0_ref, rkv_ref, ld_ref, ag_ref,
                    lerp_scr, tw_scr, ta_scr, tg_scr):
    i = pl.program_id(0)
    rw, ra = ranks

    @pl.when(pl.program_id(1) == 0)
    def _():
        h = _rms(x_ref[...], g_ref[...])
        prev = _rms(xprev_ref[...], g_ref[...])[SUBLANE - 1:SUBLANE, :]
        prev = jnp.where(i % seq_tiles == 0, 0.0, prev)
        row = lax.broadcasted_iota(jnp.int32, h.shape, 0)
        dx = jnp.where(row == 0, prev, pltpu.roll(h, 1, 0)) - h
        for idx in range(6):
            lerp_scr[idx] = (h + dx * mu_ref[idx:idx + 1, :]).astype(BF16)
        tw_scr[...] = jnp.tanh(jnp.dot(lerp_scr[3], l1_ref[:, :rw],
                                       preferred_element_type=F32)).astype(BF16)
        ta_scr[...] = jnp.dot(lerp_scr[4], l1_ref[:, rw:rw + ra],
                              preferred_element_type=F32).astype(BF16)
        tg_scr[...] = jax.nn.sigmoid(jnp.dot(lerp_scr[5], l1_ref[:, rw + ra:],
                                             preferred_element_type=F32)).astype(BF16)

    rkv_ref[0] = jnp.dot(lerp_scr[0], wr_ref[...], preferred_element_type=F32).astype(BF16)
    rkv_ref[1] = jnp.dot(lerp_scr[1], wk_ref[...], preferred_element_type=F32).astype(BF16)
    rkv_ref[2] = jnp.dot(lerp_scr[2], wv_ref[...], preferred_element_type=F32).astype(BF16)
    z = -(w0_ref[...] + jnp.dot(tw_scr[...], w2_ref[...], preferred_element_type=F32))
    softplus = jnp.maximum(z, 0.0) + jnp.log1p(jnp.exp(-jnp.abs(z)))
    ld_ref[...] = -jnp.exp(-softplus - 0.5)
    ag_ref[0] = jax.nn.sigmoid(
        a0_ref[...] + jnp.dot(ta_scr[...], a2_ref[...], preferred_element_type=F32)).astype(BF16)
    ag_ref[1] = jnp.dot(tg_scr[...], g2_ref[...], preferred_element_type=F32).astype(BF16)


def _pad_lora(wa, wb):
    rank = wa.shape[1]
    rp = -(-rank // LANE) * LANE
    return (jnp.pad(wa, ((0, 0), (0, rp - rank))).astype(BF16),
            jnp.pad(wb, ((0, rp - rank), (0, 0))).astype(BF16))


def _rwkv_proj(batch, x, norm_g, mu, w_rkv, w0, w1, w2, a0, a1, a2, g1, g2):
    m, d = x.shape
    tm = _pick_tile(m // batch, 512, SUBLANE)
    tn = _pick_tile(d, 256, LANE)
    w1p, w2p = _pad_lora(w1, w2)
    a1p, a2p = _pad_lora(a1, a2)
    g1p, g2p = _pad_lora(g1, g2)
    rw, ra, rg = w1p.shape[1], a1p.shape[1], g1p.shape[1]
    l1 = jnp.concatenate([w1p, a1p, g1p], axis=1)
    w_bf = w_rkv.astype(BF16)
    mu_ord = mu[jnp.array([0, 2, 3, 1, 4, 5])]
    col = lambda i, j: (0, j)

    def w_plane(p):
        return pl.BlockSpec((None, d, tn), lambda i, j: (p, 0, j))

    def planes(n):
        return pl.BlockSpec((n, tm, tn), lambda i, j: (0, i, j))

    body = functools.partial(_rwkv_proj_body, (m // batch) // tm, (rw, ra))
    return pl.pallas_call(
        body,
        grid=(m // tm, d // tn),
        in_specs=[
            pl.BlockSpec((tm, d), lambda i, j: (i, 0)),
            pl.BlockSpec((SUBLANE, d), lambda i, j: (jnp.maximum(i * (tm // SUBLANE) - 1, 0), 0)),
            pl.BlockSpec((1, d), lambda i, j: (0, 0)),
            pl.BlockSpec((6, d), lambda i, j: (0, 0)),
            w_plane(0), w_plane(1), w_plane(2),
            pl.BlockSpec((d, rw + ra + rg), lambda i, j: (0, 0)),
            pl.BlockSpec((rw, tn), col),
            pl.BlockSpec((ra, tn), col),
            pl.BlockSpec((rg, tn), col),
            pl.BlockSpec((1, tn), col),
            pl.BlockSpec((1, tn), col),
        ],
        out_specs=[planes(3), pl.BlockSpec((tm, tn), lambda i, j: (i, j)), planes(2)],
        out_shape=[jax.ShapeDtypeStruct((3, m, d), BF16), jax.ShapeDtypeStruct((m, d), F32),
                   jax.ShapeDtypeStruct((2, m, d), BF16)],
        scratch_shapes=[pltpu.VMEM((6, tm, d), BF16), pltpu.VMEM((tm, rw), BF16),
                        pltpu.VMEM((tm, ra), BF16), pltpu.VMEM((tm, rg), BF16)],
        compiler_params=_params(("parallel", "arbitrary")),
        name="rwkv_proj",
    )(x, x, norm_g.reshape(1, d), mu_ord, w_bf, w_bf, w_bf, l1, w2p, a2p, g2p,
      w0.reshape(1, d), a0.reshape(1, d))


def _bf(x):
    return x.astype(BF16)


def _dot(a, b):
    return jnp.dot(_bf(a), _bf(b), preferred_element_type=F32)


def _dot_nt(a, b):
    return lax.dot_general(_bf(a), _bf(b), (((1,), (1,)), ((), ())), preferred_element_type=F32)


def _dot_tn(a, b):
    return lax.dot_general(_bf(a), _bf(b), (((0,), (0,)), ((), ())), preferred_element_type=F32)


def _scan_body(n_streams, r_ref, ld_ref, k_ref, v_ref, al_ref, g_ref, kk_w_ref, ka_w_ref, rk_w_ref,
               lnw_ref, lnb_ref, o_ref, s_ref):
    c = SCAN_CHUNK
    w = HEADS_PER_GROUP * RWKV_HEAD
    wc = HEADS_PER_GROUP * c

    @pl.when(pl.program_id(2) == 0)
    def _():
        s_ref[...] = jnp.zeros_like(s_ref)

    row = lax.broadcasted_iota(jnp.int32, (c, w), 0)
    bd_r = lax.broadcasted_iota(jnp.int32, (w, w), 0) // RWKV_HEAD
    bd_c = lax.broadcasted_iota(jnp.int32, (w, w), 1) // RWKV_HEAD
    head_mask = bd_r == bd_c
    head_ones = jnp.where(head_mask, 1.0, 0.0).astype(BF16)
    t_idx = lax.broadcasted_iota(jnp.int32, (c, wc), 0)
    s_idx = lax.broadcasted_iota(jnp.int32, (c, wc), 1) % c
    strict = s_idx < t_idx
    incl = s_idx <= t_idx
    eye = jnp.where(s_idx == t_idx, 1.0, 0.0)

    def head_sum(x):
        return jnp.dot(_bf(x), head_ones, preferred_element_type=F32)

    def block_diag(x):
        xb = _bf(x)
        return jnp.where(head_mask, jnp.concatenate([xb] * HEADS_PER_GROUP, axis=0),
                         jnp.zeros((), BF16))

    def stream(si):
        ln = pl.ds(si * w, w)
        r = r_ref[:, ln].astype(F32)
        ld = ld_ref[:, ln]
        k = k_ref[:, ln].astype(F32)
        v = v_ref[:, ln].astype(F32)
        al = al_ref[:, ln].astype(F32)

        cum = ld
        shift = 1
        while shift < c:
            cum = cum + jnp.where(row >= shift, pltpu.roll(cum, shift, 0), 0.0)
            shift *= 2
        p_incl = jnp.exp(cum)
        p_prev = jnp.exp(cum - ld)
        p_inv = jnp.exp(-cum)
        p_last = p_incl[c - 1:c, :]

        kk = k * kk_w_ref[:, ln]
        ss = head_sum(kk * kk)
        yield
        kk = kk / jnp.maximum(jnp.sqrt(ss), 1e-12)
        kmod = k * (1.0 + (al - 1.0) * ka_w_ref[:, ln])

        a_t = -kk * p_prev
        b_t = kk * al * p_inv
        k_t = kmod * p_inv
        r_t = r * p_incl

        ar = jnp.concatenate([a_t, r_t], axis=0)
        m_b = _dot_nt(ar, block_diag(b_t))
        m_k = _dot_nt(ar, block_diag(k_t))
        yield
        a_ab = jnp.where(strict, m_b[:c], 0.0)
        a_rb = jnp.where(incl, m_b[c:], 0.0)
        a_ak = jnp.where(strict, m_k[:c], 0.0)
        a_rk = jnp.where(incl, m_k[c:], 0.0)

        base_level = 3
        pw = jnp.where((t_idx >> base_level) == (s_idx >> base_level), a_ab, 0.0)
        t_inv = eye
        for step in range(base_level):
            if step < base_level - 1:
                z = _dot(jnp.concatenate([pw, t_inv], axis=0), block_diag(pw))
                yield
                pw, t_inv = z[:c], t_inv + z[c:]
            else:
                z = _dot(t_inv, block_diag(pw))
                yield
                t_inv = t_inv + z
        n_bd = block_diag(a_ab)
        level = base_level
        while (1 << level) < c:
            tb = t_idx >> level
            sb = s_idx >> level
            sel = ((tb & 1) == 1) & (sb == tb - 1)
            tn = _dot(t_inv, n_bd)
            yield
            x = _dot(tn, block_diag(t_inv))
            yield
            t_inv = t_inv + jnp.where(sel, x, 0.0)
            level += 1

        s0 = s_ref[si]
        w1 = _dot_nt(ar, s0)
        w2 = _dot(jnp.concatenate([a_ak, a_rk], axis=0), block_diag(v))
        yield
        u = _dot(t_inv, block_diag(w1[:c] + w2[:c]))
        yield
        y = w1[c:] + w2[c:] + _dot(a_rb, block_diag(u))

        uv = jnp.concatenate([u, v], axis=0)
        bk = jnp.concatenate([b_t, k_t], axis=0) * p_last
        s_new = _dot_tn(uv, bk)
        yield
        s_ref[si] = jnp.where(head_mask, s0 * p_last + s_new, 0.0)

        inv_n = 1.0 / RWKV_HEAD
        sums = head_sum(jnp.concatenate([y, r * kmod * rk_w_ref[:, ln]], axis=0))
        yield
        dlt = y - sums[:c] * inv_n
        bsum = sums[c:]
        var = head_sum(dlt * dlt) * inv_n
        yield
        yn = dlt * lax.rsqrt(var + GN_EPS) * lnw_ref[:, ln] + lnb_ref[:, ln]
        bonus = bsum * v
        o_ref[:, ln] = ((yn + bonus) * g_ref[:, ln]).astype(o_ref.dtype)

    _round_robin([stream(si) for si in range(n_streams)])


def _rwkv_scan(batch, rkv, ld, ag, k_k, k_a, r_k, ln_w, ln_b):
    _, m, d = rkv.shape
    t = m // batch
    c = SCAN_CHUNK
    w = HEADS_PER_GROUP * RWKV_HEAD
    assert d % w == 0 and t % c == 0 and c == RWKV_HEAD
    n_streams = _pick_tile(d // w, SCAN_STREAMS, 1)
    wb = n_streams * w
    nc = t // c
    tok = pl.BlockSpec((c, wb), lambda b, gi, ci: (b * nc + ci, gi))
    par = pl.BlockSpec((1, wb), lambda b, gi, ci: (0, gi))

    def plane(p):
        return pl.BlockSpec((None, c, wb), lambda b, gi, ci: (p, b * nc + ci, gi))

    return pl.pallas_call(
        functools.partial(_scan_body, n_streams),
        grid=(batch, d // wb, nc),
        in_specs=[plane(0), tok, plane(1), plane(2), plane(0), plane(1)] + [par] * 5,
        out_specs=tok,
        out_shape=jax.ShapeDtypeStruct((m, d), BF16),
        scratch_shapes=[pltpu.VMEM((n_streams, w, w), F32)],
        compiler_params=_params(("parallel", "parallel", "arbitrary")),
        name="rwkv_scan",
    )(rkv, ld, rkv, rkv, ag, ag, *[p.reshape(1, d) for p in (k_k, k_a, r_k, ln_w, ln_b)])


def _rwkv_mix(batch, x, norm_g, mu, w_rkv, w0, w1, w2, a0, a1, a2, g1, g2,
              k_k, k_a, r_k, ln_w, ln_b, w_o):
    rkv, ld, ag = _rwkv_proj(batch, x, norm_g, mu, w_rkv, w0, w1, w2, a0, a1, a2, g1, g2)
    yg = _rwkv_scan(batch, rkv, ld, ag, k_k, k_a, r_k.reshape(-1), ln_w, ln_b)
    return _simple_mm("rwkv_out", [_rows(yg)], lambda t: t, w_o.astype(BF16), [_epi_tile(x)],
                      lambda acc, x_t: (x_t + acc,), tn_target=x.shape[1])


def _attn_body(tq, tk, nh, q_ref, k_ref, v_ref, o_ref):
    qi = pl.program_id(2)
    dv = v_ref.shape[-1]

    def head_step(h, j, carry, masked, out):
        m_i, l_i, acc = carry
        start = pl.multiple_of(j * tk, tk)
        kb = k_ref[h, pl.ds(start, tk), :]
        s = lax.dot_general(q_ref[h], kb, (((1,), (1,)), ((), ())), preferred_element_type=F32)
        yield
        if masked:
            q_chunk = (qi * tq + lax.broadcasted_iota(jnp.int32, (tq, tk), 0)) // ATTN_CHUNK
            k_chunk = (j * tk + lax.broadcasted_iota(jnp.int32, (tq, tk), 1)) // ATTN_CHUNK
            s = jnp.where(k_chunk <= q_chunk, s, jnp.finfo(F32).min)
        m_new = jnp.maximum(m_i, jnp.max(s, axis=-1, keepdims=True))
        p = jnp.exp(s - m_new)
        alpha = jnp.exp(m_i - m_new)
        l_new = alpha * l_i + jnp.sum(p, axis=-1, keepdims=True)
        pv = jnp.dot(p.astype(BF16), v_ref[h, pl.ds(start, tk), :], preferred_element_type=F32)
        yield
        out.append((m_new, l_new, alpha * acc + pv))

    def step(j, carries, masked):
        outs = [[] for _ in range(nh)]
        _round_robin([head_step(h, j, carries[h], masked, outs[h]) for h in range(nh)])
        return tuple(o[0] for o in outs)

    init = tuple((jnp.full((tq, 1), -jnp.inf, F32), jnp.zeros((tq, 1), F32),
                  jnp.zeros((tq, dv), F32)) for _ in range(nh))
    carries = lax.fori_loop(0, qi, lambda j, cr: step(j, cr, False), init)
    carries = step(qi, carries, True)
    for h, (_, l_i, acc) in enumerate(carries):
        o_ref[:, h * dv:(h + 1) * dv] = (acc / l_i).astype(o_ref.dtype)


def _attention(batch, q, k, v):
    heads, m, dqk = q.shape
    dv = v.shape[-1]
    t = m // batch
    tq = _pick_tile(t, ATTN_Q_TILE, ATTN_CHUNK)
    nq = t // tq
    nh = _pick_tile(heads, ATTN_HEADS_PER_STEP, 1)
    return pl.pallas_call(
        functools.partial(_attn_body, tq, tq, nh),
        grid=(batch, heads // nh, nq),
        in_specs=[
            pl.BlockSpec((nh, tq, dqk), lambda b, h, i: (h, b * nq + i, 0)),
            pl.BlockSpec((nh, t, dqk), lambda b, h, i: (h, b, 0)),
            pl.BlockSpec((nh, t, dv), lambda b, h, i: (h, b, 0)),
        ],
        out_specs=pl.BlockSpec((tq, nh * dv), lambda b, h, i: (b * nq + i, h)),
        out_shape=jax.ShapeDtypeStruct((m, heads * dv), BF16),
        compiler_params=_params(("parallel", "parallel", "arbitrary")),
        name="mla_attention",
    )(q, k, v)


def _mla_mix(batch, x, positions, norm_g, w_down, q_a_norm, kv_a_norm, w_uq, w_ukv,
             q_norm, k_norm, w_o):
    m, d = x.shape
    q_lora = q_a_norm.shape[0]
    kv_lora = kv_a_norm.shape[0]
    heads, qk_dim = w_uq.shape[1], w_uq.shape[2]
    rope = w_down.shape[1] - q_lora - kv_lora
    nope = qk_dim - rope
    dv = w_ukv.shape[2] - nope
    half = rope // 2
    assert nope == LANE and dv == LANE and rope <= LANE and q_lora == kv_lora and q_lora % LANE == 0
    scale = 1.0 / math.sqrt(qk_dim)

    perm = jnp.concatenate([jnp.arange(half, rope), jnp.arange(0, half)])
    lane_pad = lambda a: jnp.pad(a, [(0, 0)] * (a.ndim - 1) + [(0, LANE - a.shape[-1])])

    inv_freq = ROPE_THETA ** (-jnp.arange(0, rope, 2, dtype=F32) / rope)
    ang = positions.reshape(m, 1).astype(F32) * inv_freq
    cos, sin = jnp.cos(ang), jnp.sin(ang)
    cos_t = lane_pad(jnp.concatenate([cos, cos], axis=-1))
    sin_t = lane_pad(jnp.concatenate([-sin, sin], axis=-1))

    w_kpe = w_down[:, q_lora + kv_lora:]
    w_down_x = jnp.concatenate(
        [w_down[:, :q_lora + kv_lora], lane_pad(w_kpe), lane_pad(w_kpe[:, perm])], axis=1).astype(BF16)
    c = _simple_mm("mla_down", [_rows(x), _vec(norm_g)], _rms, w_down_x, [], lambda acc: (acc,),
                   tn_target=w_down_x.shape[1])
    qb = q_lora // q_lora
    tm = _pick_tile(m, 512, 8)

    def norm_rope(body, pe, rot, g_body, g_pe, g_rot, cos_b, sin_b, extra_scale):
        ss = jnp.sum(body * body, axis=-1, keepdims=True) + jnp.sum(pe * pe, axis=-1, keepdims=True)
        s = lax.rsqrt(ss / qk_dim + RMS_EPS) * extra_scale
        return jnp.concatenate([body * s * g_body, (pe * g_pe * cos_b + rot * g_rot * sin_b) * s], axis=-1)

    w_q_pe = w_uq[:, :, nope:]
    w_q_x = jnp.concatenate([w_uq[:, :, :nope], lane_pad(w_q_pe), lane_pad(w_q_pe[:, :, perm])],
                            axis=-1).reshape(q_lora, heads * 3 * LANE).astype(BF16)
    gq = jnp.concatenate([q_norm[:nope], lane_pad(q_norm[nope:]), lane_pad(q_norm[nope:][perm])]).reshape(1, -1)

    hb = _pick_tile(heads, MLA_HEADS_PER_STEP, 1)

    def q_epi(acc, g_t, cos_b, sin_b):
        per_head = []
        for hh in range(hb):
            a = acc[:, hh * 3 * LANE:(hh + 1) * 3 * LANE]
            per_head.append(norm_rope(a[:, :LANE], a[:, LANE:2 * LANE], a[:, 2 * LANE:], g_t[:, :LANE],
                                      g_t[:, LANE:2 * LANE], g_t[:, 2 * LANE:], cos_b, sin_b, scale))
        return (jnp.stack(per_head, axis=0),)

    q = _mm("mla_q", m,
            [(c, _row_spec(tm, q_lora, 0)), (q_a_norm.reshape(1, -1), _vec_spec(q_lora))], _rms, w_q_x,
            [(gq, _vec_spec(3 * LANE)), (cos_t, _row_spec(tm, LANE)), (sin_t, _row_spec(tm, LANE))], q_epi,
            [(jax.ShapeDtypeStruct((heads, m, 2 * LANE), BF16),
              pl.BlockSpec((hb, tm, 2 * LANE), lambda i, j: (j, i, 0)))],
            tm=tm, tn=hb * 3 * LANE)[0]

    w_kv_x = w_ukv.reshape(kv_lora, heads * (nope + dv)).astype(BF16)
    gk = jnp.concatenate([k_norm[:nope], lane_pad(k_norm[nope:]), lane_pad(k_norm[nope:][perm])]).reshape(1, -1)
    pe_block = (q_lora + kv_lora) // LANE

    def kv_epi(acc, g_t, pe, rot, cos_b, sin_b):
        k_out, v_out = [], []
        for hh in range(hb):
            a = acc[:, hh * (nope + dv):(hh + 1) * (nope + dv)]
            k_out.append(norm_rope(a[:, :nope], pe, rot, g_t[:, :LANE], g_t[:, LANE:2 * LANE],
                                   g_t[:, 2 * LANE:], cos_b, sin_b, 1.0))
            v_out.append(a[:, nope:])
        return jnp.stack(k_out, axis=0), jnp.stack(v_out, axis=0)

    k, v = _mm("mla_kv", m,
               [(c, _row_spec(tm, kv_lora, qb)), (kv_a_norm.reshape(1, -1), _vec_spec(kv_lora))], _rms, w_kv_x,
               [(gk, _vec_spec(3 * LANE)), (c, _row_spec(tm, LANE, pe_block)),
                (c, _row_spec(tm, LANE, pe_block + 1)), (cos_t, _row_spec(tm, LANE)),
                (sin_t, _row_spec(tm, LANE))], kv_epi,
               [(jax.ShapeDtypeStruct((heads, m, 2 * LANE), BF16),
                 pl.BlockSpec((hb, tm, 2 * LANE), lambda i, j: (j, i, 0))),
                (jax.ShapeDtypeStruct((heads, m, dv), BF16),
                 pl.BlockSpec((hb, tm, dv), lambda i, j: (j, i, 0)))],
               tm=tm, tn=hb * (nope + dv))

    o = _attention(batch, q, k, v)
    return _simple_mm("mla_out", [_rows(o)], lambda t: t, w_o.reshape(heads * dv, d).astype(BF16),
                      [_epi_tile(x)], lambda acc, x_t: (x_t + acc,), tn_target=d)


def kernel(x, positions, ffn_norm, ffn_w13, ffn_w2, mix_norm, rwkv_mu, rwkv_w_rkv, rwkv_w0, rwkv_w1, rwkv_w2, rwkv_a0, rwkv_a1, rwkv_a2, rwkv_g1, rwkv_g2, rwkv_k_k, rwkv_k_a, rwkv_r_k, rwkv_ln_w, rwkv_ln_b, rwkv_w_o, mla_w_down, mla_q_a_norm, mla_kv_a_norm, mla_w_uq, mla_w_ukv, mla_q_norm, mla_k_norm, mla_w_o):
    batch, seq, d = x.shape
    depth = ffn_norm.shape[0]
    n_mixers = 2
    xf = x.reshape(batch * seq, d)
    d_ff = ffn_w2.shape[2]
    w13p, w2p = _ffn_weights(ffn_w13.reshape((2 * depth,) + ffn_w13.shape[2:]),
                             ffn_w2.reshape((2 * depth,) + ffn_w2.shape[2:]))
    for i in range(depth):
        j = i // n_mixers
        xf = _ffn(xf, ffn_norm[i, 0], w13p, w2p, 2 * i, d_ff)
        if i % n_mixers == 0:
            xf = _rwkv_mix(batch, xf, mix_norm[i], rwkv_mu[j], rwkv_w_rkv[j], rwkv_w0[j], rwkv_w1[j],
                           rwkv_w2[j], rwkv_a0[j], rwkv_a1[j], rwkv_a2[j], rwkv_g1[j], rwkv_g2[j],
                           rwkv_k_k[j], rwkv_k_a[j], rwkv_r_k[j], rwkv_ln_w[j], rwkv_ln_b[j],
                           rwkv_w_o[j])
        else:
            xf = _mla_mix(batch, xf, positions, mix_norm[i], mla_w_down[j], mla_q_a_norm[j],
                          mla_kv_a_norm[j], mla_w_uq[j], mla_w_ukv[j], mla_q_norm[j], mla_k_norm[j],
                          mla_w_o[j])
        xf = _ffn(xf, ffn_norm[i, 1], w13p, w2p, 2 * i + 1, d_ff)
    return xf.reshape(batch, seq, d)
```

```python
import functools
import math

import jax
import jax.numpy as jnp
from jax import lax
from jax.experimental import pallas as pl
from jax.experimental.pallas import tpu as pltpu

F32 = jnp.float32
BF16 = jnp.bfloat16

RMS_EPS = 1e-6
GN_EPS = 64e-5
ROPE_THETA = 10000.0
ATTN_CHUNK = 64
RWKV_HEAD = 64
SCAN_CHUNK = 64
HEADS_PER_GROUP = 4
ATTN_Q_TILE = 512
ATTN_HEADS_PER_STEP = 2
FFN_TILE = 512
MLA_HEADS_PER_STEP = 4
SCAN_STREAMS = 8
LANE = 128
SUBLANE = 8
VMEM_LIMIT_BYTES = 56 * 1024 * 1024


def _params(semantics):
    return pltpu.CompilerParams(dimension_semantics=semantics,
                                vmem_limit_bytes=VMEM_LIMIT_BYTES)


def _pick_tile(n, target, quantum):
    if n <= target:
        return n
    t = (target // quantum) * quantum
    while t > quantum and n % t:
        t -= quantum
    assert n % t == 0, (n, target, quantum)
    return t


def _round_robin(gens):
    live = list(gens)
    while live:
        live = [g for g in live if next(g, True) is None]


def _rms(x, g):
    return x * lax.rsqrt(jnp.mean(x * x, axis=-1, keepdims=True) + RMS_EPS) * g


def _ffn_body(f_last, x_ref, g_ref, wg_ref, wu_ref, w2_ref, o_ref, xn_ref):
    j = pl.program_id(1)
    last = pl.num_programs(1) - 1
    tf = wg_ref.shape[1]

    @pl.when(j == 0)
    def _():
        xn_ref[...] = _rms(x_ref[...], g_ref[...]).astype(BF16)
        o_ref[...] = jnp.zeros_like(o_ref)

    def add_tile(width):
        xn = xn_ref[...]
        gate = jnp.dot(xn, wg_ref[:, :width], preferred_element_type=F32)
        up = jnp.dot(xn, wu_ref[:, :width], preferred_element_type=F32)
        act = (gate * jax.nn.sigmoid(gate) * up).astype(BF16)
        o_ref[...] += jnp.dot(act, w2_ref[:width, :], preferred_element_type=F32)

    if f_last == tf:
        add_tile(tf)
    else:
        pl.when(j < last)(functools.partial(add_tile, tf))
        pl.when(j == last)(functools.partial(add_tile, f_last))

    @pl.when(j == last)
    def _():
        o_ref[...] = x_ref[...] + 0.5 * o_ref[...]


def _ffn_tile(f):
    return min(FFN_TILE, -(-f // LANE) * LANE)


def _tile_cast_body(tiles, x_ref, o_ref):
    tw = o_ref.shape[-1]
    for t, (c0, width) in enumerate(tiles):
        o_ref[t, :, :width] = x_ref[:, c0:c0 + width].astype(BF16)
        if width < tw:
            o_ref[t, :, width:] = jnp.zeros((o_ref.shape[1], tw - width), BF16)


def _tile_cast(name, w, tiles, tw):
    n, rows, cols = w.shape
    tr = _pick_tile(rows, 128, SUBLANE)
    return pl.pallas_call(
        functools.partial(_tile_cast_body, tuple(tiles)),
        grid=(n, rows // tr),
        in_specs=[pl.BlockSpec((None, tr, cols), lambda l, i: (l, i, 0))],
        out_specs=pl.BlockSpec((None, len(tiles), tr, tw), lambda l, i: (l, 0, i, 0)),
        out_shape=jax.ShapeDtypeStruct((n, len(tiles), rows, tw), BF16),
        compiler_params=_params(("parallel", "parallel")),
        name=name,
    )(w)


def _cast_w2_body(f, x_ref, o_ref):
    tr = o_ref.shape[0]
    row = pl.program_id(1) * tr + lax.broadcasted_iota(jnp.int32, o_ref.shape, 0)
    o_ref[...] = jnp.where(row < f, x_ref[...], 0.0).astype(BF16)


def _ffn_weights(w13, w2):
    n, d, f2 = w13.shape
    f = f2 // 2
    assert f % LANE == 0
    tf = _ffn_tile(f)
    fp = -(-f // tf) * tf
    half = [(t * tf, min(tf, f - t * tf)) for t in range(fp // tf)]
    w13p = _tile_cast("cast_w13", w13, half + [(f + c0, width) for c0, width in half], tf)
    w2p = pl.pallas_call(
        functools.partial(_cast_w2_body, f),
        grid=(n, fp // tf),
        in_specs=[pl.BlockSpec((None, tf, d), lambda l, i: (l, i, 0))],
        out_specs=pl.BlockSpec((None, tf, d), lambda l, i: (l, i, 0)),
        out_shape=jax.ShapeDtypeStruct((n, fp, d), BF16),
        compiler_params=_params(("parallel", "parallel")),
        name="cast_w2",
    )(w2)
    return w13p, w2p


def _ffn(x, g, w13p, w2p, layer, f):
    m, d = x.shape
    fp = w2p.shape[1]
    tf = _ffn_tile(f)
    nj = fp // tf
    tm = _pick_tile(m, 512, 8)
    return pl.pallas_call(
        functools.partial(_ffn_body, f - (nj - 1) * tf),
        grid=(m // tm, nj),
        in_specs=[
            pl.BlockSpec((tm, d), lambda i, j: (i, 0)),
            pl.BlockSpec((1, d), lambda i, j: (0, 0)),
            pl.BlockSpec((None, None, d, tf), lambda i, j: (layer, j, 0, 0)),
            pl.BlockSpec((None, None, d, tf), lambda i, j: (layer, j + nj, 0, 0)),
            pl.BlockSpec((None, tf, d), lambda i, j: (layer, j, 0)),
        ],
        out_specs=pl.BlockSpec((tm, d), lambda i, j: (i, 0)),
        out_shape=jax.ShapeDtypeStruct((m, d), F32),
        scratch_shapes=[pltpu.VMEM((tm, d), BF16)],
        compiler_params=_params(("parallel", "arbitrary")),
        name="ffn",
    )(x, g.reshape(1, d), w13p, w13p, w2p)


def _mm_body(n_lhs, n_epi, n_out, lhs_fn, epi_fn, *refs):
    lhs_refs = refs[:n_lhs]
    w_ref = refs[n_lhs]
    epi_refs = refs[n_lhs + 1:n_lhs + 1 + n_epi]
    out_refs = refs[n_lhs + 1 + n_epi:n_lhs + 1 + n_epi + n_out]
    lhs_scratch = refs[-1]

    @pl.when(pl.program_id(1) == 0)
    def _():
        lhs_scratch[...] = lhs_fn(*[r[...] for r in lhs_refs]).astype(BF16)

    acc = jnp.dot(lhs_scratch[...], w_ref[...], preferred_element_type=F32)
    outs = epi_fn(acc, *[r[...] for r in epi_refs])
    for o_ref, o in zip(out_refs, outs):
        o_ref[...] = o.astype(o_ref.dtype)


def _mm(name, m, lhs, lhs_fn, w, epi, epi_fn, outs, *, tm, tn):
    k, n = w.shape
    body = functools.partial(_mm_body, len(lhs), len(epi), len(outs), lhs_fn, epi_fn)
    return pl.pallas_call(
        body,
        grid=(m // tm, n // tn),
        in_specs=[s for _, s in lhs] + [pl.BlockSpec((k, tn), lambda i, j: (0, j))]
        + [s for _, s in epi],
        out_specs=[s for _, s in outs],
        out_shape=[s for s, _ in outs],
        scratch_shapes=[pltpu.VMEM((tm, k), BF16)],
        compiler_params=_params(("parallel", "arbitrary")),
        name=name,
    )(*[a for a, _ in lhs], w, *[a for a, _ in epi])


def _row_spec(tm, width, col_block=0):
    return pl.BlockSpec((tm, width), lambda i, j: (i, col_block))


def _tile_spec(tm, tn):
    return pl.BlockSpec((tm, tn), lambda i, j: (i, j))


def _vec_spec(width):
    return pl.BlockSpec((1, width), lambda i, j: (0, 0))


def _colvec_spec(tn):
    return pl.BlockSpec((1, tn), lambda i, j: (0, j))


def _simple_mm(name, lhs, lhs_fn, w, epi, epi_fn, out_dtype=F32, tm_target=512, tn_target=1024):
    m = lhs[0][0].shape[0]
    n = w.shape[1]
    tm = _pick_tile(m, tm_target, 8)
    tn = _pick_tile(n, tn_target, LANE)
    lhs = [(a, spec(tm)) for a, spec in lhs]
    epi = [(a, spec(tm, tn)) for a, spec in epi]
    out = (jax.ShapeDtypeStruct((m, n), out_dtype), _tile_spec(tm, tn))
    return _mm(name, m, lhs, lhs_fn, w, epi, epi_fn, [out], tm=tm, tn=tn)[0]


def _rows(a):
    return a, lambda tm: _row_spec(tm, a.shape[1])


def _vec(a):
    a = a.reshape(1, -1)
    return a, lambda tm: _vec_spec(a.shape[1])


def _epi_tile(a):
    return a, lambda tm, tn: _tile_spec(tm, tn)


def _epi_colvec(a):
    return a.reshape(1, -1), lambda tm, tn: _colvec_spec(tn)


def _rwkv_proj_body(seq_tiles, ranks, x_ref, xprev_ref, g_ref, mu_ref, wr_ref, wk_ref, wv_ref, l1_ref,
                    w2_ref, a2_ref, g2_ref, w0_ref, a0_ref, rkv_ref, ld_ref, ag_ref,
                    lerp_scr, tw_scr, ta_scr, tg_scr):
    i = pl.program_id(0)
    rw, ra = ranks

    @pl.when(pl.program_id(1) == 0)
    def _():
        h = _rms(x_ref[...], g_ref[...])
        prev = _rms(xprev_ref[...], g_ref[...])[SUBLANE - 1:SUBLANE, :]
        prev = jnp.where(i % seq_tiles == 0, 0.0, prev)
        row = lax.broadcasted_iota(jnp.int32, h.shape, 0)
        dx = jnp.where(row == 0, prev, pltpu.roll(h, 1, 0)) - h
        for idx in range(6):
            lerp_scr[idx] = (h + dx * mu_ref[idx:idx + 1, :]).astype(BF16)
        tw_scr[...] = jnp.tanh(jnp.dot(lerp_scr[3], l1_ref[:, :rw],
                                       preferred_element_type=F32)).astype(BF16)
        ta_scr[...] = jnp.dot(lerp_scr[4], l1_ref[:, rw:rw + ra],
                              preferred_element_type=F32).astype(BF16)
        tg_scr[...] = jax.nn.sigmoid(jnp.dot(lerp_scr[5], l1_ref[:, rw + ra:],
                                             preferred_element_type=F32)).astype(BF16)

    rkv_ref[0] = jnp.dot(lerp_scr[0], wr_ref[...], preferred_element_type=F32).astype(BF16)
    rkv_ref[1] = jnp.dot(lerp_scr[1], wk_ref[...], preferred_element_type=F32).astype(BF16)
    rkv_ref[2] = jnp.dot(lerp_scr[2], wv_ref[...], preferred_element_type=F32).astype(BF16)
    z = -(w0_ref[...] + jnp.dot(tw_scr[...], w2_ref[...], preferred_element_type=F32))
    softplus = jnp.maximum(z, 0.0) + jnp.log1p(jnp.exp(-jnp.abs(z)))
    ld_ref[...] = -jnp.exp(-softplus - 0.5)
    ag_ref[0] = jax.nn.sigmoid(
        a0_ref[...] + jnp.dot(ta_scr[...], a2_ref[...], preferred_element_type=F32)).astype(BF16)
    ag_ref[1] = jnp.dot(tg_scr[...], g2_ref[...], preferred_element_type=F32).astype(BF16)


def _pad_lora(wa, wb):
    rank = wa.shape[1]
    rp = -(-rank // LANE) * LANE
    return (jnp.pad(wa, ((0, 0), (0, rp - rank))).astype(BF16),
            jnp.pad(wb, ((0, rp - rank), (0, 0))).astype(BF16))


def _rwkv_proj(batch, x, norm_g, mu, w_rkv, w0, w1, w2, a0, a1, a2, g1, g2):
    m, d = x.shape
    tm = _pick_tile(m // batch, 512, SUBLANE)
    tn = _pick_tile(d, 256, LANE)
    w1p, w2p = _pad_lora(w1, w2)
    a1p, a2p = _pad_lora(a1, a2)
    g1p, g2p = _pad_lora(g1, g2)
    rw, ra, rg = w1p.shape[1], a1p.shape[1], g1p.shape[1]
    l1 = jnp.concatenate([w1p, a1p, g1p], axis=1)
    w_bf = _tile_cast("cast_w_rkv", w_rkv, [(t * tn, tn) for t in range(d // tn)], tn)
    mu_ord = mu[jnp.array([0, 2, 3, 1, 4, 5])]
    col = lambda i, j: (0, j)

    def w_plane(p):
        return pl.BlockSpec((None, None, d, tn), lambda i, j: (p, j, 0, 0))

    def planes(n):
        return pl.BlockSpec((n, tm, tn), lambda i, j: (0, i, j))

    body = functools.partial(_rwkv_proj_body, (m // batch) // tm, (rw, ra))
    return pl.pallas_call(
        body,
        grid=(m // tm, d // tn),
        in_specs=[
            pl.BlockSpec((tm, d), lambda i, j: (i, 0)),
            pl.BlockSpec((SUBLANE, d), lambda i, j: (jnp.maximum(i * (tm // SUBLANE) - 1, 0), 0)),
            pl.BlockSpec((1, d), lambda i, j: (0, 0)),
            pl.BlockSpec((6, d), lambda i, j: (0, 0)),
            w_plane(0), w_plane(1), w_plane(2),
            pl.BlockSpec((d, rw + ra + rg), lambda i, j: (0, 0)),
            pl.BlockSpec((rw, tn), col),
            pl.BlockSpec((ra, tn), col),
            pl.BlockSpec((rg, tn), col),
            pl.BlockSpec((1, tn), col),
            pl.BlockSpec((1, tn), col),
        ],
        out_specs=[planes(3), pl.BlockSpec((tm, tn), lambda i, j: (i, j)), planes(2)],
        out_shape=[jax.ShapeDtypeStruct((3, m, d), BF16), jax.ShapeDtypeStruct((m, d), F32),
                   jax.ShapeDtypeStruct((2, m, d), BF16)],
        scratch_shapes=[pltpu.VMEM((6, tm, d), BF16), pltpu.VMEM((tm, rw), BF16),
                        pltpu.VMEM((tm, ra), BF16), pltpu.VMEM((tm, rg), BF16)],
        compiler_params=_params(("parallel", "arbitrary")),
        name="rwkv_proj",
    )(x, x, norm_g.reshape(1, d), mu_ord, w_bf, w_bf, w_bf, l1, w2p, a2p, g2p,
      w0.reshape(1, d), a0.reshape(1, d))


def _bf(x):
    return x.astype(BF16)


def _dot(a, b):
    return jnp.dot(_bf(a), _bf(b), preferred_element_type=F32)


def _dot_nt(a, b):
    return lax.dot_general(_bf(a), _bf(b), (((1,), (1,)), ((), ())), preferred_element_type=F32)


def _dot_tn(a, b):
    return lax.dot_general(_bf(a), _bf(b), (((0,), (0,)), ((), ())), preferred_element_type=F32)


def _scan_body(n_streams, r_ref, ld_ref, k_ref, v_ref, al_ref, g_ref, kk_w_ref, ka_w_ref, rk_w_ref,
               lnw_ref, lnb_ref, o_ref, s_ref):
    c = SCAN_CHUNK
    w = HEADS_PER_GROUP * RWKV_HEAD
    wc = HEADS_PER_GROUP * c

    @pl.when(pl.program_id(2) == 0)
    def _():
        s_ref[...] = jnp.zeros_like(s_ref)

    row = lax.broadcasted_iota(jnp.int32, (c, w), 0)
    bd_r = lax.broadcasted_iota(jnp.int32, (w, w), 0) // RWKV_HEAD
    bd_c = lax.broadcasted_iota(jnp.int32, (w, w), 1) // RWKV_HEAD
    head_mask = bd_r == bd_c
    head_ones = jnp.where(head_mask, 1.0, 0.0).astype(BF16)
    t_idx = lax.broadcasted_iota(jnp.int32, (c, wc), 0)
    s_idx = lax.broadcasted_iota(jnp.int32, (c, wc), 1) % c
    strict = s_idx < t_idx
    incl = s_idx <= t_idx
    eye = jnp.where(s_idx == t_idx, 1.0, 0.0)

    def head_sum(x):
        return jnp.dot(_bf(x), head_ones, preferred_element_type=F32)

    def block_diag(x):
        xb = _bf(x)
        return jnp.where(head_mask, jnp.concatenate([xb] * HEADS_PER_GROUP, axis=0),
                         jnp.zeros((), BF16))

    def stream(si):
        ln = pl.ds(si * w, w)
        r = r_ref[:, ln].astype(F32)
        ld = ld_ref[:, ln]
        k = k_ref[:, ln].astype(F32)
        v = v_ref[:, ln].astype(F32)
        al = al_ref[:, ln].astype(F32)

        cum = ld
        shift = 1
        while shift < c:
            cum = cum + jnp.where(row >= shift, pltpu.roll(cum, shift, 0), 0.0)
            shift *= 2
        p_incl = jnp.exp(cum)
        p_prev = jnp.exp(cum - ld)
        p_inv = jnp.exp(-cum)
        p_last = p_incl[c - 1:c, :]

        kk = k * kk_w_ref[:, ln]
        ss = head_sum(kk * kk)
        yield
        kk = kk / jnp.maximum(jnp.sqrt(ss), 1e-12)
        kmod = k * (1.0 + (al - 1.0) * ka_w_ref[:, ln])

        a_t = -kk * p_prev
        b_t = kk * al * p_inv
        k_t = kmod * p_inv
        r_t = r * p_incl

        ar = jnp.concatenate([a_t, r_t], axis=0)
        m_b = _dot_nt(ar, block_diag(b_t))
        m_k = _dot_nt(ar, block_diag(k_t))
        yield
        a_ab = jnp.where(strict, m_b[:c], 0.0)
        a_rb = jnp.where(incl, m_b[c:], 0.0)
        a_ak = jnp.where(strict, m_k[:c], 0.0)
        a_rk = jnp.where(incl, m_k[c:], 0.0)

        base_level = 3
        pw = jnp.where((t_idx >> base_level) == (s_idx >> base_level), a_ab, 0.0)
        t_inv = eye
        for step in range(base_level):
            if step < base_level - 1:
                z = _dot(jnp.concatenate([pw, t_inv], axis=0), block_diag(pw))
                yield
                pw, t_inv = z[:c], t_inv + z[c:]
            else:
                z = _dot(t_inv, block_diag(pw))
                yield
                t_inv = t_inv + z
        n_bd = block_diag(a_ab)
        level = base_level
        while (1 << level) < c:
            tb = t_idx >> level
            sb = s_idx >> level
            sel = ((tb & 1) == 1) & (sb == tb - 1)
            tn = _dot(t_inv, n_bd)
            yield
            x = _dot(tn, block_diag(t_inv))
            yield
            t_inv = t_inv + jnp.where(sel, x, 0.0)
            level += 1

        s0 = s_ref[si]
        w1 = _dot_nt(ar, s0)
        w2 = _dot(jnp.concatenate([a_ak, a_rk], axis=0), block_diag(v))
        yield
        u = _dot(t_inv, block_diag(w1[:c] + w2[:c]))
        yield
        y = w1[c:] + w2[c:] + _dot(a_rb, block_diag(u))

        uv = jnp.concatenate([u, v], axis=0)
        bk = jnp.concatenate([b_t, k_t], axis=0) * p_last
        s_new = _dot_tn(uv, bk)
        yield
        s_ref[si] = jnp.where(head_mask, s0 * p_last + s_new, 0.0)

        inv_n = 1.0 / RWKV_HEAD
        sums = head_sum(jnp.concatenate([y, r * kmod * rk_w_ref[:, ln]], axis=0))
        yield
        dlt = y - sums[:c] * inv_n
        bsum = sums[c:]
        var = head_sum(dlt * dlt) * inv_n
        yield
        yn = dlt * lax.rsqrt(var + GN_EPS) * lnw_ref[:, ln] + lnb_ref[:, ln]
        bonus = bsum * v
        o_ref[:, ln] = ((yn + bonus) * g_ref[:, ln]).astype(o_ref.dtype)

    _round_robin([stream(si) for si in range(n_streams)])


def _rwkv_scan(batch, rkv, ld, ag, k_k, k_a, r_k, ln_w, ln_b):
    _, m, d = rkv.shape
    t = m // batch
    c = SCAN_CHUNK
    w = HEADS_PER_GROUP * RWKV_HEAD
    assert d % w == 0 and t % c == 0 and c == RWKV_HEAD
    n_streams = _pick_tile(d // w, SCAN_STREAMS, 1)
    wb = n_streams * w
    nc = t // c
    tok = pl.BlockSpec((c, wb), lambda b, gi, ci: (b * nc + ci, gi))
    par = pl.BlockSpec((1, wb), lambda b, gi, ci: (0, gi))

    def plane(p):
        return pl.BlockSpec((None, c, wb), lambda b, gi, ci: (p, b * nc + ci, gi))

    return pl.pallas_call(
        functools.partial(_scan_body, n_streams),
        grid=(batch, d // wb, nc),
        in_specs=[plane(0), tok, plane(1), plane(2), plane(0), plane(1)] + [par] * 5,
        out_specs=tok,
        out_shape=jax.ShapeDtypeStruct((m, d), BF16),
        scratch_shapes=[pltpu.VMEM((n_streams, w, w), F32)],
        compiler_params=_params(("parallel", "parallel", "arbitrary")),
        name="rwkv_scan",
    )(rkv, ld, rkv, rkv, ag, ag, *[p.reshape(1, d) for p in (k_k, k_a, r_k, ln_w, ln_b)])


def _rwkv_mix(batch, x, norm_g, mu, w_rkv, w0, w1, w2, a0, a1, a2, g1, g2,
              k_k, k_a, r_k, ln_w, ln_b, w_o):
    rkv, ld, ag = _rwkv_proj(batch, x, norm_g, mu, w_rkv, w0, w1, w2, a0, a1, a2, g1, g2)
    yg = _rwkv_scan(batch, rkv, ld, ag, k_k, k_a, r_k.reshape(-1), ln_w, ln_b)
    return _simple_mm("rwkv_out", [_rows(yg)], lambda t: t, w_o.astype(BF16), [_epi_tile(x)],
                      lambda acc, x_t: (x_t + acc,), tn_target=x.shape[1])


def _attn_body(tq, tk, nh, q_ref, k_ref, v_ref, o_ref):
    qi = pl.program_id(2)
    dv = v_ref.shape[-1]

    def head_step(h, j, carry, masked, out):
        m_i, l_i, acc = carry
        start = pl.multiple_of(j * tk, tk)
        kb = k_ref[h, pl.ds(start, tk), :]
        s = lax.dot_general(q_ref[h], kb, (((1,), (1,)), ((), ())), preferred_element_type=F32)
        yield
        if masked:
            q_chunk = (qi * tq + lax.broadcasted_iota(jnp.int32, (tq, tk), 0)) // ATTN_CHUNK
            k_chunk = (j * tk + lax.broadcasted_iota(jnp.int32, (tq, tk), 1)) // ATTN_CHUNK
            s = jnp.where(k_chunk <= q_chunk, s, jnp.finfo(F32).min)
        m_new = jnp.maximum(m_i, jnp.max(s, axis=-1, keepdims=True))
        p = jnp.exp(s - m_new)
        alpha = jnp.exp(m_i - m_new)
        l_new = alpha * l_i + jnp.sum(p, axis=-1, keepdims=True)
        pv = jnp.dot(p.astype(BF16), v_ref[h, pl.ds(start, tk), :], preferred_element_type=F32)
        yield
        out.append((m_new, l_new, alpha * acc + pv))

    def step(j, carries, masked):
        outs = [[] for _ in range(nh)]
        _round_robin([head_step(h, j, carries[h], masked, outs[h]) for h in range(nh)])
        return tuple(o[0] for o in outs)

    init = tuple((jnp.full((tq, 1), -jnp.inf, F32), jnp.zeros((tq, 1), F32),
                  jnp.zeros((tq, dv), F32)) for _ in range(nh))
    carries = lax.fori_loop(0, qi, lambda j, cr: step(j, cr, False), init)
    carries = step(qi, carries, True)
    for h, (_, l_i, acc) in enumerate(carries):
        o_ref[:, h * dv:(h + 1) * dv] = (acc / l_i).astype(o_ref.dtype)


def _attention(batch, q, k, v):
    heads, m, dqk = q.shape
    dv = v.shape[-1]
    t = m // batch
    tq = _pick_tile(t, ATTN_Q_TILE, ATTN_CHUNK)
    nq = t // tq
    nh = _pick_tile(heads, ATTN_HEADS_PER_STEP, 1)
    return pl.pallas_call(
        functools.partial(_attn_body, tq, tq, nh),
        grid=(batch, heads // nh, nq),
        in_specs=[
            pl.BlockSpec((nh, tq, dqk), lambda b, h, i: (h, b * nq + i, 0)),
            pl.BlockSpec((nh, t, dqk), lambda b, h, i: (h, b, 0)),
            pl.BlockSpec((nh, t, dv), lambda b, h, i: (h, b, 0)),
        ],
        out_specs=pl.BlockSpec((tq, nh * dv), lambda b, h, i: (b * nq + i, h)),
        out_shape=jax.ShapeDtypeStruct((m, heads * dv), BF16),
        compiler_params=_params(("parallel", "parallel", "arbitrary")),
        name="mla_attention",
    )(q, k, v)


def _mla_mix(batch, x, positions, norm_g, w_down, q_a_norm, kv_a_norm, w_uq, w_ukv,
             q_norm, k_norm, w_o):
    m, d = x.shape
    q_lora = q_a_norm.shape[0]
    kv_lora = kv_a_norm.shape[0]
    heads, qk_dim = w_uq.shape[1], w_uq.shape[2]
    rope = w_down.shape[1] - q_lora - kv_lora
    nope = qk_dim - rope
    dv = w_ukv.shape[2] - nope
    half = rope // 2
    assert nope == LANE and dv == LANE and rope <= LANE and q_lora == kv_lora and q_lora % LANE == 0
    scale = 1.0 / math.sqrt(qk_dim)

    perm = jnp.concatenate([jnp.arange(half, rope), jnp.arange(0, half)])
    lane_pad = lambda a: jnp.pad(a, [(0, 0)] * (a.ndim - 1) + [(0, LANE - a.shape[-1])])

    inv_freq = ROPE_THETA ** (-jnp.arange(0, rope, 2, dtype=F32) / rope)
    ang = positions.reshape(m, 1).astype(F32) * inv_freq
    cos, sin = jnp.cos(ang), jnp.sin(ang)
    cos_t = lane_pad(jnp.concatenate([cos, cos], axis=-1))
    sin_t = lane_pad(jnp.concatenate([-sin, sin], axis=-1))

    w_kpe = w_down[:, q_lora + kv_lora:]
    w_down_x = jnp.concatenate(
        [w_down[:, :q_lora + kv_lora], lane_pad(w_kpe), lane_pad(w_kpe[:, perm])], axis=1).astype(BF16)
    c = _simple_mm("mla_down", [_rows(x), _vec(norm_g)], _rms, w_down_x, [], lambda acc: (acc,),
                   tn_target=w_down_x.shape[1])
    qb = q_lora // q_lora
    tm = _pick_tile(m, 512, 8)

    def norm_rope(body, pe, rot, g_body, g_pe, g_rot, cos_b, sin_b, extra_scale):
        ss = jnp.sum(body * body, axis=-1, keepdims=True) + jnp.sum(pe * pe, axis=-1, keepdims=True)
        s = lax.rsqrt(ss / qk_dim + RMS_EPS) * extra_scale
        return jnp.concatenate([body * s * g_body, (pe * g_pe * cos_b + rot * g_rot * sin_b) * s], axis=-1)

    w_q_pe = w_uq[:, :, nope:]
    w_q_x = jnp.concatenate([w_uq[:, :, :nope], lane_pad(w_q_pe), lane_pad(w_q_pe[:, :, perm])],
                            axis=-1).reshape(q_lora, heads * 3 * LANE).astype(BF16)
    gq = jnp.concatenate([q_norm[:nope], lane_pad(q_norm[nope:]), lane_pad(q_norm[nope:][perm])]).reshape(1, -1)

    hb = _pick_tile(heads, MLA_HEADS_PER_STEP, 1)

    def q_epi(acc, g_t, cos_b, sin_b):
        per_head = []
        for hh in range(hb):
            a = acc[:, hh * 3 * LANE:(hh + 1) * 3 * LANE]
            per_head.append(norm_rope(a[:, :LANE], a[:, LANE:2 * LANE], a[:, 2 * LANE:], g_t[:, :LANE],
                                      g_t[:, LANE:2 * LANE], g_t[:, 2 * LANE:], cos_b, sin_b, scale))
        return (jnp.stack(per_head, axis=0),)

    q = _mm("mla_q", m,
            [(c, _row_spec(tm, q_lora, 0)), (q_a_norm.reshape(1, -1), _vec_spec(q_lora))], _rms, w_q_x,
            [(gq, _vec_spec(3 * LANE)), (cos_t, _row_spec(tm, LANE)), (sin_t, _row_spec(tm, LANE))], q_epi,
            [(jax.ShapeDtypeStruct((heads, m, 2 * LANE), BF16),
              pl.BlockSpec((hb, tm, 2 * LANE), lambda i, j: (j, i, 0)))],
            tm=tm, tn=hb * 3 * LANE)[0]

    w_kv_x = w_ukv.reshape(kv_lora, heads * (nope + dv)).astype(BF16)
    gk = jnp.concatenate([k_norm[:nope], lane_pad(k_norm[nope:]), lane_pad(k_norm[nope:][perm])]).reshape(1, -1)
    pe_block = (q_lora + kv_lora) // LANE

    def kv_epi(acc, g_t, pe, rot, cos_b, sin_b):
        k_out, v_out = [], []
        for hh in range(hb):
            a = acc[:, hh * (nope + dv):(hh + 1) * (nope + dv)]
            k_out.append(norm_rope(a[:, :nope], pe, rot, g_t[:, :LANE], g_t[:, LANE:2 * LANE],
                                   g_t[:, 2 * LANE:], cos_b, sin_b, 1.0))
            v_out.append(a[:, nope:])
        return jnp.stack(k_out, axis=0), jnp.stack(v_out, axis=0)

    k, v = _mm("mla_kv", m,
               [(c, _row_spec(tm, kv_lora, qb)), (kv_a_norm.reshape(1, -1), _vec_spec(kv_lora))], _rms, w_kv_x,
               [(gk, _vec_spec(3 * LANE)), (c, _row_spec(tm, LANE, pe_block)),
                (c, _row_spec(tm, LANE, pe_block + 1)), (cos_t, _row_spec(tm, LANE)),
                (sin_t, _row_spec(tm, LANE))], kv_epi,
               [(jax.ShapeDtypeStruct((heads, m, 2 * LANE), BF16),
                 pl.BlockSpec((hb, tm, 2 * LANE), lambda i, j: (j, i, 0))),
                (jax.ShapeDtypeStruct((heads, m, dv), BF16),
                 pl.BlockSpec((hb, tm, dv), lambda i, j: (j, i, 0)))],
               tm=tm, tn=hb * (nope + dv))

    o = _attention(batch, q, k, v)
    return _simple_mm("mla_out", [_rows(o)], lambda t: t, w_o.reshape(heads * dv, d).astype(BF16),
                      [_epi_tile(x)], lambda acc, x_t: (x_t + acc,), tn_target=d)


def kernel(x, positions, ffn_norm, ffn_w13, ffn_w2, mix_norm, rwkv_mu, rwkv_w_rkv, rwkv_w0, rwkv_w1, rwkv_w2, rwkv_a0, rwkv_a1, rwkv_a2, rwkv_g1, rwkv_g2, rwkv_k_k, rwkv_k_a, rwkv_r_k, rwkv_ln_w, rwkv_ln_b, rwkv_w_o, mla_w_down, mla_q_a_norm, mla_kv_a_norm, mla_w_uq, mla_w_ukv, mla_q_norm, mla_k_norm, mla_w_o):
    batch, seq, d = x.shape
    depth = ffn_norm.shape[0]
    n_mixers = 2
    xf = x.reshape(batch * seq, d)
    d_ff = ffn_w2.shape[2]
    w13p, w2p = _ffn_weights(ffn_w13.reshape((2 * depth,) + ffn_w13.shape[2:]),
                             ffn_w2.reshape((2 * depth,) + ffn_w2.shape[2:]))
    for i in range(depth):
        j = i // n_mixers
        xf = _ffn(xf, ffn_norm[i, 0], w13p, w2p, 2 * i, d_ff)
        if i % n_mixers == 0:
            xf = _rwkv_mix(batch, xf, mix_norm[i], rwkv_mu[j], rwkv_w_rkv[j], rwkv_w0[j], rwkv_w1[j],
                           rwkv_w2[j], rwkv_a0[j], rwkv_a1[j], rwkv_a2[j], rwkv_g1[j], rwkv_g2[j],
                           rwkv_k_k[j], rwkv_k_a[j], rwkv_r_k[j], rwkv_ln_w[j], rwkv_ln_b[j],
                           rwkv_w_o[j])
        else:
            xf = _mla_mix(batch, xf, positions, mix_norm[i], mla_w_down[j], mla_q_a_norm[j],
                          mla_kv_a_norm[j], mla_w_uq[j], mla_w_ukv[j], mla_q_norm[j], mla_k_norm[j],
                          mla_w_o[j])
        xf = _ffn(xf, ffn_norm[i, 1], w13p, w2p, 2 * i + 1, d_ff)
    return xf.reshape(batch, seq, d)
```

```python
import functools
import math

import jax
import jax.numpy as jnp
from jax import lax
from jax.experimental import pallas as pl
from jax.experimental.pallas import tpu as pltpu

F32 = jnp.float32
BF16 = jnp.bfloat16

RMS_EPS = 1e-6
GN_EPS = 64e-5
ROPE_THETA = 10000.0
ATTN_CHUNK = 64
RWKV_HEAD = 64
SCAN_CHUNK = 64
HEADS_PER_GROUP = 4
MAX_PLAIN_SCORE = 30.0
ATTN_Q_TILE = 512
ATTN_HEADS_PER_STEP = 4
FFN_TILE = 512
MLA_HEADS_PER_STEP = 4
SCAN_STREAMS = 8
LANE = 128
SUBLANE = 8
VMEM_LIMIT_BYTES = 56 * 1024 * 1024


def _params(semantics):
    return pltpu.CompilerParams(dimension_semantics=semantics,
                                vmem_limit_bytes=VMEM_LIMIT_BYTES)


def _pick_tile(n, target, quantum):
    if n <= target:
        return n
    t = (target // quantum) * quantum
    while t > quantum and n % t:
        t -= quantum
    assert n % t == 0, (n, target, quantum)
    return t


def _round_robin(gens):
    live = list(gens)
    while live:
        live = [g for g in live if next(g, True) is None]


def _rms(x, g):
    return x * lax.rsqrt(jnp.mean(x * x, axis=-1, keepdims=True) + RMS_EPS) * g


def _ffn_body(nj, x_ref, xnext_ref, g_ref, wg_ref, wu_ref, w2_ref, o_ref, xn_ref):
    i = pl.program_id(0)
    j = pl.program_id(1)
    slot = i % 2

    @pl.when((i == 0) & (j == 0))
    def _():
        xn_ref[0] = _rms(x_ref[...], g_ref[...]).astype(BF16)

    def tile_sum():
        xn = xn_ref[slot]
        gate = jnp.dot(xn, wg_ref[...], preferred_element_type=F32)
        up = jnp.dot(xn, wu_ref[...], preferred_element_type=F32)
        act = (gate * jax.nn.sigmoid(gate) * up).astype(BF16)
        return jnp.dot(act, w2_ref[...], preferred_element_type=F32)

    def finish(total):
        o_ref[...] = x_ref[...] + 0.5 * total
        xn_ref[1 - slot] = _rms(xnext_ref[...], g_ref[...]).astype(BF16)

    if nj == 1:
        finish(tile_sum())
    else:
        @pl.when(j == 0)
        def _():
            o_ref[...] = tile_sum()

        if nj > 2:
            @pl.when((j > 0) & (j < nj - 1))
            def _():
                o_ref[...] += tile_sum()

        @pl.when(j == nj - 1)
        def _():
            finish(o_ref[...] + tile_sum())


def _ffn_tile(f):
    return min(FFN_TILE, -(-f // LANE) * LANE)


def _tile_cast_body(tiles, x_ref, o_ref):
    tw = o_ref.shape[-1]
    for t, (c0, width) in enumerate(tiles):
        o_ref[t, :, :width] = x_ref[:, c0:c0 + width].astype(BF16)
        if width < tw:
            o_ref[t, :, width:] = jnp.zeros((o_ref.shape[1], tw - width), BF16)


def _tile_cast(name, w, tiles, tw):
    n, rows, cols = w.shape
    tr = _pick_tile(rows, 128, SUBLANE)
    return pl.pallas_call(
        functools.partial(_tile_cast_body, tuple(tiles)),
        grid=(n, rows // tr),
        in_specs=[pl.BlockSpec((None, tr, cols), lambda l, i: (l, i, 0))],
        out_specs=pl.BlockSpec((None, len(tiles), tr, tw), lambda l, i: (l, 0, i, 0)),
        out_shape=jax.ShapeDtypeStruct((n, len(tiles), rows, tw), BF16),
        compiler_params=_params(("parallel", "parallel")),
        name=name,
    )(w)


def _cast_w2_body(f, x_ref, o_ref):
    tr = o_ref.shape[0]
    row = pl.program_id(1) * tr + lax.broadcasted_iota(jnp.int32, o_ref.shape, 0)
    o_ref[...] = jnp.where(row < f, x_ref[...], 0.0).astype(BF16)


def _ffn_weights(w13, w2):
    n, d, f2 = w13.shape
    f = f2 // 2
    assert f % LANE == 0
    tf = _ffn_tile(f)
    fp = -(-f // tf) * tf
    half = [(t * tf, min(tf, f - t * tf)) for t in range(fp // tf)]
    w13p = _tile_cast("cast_w13", w13, half + [(f + c0, width) for c0, width in half], tf)
    w2p = pl.pallas_call(
        functools.partial(_cast_w2_body, f),
        grid=(n, fp // tf),
        in_specs=[pl.BlockSpec((None, tf, d), lambda l, i: (l, i, 0))],
        out_specs=pl.BlockSpec((None, tf, d), lambda l, i: (l, i, 0)),
        out_shape=jax.ShapeDtypeStruct((n, fp, d), BF16),
        compiler_params=_params(("parallel", "parallel")),
        name="cast_w2",
    )(w2)
    return w13p, w2p


def _ffn(x, g, w13p, w2p, layer, f):
    m, d = x.shape
    fp = w2p.shape[1]
    tf = _ffn_tile(f)
    nj = fp // tf
    tm = _pick_tile(m, 512, 8)
    ni = m // tm
    return pl.pallas_call(
        functools.partial(_ffn_body, nj),
        grid=(ni, nj),
        in_specs=[
            pl.BlockSpec((tm, d), lambda i, j: (i, 0)),
            pl.BlockSpec((tm, d), lambda i, j: (jnp.minimum(i + 1, ni - 1), 0)),
            pl.BlockSpec((1, d), lambda i, j: (0, 0)),
            pl.BlockSpec((None, None, d, tf), lambda i, j: (layer, j, 0, 0)),
            pl.BlockSpec((None, None, d, tf), lambda i, j: (layer, j + nj, 0, 0)),
            pl.BlockSpec((None, tf, d), lambda i, j: (layer, j, 0)),
        ],
        out_specs=pl.BlockSpec((tm, d), lambda i, j: (i, 0)),
        out_shape=jax.ShapeDtypeStruct((m, d), F32),
        scratch_shapes=[pltpu.VMEM((2, tm, d), BF16)],
        compiler_params=_params(("arbitrary", "arbitrary")),
        name="ffn",
    )(x, x, g.reshape(1, d), w13p, w13p, w2p)


def _mm_body(n_lhs, n_epi, n_out, lhs_fn, epi_fn, *refs):
    lhs_refs = refs[:n_lhs]
    w_ref = refs[n_lhs]
    epi_refs = refs[n_lhs + 1:n_lhs + 1 + n_epi]
    out_refs = refs[n_lhs + 1 + n_epi:n_lhs + 1 + n_epi + n_out]
    lhs_scratch = refs[-1]

    @pl.when(pl.program_id(1) == 0)
    def _():
        lhs_scratch[...] = lhs_fn(*[r[...] for r in lhs_refs]).astype(BF16)

    acc = jnp.dot(lhs_scratch[...], w_ref[...], preferred_element_type=F32)
    outs = epi_fn(acc, *[r[...] for r in epi_refs])
    for o_ref, o in zip(out_refs, outs):
        o_ref[...] = o.astype(o_ref.dtype)


def _mm(name, m, lhs, lhs_fn, w, epi, epi_fn, outs, *, tm, tn):
    k, n = w.shape
    body = functools.partial(_mm_body, len(lhs), len(epi), len(outs), lhs_fn, epi_fn)
    return pl.pallas_call(
        body,
        grid=(m // tm, n // tn),
        in_specs=[s for _, s in lhs] + [pl.BlockSpec((k, tn), lambda i, j: (0, j))]
        + [s for _, s in epi],
        out_specs=[s for _, s in outs],
        out_shape=[s for s, _ in outs],
        scratch_shapes=[pltpu.VMEM((tm, k), BF16)],
        compiler_params=_params(("parallel", "arbitrary")),
        name=name,
    )(*[a for a, _ in lhs], w, *[a for a, _ in epi])


def _row_spec(tm, width, col_block=0):
    return pl.BlockSpec((tm, width), lambda i, j: (i, col_block))


def _tile_spec(tm, tn):
    return pl.BlockSpec((tm, tn), lambda i, j: (i, j))


def _vec_spec(width):
    return pl.BlockSpec((1, width), lambda i, j: (0, 0))


def _colvec_spec(tn):
    return pl.BlockSpec((1, tn), lambda i, j: (0, j))


def _simple_mm(name, lhs, lhs_fn, w, epi, epi_fn, out_dtype=F32, tm_target=512, tn_target=1024):
    m = lhs[0][0].shape[0]
    n = w.shape[1]
    tm = _pick_tile(m, tm_target, 8)
    tn = _pick_tile(n, tn_target, LANE)
    lhs = [(a, spec(tm)) for a, spec in lhs]
    epi = [(a, spec(tm, tn)) for a, spec in epi]
    out = (jax.ShapeDtypeStruct((m, n), out_dtype), _tile_spec(tm, tn))
    return _mm(name, m, lhs, lhs_fn, w, epi, epi_fn, [out], tm=tm, tn=tn)[0]


def _rows(a):
    return a, lambda tm: _row_spec(tm, a.shape[1])


def _vec(a):
    a = a.reshape(1, -1)
    return a, lambda tm: _vec_spec(a.shape[1])


def _epi_tile(a):
    return a, lambda tm, tn: _tile_spec(tm, tn)


def _epi_colvec(a):
    return a.reshape(1, -1), lambda tm, tn: _colvec_spec(tn)


def _rwkv_proj_body(seq_tiles, ranks, x_ref, xprev_ref, g_ref, mu_ref, wr_ref, wk_ref, wv_ref, l1_ref,
                    w2_ref, a2_ref, g2_ref, w0_ref, a0_ref, rkv_ref, ld_ref, ag_ref,
                    lerp_scr, tw_scr, ta_scr, tg_scr):
    i = pl.program_id(0)
    rw, ra = ranks

    @pl.when(pl.program_id(1) == 0)
    def _():
        h = _rms(x_ref[...], g_ref[...])
        prev = _rms(xprev_ref[...], g_ref[...])[SUBLANE - 1:SUBLANE, :]
        prev = jnp.where(i % seq_tiles == 0, 0.0, prev)
        row = lax.broadcasted_iota(jnp.int32, h.shape, 0)
        dx = jnp.where(row == 0, prev, pltpu.roll(h, 1, 0)) - h
        for idx in range(6):
            lerp_scr[idx] = (h + dx * mu_ref[idx:idx + 1, :]).astype(BF16)
        tw_scr[...] = jnp.tanh(jnp.dot(lerp_scr[3], l1_ref[:, :rw],
                                       preferred_element_type=F32)).astype(BF16)
        ta_scr[...] = jnp.dot(lerp_scr[4], l1_ref[:, rw:rw + ra],
                              preferred_element_type=F32).astype(BF16)
        tg_scr[...] = jax.nn.sigmoid(jnp.dot(lerp_scr[5], l1_ref[:, rw + ra:],
                                             preferred_element_type=F32)).astype(BF16)

    rkv_ref[0] = jnp.dot(lerp_scr[0], wr_ref[...], preferred_element_type=F32).astype(BF16)
    rkv_ref[1] = jnp.dot(lerp_scr[1], wk_ref[...], preferred_element_type=F32).astype(BF16)
    rkv_ref[2] = jnp.dot(lerp_scr[2], wv_ref[...], preferred_element_type=F32).astype(BF16)
    z = -(w0_ref[...] + jnp.dot(tw_scr[...], w2_ref[...], preferred_element_type=F32))
    softplus = jnp.maximum(z, 0.0) + jnp.log1p(jnp.exp(-jnp.abs(z)))
    ld_ref[...] = -jnp.exp(-softplus - 0.5)
    ag_ref[0] = jax.nn.sigmoid(
        a0_ref[...] + jnp.dot(ta_scr[...], a2_ref[...], preferred_element_type=F32)).astype(BF16)
    ag_ref[1] = jnp.dot(tg_scr[...], g2_ref[...], preferred_element_type=F32).astype(BF16)


def _pad_lora(wa, wb):
    rank = wa.shape[1]
    rp = -(-rank // LANE) * LANE
    return (jnp.pad(wa, ((0, 0), (0, rp - rank))).astype(BF16),
            jnp.pad(wb, ((0, rp - rank), (0, 0))).astype(BF16))


def _rwkv_proj(batch, x, norm_g, mu, w_rkv, w0, w1, w2, a0, a1, a2, g1, g2):
    m, d = x.shape
    tm = _pick_tile(m // batch, 512, SUBLANE)
    tn = _pick_tile(d, 256, LANE)
    w1p, w2p = _pad_lora(w1, w2)
    a1p, a2p = _pad_lora(a1, a2)
    g1p, g2p = _pad_lora(g1, g2)
    rw, ra, rg = w1p.shape[1], a1p.shape[1], g1p.shape[1]
    l1 = jnp.concatenate([w1p, a1p, g1p], axis=1)
    w_bf = _tile_cast("cast_w_rkv", w_rkv, [(t * tn, tn) for t in range(d // tn)], tn)
    mu_ord = mu[jnp.array([0, 2, 3, 1, 4, 5])]
    col = lambda i, j: (0, j)

    def w_plane(p):
        return pl.BlockSpec((None, None, d, tn), lambda i, j: (p, j, 0, 0))

    def planes(n):
        return pl.BlockSpec((n, tm, tn), lambda i, j: (0, i, j))

    body = functools.partial(_rwkv_proj_body, (m // batch) // tm, (rw, ra))
    return pl.pallas_call(
        body,
        grid=(m // tm, d // tn),
        in_specs=[
            pl.BlockSpec((tm, d), lambda i, j: (i, 0)),
            pl.BlockSpec((SUBLANE, d), lambda i, j: (jnp.maximum(i * (tm // SUBLANE) - 1, 0), 0)),
            pl.BlockSpec((1, d), lambda i, j: (0, 0)),
            pl.BlockSpec((6, d), lambda i, j: (0, 0)),
            w_plane(0), w_plane(1), w_plane(2),
            pl.BlockSpec((d, rw + ra + rg), lambda i, j: (0, 0)),
            pl.BlockSpec((rw, tn), col),
            pl.BlockSpec((ra, tn), col),
            pl.BlockSpec((rg, tn), col),
            pl.BlockSpec((1, tn), col),
            pl.BlockSpec((1, tn), col),
        ],
        out_specs=[planes(3), pl.BlockSpec((tm, tn), lambda i, j: (i, j)), planes(2)],
        out_shape=[jax.ShapeDtypeStruct((3, m, d), BF16), jax.ShapeDtypeStruct((m, d), F32),
                   jax.ShapeDtypeStruct((2, m, d), BF16)],
        scratch_shapes=[pltpu.VMEM((6, tm, d), BF16), pltpu.VMEM((tm, rw), BF16),
                        pltpu.VMEM((tm, ra), BF16), pltpu.VMEM((tm, rg), BF16)],
        compiler_params=_params(("parallel", "arbitrary")),
        name="rwkv_proj",
    )(x, x, norm_g.reshape(1, d), mu_ord, w_bf, w_bf, w_bf, l1, w2p, a2p, g2p,
      w0.reshape(1, d), a0.reshape(1, d))


def _bf(x):
    return x.astype(BF16)


def _dot(a, b):
    return jnp.dot(_bf(a), _bf(b), preferred_element_type=F32)


def _dot_nt(a, b):
    return lax.dot_general(_bf(a), _bf(b), (((1,), (1,)), ((), ())), preferred_element_type=F32)


def _dot_tn(a, b):
    return lax.dot_general(_bf(a), _bf(b), (((0,), (0,)), ((), ())), preferred_element_type=F32)


def _scan_body(n_streams, r_ref, ld_ref, k_ref, v_ref, al_ref, g_ref, kk_w_ref, ka_w_ref, rk_w_ref,
               lnw_ref, lnb_ref, o_ref, s_ref):
    c = SCAN_CHUNK
    w = HEADS_PER_GROUP * RWKV_HEAD
    wc = HEADS_PER_GROUP * c

    @pl.when(pl.program_id(2) == 0)
    def _():
        s_ref[...] = jnp.zeros_like(s_ref)

    tri = jnp.where(lax.broadcasted_iota(jnp.int32, (c, c), 0)
                    >= lax.broadcasted_iota(jnp.int32, (c, c), 1), 1.0, 0.0).astype(BF16)
    bd_r = lax.broadcasted_iota(jnp.int32, (w, w), 0) // RWKV_HEAD
    bd_c = lax.broadcasted_iota(jnp.int32, (w, w), 1) // RWKV_HEAD
    head_mask = bd_r == bd_c
    head_ones = jnp.where(head_mask, 1.0, 0.0).astype(BF16)
    t_idx = lax.broadcasted_iota(jnp.int32, (c, wc), 0)
    s_idx = lax.broadcasted_iota(jnp.int32, (c, wc), 1) % c
    strict = s_idx < t_idx
    incl = s_idx <= t_idx
    eye = jnp.where(s_idx == t_idx, 1.0, 0.0)
    base_level = 3
    base_blocks = (t_idx >> base_level) == (s_idx >> base_level)
    join = {lv: (((t_idx >> lv) & 1) == 1) & ((s_idx >> lv) == (t_idx >> lv) - 1)
            for lv in range(base_level, c.bit_length() - 1)}

    def head_sum(x):
        return jnp.dot(_bf(x), head_ones, preferred_element_type=F32)

    def block_diag(x):
        xb = _bf(x)
        return jnp.where(head_mask, jnp.concatenate([xb] * HEADS_PER_GROUP, axis=0),
                         jnp.zeros((), BF16))

    def stream(si):
        ln = pl.ds(si * w, w)
        r = r_ref[:, ln].astype(F32)
        ld = ld_ref[:, ln]
        k = k_ref[:, ln].astype(F32)
        v = v_ref[:, ln].astype(F32)
        al = al_ref[:, ln].astype(F32)

        ld_hi = _bf(ld)
        rest = ld - ld_hi.astype(F32)
        ld_mid = _bf(rest)
        ld_lo = _bf(rest - ld_mid.astype(F32))
        cum = (jnp.dot(tri, ld_hi, preferred_element_type=F32)
               + jnp.dot(tri, ld_mid, preferred_element_type=F32)
               + jnp.dot(tri, ld_lo, preferred_element_type=F32))
        yield
        p_incl = jnp.exp(cum)
        p_prev = jnp.exp(cum - ld)
        p_inv = jnp.exp(-cum)
        p_last = p_incl[c - 1:c, :]

        kk = k * kk_w_ref[:, ln]
        ss = head_sum(kk * kk)
        yield
        kk = kk / jnp.maximum(jnp.sqrt(ss), 1e-12)
        kmod = k * (1.0 + (al - 1.0) * ka_w_ref[:, ln])

        a_t = -kk * p_prev
        b_t = kk * al * p_inv
        k_t = kmod * p_inv
        r_t = r * p_incl

        ar = jnp.concatenate([a_t, r_t], axis=0)
        m_b = _dot_nt(ar, block_diag(b_t))
        m_k = _dot_nt(ar, block_diag(k_t))
        yield
        a_ab = jnp.where(strict, m_b[:c], 0.0)
        a_rb = jnp.where(incl, m_b[c:], 0.0)
        a_ak = jnp.where(strict, m_k[:c], 0.0)
        a_rk = jnp.where(incl, m_k[c:], 0.0)

        pw = jnp.where(base_blocks, a_ab, 0.0)
        t_inv = eye
        for step in range(base_level):
            if step < base_level - 1:
                z = _dot(jnp.concatenate([pw, t_inv], axis=0), block_diag(pw))
                yield
                pw, t_inv = z[:c], t_inv + z[c:]
            else:
                z = _dot(t_inv, block_diag(pw))
                yield
                t_inv = t_inv + z
        n_bd = block_diag(a_ab)
        level = base_level
        while (1 << level) < c:
            tn = _dot(t_inv, n_bd)
            yield
            x = _dot(tn, block_diag(t_inv))
            yield
            t_inv = t_inv + jnp.where(join[level], x, 0.0)
            level += 1

        s0 = s_ref[si]
        w1 = _dot_nt(ar, s0)
        w2 = _dot(jnp.concatenate([a_ak, a_rk], axis=0), block_diag(v))
        yield
        u = _dot(t_inv, block_diag(w1[:c] + w2[:c]))
        yield
        y = w1[c:] + w2[c:] + _dot(a_rb, block_diag(u))

        uv = jnp.concatenate([u, v], axis=0)
        bk = jnp.concatenate([b_t, k_t], axis=0) * p_last
        s_new = _dot_tn(uv, bk)
        yield
        s_ref[si] = jnp.where(head_mask, s0 * p_last + s_new, 0.0)

        inv_n = 1.0 / RWKV_HEAD
        sums = head_sum(jnp.concatenate([y, r * kmod * rk_w_ref[:, ln]], axis=0))
        yield
        dlt = y - sums[:c] * inv_n
        bsum = sums[c:]
        var = head_sum(dlt * dlt) * inv_n
        yield
        yn = dlt * lax.rsqrt(var + GN_EPS) * lnw_ref[:, ln] + lnb_ref[:, ln]
        bonus = bsum * v
        o_ref[:, ln] = ((yn + bonus) * g_ref[:, ln]).astype(o_ref.dtype)

    _round_robin([stream(si) for si in range(n_streams)])


def _rwkv_scan(batch, rkv, ld, ag, k_k, k_a, r_k, ln_w, ln_b):
    _, m, d = rkv.shape
    t = m // batch
    c = SCAN_CHUNK
    w = HEADS_PER_GROUP * RWKV_HEAD
    assert d % w == 0 and t % c == 0 and c == RWKV_HEAD
    n_streams = _pick_tile(d // w, SCAN_STREAMS, 1)
    wb = n_streams * w
    nc = t // c
    tok = pl.BlockSpec((c, wb), lambda b, gi, ci: (b * nc + ci, gi))
    par = pl.BlockSpec((1, wb), lambda b, gi, ci: (0, gi))

    def plane(p):
        return pl.BlockSpec((None, c, wb), lambda b, gi, ci: (p, b * nc + ci, gi))

    return pl.pallas_call(
        functools.partial(_scan_body, n_streams),
        grid=(batch, d // wb, nc),
        in_specs=[plane(0), tok, plane(1), plane(2), plane(0), plane(1)] + [par] * 5,
        out_specs=tok,
        out_shape=jax.ShapeDtypeStruct((m, d), BF16),
        scratch_shapes=[pltpu.VMEM((n_streams, w, w), F32)],
        compiler_params=_params(("parallel", "parallel", "arbitrary")),
        name="rwkv_scan",
    )(rkv, ld, rkv, rkv, ag, ag, *[p.reshape(1, d) for p in (k_k, k_a, r_k, ln_w, ln_b)])


def _rwkv_mix(batch, x, norm_g, mu, w_rkv, w0, w1, w2, a0, a1, a2, g1, g2,
              k_k, k_a, r_k, ln_w, ln_b, w_o):
    rkv, ld, ag = _rwkv_proj(batch, x, norm_g, mu, w_rkv, w0, w1, w2, a0, a1, a2, g1, g2)
    yg = _rwkv_scan(batch, rkv, ld, ag, k_k, k_a, r_k.reshape(-1), ln_w, ln_b)
    return _simple_mm("rwkv_out", [_rows(yg)], lambda t: t, w_o.astype(BF16), [_epi_tile(x)],
                      lambda acc, x_t: (x_t + acc,), tn_target=x.shape[1])


def _attn_body(tq, tk, nh, small_ref, q_ref, k_ref, v_ref, o_ref, v1_ref):
    qi = pl.program_id(2)
    dv = v_ref.shape[-1]

    def scores(h, j, masked):
        start = pl.multiple_of(j * tk, tk)
        kb = k_ref[h, pl.ds(start, tk), :]
        s = lax.dot_general(q_ref[h], kb, (((1,), (1,)), ((), ())), preferred_element_type=F32)
        if masked:
            q_chunk = (qi * tq + lax.broadcasted_iota(jnp.int32, (tq, tk), 0)) // ATTN_CHUNK
            k_chunk = (j * tk + lax.broadcasted_iota(jnp.int32, (tq, tk), 1)) // ATTN_CHUNK
            s = jnp.where(k_chunk <= q_chunk, s, jnp.finfo(F32).min)
        return s

    def values(h, j):
        return v_ref[h, pl.ds(pl.multiple_of(j * tk, tk), tk), :]

    def all_heads(make_gen):
        outs = [[] for _ in range(nh)]
        _round_robin([make_gen(h, outs[h]) for h in range(nh)])
        return tuple(o[0] for o in outs)

    def sweep(step, init):
        carries = lax.fori_loop(
            0, qi, lambda j, cr: all_heads(lambda h, out: step(h, j, cr[h], False, out)), init)
        return all_heads(lambda h, out: step(h, qi, carries[h], True, out))

    def online_step(h, j, carry, masked, out):
        m_i, l_i, acc = carry
        s = scores(h, j, masked)
        yield
        m_new = jnp.maximum(m_i, jnp.max(s, axis=-1, keepdims=True))
        p = jnp.exp(s - m_new)
        alpha = jnp.exp(m_i - m_new)
        l_new = alpha * l_i + jnp.sum(p, axis=-1, keepdims=True)
        pv = jnp.dot(p.astype(BF16), values(h, j), preferred_element_type=F32)
        yield
        out.append((m_new, l_new, alpha * acc + pv))

    def plain_step(h, j, acc, masked, out):
        s = scores(h, j, masked)
        yield
        p = jnp.exp(s).astype(BF16)
        v1 = v1_ref[h, pl.ds(pl.multiple_of(j * tk, tk), tk), :]
        pv = jnp.dot(p, v1, preferred_element_type=F32)
        yield
        out.append(acc + pv)

    @pl.when(small_ref[0] == 1)
    def _():
        @pl.when(qi == 0)
        def _():
            v1_ref[:, :, :dv] = v_ref[...]
            v1_ref[:, :, dv:] = jnp.ones(v_ref.shape, BF16)

        init = tuple(jnp.zeros((tq, 2 * dv), F32) for _ in range(nh))
        for h, acc in enumerate(sweep(plain_step, init)):
            o_ref[:, h * dv:(h + 1) * dv] = (acc[:, :dv] / acc[:, dv:]).astype(o_ref.dtype)

    @pl.when(small_ref[0] != 1)
    def _():
        init = tuple((jnp.full((tq, 1), -jnp.inf, F32), jnp.zeros((tq, 1), F32),
                      jnp.zeros((tq, dv), F32)) for _ in range(nh))
        for h, (_, l_i, acc) in enumerate(sweep(online_step, init)):
            o_ref[:, h * dv:(h + 1) * dv] = (acc / l_i).astype(o_ref.dtype)


def _attention(batch, q, k, v, small_scores):
    heads, m, dqk = q.shape
    dv = v.shape[-1]
    t = m // batch
    tq = _pick_tile(t, ATTN_Q_TILE, ATTN_CHUNK)
    nq = t // tq
    nh = _pick_tile(heads, ATTN_HEADS_PER_STEP, 1)
    return pl.pallas_call(
        functools.partial(_attn_body, tq, tq, nh),
        grid=(batch, heads // nh, nq),
        in_specs=[
            pl.BlockSpec(memory_space=pltpu.SMEM),
            pl.BlockSpec((nh, tq, dqk), lambda b, h, i: (h, b * nq + i, 0)),
            pl.BlockSpec((nh, t, dqk), lambda b, h, i: (h, b, 0)),
            pl.BlockSpec((nh, t, dv), lambda b, h, i: (h, b, 0)),
        ],
        out_specs=pl.BlockSpec((tq, nh * dv), lambda b, h, i: (b * nq + i, h)),
        out_shape=jax.ShapeDtypeStruct((m, heads * dv), BF16),
        scratch_shapes=[pltpu.VMEM((nh, t, 2 * dv), BF16)],
        compiler_params=_params(("parallel", "parallel", "arbitrary")),
        name="mla_attention",
    )(small_scores, q, k, v)


def _mla_mix(batch, x, positions, norm_g, w_down, q_a_norm, kv_a_norm, w_uq, w_ukv,
             q_norm, k_norm, w_o):
    m, d = x.shape
    q_lora = q_a_norm.shape[0]
    kv_lora = kv_a_norm.shape[0]
    heads, qk_dim = w_uq.shape[1], w_uq.shape[2]
    rope = w_down.shape[1] - q_lora - kv_lora
    nope = qk_dim - rope
    dv = w_ukv.shape[2] - nope
    half = rope // 2
    assert nope == LANE and dv == LANE and rope <= LANE and q_lora == kv_lora and q_lora % LANE == 0
    scale = 1.0 / math.sqrt(qk_dim)

    perm = jnp.concatenate([jnp.arange(half, rope), jnp.arange(0, half)])
    lane_pad = lambda a: jnp.pad(a, [(0, 0)] * (a.ndim - 1) + [(0, LANE - a.shape[-1])])

    inv_freq = ROPE_THETA ** (-jnp.arange(0, rope, 2, dtype=F32) / rope)
    ang = positions.reshape(m, 1).astype(F32) * inv_freq
    cos, sin = jnp.cos(ang), jnp.sin(ang)
    cos_t = lane_pad(jnp.concatenate([cos, cos], axis=-1))
    sin_t = lane_pad(jnp.concatenate([-sin, sin], axis=-1))

    w_kpe = w_down[:, q_lora + kv_lora:]
    w_down_x = jnp.concatenate(
        [w_down[:, :q_lora + kv_lora], lane_pad(w_kpe), lane_pad(w_kpe[:, perm])], axis=1).astype(BF16)
    c = _simple_mm("mla_down", [_rows(x), _vec(norm_g)], _rms, w_down_x, [], lambda acc: (acc,),
                   tn_target=w_down_x.shape[1])
    qb = q_lora // q_lora
    tm = _pick_tile(m, 512, 8)

    def norm_rope(body, pe, rot, g_body, g_pe, g_rot, cos_b, sin_b, extra_scale):
        ss = jnp.sum(body * body, axis=-1, keepdims=True) + jnp.sum(pe * pe, axis=-1, keepdims=True)
        s = lax.rsqrt(ss / qk_dim + RMS_EPS) * extra_scale
        return jnp.concatenate([body * s * g_body, (pe * g_pe * cos_b + rot * g_rot * sin_b) * s], axis=-1)

    w_q_pe = w_uq[:, :, nope:]
    w_q_x = jnp.concatenate([w_uq[:, :, :nope], lane_pad(w_q_pe), lane_pad(w_q_pe[:, :, perm])],
                            axis=-1).reshape(q_lora, heads * 3 * LANE).astype(BF16)
    gq = jnp.concatenate([q_norm[:nope], lane_pad(q_norm[nope:]), lane_pad(q_norm[nope:][perm])]).reshape(1, -1)

    hb = _pick_tile(heads, MLA_HEADS_PER_STEP, 1)

    def q_epi(acc, g_t, cos_b, sin_b):
        per_head = []
        for hh in range(hb):
            a = acc[:, hh * 3 * LANE:(hh + 1) * 3 * LANE]
            per_head.append(norm_rope(a[:, :LANE], a[:, LANE:2 * LANE], a[:, 2 * LANE:], g_t[:, :LANE],
                                      g_t[:, LANE:2 * LANE], g_t[:, 2 * LANE:], cos_b, sin_b, scale))
        return (jnp.stack(per_head, axis=0),)

    q = _mm("mla_q", m,
            [(c, _row_spec(tm, q_lora, 0)), (q_a_norm.reshape(1, -1), _vec_spec(q_lora))], _rms, w_q_x,
            [(gq, _vec_spec(3 * LANE)), (cos_t, _row_spec(tm, LANE)), (sin_t, _row_spec(tm, LANE))], q_epi,
            [(jax.ShapeDtypeStruct((heads, m, 2 * LANE), BF16),
              pl.BlockSpec((hb, tm, 2 * LANE), lambda i, j: (j, i, 0)))],
            tm=tm, tn=hb * 3 * LANE)[0]

    w_kv_x = w_ukv.reshape(kv_lora, heads * (nope + dv)).astype(BF16)
    gk = jnp.concatenate([k_norm[:nope], lane_pad(k_norm[nope:]), lane_pad(k_norm[nope:][perm])]).reshape(1, -1)
    pe_block = (q_lora + kv_lora) // LANE

    def kv_epi(acc, g_t, pe, rot, cos_b, sin_b):
        k_out, v_out = [], []
        for hh in range(hb):
            a = acc[:, hh * (nope + dv):(hh + 1) * (nope + dv)]
            k_out.append(norm_rope(a[:, :nope], pe, rot, g_t[:, :LANE], g_t[:, LANE:2 * LANE],
                                   g_t[:, 2 * LANE:], cos_b, sin_b, 1.0))
            v_out.append(a[:, nope:])
        return jnp.stack(k_out, axis=0), jnp.stack(v_out, axis=0)

    k, v = _mm("mla_kv", m,
               [(c, _row_spec(tm, kv_lora, qb)), (kv_a_norm.reshape(1, -1), _vec_spec(kv_lora))], _rms, w_kv_x,
               [(gk, _vec_spec(3 * LANE)), (c, _row_spec(tm, LANE, pe_block)),
                (c, _row_spec(tm, LANE, pe_block + 1)), (cos_t, _row_spec(tm, LANE)),
                (sin_t, _row_spec(tm, LANE))], kv_epi,
               [(jax.ShapeDtypeStruct((heads, m, 2 * LANE), BF16),
                 pl.BlockSpec((hb, tm, 2 * LANE), lambda i, j: (j, i, 0))),
                (jax.ShapeDtypeStruct((heads, m, dv), BF16),
                 pl.BlockSpec((hb, tm, dv), lambda i, j: (j, i, 0)))],
               tm=tm, tn=hb * (nope + dv))

    score_bound = scale * qk_dim * jnp.max(jnp.abs(q_norm)) * jnp.max(jnp.abs(k_norm))
    small_scores = (score_bound <= MAX_PLAIN_SCORE).astype(jnp.int32).reshape(1)
    o = _attention(batch, q, k, v, small_scores)
    return _simple_mm("mla_out", [_rows(o)], lambda t: t, w_o.reshape(heads * dv, d).astype(BF16),
                      [_epi_tile(x)], lambda acc, x_t: (x_t + acc,), tn_target=d)


def kernel(x, positions, ffn_norm, ffn_w13, ffn_w2, mix_norm, rwkv_mu, rwkv_w_rkv, rwkv_w0, rwkv_w1, rwkv_w2, rwkv_a0, rwkv_a1, rwkv_a2, rwkv_g1, rwkv_g2, rwkv_k_k, rwkv_k_a, rwkv_r_k, rwkv_ln_w, rwkv_ln_b, rwkv_w_o, mla_w_down, mla_q_a_norm, mla_kv_a_norm, mla_w_uq, mla_w_ukv, mla_q_norm, mla_k_norm, mla_w_o):
    batch, seq, d = x.shape
    depth = ffn_norm.shape[0]
    n_mixers = 2
    xf = x.reshape(batch * seq, d)
    d_ff = ffn_w2.shape[2]
    w13p, w2p = _ffn_weights(ffn_w13.reshape((2 * depth,) + ffn_w13.shape[2:]),
                             ffn_w2.reshape((2 * depth,) + ffn_w2.shape[2:]))
    for i in range(depth):
        j = i // n_mixers
        xf = _ffn(xf, ffn_norm[i, 0], w13p, w2p, 2 * i, d_ff)
        if i % n_mixers == 0:
            xf = _rwkv_mix(batch, xf, mix_norm[i], rwkv_mu[j], rwkv_w_rkv[j], rwkv_w0[j], rwkv_w1[j],
                           rwkv_w2[j], rwkv_a0[j], rwkv_a1[j], rwkv_a2[j], rwkv_g1[j], rwkv_g2[j],
                           rwkv_k_k[j], rwkv_k_a[j], rwkv_r_k[j], rwkv_ln_w[j], rwkv_ln_b[j],
                           rwkv_w_o[j])
        else:
            xf = _mla_mix(batch, xf, positions, mix_norm[i], mla_w_down[j], mla_q_a_norm[j],
                          mla_kv_a_norm[j], mla_w_uq[j], mla_w_ukv[j], mla_q_norm[j], mla_k_norm[j],
                          mla_w_o[j])
        xf = _ffn(xf, ffn_norm[i, 1], w13p, w2p, 2 * i + 1, d_ff)
    return xf.reshape(batch, seq, d)
```

```python
import functools
import math

import jax
import jax.numpy as jnp
from jax import lax
from jax.experimental import pallas as pl
from jax.experimental.pallas import tpu as pltpu

F32 = jnp.float32
BF16 = jnp.bfloat16

RMS_EPS = 1e-6
GN_EPS = 64e-5
ROPE_THETA = 10000.0
ATTN_CHUNK = 64
RWKV_HEAD = 64
SCAN_CHUNK = 64
HEADS_PER_GROUP = 4
MAX_PLAIN_SCORE = 30.0
ATTN_Q_TILE = 512
ATTN_HEADS_PER_STEP = 4
FFN_TILE = 512
MLA_HEADS_PER_STEP = 4
SCAN_BATCH_ROWS = 2
SCAN_STREAMS = 8
LANE = 128
SUBLANE = 8
VMEM_LIMIT_BYTES = 56 * 1024 * 1024


def _params(semantics):
    return pltpu.CompilerParams(dimension_semantics=semantics,
                                vmem_limit_bytes=VMEM_LIMIT_BYTES)


def _pick_tile(n, target, quantum):
    if n <= target:
        return n
    t = (target // quantum) * quantum
    while t > quantum and n % t:
        t -= quantum
    assert n % t == 0, (n, target, quantum)
    return t


def _round_robin(gens):
    live = list(gens)
    while live:
        live = [g for g in live if next(g, True) is None]


def _rms(x, g):
    return x * lax.rsqrt(jnp.mean(x * x, axis=-1, keepdims=True) + RMS_EPS) * g


def _ffn_body(x_ref, g_ref, wg_ref, wu_ref, w2_ref, o_ref, xn_ref):
    j = pl.program_id(1)

    @pl.when(j == 0)
    def _():
        xn_ref[...] = _rms(x_ref[...], g_ref[...]).astype(BF16)
        o_ref[...] = jnp.zeros_like(o_ref)

    xn = xn_ref[...]
    gate = jnp.dot(xn, wg_ref[...], preferred_element_type=F32)
    up = jnp.dot(xn, wu_ref[...], preferred_element_type=F32)
    act = (gate * jax.nn.sigmoid(gate) * up).astype(BF16)
    o_ref[...] += jnp.dot(act, w2_ref[...], preferred_element_type=F32)

    @pl.when(j == pl.num_programs(1) - 1)
    def _():
        o_ref[...] = x_ref[...] + 0.5 * o_ref[...]


def _ffn_tile(f):
    return min(FFN_TILE, -(-f // LANE) * LANE)


def _tile_cast_body(tiles, x_ref, o_ref):
    tw = o_ref.shape[-1]
    for t, (c0, width) in enumerate(tiles):
        o_ref[t, :, :width] = x_ref[:, c0:c0 + width].astype(BF16)
        if width < tw:
            o_ref[t, :, width:] = jnp.zeros((o_ref.shape[1], tw - width), BF16)


def _tile_cast(name, w, tiles, tw):
    n, rows, cols = w.shape
    tr = _pick_tile(rows, 128, SUBLANE)
    return pl.pallas_call(
        functools.partial(_tile_cast_body, tuple(tiles)),
        grid=(n, rows // tr),
        in_specs=[pl.BlockSpec((None, tr, cols), lambda l, i: (l, i, 0))],
        out_specs=pl.BlockSpec((None, len(tiles), tr, tw), lambda l, i: (l, 0, i, 0)),
        out_shape=jax.ShapeDtypeStruct((n, len(tiles), rows, tw), BF16),
        compiler_params=_params(("parallel", "parallel")),
        name=name,
    )(w)


def _cast_w2_body(f, x_ref, o_ref):
    tr = o_ref.shape[0]
    row = pl.program_id(1) * tr + lax.broadcasted_iota(jnp.int32, o_ref.shape, 0)
    o_ref[...] = jnp.where(row < f, x_ref[...], 0.0).astype(BF16)


def _ffn_weights(w13, w2):
    n, d, f2 = w13.shape
    f = f2 // 2
    assert f % LANE == 0
    tf = _ffn_tile(f)
    fp = -(-f // tf) * tf
    half = [(t * tf, min(tf, f - t * tf)) for t in range(fp // tf)]
    w13p = _tile_cast("cast_w13", w13, half + [(f + c0, width) for c0, width in half], tf)
    w2p = pl.pallas_call(
        functools.partial(_cast_w2_body, f),
        grid=(n, fp // tf),
        in_specs=[pl.BlockSpec((None, tf, d), lambda l, i: (l, i, 0))],
        out_specs=pl.BlockSpec((None, tf, d), lambda l, i: (l, i, 0)),
        out_shape=jax.ShapeDtypeStruct((n, fp, d), BF16),
        compiler_params=_params(("parallel", "parallel")),
        name="cast_w2",
    )(w2)
    return w13p, w2p


def _ffn(x, g, w13p, w2p, layer, f):
    m, d = x.shape
    fp = w2p.shape[1]
    tf = _ffn_tile(f)
    nj = fp // tf
    tm = _pick_tile(m, 512, 8)
    return pl.pallas_call(
        _ffn_body,
        grid=(m // tm, nj),
        in_specs=[
            pl.BlockSpec((tm, d), lambda i, j: (i, 0)),
            pl.BlockSpec((1, d), lambda i, j: (0, 0)),
            pl.BlockSpec((None, None, d, tf), lambda i, j: (layer, j, 0, 0)),
            pl.BlockSpec((None, None, d, tf), lambda i, j: (layer, j + nj, 0, 0)),
            pl.BlockSpec((None, tf, d), lambda i, j: (layer, j, 0)),
        ],
        out_specs=pl.BlockSpec((tm, d), lambda i, j: (i, 0)),
        out_shape=jax.ShapeDtypeStruct((m, d), F32),
        scratch_shapes=[pltpu.VMEM((tm, d), BF16)],
        compiler_params=_params(("parallel", "arbitrary")),
        name="ffn",
    )(x, g.reshape(1, d), w13p, w13p, w2p)


def _mm_body(n_lhs, n_epi, n_out, lhs_fn, epi_fn, *refs):
    lhs_refs = refs[:n_lhs]
    w_ref = refs[n_lhs]
    epi_refs = refs[n_lhs + 1:n_lhs + 1 + n_epi]
    out_refs = refs[n_lhs + 1 + n_epi:n_lhs + 1 + n_epi + n_out]
    lhs_scratch = refs[-1]

    @pl.when(pl.program_id(1) == 0)
    def _():
        lhs_scratch[...] = lhs_fn(*[r[...] for r in lhs_refs]).astype(BF16)

    acc = jnp.dot(lhs_scratch[...], w_ref[...], preferred_element_type=F32)
    outs = epi_fn(acc, *[r[...] for r in epi_refs])
    for o_ref, o in zip(out_refs, outs):
        o_ref[...] = o.astype(o_ref.dtype)


def _mm(name, m, lhs, lhs_fn, w, epi, epi_fn, outs, *, tm, tn):
    k, n = w.shape
    body = functools.partial(_mm_body, len(lhs), len(epi), len(outs), lhs_fn, epi_fn)
    return pl.pallas_call(
        body,
        grid=(m // tm, n // tn),
        in_specs=[s for _, s in lhs] + [pl.BlockSpec((k, tn), lambda i, j: (0, j))]
        + [s for _, s in epi],
        out_specs=[s for _, s in outs],
        out_shape=[s for s, _ in outs],
        scratch_shapes=[pltpu.VMEM((tm, k), BF16)],
        compiler_params=_params(("parallel", "arbitrary")),
        name=name,
    )(*[a for a, _ in lhs], w, *[a for a, _ in epi])


def _row_spec(tm, width, col_block=0):
    return pl.BlockSpec((tm, width), lambda i, j: (i, col_block))


def _tile_spec(tm, tn):
    return pl.BlockSpec((tm, tn), lambda i, j: (i, j))


def _vec_spec(width):
    return pl.BlockSpec((1, width), lambda i, j: (0, 0))


def _colvec_spec(tn):
    return pl.BlockSpec((1, tn), lambda i, j: (0, j))


def _simple_mm(name, lhs, lhs_fn, w, epi, epi_fn, out_dtype=F32, tm_target=512, tn_target=1024):
    m = lhs[0][0].shape[0]
    n = w.shape[1]
    tm = _pick_tile(m, tm_target, 8)
    tn = _pick_tile(n, tn_target, LANE)
    lhs = [(a, spec(tm)) for a, spec in lhs]
    epi = [(a, spec(tm, tn)) for a, spec in epi]
    out = (jax.ShapeDtypeStruct((m, n), out_dtype), _tile_spec(tm, tn))
    return _mm(name, m, lhs, lhs_fn, w, epi, epi_fn, [out], tm=tm, tn=tn)[0]


def _rows(a):
    return a, lambda tm: _row_spec(tm, a.shape[1])


def _vec(a):
    a = a.reshape(1, -1)
    return a, lambda tm: _vec_spec(a.shape[1])


def _epi_tile(a):
    return a, lambda tm, tn: _tile_spec(tm, tn)


def _epi_colvec(a):
    return a.reshape(1, -1), lambda tm, tn: _colvec_spec(tn)


def _rwkv_proj_body(seq_tiles, ranks, x_ref, xprev_ref, g_ref, mu_ref, wr_ref, wk_ref, wv_ref, l1_ref,
                    w2_ref, a2_ref, g2_ref, w0_ref, a0_ref, rkv_ref, ld_ref, ag_ref,
                    lerp_scr, tw_scr, ta_scr, tg_scr):
    i = pl.program_id(0)
    rw, ra = ranks

    @pl.when(pl.program_id(1) == 0)
    def _():
        h = _rms(x_ref[...], g_ref[...])
        prev = _rms(xprev_ref[...], g_ref[...])[SUBLANE - 1:SUBLANE, :]
        prev = jnp.where(i % seq_tiles == 0, 0.0, prev)
        row = lax.broadcasted_iota(jnp.int32, h.shape, 0)
        dx = jnp.where(row == 0, prev, pltpu.roll(h, 1, 0)) - h
        for idx in range(6):
            lerp_scr[idx] = (h + dx * mu_ref[idx:idx + 1, :]).astype(BF16)
        tw_scr[...] = jnp.tanh(jnp.dot(lerp_scr[3], l1_ref[:, :rw],
                                       preferred_element_type=F32)).astype(BF16)
        ta_scr[...] = jnp.dot(lerp_scr[4], l1_ref[:, rw:rw + ra],
                              preferred_element_type=F32).astype(BF16)
        tg_scr[...] = jax.nn.sigmoid(jnp.dot(lerp_scr[5], l1_ref[:, rw + ra:],
                                             preferred_element_type=F32)).astype(BF16)

    rkv_ref[0] = jnp.dot(lerp_scr[0], wr_ref[...], preferred_element_type=F32).astype(BF16)
    rkv_ref[1] = jnp.dot(lerp_scr[1], wk_ref[...], preferred_element_type=F32).astype(BF16)
    rkv_ref[2] = jnp.dot(lerp_scr[2], wv_ref[...], preferred_element_type=F32).astype(BF16)
    z = -(w0_ref[...] + jnp.dot(tw_scr[...], w2_ref[...], preferred_element_type=F32))
    softplus = jnp.maximum(z, 0.0) + jnp.log1p(jnp.exp(-jnp.abs(z)))
    ld_ref[...] = -jnp.exp(-softplus - 0.5)
    ag_ref[0] = jax.nn.sigmoid(
        a0_ref[...] + jnp.dot(ta_scr[...], a2_ref[...], preferred_element_type=F32)).astype(BF16)
    ag_ref[1] = jnp.dot(tg_scr[...], g2_ref[...], preferred_element_type=F32).astype(BF16)


def _pad_lora(wa, wb):
    rank = wa.shape[1]
    rp = -(-rank // LANE) * LANE
    return (jnp.pad(wa, ((0, 0), (0, rp - rank))).astype(BF16),
            jnp.pad(wb, ((0, rp - rank), (0, 0))).astype(BF16))


def _rwkv_proj(batch, x, norm_g, mu, w_rkv, w0, w1, w2, a0, a1, a2, g1, g2):
    m, d = x.shape
    tm = _pick_tile(m // batch, 512, SUBLANE)
    tn = _pick_tile(d, 256, LANE)
    w1p, w2p = _pad_lora(w1, w2)
    a1p, a2p = _pad_lora(a1, a2)
    g1p, g2p = _pad_lora(g1, g2)
    rw, ra, rg = w1p.shape[1], a1p.shape[1], g1p.shape[1]
    l1 = jnp.concatenate([w1p, a1p, g1p], axis=1)
    w_bf = _tile_cast("cast_w_rkv", w_rkv, [(t * tn, tn) for t in range(d // tn)], tn)
    mu_ord = mu[jnp.array([0, 2, 3, 1, 4, 5])]
    col = lambda i, j: (0, j)

    def w_plane(p):
        return pl.BlockSpec((None, None, d, tn), lambda i, j: (p, j, 0, 0))

    def planes(n):
        return pl.BlockSpec((n, tm, tn), lambda i, j: (0, i, j))

    body = functools.partial(_rwkv_proj_body, (m // batch) // tm, (rw, ra))
    return pl.pallas_call(
        body,
        grid=(m // tm, d // tn),
        in_specs=[
            pl.BlockSpec((tm, d), lambda i, j: (i, 0)),
            pl.BlockSpec((SUBLANE, d), lambda i, j: (jnp.maximum(i * (tm // SUBLANE) - 1, 0), 0)),
            pl.BlockSpec((1, d), lambda i, j: (0, 0)),
            pl.BlockSpec((6, d), lambda i, j: (0, 0)),
            w_plane(0), w_plane(1), w_plane(2),
            pl.BlockSpec((d, rw + ra + rg), lambda i, j: (0, 0)),
            pl.BlockSpec((rw, tn), col),
            pl.BlockSpec((ra, tn), col),
            pl.BlockSpec((rg, tn), col),
            pl.BlockSpec((1, tn), col),
            pl.BlockSpec((1, tn), col),
        ],
        out_specs=[planes(3), pl.BlockSpec((tm, tn), lambda i, j: (i, j)), planes(2)],
        out_shape=[jax.ShapeDtypeStruct((3, m, d), BF16), jax.ShapeDtypeStruct((m, d), F32),
                   jax.ShapeDtypeStruct((2, m, d), BF16)],
        scratch_shapes=[pltpu.VMEM((6, tm, d), BF16), pltpu.VMEM((tm, rw), BF16),
                        pltpu.VMEM((tm, ra), BF16), pltpu.VMEM((tm, rg), BF16)],
        compiler_params=_params(("parallel", "arbitrary")),
        name="rwkv_proj",
    )(x, x, norm_g.reshape(1, d), mu_ord, w_bf, w_bf, w_bf, l1, w2p, a2p, g2p,
      w0.reshape(1, d), a0.reshape(1, d))


def _bf(x):
    return x.astype(BF16)


def _dot(a, b):
    return jnp.dot(_bf(a), _bf(b), preferred_element_type=F32)


def _dot_nt(a, b):
    return lax.dot_general(_bf(a), _bf(b), (((1,), (1,)), ((), ())), preferred_element_type=F32)


def _dot_tn(a, b):
    return lax.dot_general(_bf(a), _bf(b), (((0,), (0,)), ((), ())), preferred_element_type=F32)


def _scan_body(n_streams, nb, r_ref, ld_ref, k_ref, v_ref, al_ref, g_ref, kk_w_ref, ka_w_ref,
               rk_w_ref, lnw_ref, lnb_ref, o_ref, s_ref):
    c = SCAN_CHUNK
    w = HEADS_PER_GROUP * RWKV_HEAD
    wc = HEADS_PER_GROUP * c

    @pl.when(pl.program_id(2) == 0)
    def _():
        s_ref[...] = jnp.zeros_like(s_ref)

    tri = jnp.where(lax.broadcasted_iota(jnp.int32, (c, c), 0)
                    >= lax.broadcasted_iota(jnp.int32, (c, c), 1), 1.0, 0.0).astype(BF16)
    bd_r = lax.broadcasted_iota(jnp.int32, (w, w), 0) // RWKV_HEAD
    bd_c = lax.broadcasted_iota(jnp.int32, (w, w), 1) // RWKV_HEAD
    head_mask = bd_r == bd_c
    head_ones = jnp.where(head_mask, 1.0, 0.0).astype(BF16)
    t_idx = lax.broadcasted_iota(jnp.int32, (c, wc), 0)
    s_idx = lax.broadcasted_iota(jnp.int32, (c, wc), 1) % c
    strict = s_idx < t_idx
    incl = s_idx <= t_idx
    eye = jnp.where(s_idx == t_idx, 1.0, 0.0)
    base_level = 3
    base_blocks = (t_idx >> base_level) == (s_idx >> base_level)
    join = {lv: (((t_idx >> lv) & 1) == 1) & ((s_idx >> lv) == (t_idx >> lv) - 1)
            for lv in range(base_level, c.bit_length() - 1)}

    def head_sum(x):
        return jnp.dot(_bf(x), head_ones, preferred_element_type=F32)

    def block_diag(x):
        xb = _bf(x)
        return jnp.where(head_mask, jnp.concatenate([xb] * HEADS_PER_GROUP, axis=0),
                         jnp.zeros((), BF16))

    def stream(bi, si):
        st = bi * n_streams + si
        ln = pl.ds(si * w, w)
        r = r_ref[bi, :, ln].astype(F32)
        ld = ld_ref[bi, :, ln]
        k = k_ref[bi, :, ln].astype(F32)
        v = v_ref[bi, :, ln].astype(F32)
        al = al_ref[bi, :, ln].astype(F32)

        ld_hi = _bf(ld)
        rest = ld - ld_hi.astype(F32)
        ld_mid = _bf(rest)
        ld_lo = _bf(rest - ld_mid.astype(F32))
        cum = (jnp.dot(tri, ld_hi, preferred_element_type=F32)
               + jnp.dot(tri, ld_mid, preferred_element_type=F32)
               + jnp.dot(tri, ld_lo, preferred_element_type=F32))
        yield
        p_incl = jnp.exp(cum)
        p_prev = jnp.exp(cum - ld)
        p_inv = jnp.exp(-cum)
        p_last = p_incl[c - 1:c, :]

        kk = k * kk_w_ref[:, ln]
        ss = head_sum(kk * kk)
        yield
        kk = kk / jnp.maximum(jnp.sqrt(ss), 1e-12)
        kmod = k * (1.0 + (al - 1.0) * ka_w_ref[:, ln])

        a_t = -kk * p_prev
        b_t = kk * al * p_inv
        k_t = kmod * p_inv
        r_t = r * p_incl

        ar = jnp.concatenate([a_t, r_t], axis=0)
        m_b = _dot_nt(ar, block_diag(b_t))
        m_k = _dot_nt(ar, block_diag(k_t))
        yield
        a_ab = jnp.where(strict, m_b[:c], 0.0)
        a_rb = jnp.where(incl, m_b[c:], 0.0)
        a_ak = jnp.where(strict, m_k[:c], 0.0)
        a_rk = jnp.where(incl, m_k[c:], 0.0)

        pw = jnp.where(base_blocks, a_ab, 0.0)
        t_inv = eye
        for step in range(base_level):
            if step < base_level - 1:
                z = _dot(jnp.concatenate([pw, t_inv], axis=0), block_diag(pw))
                yield
                pw, t_inv = z[:c], t_inv + z[c:]
            else:
                z = _dot(t_inv, block_diag(pw))
                yield
                t_inv = t_inv + z
        n_bd = block_diag(a_ab)
        level = base_level
        while (1 << level) < c:
            tn = _dot(t_inv, n_bd)
            yield
            x = _dot(tn, block_diag(t_inv))
            yield
            t_inv = t_inv + jnp.where(join[level], x, 0.0)
            level += 1

        s0 = s_ref[st]
        w1 = _dot_nt(ar, s0)
        w2 = _dot(jnp.concatenate([a_ak, a_rk], axis=0), block_diag(v))
        yield
        u = _dot(t_inv, block_diag(w1[:c] + w2[:c]))
        yield
        y = w1[c:] + w2[c:] + _dot(a_rb, block_diag(u))

        uv = jnp.concatenate([u, v], axis=0)
        bk = jnp.concatenate([b_t, k_t], axis=0) * p_last
        s_new = _dot_tn(uv, bk)
        yield
        s_ref[st] = jnp.where(head_mask, s0 * p_last + s_new, 0.0)

        inv_n = 1.0 / RWKV_HEAD
        sums = head_sum(jnp.concatenate([y, r * kmod * rk_w_ref[:, ln]], axis=0))
        yield
        dlt = y - sums[:c] * inv_n
        bsum = sums[c:]
        var = head_sum(dlt * dlt) * inv_n
        yield
        yn = dlt * lax.rsqrt(var + GN_EPS) * lnw_ref[:, ln] + lnb_ref[:, ln]
        bonus = bsum * v
        o_ref[bi, :, ln] = ((yn + bonus) * g_ref[bi, :, ln]).astype(o_ref.dtype)

    _round_robin([stream(bi, si) for bi in range(nb) for si in range(n_streams)])


def _rwkv_scan(batch, rkv, ld, ag, k_k, k_a, r_k, ln_w, ln_b):
    _, m, d = rkv.shape
    t = m // batch
    c = SCAN_CHUNK
    w = HEADS_PER_GROUP * RWKV_HEAD
    assert d % w == 0 and t % c == 0 and c == RWKV_HEAD
    n_streams = _pick_tile(d // w, SCAN_STREAMS, 1)
    wb = n_streams * w
    nc = t // c
    nb = _pick_tile(batch, SCAN_BATCH_ROWS, 1)
    tok = pl.BlockSpec((nb, c, wb), lambda b, gi, ci: (b, ci, gi))
    par = pl.BlockSpec((1, wb), lambda b, gi, ci: (0, gi))

    def plane(p):
        return pl.BlockSpec((None, nb, c, wb), lambda b, gi, ci: (p, b, ci, gi))

    rkv4 = rkv.reshape(3, batch, t, d)
    ag4 = ag.reshape(2, batch, t, d)
    return pl.pallas_call(
        functools.partial(_scan_body, n_streams, nb),
        grid=(batch // nb, d // wb, nc),
        in_specs=[plane(0), tok, plane(1), plane(2), plane(0), plane(1)] + [par] * 5,
        out_specs=tok,
        out_shape=jax.ShapeDtypeStruct((batch, t, d), BF16),
        scratch_shapes=[pltpu.VMEM((nb * n_streams, w, w), F32)],
        compiler_params=_params(("parallel", "parallel", "arbitrary")),
        name="rwkv_scan",
    )(rkv4, ld.reshape(batch, t, d), rkv4, rkv4, ag4, ag4,
      *[p.reshape(1, d) for p in (k_k, k_a, r_k, ln_w, ln_b)]).reshape(m, d)


def _rwkv_mix(batch, x, norm_g, mu, w_rkv, w0, w1, w2, a0, a1, a2, g1, g2,
              k_k, k_a, r_k, ln_w, ln_b, w_o):
    rkv, ld, ag = _rwkv_proj(batch, x, norm_g, mu, w_rkv, w0, w1, w2, a0, a1, a2, g1, g2)
    yg = _rwkv_scan(batch, rkv, ld, ag, k_k, k_a, r_k.reshape(-1), ln_w, ln_b)
    return _simple_mm("rwkv_out", [_rows(yg)], lambda t: t, w_o.astype(BF16), [_epi_tile(x)],
                      lambda acc, x_t: (x_t + acc,), tn_target=x.shape[1])


def _attn_body(tq, tk, nh, small_ref, q_ref, k_ref, v_ref, o_ref, v1_ref):
    qi = pl.program_id(2)
    dv = v_ref.shape[-1]

    def scores(h, j, masked):
        start = pl.multiple_of(j * tk, tk)
        kb = k_ref[h, pl.ds(start, tk), :]
        s = lax.dot_general(q_ref[h], kb, (((1,), (1,)), ((), ())), preferred_element_type=F32)
        if masked:
            q_chunk = (qi * tq + lax.broadcasted_iota(jnp.int32, (tq, tk), 0)) // ATTN_CHUNK
            k_chunk = (j * tk + lax.broadcasted_iota(jnp.int32, (tq, tk), 1)) // ATTN_CHUNK
            s = jnp.where(k_chunk <= q_chunk, s, jnp.finfo(F32).min)
        return s

    def values(h, j):
        return v_ref[h, pl.ds(pl.multiple_of(j * tk, tk), tk), :]

    def all_heads(make_gen):
        outs = [[] for _ in range(nh)]
        _round_robin([make_gen(h, outs[h]) for h in range(nh)])
        return tuple(o[0] for o in outs)

    def sweep(step, init, diag_step=None):
        carries = lax.fori_loop(
            0, qi, lambda j, cr: all_heads(lambda h, out: step(h, j, cr[h], False, out)), init)
        if diag_step is not None:
            return all_heads(lambda h, out: diag_step(h, carries[h], out))
        return all_heads(lambda h, out: step(h, qi, carries[h], True, out))

    def online_step(h, j, carry, masked, out):
        m_i, l_i, acc = carry
        s = scores(h, j, masked)
        yield
        m_new = jnp.maximum(m_i, jnp.max(s, axis=-1, keepdims=True))
        p = jnp.exp(s - m_new)
        alpha = jnp.exp(m_i - m_new)
        l_new = alpha * l_i + jnp.sum(p, axis=-1, keepdims=True)
        pv = jnp.dot(p.astype(BF16), values(h, j), preferred_element_type=F32)
        yield
        out.append((m_new, l_new, alpha * acc + pv))

    def plain_step(h, j, acc, masked, out):
        s = scores(h, j, masked)
        yield
        p = jnp.exp(s).astype(BF16)
        v1 = v1_ref[h, pl.ds(pl.multiple_of(j * tk, tk), tk), :]
        pv = jnp.dot(p, v1, preferred_element_type=F32)
        yield
        out.append(acc + pv)

    def plain_diag(h, acc, out):
        half = tk // 2
        base = pl.multiple_of(qi * tk, tk)
        visible = (lax.broadcasted_iota(jnp.int32, (tq, half), 1) // ATTN_CHUNK
                   <= lax.broadcasted_iota(jnp.int32, (tq, half), 0) // ATTN_CHUNK)

        def weights(q_rows, key0, mask):
            kb = k_ref[h, pl.ds(key0, half), :]
            s = lax.dot_general(q_rows, kb, (((1,), (1,)), ((), ())), preferred_element_type=F32)
            return jnp.where(mask, jnp.exp(s), 0.0).astype(BF16)

        p_a = weights(q_ref[h], base, visible)
        yield
        pv_a = jnp.dot(p_a, v1_ref[h, pl.ds(base, half), :], preferred_element_type=F32)
        p_b = weights(q_ref[h, half:, :], base + half, visible[:half])
        yield
        pv_b = jnp.dot(p_b, v1_ref[h, pl.ds(base + half, half), :], preferred_element_type=F32)
        yield
        out.append(acc + jnp.concatenate([pv_a[:half], pv_a[half:] + pv_b], axis=0))

    @pl.when(small_ref[0] == 1)
    def _():
        @pl.when(qi == 0)
        def _():
            v1_ref[:, :, :dv] = v_ref[...]
            v1_ref[:, :, dv:] = jnp.ones(v_ref.shape, BF16)

        init = tuple(jnp.zeros((tq, 2 * dv), F32) for _ in range(nh))
        for h, acc in enumerate(sweep(plain_step, init, plain_diag)):
            o_ref[:, h * dv:(h + 1) * dv] = (acc[:, :dv] / acc[:, dv:]).astype(o_ref.dtype)

    @pl.when(small_ref[0] != 1)
    def _():
        init = tuple((jnp.full((tq, 1), -jnp.inf, F32), jnp.zeros((tq, 1), F32),
                      jnp.zeros((tq, dv), F32)) for _ in range(nh))
        for h, (_, l_i, acc) in enumerate(sweep(online_step, init)):
            o_ref[:, h * dv:(h + 1) * dv] = (acc / l_i).astype(o_ref.dtype)


def _attention(batch, q, k, v, small_scores):
    heads, m, dqk = q.shape
    dv = v.shape[-1]
    t = m // batch
    tq = _pick_tile(t, ATTN_Q_TILE, 2 * ATTN_CHUNK)
    assert tq % (2 * ATTN_CHUNK) == 0
    nq = t // tq
    nh = _pick_tile(heads, ATTN_HEADS_PER_STEP, 1)
    return pl.pallas_call(
        functools.partial(_attn_body, tq, tq, nh),
        grid=(batch, heads // nh, nq),
        in_specs=[
            pl.BlockSpec(memory_space=pltpu.SMEM),
            pl.BlockSpec((nh, tq, dqk), lambda b, h, i: (h, b * nq + i, 0)),
            pl.BlockSpec((nh, t, dqk), lambda b, h, i: (h, b, 0)),
            pl.BlockSpec((nh, t, dv), lambda b, h, i: (h, b, 0)),
        ],
        out_specs=pl.BlockSpec((tq, nh * dv), lambda b, h, i: (b * nq + i, h)),
        out_shape=jax.ShapeDtypeStruct((m, heads * dv), BF16),
        scratch_shapes=[pltpu.VMEM((nh, t, 2 * dv), BF16)],
        compiler_params=_params(("parallel", "parallel", "arbitrary")),
        name="mla_attention",
    )(small_scores, q, k, v)


def _mla_mix(batch, x, positions, norm_g, w_down, q_a_norm, kv_a_norm, w_uq, w_ukv,
             q_norm, k_norm, w_o):
    m, d = x.shape
    q_lora = q_a_norm.shape[0]
    kv_lora = kv_a_norm.shape[0]
    heads, qk_dim = w_uq.shape[1], w_uq.shape[2]
    rope = w_down.shape[1] - q_lora - kv_lora
    nope = qk_dim - rope
    dv = w_ukv.shape[2] - nope
    half = rope // 2
    assert nope == LANE and dv == LANE and rope <= LANE and q_lora == kv_lora and q_lora % LANE == 0
    scale = 1.0 / math.sqrt(qk_dim)

    perm = jnp.concatenate([jnp.arange(half, rope), jnp.arange(0, half)])
    lane_pad = lambda a: jnp.pad(a, [(0, 0)] * (a.ndim - 1) + [(0, LANE - a.shape[-1])])

    inv_freq = ROPE_THETA ** (-jnp.arange(0, rope, 2, dtype=F32) / rope)
    ang = positions.reshape(m, 1).astype(F32) * inv_freq
    cos, sin = jnp.cos(ang), jnp.sin(ang)
    cos_t = lane_pad(jnp.concatenate([cos, cos], axis=-1))
    sin_t = lane_pad(jnp.concatenate([-sin, sin], axis=-1))

    w_kpe = w_down[:, q_lora + kv_lora:]
    w_down_x = jnp.concatenate(
        [w_down[:, :q_lora + kv_lora], lane_pad(w_kpe), lane_pad(w_kpe[:, perm])], axis=1).astype(BF16)
    c = _simple_mm("mla_down", [_rows(x), _vec(norm_g)], _rms, w_down_x, [], lambda acc: (acc,),
                   tn_target=w_down_x.shape[1])
    qb = q_lora // q_lora
    tm = _pick_tile(m, 512, 8)

    def norm_rope(body, pe, rot, g_body, g_pe, g_rot, cos_b, sin_b, extra_scale):
        ss = jnp.sum(body * body, axis=-1, keepdims=True) + jnp.sum(pe * pe, axis=-1, keepdims=True)
        s = lax.rsqrt(ss / qk_dim + RMS_EPS) * extra_scale
        return jnp.concatenate([body * s * g_body, (pe * g_pe * cos_b + rot * g_rot * sin_b) * s], axis=-1)

    w_q_pe = w_uq[:, :, nope:]
    w_q_x = jnp.concatenate([w_uq[:, :, :nope], lane_pad(w_q_pe), lane_pad(w_q_pe[:, :, perm])],
                            axis=-1).reshape(q_lora, heads * 3 * LANE).astype(BF16)
    gq = jnp.concatenate([q_norm[:nope], lane_pad(q_norm[nope:]), lane_pad(q_norm[nope:][perm])]).reshape(1, -1)

    hb = _pick_tile(heads, MLA_HEADS_PER_STEP, 1)

    def q_epi(acc, g_t, cos_b, sin_b):
        per_head = []
        for hh in range(hb):
            a = acc[:, hh * 3 * LANE:(hh + 1) * 3 * LANE]
            per_head.append(norm_rope(a[:, :LANE], a[:, LANE:2 * LANE], a[:, 2 * LANE:], g_t[:, :LANE],
                                      g_t[:, LANE:2 * LANE], g_t[:, 2 * LANE:], cos_b, sin_b, scale))
        return (jnp.stack(per_head, axis=0),)

    q = _mm("mla_q", m,
            [(c, _row_spec(tm, q_lora, 0)), (q_a_norm.reshape(1, -1), _vec_spec(q_lora))], _rms, w_q_x,
            [(gq, _vec_spec(3 * LANE)), (cos_t, _row_spec(tm, LANE)), (sin_t, _row_spec(tm, LANE))], q_epi,
            [(jax.ShapeDtypeStruct((heads, m, 2 * LANE), BF16),
              pl.BlockSpec((hb, tm, 2 * LANE), lambda i, j: (j, i, 0)))],
            tm=tm, tn=hb * 3 * LANE)[0]

    w_kv_x = w_ukv.reshape(kv_lora, heads * (nope + dv)).astype(BF16)
    gk = jnp.concatenate([k_norm[:nope], lane_pad(k_norm[nope:]), lane_pad(k_norm[nope:][perm])]).reshape(1, -1)
    pe_block = (q_lora + kv_lora) // LANE

    def kv_epi(acc, g_t, pe, rot, cos_b, sin_b):
        k_out, v_out = [], []
        for hh in range(hb):
            a = acc[:, hh * (nope + dv):(hh + 1) * (nope + dv)]
            k_out.append(norm_rope(a[:, :nope], pe, rot, g_t[:, :LANE], g_t[:, LANE:2 * LANE],
                                   g_t[:, 2 * LANE:], cos_b, sin_b, 1.0))
            v_out.append(a[:, nope:])
        return jnp.stack(k_out, axis=0), jnp.stack(v_out, axis=0)

    k, v = _mm("mla_kv", m,
               [(c, _row_spec(tm, kv_lora, qb)), (kv_a_norm.reshape(1, -1), _vec_spec(kv_lora))], _rms, w_kv_x,
               [(gk, _vec_spec(3 * LANE)), (c, _row_spec(tm, LANE, pe_block)),
                (c, _row_spec(tm, LANE, pe_block + 1)), (cos_t, _row_spec(tm, LANE)),
                (sin_t, _row_spec(tm, LANE))], kv_epi,
               [(jax.ShapeDtypeStruct((heads, m, 2 * LANE), BF16),
                 pl.BlockSpec((hb, tm, 2 * LANE), lambda i, j: (j, i, 0))),
                (jax.ShapeDtypeStruct((heads, m, dv), BF16),
                 pl.BlockSpec((hb, tm, dv), lambda i, j: (j, i, 0)))],
               tm=tm, tn=hb * (nope + dv))

    score_bound = scale * qk_dim * jnp.max(jnp.abs(q_norm)) * jnp.max(jnp.abs(k_norm))
    small_scores = (score_bound <= MAX_PLAIN_SCORE).astype(jnp.int32).reshape(1)
    o = _attention(batch, q, k, v, small_scores)
    return _simple_mm("mla_out", [_rows(o)], lambda t: t, w_o.reshape(heads * dv, d).astype(BF16),
                      [_epi_tile(x)], lambda acc, x_t: (x_t + acc,), tn_target=d)


def kernel(x, positions, ffn_norm, ffn_w13, ffn_w2, mix_norm, rwkv_mu, rwkv_w_rkv, rwkv_w0, rwkv_w1, rwkv_w2, rwkv_a0, rwkv_a1, rwkv_a2, rwkv_g1, rwkv_g2, rwkv_k_k, rwkv_k_a, rwkv_r_k, rwkv_ln_w, rwkv_ln_b, rwkv_w_o, mla_w_down, mla_q_a_norm, mla_kv_a_norm, mla_w_uq, mla_w_ukv, mla_q_norm, mla_k_norm, mla_w_o):
    batch, seq, d = x.shape
    depth = ffn_norm.shape[0]
    n_mixers = 2
    xf = x.reshape(batch * seq, d)
    d_ff = ffn_w2.shape[2]
    w13p, w2p = _ffn_weights(ffn_w13.reshape((2 * depth,) + ffn_w13.shape[2:]),
                             ffn_w2.reshape((2 * depth,) + ffn_w2.shape[2:]))
    for i in range(depth):
        j = i // n_mixers
        xf = _ffn(xf, ffn_norm[i, 0], w13p, w2p, 2 * i, d_ff)
        if i % n_mixers == 0:
            xf = _rwkv_mix(batch, xf, mix_norm[i], rwkv_mu[j], rwkv_w_rkv[j], rwkv_w0[j], rwkv_w1[j],
                           rwkv_w2[j], rwkv_a0[j], rwkv_a1[j], rwkv_a2[j], rwkv_g1[j], rwkv_g2[j],
                           rwkv_k_k[j], rwkv_k_a[j], rwkv_r_k[j], rwkv_ln_w[j], rwkv_ln_b[j],
                           rwkv_w_o[j])
        else:
            xf = _mla_mix(batch, xf, positions, mix_norm[i], mla_w_down[j], mla_q_a_norm[j],
                          mla_kv_a_norm[j], mla_w_uq[j], mla_w_ukv[j], mla_q_norm[j], mla_k_norm[j],
                          mla_w_o[j])
        xf = _ffn(xf, ffn_norm[i, 1], w13p, w2p, 2 * i + 1, d_ff)
    return xf.reshape(batch, seq, d)
```

```python
import functools
import math

import jax
import jax.numpy as jnp
from jax import lax
from jax.experimental import pallas as pl
from jax.experimental.pallas import tpu as pltpu

F32 = jnp.float32
BF16 = jnp.bfloat16

RMS_EPS = 1e-6
GN_EPS = 64e-5
ROPE_THETA = 10000.0
ATTN_CHUNK = 64
RWKV_HEAD = 64
SCAN_CHUNK = 64
HEADS_PER_GROUP = 4
MAX_PLAIN_SCORE = 30.0
ATTN_Q_TILE = 512
ATTN_HEADS_PER_STEP = 4
FFN_TILE = 512
MLA_HEADS_PER_STEP = 16
SCAN_BATCH_ROWS = 1
SCAN_STREAMS = 8
LANE = 128
SUBLANE = 8
VMEM_LIMIT_BYTES = 56 * 1024 * 1024


def _params(semantics):
    return pltpu.CompilerParams(dimension_semantics=semantics,
                                vmem_limit_bytes=VMEM_LIMIT_BYTES)


def _pick_tile(n, target, quantum):
    if n <= target:
        return n
    t = (target // quantum) * quantum
    while t > quantum and n % t:
        t -= quantum
    assert n % t == 0, (n, target, quantum)
    return t


def _round_robin(gens):
    live = list(gens)
    while live:
        live = [g for g in live if next(g, True) is None]


def _rms(x, g):
    return x * lax.rsqrt(jnp.mean(x * x, axis=-1, keepdims=True) + RMS_EPS) * g


def _ffn_body(x_ref, g_ref, wg_ref, wu_ref, w2_ref, o_ref, xn_ref):
    j = pl.program_id(1)

    @pl.when(j == 0)
    def _():
        xn_ref[...] = _rms(x_ref[...], g_ref[...]).astype(BF16)
        o_ref[...] = jnp.zeros_like(o_ref)

    xn = xn_ref[...]
    gate = jnp.dot(xn, wg_ref[...], preferred_element_type=F32)
    up = jnp.dot(xn, wu_ref[...], preferred_element_type=F32)
    act = (gate * jax.nn.sigmoid(gate) * up).astype(BF16)
    o_ref[...] += jnp.dot(act, w2_ref[...], preferred_element_type=F32)

    @pl.when(j == pl.num_programs(1) - 1)
    def _():
        o_ref[...] = x_ref[...] + 0.5 * o_ref[...]


def _ffn_tile(f):
    return min(FFN_TILE, -(-f // LANE) * LANE)


def _tile_cast_body(tiles, x_ref, o_ref):
    tw = o_ref.shape[-1]
    for t, (c0, width) in enumerate(tiles):
        o_ref[t, :, :width] = x_ref[:, c0:c0 + width].astype(BF16)
        if width < tw:
            o_ref[t, :, width:] = jnp.zeros((o_ref.shape[1], tw - width), BF16)


def _tile_cast(name, w, tiles, tw):
    n, rows, cols = w.shape
    tr = _pick_tile(rows, 128, SUBLANE)
    return pl.pallas_call(
        functools.partial(_tile_cast_body, tuple(tiles)),
        grid=(n, rows // tr),
        in_specs=[pl.BlockSpec((None, tr, cols), lambda l, i: (l, i, 0))],
        out_specs=pl.BlockSpec((None, len(tiles), tr, tw), lambda l, i: (l, 0, i, 0)),
        out_shape=jax.ShapeDtypeStruct((n, len(tiles), rows, tw), BF16),
        compiler_params=_params(("parallel", "parallel")),
        name=name,
    )(w)


def _cast_w2_body(f, x_ref, o_ref):
    tr = o_ref.shape[0]
    row = pl.program_id(1) * tr + lax.broadcasted_iota(jnp.int32, o_ref.shape, 0)
    o_ref[...] = jnp.where(row < f, x_ref[...], 0.0).astype(BF16)


def _ffn_weights(w13, w2):
    n, d, f2 = w13.shape
    f = f2 // 2
    assert f % LANE == 0
    tf = _ffn_tile(f)
    fp = -(-f // tf) * tf
    half = [(t * tf, min(tf, f - t * tf)) for t in range(fp // tf)]
    w13p = _tile_cast("cast_w13", w13, half + [(f + c0, width) for c0, width in half], tf)
    w2p = pl.pallas_call(
        functools.partial(_cast_w2_body, f),
        grid=(n, fp // tf),
        in_specs=[pl.BlockSpec((None, tf, d), lambda l, i: (l, i, 0))],
        out_specs=pl.BlockSpec((None, tf, d), lambda l, i: (l, i, 0)),
        out_shape=jax.ShapeDtypeStruct((n, fp, d), BF16),
        compiler_params=_params(("parallel", "parallel")),
        name="cast_w2",
    )(w2)
    return w13p, w2p


def _ffn(x, g, w13p, w2p, layer, f):
    m, d = x.shape
    fp = w2p.shape[1]
    tf = _ffn_tile(f)
    nj = fp // tf
    tm = _pick_tile(m, 512, 8)
    return pl.pallas_call(
        _ffn_body,
        grid=(m // tm, nj),
        in_specs=[
            pl.BlockSpec((tm, d), lambda i, j: (i, 0)),
            pl.BlockSpec((1, d), lambda i, j: (0, 0)),
            pl.BlockSpec((None, None, d, tf), lambda i, j: (layer, j, 0, 0)),
            pl.BlockSpec((None, None, d, tf), lambda i, j: (layer, j + nj, 0, 0)),
            pl.BlockSpec((None, tf, d), lambda i, j: (layer, j, 0)),
        ],
        out_specs=pl.BlockSpec((tm, d), lambda i, j: (i, 0)),
        out_shape=jax.ShapeDtypeStruct((m, d), F32),
        scratch_shapes=[pltpu.VMEM((tm, d), BF16)],
        compiler_params=_params(("parallel", "arbitrary")),
        name="ffn",
    )(x, g.reshape(1, d), w13p, w13p, w2p)


def _mm_body(n_lhs, n_epi, n_out, lhs_fn, epi_fn, *refs):
    lhs_refs = refs[:n_lhs]
    w_ref = refs[n_lhs]
    epi_refs = refs[n_lhs + 1:n_lhs + 1 + n_epi]
    out_refs = refs[n_lhs + 1 + n_epi:n_lhs + 1 + n_epi + n_out]
    lhs_scratch = refs[-1]

    @pl.when(pl.program_id(1) == 0)
    def _():
        lhs_scratch[...] = lhs_fn(*[r[...] for r in lhs_refs]).astype(BF16)

    acc = jnp.dot(lhs_scratch[...], w_ref[...], preferred_element_type=F32)
    outs = epi_fn(acc, *[r[...] for r in epi_refs])
    for o_ref, o in zip(out_refs, outs):
        o_ref[...] = o.astype(o_ref.dtype)


def _mm(name, m, lhs, lhs_fn, w, epi, epi_fn, outs, *, tm, tn):
    k, n = w.shape
    body = functools.partial(_mm_body, len(lhs), len(epi), len(outs), lhs_fn, epi_fn)
    return pl.pallas_call(
        body,
        grid=(m // tm, n // tn),
        in_specs=[s for _, s in lhs] + [pl.BlockSpec((k, tn), lambda i, j: (0, j))]
        + [s for _, s in epi],
        out_specs=[s for _, s in outs],
        out_shape=[s for s, _ in outs],
        scratch_shapes=[pltpu.VMEM((tm, k), BF16)],
        compiler_params=_params(("parallel", "arbitrary")),
        name=name,
    )(*[a for a, _ in lhs], w, *[a for a, _ in epi])


def _row_spec(tm, width, col_block=0):
    return pl.BlockSpec((tm, width), lambda i, j: (i, col_block))


def _tile_spec(tm, tn):
    return pl.BlockSpec((tm, tn), lambda i, j: (i, j))


def _vec_spec(width):
    return pl.BlockSpec((1, width), lambda i, j: (0, 0))


def _colvec_spec(tn):
    return pl.BlockSpec((1, tn), lambda i, j: (0, j))


def _simple_mm(name, lhs, lhs_fn, w, epi, epi_fn, out_dtype=F32, tm_target=512, tn_target=1024):
    m = lhs[0][0].shape[0]
    n = w.shape[1]
    tm = _pick_tile(m, tm_target, 8)
    tn = _pick_tile(n, tn_target, LANE)
    lhs = [(a, spec(tm)) for a, spec in lhs]
    epi = [(a, spec(tm, tn)) for a, spec in epi]
    out = (jax.ShapeDtypeStruct((m, n), out_dtype), _tile_spec(tm, tn))
    return _mm(name, m, lhs, lhs_fn, w, epi, epi_fn, [out], tm=tm, tn=tn)[0]


def _rows(a):
    return a, lambda tm: _row_spec(tm, a.shape[1])


def _vec(a):
    a = a.reshape(1, -1)
    return a, lambda tm: _vec_spec(a.shape[1])


def _epi_tile(a):
    return a, lambda tm, tn: _tile_spec(tm, tn)


def _epi_colvec(a):
    return a.reshape(1, -1), lambda tm, tn: _colvec_spec(tn)


def _rwkv_proj_body(seq_tiles, ranks, x_ref, xprev_ref, g_ref, mu_ref, wr_ref, wk_ref, wv_ref, l1_ref,
                    w2_ref, a2_ref, g2_ref, w0_ref, a0_ref, rkv_ref, ld_ref, ag_ref,
                    lerp_scr, tw_scr, ta_scr, tg_scr):
    i = pl.program_id(0)
    rw, ra = ranks

    @pl.when(pl.program_id(1) == 0)
    def _():
        h = _rms(x_ref[...], g_ref[...])
        prev = _rms(xprev_ref[...], g_ref[...])[SUBLANE - 1:SUBLANE, :]
        prev = jnp.where(i % seq_tiles == 0, 0.0, prev)
        row = lax.broadcasted_iota(jnp.int32, h.shape, 0)
        dx = jnp.where(row == 0, prev, pltpu.roll(h, 1, 0)) - h
        for idx in range(6):
            lerp_scr[idx] = (h + dx * mu_ref[idx:idx + 1, :]).astype(BF16)
        tw_scr[...] = jnp.tanh(jnp.dot(lerp_scr[3], l1_ref[:, :rw],
                                       preferred_element_type=F32)).astype(BF16)
        ta_scr[...] = jnp.dot(lerp_scr[4], l1_ref[:, rw:rw + ra],
                              preferred_element_type=F32).astype(BF16)
        tg_scr[...] = jax.nn.sigmoid(jnp.dot(lerp_scr[5], l1_ref[:, rw + ra:],
                                             preferred_element_type=F32)).astype(BF16)

    rkv_ref[0] = jnp.dot(lerp_scr[0], wr_ref[...], preferred_element_type=F32).astype(BF16)
    rkv_ref[1] = jnp.dot(lerp_scr[1], wk_ref[...], preferred_element_type=F32).astype(BF16)
    rkv_ref[2] = jnp.dot(lerp_scr[2], wv_ref[...], preferred_element_type=F32).astype(BF16)
    z = -(w0_ref[...] + jnp.dot(tw_scr[...], w2_ref[...], preferred_element_type=F32))
    softplus = jnp.maximum(z, 0.0) + jnp.log1p(jnp.exp(-jnp.abs(z)))
    ld_ref[...] = -jnp.exp(-softplus - 0.5)
    ag_ref[0] = jax.nn.sigmoid(
        a0_ref[...] + jnp.dot(ta_scr[...], a2_ref[...], preferred_element_type=F32)).astype(BF16)
    ag_ref[1] = jnp.dot(tg_scr[...], g2_ref[...], preferred_element_type=F32).astype(BF16)


def _pad_lora(wa, wb):
    rank = wa.shape[1]
    rp = -(-rank // LANE) * LANE
    return (jnp.pad(wa, ((0, 0), (0, rp - rank))).astype(BF16),
            jnp.pad(wb, ((0, rp - rank), (0, 0))).astype(BF16))


def _rwkv_proj(batch, x, norm_g, mu, w_rkv, w0, w1, w2, a0, a1, a2, g1, g2):
    m, d = x.shape
    tm = _pick_tile(m // batch, 512, SUBLANE)
    tn = _pick_tile(d, 256, LANE)
    w1p, w2p = _pad_lora(w1, w2)
    a1p, a2p = _pad_lora(a1, a2)
    g1p, g2p = _pad_lora(g1, g2)
    rw, ra, rg = w1p.shape[1], a1p.shape[1], g1p.shape[1]
    l1 = jnp.concatenate([w1p, a1p, g1p], axis=1)
    w_bf = _tile_cast("cast_w_rkv", w_rkv, [(t * tn, tn) for t in range(d // tn)], tn)
    mu_ord = mu[jnp.array([0, 2, 3, 1, 4, 5])]
    col = lambda i, j: (0, j)

    def w_plane(p):
        return pl.BlockSpec((None, None, d, tn), lambda i, j: (p, j, 0, 0))

    def planes(n):
        return pl.BlockSpec((n, tm, tn), lambda i, j: (0, i, j))

    body = functools.partial(_rwkv_proj_body, (m // batch) // tm, (rw, ra))
    return pl.pallas_call(
        body,
        grid=(m // tm, d // tn),
        in_specs=[
            pl.BlockSpec((tm, d), lambda i, j: (i, 0)),
            pl.BlockSpec((SUBLANE, d), lambda i, j: (jnp.maximum(i * (tm // SUBLANE) - 1, 0), 0)),
            pl.BlockSpec((1, d), lambda i, j: (0, 0)),
            pl.BlockSpec((6, d), lambda i, j: (0, 0)),
            w_plane(0), w_plane(1), w_plane(2),
            pl.BlockSpec((d, rw + ra + rg), lambda i, j: (0, 0)),
            pl.BlockSpec((rw, tn), col),
            pl.BlockSpec((ra, tn), col),
            pl.BlockSpec((rg, tn), col),
            pl.BlockSpec((1, tn), col),
            pl.BlockSpec((1, tn), col),
        ],
        out_specs=[planes(3), pl.BlockSpec((tm, tn), lambda i, j: (i, j)), planes(2)],
        out_shape=[jax.ShapeDtypeStruct((3, m, d), BF16), jax.ShapeDtypeStruct((m, d), F32),
                   jax.ShapeDtypeStruct((2, m, d), BF16)],
        scratch_shapes=[pltpu.VMEM((6, tm, d), BF16), pltpu.VMEM((tm, rw), BF16),
                        pltpu.VMEM((tm, ra), BF16), pltpu.VMEM((tm, rg), BF16)],
        compiler_params=_params(("parallel", "arbitrary")),
        name="rwkv_proj",
    )(x, x, norm_g.reshape(1, d), mu_ord, w_bf, w_bf, w_bf, l1, w2p, a2p, g2p,
      w0.reshape(1, d), a0.reshape(1, d))


def _bf(x):
    return x.astype(BF16)


def _dot(a, b):
    return jnp.dot(_bf(a), _bf(b), preferred_element_type=F32)


def _dot_nt(a, b):
    return lax.dot_general(_bf(a), _bf(b), (((1,), (1,)), ((), ())), preferred_element_type=F32)


def _dot_tn(a, b):
    return lax.dot_general(_bf(a), _bf(b), (((0,), (0,)), ((), ())), preferred_element_type=F32)


def _scan_body(n_streams, nb, r_ref, ld_ref, k_ref, v_ref, al_ref, g_ref, kk_w_ref, ka_w_ref,
               rk_w_ref, lnw_ref, lnb_ref, o_ref, s_ref):
    c = SCAN_CHUNK
    w = HEADS_PER_GROUP * RWKV_HEAD
    wc = HEADS_PER_GROUP * c

    @pl.when(pl.program_id(2) == 0)
    def _():
        s_ref[...] = jnp.zeros_like(s_ref)

    tri = jnp.where(lax.broadcasted_iota(jnp.int32, (c, c), 0)
                    >= lax.broadcasted_iota(jnp.int32, (c, c), 1), 1.0, 0.0).astype(BF16)
    bd_r = lax.broadcasted_iota(jnp.int32, (w, w), 0) // RWKV_HEAD
    bd_c = lax.broadcasted_iota(jnp.int32, (w, w), 1) // RWKV_HEAD
    head_mask = bd_r == bd_c
    head_ones = jnp.where(head_mask, 1.0, 0.0).astype(BF16)
    t_idx = lax.broadcasted_iota(jnp.int32, (c, wc), 0)
    s_idx = lax.broadcasted_iota(jnp.int32, (c, wc), 1) % c
    strict = s_idx < t_idx
    incl = s_idx <= t_idx
    eye = jnp.where(s_idx == t_idx, 1.0, 0.0)
    base_level = 3
    base_blocks = (t_idx >> base_level) == (s_idx >> base_level)
    join = {lv: (((t_idx >> lv) & 1) == 1) & ((s_idx >> lv) == (t_idx >> lv) - 1)
            for lv in range(base_level, c.bit_length() - 1)}

    def head_sum(x):
        return jnp.dot(_bf(x), head_ones, preferred_element_type=F32)

    def block_diag(x):
        xb = _bf(x)
        return jnp.where(head_mask, jnp.concatenate([xb] * HEADS_PER_GROUP, axis=0),
                         jnp.zeros((), BF16))

    def stream(bi, si):
        st = bi * n_streams + si
        ln = pl.ds(si * w, w)
        r = r_ref[bi, :, ln].astype(F32)
        ld = ld_ref[bi, :, ln]
        k = k_ref[bi, :, ln].astype(F32)
        v = v_ref[bi, :, ln].astype(F32)
        al = al_ref[bi, :, ln].astype(F32)

        ld_hi = _bf(ld)
        rest = ld - ld_hi.astype(F32)
        ld_mid = _bf(rest)
        ld_lo = _bf(rest - ld_mid.astype(F32))
        cum = (jnp.dot(tri, ld_hi, preferred_element_type=F32)
               + jnp.dot(tri, ld_mid, preferred_element_type=F32)
               + jnp.dot(tri, ld_lo, preferred_element_type=F32))
        yield
        p_incl = jnp.exp(cum)
        p_prev = jnp.exp(cum - ld)
        p_inv = jnp.exp(-cum)
        p_last = p_incl[c - 1:c, :]

        kk = k * kk_w_ref[:, ln]
        ss = head_sum(kk * kk)
        yield
        kk = kk / jnp.maximum(jnp.sqrt(ss), 1e-12)
        kmod = k * (1.0 + (al - 1.0) * ka_w_ref[:, ln])

        a_t = -kk * p_prev
        b_t = kk * al * p_inv
        k_t = kmod * p_inv
        r_t = r * p_incl

        ar = jnp.concatenate([a_t, r_t], axis=0)
        m_b = _dot_nt(ar, block_diag(b_t))
        m_k = _dot_nt(ar, block_diag(k_t))
        yield
        a_ab = jnp.where(strict, m_b[:c], 0.0)
        a_rb = jnp.where(incl, m_b[c:], 0.0)
        a_ak = jnp.where(strict, m_k[:c], 0.0)
        a_rk = jnp.where(incl, m_k[c:], 0.0)

        pw = jnp.where(base_blocks, a_ab, 0.0)
        t_inv = eye
        for step in range(base_level):
            if step < base_level - 1:
                z = _dot(jnp.concatenate([pw, t_inv], axis=0), block_diag(pw))
                yield
                pw, t_inv = z[:c], t_inv + z[c:]
            else:
                z = _dot(t_inv, block_diag(pw))
                yield
                t_inv = t_inv + z
        n_bd = block_diag(a_ab)
        level = base_level
        while (1 << level) < c:
            tn = _dot(t_inv, n_bd)
            yield
            x = _dot(tn, block_diag(t_inv))
            yield
            t_inv = t_inv + jnp.where(join[level], x, 0.0)
            level += 1

        s0 = s_ref[st]
        w1 = _dot_nt(ar, s0)
        w2 = _dot(jnp.concatenate([a_ak, a_rk], axis=0), block_diag(v))
        yield
        u = _dot(t_inv, block_diag(w1[:c] + w2[:c]))
        yield
        y = w1[c:] + w2[c:] + _dot(a_rb, block_diag(u))

        uv = jnp.concatenate([u, v], axis=0)
        bk = jnp.concatenate([b_t, k_t], axis=0) * p_last
        s_new = _dot_tn(uv, bk)
        yield
        s_ref[st] = jnp.where(head_mask, s0 * p_last + s_new, 0.0)

        inv_n = 1.0 / RWKV_HEAD
        sums = head_sum(jnp.concatenate([y, r * kmod * rk_w_ref[:, ln]], axis=0))
        yield
        dlt = y - sums[:c] * inv_n
        bsum = sums[c:]
        var = head_sum(dlt * dlt) * inv_n
        yield
        yn = dlt * lax.rsqrt(var + GN_EPS) * lnw_ref[:, ln] + lnb_ref[:, ln]
        bonus = bsum * v
        o_ref[bi, :, ln] = ((yn + bonus) * g_ref[bi, :, ln]).astype(o_ref.dtype)

    _round_robin([stream(bi, si) for bi in range(nb) for si in range(n_streams)])


def _rwkv_scan(batch, rkv, ld, ag, k_k, k_a, r_k, ln_w, ln_b):
    _, m, d = rkv.shape
    t = m // batch
    c = SCAN_CHUNK
    w = HEADS_PER_GROUP * RWKV_HEAD
    assert d % w == 0 and t % c == 0 and c == RWKV_HEAD
    n_streams = _pick_tile(d // w, SCAN_STREAMS, 1)
    wb = n_streams * w
    nc = t // c
    nb = _pick_tile(batch, SCAN_BATCH_ROWS, 1)
    tok = pl.BlockSpec((nb, c, wb), lambda b, gi, ci: (b, ci, gi))
    par = pl.BlockSpec((1, wb), lambda b, gi, ci: (0, gi))

    def plane(p):
        return pl.BlockSpec((None, nb, c, wb), lambda b, gi, ci: (p, b, ci, gi))

    rkv4 = rkv.reshape(3, batch, t, d)
    ag4 = ag.reshape(2, batch, t, d)
    return pl.pallas_call(
        functools.partial(_scan_body, n_streams, nb),
        grid=(batch // nb, d // wb, nc),
        in_specs=[plane(0), tok, plane(1), plane(2), plane(0), plane(1)] + [par] * 5,
        out_specs=tok,
        out_shape=jax.ShapeDtypeStruct((batch, t, d), BF16),
        scratch_shapes=[pltpu.VMEM((nb * n_streams, w, w), F32)],
        compiler_params=_params(("parallel", "parallel", "arbitrary")),
        name="rwkv_scan",
    )(rkv4, ld.reshape(batch, t, d), rkv4, rkv4, ag4, ag4,
      *[p.reshape(1, d) for p in (k_k, k_a, r_k, ln_w, ln_b)]).reshape(m, d)


def _rwkv_mix(batch, x, norm_g, mu, w_rkv, w0, w1, w2, a0, a1, a2, g1, g2,
              k_k, k_a, r_k, ln_w, ln_b, w_o):
    rkv, ld, ag = _rwkv_proj(batch, x, norm_g, mu, w_rkv, w0, w1, w2, a0, a1, a2, g1, g2)
    yg = _rwkv_scan(batch, rkv, ld, ag, k_k, k_a, r_k.reshape(-1), ln_w, ln_b)
    return _simple_mm("rwkv_out", [_rows(yg)], lambda t: t, w_o.astype(BF16), [_epi_tile(x)],
                      lambda acc, x_t: (x_t + acc,), tn_target=x.shape[1])


def _attn_body(tq, tk, nh, small_ref, q_ref, k_ref, v_ref, o_ref, v1_ref):
    qi = pl.program_id(2)
    dv = v_ref.shape[-1]

    def scores(h, j, masked):
        start = pl.multiple_of(j * tk, tk)
        kb = k_ref[h, pl.ds(start, tk), :]
        s = lax.dot_general(q_ref[h], kb, (((1,), (1,)), ((), ())), preferred_element_type=F32)
        if masked:
            q_chunk = (qi * tq + lax.broadcasted_iota(jnp.int32, (tq, tk), 0)) // ATTN_CHUNK
            k_chunk = (j * tk + lax.broadcasted_iota(jnp.int32, (tq, tk), 1)) // ATTN_CHUNK
            s = jnp.where(k_chunk <= q_chunk, s, jnp.finfo(F32).min)
        return s

    def values(h, j):
        return v_ref[h, pl.ds(pl.multiple_of(j * tk, tk), tk), :]

    def all_heads(make_gen):
        outs = [[] for _ in range(nh)]
        _round_robin([make_gen(h, outs[h]) for h in range(nh)])
        return tuple(o[0] for o in outs)

    def sweep(step, init, diag_step=None):
        carries = lax.fori_loop(
            0, qi, lambda j, cr: all_heads(lambda h, out: step(h, j, cr[h], False, out)), init)
        if diag_step is not None:
            return all_heads(lambda h, out: diag_step(h, carries[h], out))
        return all_heads(lambda h, out: step(h, qi, carries[h], True, out))

    def online_step(h, j, carry, masked, out):
        m_i, l_i, acc = carry
        s = scores(h, j, masked)
        yield
        m_new = jnp.maximum(m_i, jnp.max(s, axis=-1, keepdims=True))
        p = jnp.exp(s - m_new)
        alpha = jnp.exp(m_i - m_new)
        l_new = alpha * l_i + jnp.sum(p, axis=-1, keepdims=True)
        pv = jnp.dot(p.astype(BF16), values(h, j), preferred_element_type=F32)
        yield
        out.append((m_new, l_new, alpha * acc + pv))

    def plain_step(h, j, acc, masked, out):
        s = scores(h, j, masked)
        yield
        p = jnp.exp(s).astype(BF16)
        v1 = v1_ref[h, pl.ds(pl.multiple_of(j * tk, tk), tk), :]
        pv = jnp.dot(p, v1, preferred_element_type=F32)
        yield
        out.append(acc + pv)

    def plain_diag(h, acc, out):
        half = tk // 2
        base = pl.multiple_of(qi * tk, tk)
        visible = (lax.broadcasted_iota(jnp.int32, (tq, half), 1) // ATTN_CHUNK
                   <= lax.broadcasted_iota(jnp.int32, (tq, half), 0) // ATTN_CHUNK)

        def weights(q_rows, key0, mask):
            kb = k_ref[h, pl.ds(key0, half), :]
            s = lax.dot_general(q_rows, kb, (((1,), (1,)), ((), ())), preferred_element_type=F32)
            return jnp.where(mask, jnp.exp(s), 0.0).astype(BF16)

        p_a = weights(q_ref[h], base, visible)
        yield
        pv_a = jnp.dot(p_a, v1_ref[h, pl.ds(base, half), :], preferred_element_type=F32)
        p_b = weights(q_ref[h, half:, :], base + half, visible[:half])
        yield
        pv_b = jnp.dot(p_b, v1_ref[h, pl.ds(base + half, half), :], preferred_element_type=F32)
        yield
        out.append(acc + jnp.concatenate([pv_a[:half], pv_a[half:] + pv_b], axis=0))

    @pl.when(small_ref[0] == 1)
    def _():
        @pl.when(qi == 0)
        def _():
            v1_ref[:, :, :dv] = v_ref[...]
            v1_ref[:, :, dv:] = jnp.ones(v_ref.shape, BF16)

        init = tuple(jnp.zeros((tq, 2 * dv), F32) for _ in range(nh))
        for h, acc in enumerate(sweep(plain_step, init, plain_diag)):
            o_ref[:, h * dv:(h + 1) * dv] = (acc[:, :dv] / acc[:, dv:]).astype(o_ref.dtype)

    @pl.when(small_ref[0] != 1)
    def _():
        init = tuple((jnp.full((tq, 1), -jnp.inf, F32), jnp.zeros((tq, 1), F32),
                      jnp.zeros((tq, dv), F32)) for _ in range(nh))
        for h, (_, l_i, acc) in enumerate(sweep(online_step, init)):
            o_ref[:, h * dv:(h + 1) * dv] = (acc / l_i).astype(o_ref.dtype)


def _attention(batch, q, k, v, small_scores):
    heads, m, dqk = q.shape
    dv = v.shape[-1]
    t = m // batch
    tq = _pick_tile(t, ATTN_Q_TILE, 2 * ATTN_CHUNK)
    assert tq % (2 * ATTN_CHUNK) == 0
    nq = t // tq
    nh = _pick_tile(heads, ATTN_HEADS_PER_STEP, 1)
    return pl.pallas_call(
        functools.partial(_attn_body, tq, tq, nh),
        grid=(batch, heads // nh, nq),
        in_specs=[
            pl.BlockSpec(memory_space=pltpu.SMEM),
            pl.BlockSpec((nh, tq, dqk), lambda b, h, i: (h, b * nq + i, 0)),
            pl.BlockSpec((nh, t, dqk), lambda b, h, i: (h, b, 0)),
            pl.BlockSpec((nh, t, dv), lambda b, h, i: (h, b, 0)),
        ],
        out_specs=pl.BlockSpec((tq, nh * dv), lambda b, h, i: (b * nq + i, h)),
        out_shape=jax.ShapeDtypeStruct((m, heads * dv), BF16),
        scratch_shapes=[pltpu.VMEM((nh, t, 2 * dv), BF16)],
        compiler_params=_params(("parallel", "parallel", "arbitrary")),
        name="mla_attention",
    )(small_scores, q, k, v)


def _mla_mix(batch, x, positions, norm_g, w_down, q_a_norm, kv_a_norm, w_uq, w_ukv,
             q_norm, k_norm, w_o):
    m, d = x.shape
    q_lora = q_a_norm.shape[0]
    kv_lora = kv_a_norm.shape[0]
    heads, qk_dim = w_uq.shape[1], w_uq.shape[2]
    rope = w_down.shape[1] - q_lora - kv_lora
    nope = qk_dim - rope
    dv = w_ukv.shape[2] - nope
    half = rope // 2
    assert nope == LANE and dv == LANE and rope <= LANE and q_lora == kv_lora and q_lora % LANE == 0
    scale = 1.0 / math.sqrt(qk_dim)

    perm = jnp.concatenate([jnp.arange(half, rope), jnp.arange(0, half)])
    lane_pad = lambda a: jnp.pad(a, [(0, 0)] * (a.ndim - 1) + [(0, LANE - a.shape[-1])])

    inv_freq = ROPE_THETA ** (-jnp.arange(0, rope, 2, dtype=F32) / rope)
    ang = positions.reshape(m, 1).astype(F32) * inv_freq
    cos, sin = jnp.cos(ang), jnp.sin(ang)
    cos_t = lane_pad(jnp.concatenate([cos, cos], axis=-1))
    sin_t = lane_pad(jnp.concatenate([-sin, sin], axis=-1))

    w_kpe = w_down[:, q_lora + kv_lora:]
    w_down_x = jnp.concatenate(
        [w_down[:, :q_lora + kv_lora], lane_pad(w_kpe), lane_pad(w_kpe[:, perm])], axis=1).astype(BF16)
    c = _simple_mm("mla_down", [_rows(x), _vec(norm_g)], _rms, w_down_x, [], lambda acc: (acc,),
                   tn_target=w_down_x.shape[1])
    qb = q_lora // q_lora
    tm = _pick_tile(m, 512, 8)

    def norm_rope(body, pe, rot, g_body, g_pe, g_rot, cos_b, sin_b, extra_scale):
        ss = jnp.sum(body * body, axis=-1, keepdims=True) + jnp.sum(pe * pe, axis=-1, keepdims=True)
        s = lax.rsqrt(ss / qk_dim + RMS_EPS) * extra_scale
        return jnp.concatenate([body * s * g_body, (pe * g_pe * cos_b + rot * g_rot * sin_b) * s], axis=-1)

    w_q_pe = w_uq[:, :, nope:]
    w_q_x = jnp.concatenate([w_uq[:, :, :nope], lane_pad(w_q_pe), lane_pad(w_q_pe[:, :, perm])],
                            axis=-1).reshape(q_lora, heads * 3 * LANE).astype(BF16)
    gq = jnp.concatenate([q_norm[:nope], lane_pad(q_norm[nope:]), lane_pad(q_norm[nope:][perm])]).reshape(1, -1)

    hb = _pick_tile(heads, MLA_HEADS_PER_STEP, 1)

    def q_epi(acc, g_t, cos_b, sin_b):
        per_head = []
        for hh in range(hb):
            a = acc[:, hh * 3 * LANE:(hh + 1) * 3 * LANE]
            per_head.append(norm_rope(a[:, :LANE], a[:, LANE:2 * LANE], a[:, 2 * LANE:], g_t[:, :LANE],
                                      g_t[:, LANE:2 * LANE], g_t[:, 2 * LANE:], cos_b, sin_b, scale))
        return (jnp.stack(per_head, axis=0),)

    q = _mm("mla_q", m,
            [(c, _row_spec(tm, q_lora, 0)), (q_a_norm.reshape(1, -1), _vec_spec(q_lora))], _rms, w_q_x,
            [(gq, _vec_spec(3 * LANE)), (cos_t, _row_spec(tm, LANE)), (sin_t, _row_spec(tm, LANE))], q_epi,
            [(jax.ShapeDtypeStruct((heads, m, 2 * LANE), BF16),
              pl.BlockSpec((hb, tm, 2 * LANE), lambda i, j: (j, i, 0)))],
            tm=tm, tn=hb * 3 * LANE)[0]

    w_kv_x = w_ukv.reshape(kv_lora, heads * (nope + dv)).astype(BF16)
    gk = jnp.concatenate([k_norm[:nope], lane_pad(k_norm[nope:]), lane_pad(k_norm[nope:][perm])]).reshape(1, -1)
    pe_block = (q_lora + kv_lora) // LANE

    def kv_epi(acc, g_t, pe, rot, cos_b, sin_b):
        k_out, v_out = [], []
        for hh in range(hb):
            a = acc[:, hh * (nope + dv):(hh + 1) * (nope + dv)]
            k_out.append(norm_rope(a[:, :nope], pe, rot, g_t[:, :LANE], g_t[:, LANE:2 * LANE],
                                   g_t[:, 2 * LANE:], cos_b, sin_b, 1.0))
            v_out.append(a[:, nope:])
        return jnp.stack(k_out, axis=0), jnp.stack(v_out, axis=0)

    k, v = _mm("mla_kv", m,
               [(c, _row_spec(tm, kv_lora, qb)), (kv_a_norm.reshape(1, -1), _vec_spec(kv_lora))], _rms, w_kv_x,
               [(gk, _vec_spec(3 * LANE)), (c, _row_spec(tm, LANE, pe_block)),
                (c, _row_spec(tm, LANE, pe_block + 1)), (cos_t, _row_spec(tm, LANE)),
                (sin_t, _row_spec(tm, LANE))], kv_epi,
               [(jax.ShapeDtypeStruct((heads, m, 2 * LANE), BF16),
                 pl.BlockSpec((hb, tm, 2 * LANE), lambda i, j: (j, i, 0))),
                (jax.ShapeDtypeStruct((heads, m, dv), BF16),
                 pl.BlockSpec((hb, tm, dv), lambda i, j: (j, i, 0)))],
               tm=tm, tn=hb * (nope + dv))

    score_bound = scale * qk_dim * jnp.max(jnp.abs(q_norm)) * jnp.max(jnp.abs(k_norm))
    small_scores = (score_bound <= MAX_PLAIN_SCORE).astype(jnp.int32).reshape(1)
    o = _attention(batch, q, k, v, small_scores)
    return _simple_mm("mla_out", [_rows(o)], lambda t: t, w_o.reshape(heads * dv, d).astype(BF16),
                      [_epi_tile(x)], lambda acc, x_t: (x_t + acc,), tn_target=d)


def kernel(x, positions, ffn_norm, ffn_w13, ffn_w2, mix_norm, rwkv_mu, rwkv_w_rkv, rwkv_w0, rwkv_w1, rwkv_w2, rwkv_a0, rwkv_a1, rwkv_a2, rwkv_g1, rwkv_g2, rwkv_k_k, rwkv_k_a, rwkv_r_k, rwkv_ln_w, rwkv_ln_b, rwkv_w_o, mla_w_down, mla_q_a_norm, mla_kv_a_norm, mla_w_uq, mla_w_ukv, mla_q_norm, mla_k_norm, mla_w_o):
    batch, seq, d = x.shape
    depth = ffn_norm.shape[0]
    n_mixers = 2
    xf = x.reshape(batch * seq, d)
    d_ff = ffn_w2.shape[2]
    w13p, w2p = _ffn_weights(ffn_w13.reshape((2 * depth,) + ffn_w13.shape[2:]),
                             ffn_w2.reshape((2 * depth,) + ffn_w2.shape[2:]))
    for i in range(depth):
        j = i // n_mixers
        xf = _ffn(xf, ffn_norm[i, 0], w13p, w2p, 2 * i, d_ff)
        if i % n_mixers == 0:
            xf = _rwkv_mix(batch, xf, mix_norm[i], rwkv_mu[j], rwkv_w_rkv[j], rwkv_w0[j], rwkv_w1[j],
                           rwkv_w2[j], rwkv_a0[j], rwkv_a1[j], rwkv_a2[j], rwkv_g1[j], rwkv_g2[j],
                           rwkv_k_k[j], rwkv_k_a[j], rwkv_r_k[j], rwkv_ln_w[j], rwkv_ln_b[j],
                           rwkv_w_o[j])
        else:
            xf = _mla_mix(batch, xf, positions, mix_norm[i], mla_w_down[j], mla_q_a_norm[j],
                          mla_kv_a_norm[j], mla_w_uq[j], mla_w_ukv[j], mla_q_norm[j], mla_k_norm[j],
                          mla_w_o[j])
        xf = _ffn(xf, ffn_norm[i, 1], w13p, w2p, 2 * i + 1, d_ff)
    return xf.reshape(batch, seq, d)
```

```python
import functools
import math

import jax
import jax.numpy as jnp
from jax import lax
from jax.experimental import pallas as pl
from jax.experimental.pallas import tpu as pltpu

F32 = jnp.float32
BF16 = jnp.bfloat16

RMS_EPS = 1e-6
GN_EPS = 64e-5
ROPE_THETA = 10000.0
ATTN_CHUNK = 64
RWKV_HEAD = 64
SCAN_CHUNK = 64
HEADS_PER_GROUP = 4
MAX_PLAIN_SCORE = 30.0
ATTN_Q_TILE = 512
ATTN_HEADS_PER_STEP = 4
FFN_TILE = 512
MLA_HEADS_PER_STEP = 16
SCAN_CHUNKS_PER_STEP = 2
SCAN_BATCH_ROWS = 1
SCAN_STREAMS = 8
LANE = 128
SUBLANE = 8
VMEM_LIMIT_BYTES = 56 * 1024 * 1024


def _params(semantics):
    return pltpu.CompilerParams(dimension_semantics=semantics,
                                vmem_limit_bytes=VMEM_LIMIT_BYTES)


def _pick_tile(n, target, quantum):
    if n <= target:
        return n
    t = (target // quantum) * quantum
    while t > quantum and n % t:
        t -= quantum
    assert n % t == 0, (n, target, quantum)
    return t


def _round_robin(gens):
    live = list(gens)
    while live:
        live = [g for g in live if next(g, True) is None]


def _rms(x, g):
    return x * lax.rsqrt(jnp.mean(x * x, axis=-1, keepdims=True) + RMS_EPS) * g


def _ffn_body(x_ref, g_ref, wg_ref, wu_ref, w2_ref, o_ref, xn_ref):
    j = pl.program_id(1)

    @pl.when(j == 0)
    def _():
        xn_ref[...] = _rms(x_ref[...], g_ref[...]).astype(BF16)
        o_ref[...] = jnp.zeros_like(o_ref)

    xn = xn_ref[...]
    gate = jnp.dot(xn, wg_ref[...], preferred_element_type=F32)
    up = jnp.dot(xn, wu_ref[...], preferred_element_type=F32)
    act = (gate * jax.nn.sigmoid(gate) * up).astype(BF16)
    o_ref[...] += jnp.dot(act, w2_ref[...], preferred_element_type=F32)

    @pl.when(j == pl.num_programs(1) - 1)
    def _():
        o_ref[...] = x_ref[...] + 0.5 * o_ref[...]


def _ffn_tile(f):
    return min(FFN_TILE, -(-f // LANE) * LANE)


def _tile_cast_body(tiles, x_ref, o_ref):
    tw = o_ref.shape[-1]
    for t, (c0, width) in enumerate(tiles):
        o_ref[t, :, :width] = x_ref[:, c0:c0 + width].astype(BF16)
        if width < tw:
            o_ref[t, :, width:] = jnp.zeros((o_ref.shape[1], tw - width), BF16)


def _tile_cast(name, w, tiles, tw):
    n, rows, cols = w.shape
    tr = _pick_tile(rows, 128, SUBLANE)
    return pl.pallas_call(
        functools.partial(_tile_cast_body, tuple(tiles)),
        grid=(n, rows // tr),
        in_specs=[pl.BlockSpec((None, tr, cols), lambda l, i: (l, i, 0))],
        out_specs=pl.BlockSpec((None, len(tiles), tr, tw), lambda l, i: (l, 0, i, 0)),
        out_shape=jax.ShapeDtypeStruct((n, len(tiles), rows, tw), BF16),
        compiler_params=_params(("parallel", "parallel")),
        name=name,
    )(w)


def _cast_w2_body(f, x_ref, o_ref):
    tr = o_ref.shape[0]
    row = pl.program_id(1) * tr + lax.broadcasted_iota(jnp.int32, o_ref.shape, 0)
    o_ref[...] = jnp.where(row < f, x_ref[...], 0.0).astype(BF16)


def _ffn_weights(w13, w2):
    n, d, f2 = w13.shape
    f = f2 // 2
    assert f % LANE == 0
    tf = _ffn_tile(f)
    fp = -(-f // tf) * tf
    half = [(t * tf, min(tf, f - t * tf)) for t in range(fp // tf)]
    w13p = _tile_cast("cast_w13", w13, half + [(f + c0, width) for c0, width in half], tf)
    w2p = pl.pallas_call(
        functools.partial(_cast_w2_body, f),
        grid=(n, fp // tf),
        in_specs=[pl.BlockSpec((None, tf, d), lambda l, i: (l, i, 0))],
        out_specs=pl.BlockSpec((None, tf, d), lambda l, i: (l, i, 0)),
        out_shape=jax.ShapeDtypeStruct((n, fp, d), BF16),
        compiler_params=_params(("parallel", "parallel")),
        name="cast_w2",
    )(w2)
    return w13p, w2p


def _ffn(x, g, w13p, w2p, layer, f):
    m, d = x.shape
    fp = w2p.shape[1]
    tf = _ffn_tile(f)
    nj = fp // tf
    tm = _pick_tile(m, 512, 8)
    return pl.pallas_call(
        _ffn_body,
        grid=(m // tm, nj),
        in_specs=[
            pl.BlockSpec((tm, d), lambda i, j: (i, 0)),
            pl.BlockSpec((1, d), lambda i, j: (0, 0)),
            pl.BlockSpec((None, None, d, tf), lambda i, j: (layer, j, 0, 0)),
            pl.BlockSpec((None, None, d, tf), lambda i, j: (layer, j + nj, 0, 0)),
            pl.BlockSpec((None, tf, d), lambda i, j: (layer, j, 0)),
        ],
        out_specs=pl.BlockSpec((tm, d), lambda i, j: (i, 0)),
        out_shape=jax.ShapeDtypeStruct((m, d), F32),
        scratch_shapes=[pltpu.VMEM((tm, d), BF16)],
        compiler_params=_params(("parallel", "arbitrary")),
        name="ffn",
    )(x, g.reshape(1, d), w13p, w13p, w2p)


def _mm_body(n_lhs, n_epi, n_out, lhs_fn, epi_fn, *refs):
    lhs_refs = refs[:n_lhs]
    w_ref = refs[n_lhs]
    epi_refs = refs[n_lhs + 1:n_lhs + 1 + n_epi]
    out_refs = refs[n_lhs + 1 + n_epi:n_lhs + 1 + n_epi + n_out]
    lhs_scratch = refs[-1]

    @pl.when(pl.program_id(1) == 0)
    def _():
        lhs_scratch[...] = lhs_fn(*[r[...] for r in lhs_refs]).astype(BF16)

    acc = jnp.dot(lhs_scratch[...], w_ref[...], preferred_element_type=F32)
    outs = epi_fn(acc, *[r[...] for r in epi_refs])
    for o_ref, o in zip(out_refs, outs):
        o_ref[...] = o.astype(o_ref.dtype)


def _mm(name, m, lhs, lhs_fn, w, epi, epi_fn, outs, *, tm, tn):
    k, n = w.shape
    body = functools.partial(_mm_body, len(lhs), len(epi), len(outs), lhs_fn, epi_fn)
    return pl.pallas_call(
        body,
        grid=(m // tm, n // tn),
        in_specs=[s for _, s in lhs] + [pl.BlockSpec((k, tn), lambda i, j: (0, j))]
        + [s for _, s in epi],
        out_specs=[s for _, s in outs],
        out_shape=[s for s, _ in outs],
        scratch_shapes=[pltpu.VMEM((tm, k), BF16)],
        compiler_params=_params(("parallel", "arbitrary")),
        name=name,
    )(*[a for a, _ in lhs], w, *[a for a, _ in epi])


def _row_spec(tm, width, col_block=0):
    return pl.BlockSpec((tm, width), lambda i, j: (i, col_block))


def _tile_spec(tm, tn):
    return pl.BlockSpec((tm, tn), lambda i, j: (i, j))


def _vec_spec(width):
    return pl.BlockSpec((1, width), lambda i, j: (0, 0))


def _colvec_spec(tn):
    return pl.BlockSpec((1, tn), lambda i, j: (0, j))


def _simple_mm(name, lhs, lhs_fn, w, epi, epi_fn, out_dtype=F32, tm_target=512, tn_target=1024):
    m = lhs[0][0].shape[0]
    n = w.shape[1]
    tm = _pick_tile(m, tm_target, 8)
    tn = _pick_tile(n, tn_target, LANE)
    lhs = [(a, spec(tm)) for a, spec in lhs]
    epi = [(a, spec(tm, tn)) for a, spec in epi]
    out = (jax.ShapeDtypeStruct((m, n), out_dtype), _tile_spec(tm, tn))
    return _mm(name, m, lhs, lhs_fn, w, epi, epi_fn, [out], tm=tm, tn=tn)[0]


def _rows(a):
    return a, lambda tm: _row_spec(tm, a.shape[1])


def _vec(a):
    a = a.reshape(1, -1)
    return a, lambda tm: _vec_spec(a.shape[1])


def _epi_tile(a):
    return a, lambda tm, tn: _tile_spec(tm, tn)


def _epi_colvec(a):
    return a.reshape(1, -1), lambda tm, tn: _colvec_spec(tn)


def _rwkv_proj_body(seq_tiles, ranks, x_ref, xprev_ref, g_ref, mu_ref, wr_ref, wk_ref, wv_ref, l1_ref,
                    w2_ref, a2_ref, g2_ref, w0_ref, a0_ref, rkv_ref, ld_ref, ag_ref,
                    lerp_scr, tw_scr, ta_scr, tg_scr):
    i = pl.program_id(0)
    rw, ra = ranks

    @pl.when(pl.program_id(1) == 0)
    def _():
        h = _rms(x_ref[...], g_ref[...])
        prev = _rms(xprev_ref[...], g_ref[...])[SUBLANE - 1:SUBLANE, :]
        prev = jnp.where(i % seq_tiles == 0, 0.0, prev)
        row = lax.broadcasted_iota(jnp.int32, h.shape, 0)
        dx = jnp.where(row == 0, prev, pltpu.roll(h, 1, 0)) - h
        for idx in range(6):
            lerp_scr[idx] = (h + dx * mu_ref[idx:idx + 1, :]).astype(BF16)
        tw_scr[...] = jnp.tanh(jnp.dot(lerp_scr[3], l1_ref[:, :rw],
                                       preferred_element_type=F32)).astype(BF16)
        ta_scr[...] = jnp.dot(lerp_scr[4], l1_ref[:, rw:rw + ra],
                              preferred_element_type=F32).astype(BF16)
        tg_scr[...] = jax.nn.sigmoid(jnp.dot(lerp_scr[5], l1_ref[:, rw + ra:],
                                             preferred_element_type=F32)).astype(BF16)

    rkv_ref[0] = jnp.dot(lerp_scr[0], wr_ref[...], preferred_element_type=F32).astype(BF16)
    rkv_ref[1] = jnp.dot(lerp_scr[1], wk_ref[...], preferred_element_type=F32).astype(BF16)
    rkv_ref[2] = jnp.dot(lerp_scr[2], wv_ref[...], preferred_element_type=F32).astype(BF16)
    z = -(w0_ref[...] + jnp.dot(tw_scr[...], w2_ref[...], preferred_element_type=F32))
    softplus = jnp.maximum(z, 0.0) + jnp.log1p(jnp.exp(-jnp.abs(z)))
    ld_ref[...] = -jnp.exp(-softplus - 0.5)
    ag_ref[0] = jax.nn.sigmoid(
        a0_ref[...] + jnp.dot(ta_scr[...], a2_ref[...], preferred_element_type=F32)).astype(BF16)
    ag_ref[1] = jnp.dot(tg_scr[...], g2_ref[...], preferred_element_type=F32).astype(BF16)


def _pad_lora(wa, wb):
    rank = wa.shape[1]
    rp = -(-rank // LANE) * LANE
    return (jnp.pad(wa, ((0, 0), (0, rp - rank))).astype(BF16),
            jnp.pad(wb, ((0, rp - rank), (0, 0))).astype(BF16))


def _rwkv_proj(batch, x, norm_g, mu, w_rkv, w0, w1, w2, a0, a1, a2, g1, g2):
    m, d = x.shape
    tm = _pick_tile(m // batch, 512, SUBLANE)
    tn = _pick_tile(d, 512, LANE)
    w1p, w2p = _pad_lora(w1, w2)
    a1p, a2p = _pad_lora(a1, a2)
    g1p, g2p = _pad_lora(g1, g2)
    rw, ra, rg = w1p.shape[1], a1p.shape[1], g1p.shape[1]
    l1 = jnp.concatenate([w1p, a1p, g1p], axis=1)
    w_bf = _tile_cast("cast_w_rkv", w_rkv, [(t * tn, tn) for t in range(d // tn)], tn)
    mu_ord = mu[jnp.array([0, 2, 3, 1, 4, 5])]
    col = lambda i, j: (0, j)

    def w_plane(p):
        return pl.BlockSpec((None, None, d, tn), lambda i, j: (p, j, 0, 0))

    def planes(n):
        return pl.BlockSpec((n, tm, tn), lambda i, j: (0, i, j))

    body = functools.partial(_rwkv_proj_body, (m // batch) // tm, (rw, ra))
    return pl.pallas_call(
        body,
        grid=(m // tm, d // tn),
        in_specs=[
            pl.BlockSpec((tm, d), lambda i, j: (i, 0)),
            pl.BlockSpec((SUBLANE, d), lambda i, j: (jnp.maximum(i * (tm // SUBLANE) - 1, 0), 0)),
            pl.BlockSpec((1, d), lambda i, j: (0, 0)),
            pl.BlockSpec((6, d), lambda i, j: (0, 0)),
            w_plane(0), w_plane(1), w_plane(2),
            pl.BlockSpec((d, rw + ra + rg), lambda i, j: (0, 0)),
            pl.BlockSpec((rw, tn), col),
            pl.BlockSpec((ra, tn), col),
            pl.BlockSpec((rg, tn), col),
            pl.BlockSpec((1, tn), col),
            pl.BlockSpec((1, tn), col),
        ],
        out_specs=[planes(3), pl.BlockSpec((tm, tn), lambda i, j: (i, j)), planes(2)],
        out_shape=[jax.ShapeDtypeStruct((3, m, d), BF16), jax.ShapeDtypeStruct((m, d), F32),
                   jax.ShapeDtypeStruct((2, m, d), BF16)],
        scratch_shapes=[pltpu.VMEM((6, tm, d), BF16), pltpu.VMEM((tm, rw), BF16),
                        pltpu.VMEM((tm, ra), BF16), pltpu.VMEM((tm, rg), BF16)],
        compiler_params=_params(("parallel", "arbitrary")),
        name="rwkv_proj",
    )(x, x, norm_g.reshape(1, d), mu_ord, w_bf, w_bf, w_bf, l1, w2p, a2p, g2p,
      w0.reshape(1, d), a0.reshape(1, d))


def _bf(x):
    return x.astype(BF16)


def _dot(a, b):
    return jnp.dot(_bf(a), _bf(b), preferred_element_type=F32)


def _dot_nt(a, b):
    return lax.dot_general(_bf(a), _bf(b), (((1,), (1,)), ((), ())), preferred_element_type=F32)


def _dot_tn(a, b):
    return lax.dot_general(_bf(a), _bf(b), (((0,), (0,)), ((), ())), preferred_element_type=F32)


def _scan_body(n_streams, nb, cps, r_ref, ld_ref, k_ref, v_ref, al_ref, g_ref, kk_w_ref, ka_w_ref,
               rk_w_ref, lnw_ref, lnb_ref, o_ref, s_ref):
    c = SCAN_CHUNK
    w = HEADS_PER_GROUP * RWKV_HEAD
    wc = HEADS_PER_GROUP * c

    @pl.when(pl.program_id(2) == 0)
    def _():
        s_ref[...] = jnp.zeros_like(s_ref)

    tri = jnp.where(lax.broadcasted_iota(jnp.int32, (c, c), 0)
                    >= lax.broadcasted_iota(jnp.int32, (c, c), 1), 1.0, 0.0).astype(BF16)
    bd_r = lax.broadcasted_iota(jnp.int32, (w, w), 0) // RWKV_HEAD
    bd_c = lax.broadcasted_iota(jnp.int32, (w, w), 1) // RWKV_HEAD
    head_mask = bd_r == bd_c
    head_ones = jnp.where(head_mask, 1.0, 0.0).astype(BF16)
    t_idx = lax.broadcasted_iota(jnp.int32, (c, wc), 0)
    s_idx = lax.broadcasted_iota(jnp.int32, (c, wc), 1) % c
    strict = s_idx < t_idx
    incl = s_idx <= t_idx
    eye = jnp.where(s_idx == t_idx, 1.0, 0.0)
    base_level = 3
    base_blocks = (t_idx >> base_level) == (s_idx >> base_level)
    join = {lv: (((t_idx >> lv) & 1) == 1) & ((s_idx >> lv) == (t_idx >> lv) - 1)
            for lv in range(base_level, c.bit_length() - 1)}

    def head_sum(x):
        return jnp.dot(_bf(x), head_ones, preferred_element_type=F32)

    def block_diag(x):
        xb = _bf(x)
        return jnp.where(head_mask, jnp.concatenate([xb] * HEADS_PER_GROUP, axis=0),
                         jnp.zeros((), BF16))

    def stream(bi, si, cj):
        st = bi * n_streams + si
        ln = pl.ds(si * w, w)
        rows = pl.ds(cj * c, c)
        r = r_ref[bi, rows, ln].astype(F32)
        ld = ld_ref[bi, rows, ln]
        k = k_ref[bi, rows, ln].astype(F32)
        v = v_ref[bi, rows, ln].astype(F32)
        al = al_ref[bi, rows, ln].astype(F32)

        ld_hi = _bf(ld)
        rest = ld - ld_hi.astype(F32)
        ld_mid = _bf(rest)
        ld_lo = _bf(rest - ld_mid.astype(F32))
        cum = (jnp.dot(tri, ld_hi, preferred_element_type=F32)
               + jnp.dot(tri, ld_mid, preferred_element_type=F32)
               + jnp.dot(tri, ld_lo, preferred_element_type=F32))
        yield
        p_incl = jnp.exp(cum)
        p_prev = jnp.exp(cum - ld)
        p_inv = jnp.exp(-cum)
        p_last = p_incl[c - 1:c, :]

        kk = k * kk_w_ref[:, ln]
        ss = head_sum(kk * kk)
        yield
        kk = kk / jnp.maximum(jnp.sqrt(ss), 1e-12)
        kmod = k * (1.0 + (al - 1.0) * ka_w_ref[:, ln])

        a_t = -kk * p_prev
        b_t = kk * al * p_inv
        k_t = kmod * p_inv
        r_t = r * p_incl

        ar = jnp.concatenate([a_t, r_t], axis=0)
        m_b = _dot_nt(ar, block_diag(b_t))
        m_k = _dot_nt(ar, block_diag(k_t))
        yield
        a_ab = jnp.where(strict, m_b[:c], 0.0)
        a_rb = jnp.where(incl, m_b[c:], 0.0)
        a_ak = jnp.where(strict, m_k[:c], 0.0)
        a_rk = jnp.where(incl, m_k[c:], 0.0)

        pw = jnp.where(base_blocks, a_ab, 0.0)
        t_inv = eye
        for step in range(base_level):
            if step < base_level - 1:
                z = _dot(jnp.concatenate([pw, t_inv], axis=0), block_diag(pw))
                yield
                pw, t_inv = z[:c], t_inv + z[c:]
            else:
                z = _dot(t_inv, block_diag(pw))
                yield
                t_inv = t_inv + z
        n_bd = block_diag(a_ab)
        level = base_level
        while (1 << level) < c:
            tn = _dot(t_inv, n_bd)
            yield
            x = _dot(tn, block_diag(t_inv))
            yield
            t_inv = t_inv + jnp.where(join[level], x, 0.0)
            level += 1

        s0 = s_ref[st]
        w1 = _dot_nt(ar, s0)
        w2 = _dot(jnp.concatenate([a_ak, a_rk], axis=0), block_diag(v))
        yield
        u = _dot(t_inv, block_diag(w1[:c] + w2[:c]))
        yield
        y = w1[c:] + w2[c:] + _dot(a_rb, block_diag(u))

        uv = jnp.concatenate([u, v], axis=0)
        bk = jnp.concatenate([b_t, k_t], axis=0) * p_last
        s_new = _dot_tn(uv, bk)
        yield
        s_ref[st] = jnp.where(head_mask, s0 * p_last + s_new, 0.0)

        inv_n = 1.0 / RWKV_HEAD
        sums = head_sum(jnp.concatenate([y, r * kmod * rk_w_ref[:, ln]], axis=0))
        yield
        dlt = y - sums[:c] * inv_n
        bsum = sums[c:]
        var = head_sum(dlt * dlt) * inv_n
        yield
        yn = dlt * lax.rsqrt(var + GN_EPS) * lnw_ref[:, ln] + lnb_ref[:, ln]
        bonus = bsum * v
        o_ref[bi, rows, ln] = ((yn + bonus) * g_ref[bi, rows, ln]).astype(o_ref.dtype)

    def chunks(bi, si):
        for cj in range(cps):
            yield from stream(bi, si, cj)

    _round_robin([chunks(bi, si) for bi in range(nb) for si in range(n_streams)])


def _rwkv_scan(batch, rkv, ld, ag, k_k, k_a, r_k, ln_w, ln_b):
    _, m, d = rkv.shape
    t = m // batch
    c = SCAN_CHUNK
    w = HEADS_PER_GROUP * RWKV_HEAD
    assert d % w == 0 and t % c == 0 and c == RWKV_HEAD
    n_streams = _pick_tile(d // w, SCAN_STREAMS, 1)
    wb = n_streams * w
    cps = _pick_tile(t // c, SCAN_CHUNKS_PER_STEP, 1)
    nc = t // (c * cps)
    nb = _pick_tile(batch, SCAN_BATCH_ROWS, 1)
    tok = pl.BlockSpec((nb, cps * c, wb), lambda b, gi, ci: (b, ci, gi))
    par = pl.BlockSpec((1, wb), lambda b, gi, ci: (0, gi))

    def plane(p):
        return pl.BlockSpec((None, nb, cps * c, wb), lambda b, gi, ci: (p, b, ci, gi))

    rkv4 = rkv.reshape(3, batch, t, d)
    ag4 = ag.reshape(2, batch, t, d)
    return pl.pallas_call(
        functools.partial(_scan_body, n_streams, nb, cps),
        grid=(batch // nb, d // wb, nc),
        in_specs=[plane(0), tok, plane(1), plane(2), plane(0), plane(1)] + [par] * 5,
        out_specs=tok,
        out_shape=jax.ShapeDtypeStruct((batch, t, d), BF16),
        scratch_shapes=[pltpu.VMEM((nb * n_streams, w, w), F32)],
        compiler_params=_params(("parallel", "parallel", "arbitrary")),
        name="rwkv_scan",
    )(rkv4, ld.reshape(batch, t, d), rkv4, rkv4, ag4, ag4,
      *[p.reshape(1, d) for p in (k_k, k_a, r_k, ln_w, ln_b)]).reshape(m, d)


def _rwkv_mix(batch, x, norm_g, mu, w_rkv, w0, w1, w2, a0, a1, a2, g1, g2,
              k_k, k_a, r_k, ln_w, ln_b, w_o):
    rkv, ld, ag = _rwkv_proj(batch, x, norm_g, mu, w_rkv, w0, w1, w2, a0, a1, a2, g1, g2)
    yg = _rwkv_scan(batch, rkv, ld, ag, k_k, k_a, r_k.reshape(-1), ln_w, ln_b)
    return _simple_mm("rwkv_out", [_rows(yg)], lambda t: t, w_o.astype(BF16), [_epi_tile(x)],
                      lambda acc, x_t: (x_t + acc,), tn_target=x.shape[1])


def _attn_body(tq, tk, nh, small_ref, q_ref, k_ref, v_ref, o_ref, v1_ref):
    qi = pl.program_id(2)
    dv = v_ref.shape[-1]

    def scores(h, j, masked):
        start = pl.multiple_of(j * tk, tk)
        kb = k_ref[h, pl.ds(start, tk), :]
        s = lax.dot_general(q_ref[h], kb, (((1,), (1,)), ((), ())), preferred_element_type=F32)
        if masked:
            q_chunk = (qi * tq + lax.broadcasted_iota(jnp.int32, (tq, tk), 0)) // ATTN_CHUNK
            k_chunk = (j * tk + lax.broadcasted_iota(jnp.int32, (tq, tk), 1)) // ATTN_CHUNK
            s = jnp.where(k_chunk <= q_chunk, s, jnp.finfo(F32).min)
        return s

    def values(h, j):
        return v_ref[h, pl.ds(pl.multiple_of(j * tk, tk), tk), :]

    def all_heads(make_gen):
        outs = [[] for _ in range(nh)]
        _round_robin([make_gen(h, outs[h]) for h in range(nh)])
        return tuple(o[0] for o in outs)

    def sweep(step, init, diag_step=None):
        carries = lax.fori_loop(
            0, qi, lambda j, cr: all_heads(lambda h, out: step(h, j, cr[h], False, out)), init)
        if diag_step is not None:
            return all_heads(lambda h, out: diag_step(h, carries[h], out))
        return all_heads(lambda h, out: step(h, qi, carries[h], True, out))

    def online_step(h, j, carry, masked, out):
        m_i, l_i, acc = carry
        s = scores(h, j, masked)
        yield
        m_new = jnp.maximum(m_i, jnp.max(s, axis=-1, keepdims=True))
        p = jnp.exp(s - m_new)
        alpha = jnp.exp(m_i - m_new)
        l_new = alpha * l_i + jnp.sum(p, axis=-1, keepdims=True)
        pv = jnp.dot(p.astype(BF16), values(h, j), preferred_element_type=F32)
        yield
        out.append((m_new, l_new, alpha * acc + pv))

    def plain_step(h, j, acc, masked, out):
        s = scores(h, j, masked)
        yield
        p = jnp.exp(s).astype(BF16)
        v1 = v1_ref[h, pl.ds(pl.multiple_of(j * tk, tk), tk), :]
        pv = jnp.dot(p, v1, preferred_element_type=F32)
        yield
        out.append(acc + pv)

    def plain_diag(h, acc, out):
        half = tk // 2
        base = pl.multiple_of(qi * tk, tk)
        visible = (lax.broadcasted_iota(jnp.int32, (tq, half), 1) // ATTN_CHUNK
                   <= lax.broadcasted_iota(jnp.int32, (tq, half), 0) // ATTN_CHUNK)

        def weights(q_rows, key0, mask):
            kb = k_ref[h, pl.ds(key0, half), :]
            s = lax.dot_general(q_rows, kb, (((1,), (1,)), ((), ())), preferred_element_type=F32)
            return jnp.where(mask, jnp.exp(s), 0.0).astype(BF16)

        p_a = weights(q_ref[h], base, visible)
        yield
        pv_a = jnp.dot(p_a, v1_ref[h, pl.ds(base, half), :], preferred_element_type=F32)
        p_b = weights(q_ref[h, half:, :], base + half, visible[:half])
        yield
        pv_b = jnp.dot(p_b, v1_ref[h, pl.ds(base + half, half), :], preferred_element_type=F32)
        yield
        out.append(acc + jnp.concatenate([pv_a[:half], pv_a[half:] + pv_b], axis=0))

    @pl.when(small_ref[0] == 1)
    def _():
        @pl.when(qi == 0)
        def _():
            v1_ref[:, :, :dv] = v_ref[...]
            v1_ref[:, :, dv:] = jnp.ones(v_ref.shape, BF16)

        init = tuple(jnp.zeros((tq, 2 * dv), F32) for _ in range(nh))
        for h, acc in enumerate(sweep(plain_step, init, plain_diag)):
            o_ref[:, h * dv:(h + 1) * dv] = (acc[:, :dv] / acc[:, dv:]).astype(o_ref.dtype)

    @pl.when(small_ref[0] != 1)
    def _():
        init = tuple((jnp.full((tq, 1), -jnp.inf, F32), jnp.zeros((tq, 1), F32),
                      jnp.zeros((tq, dv), F32)) for _ in range(nh))
        for h, (_, l_i, acc) in enumerate(sweep(online_step, init)):
            o_ref[:, h * dv:(h + 1) * dv] = (acc / l_i).astype(o_ref.dtype)


def _attention(batch, q, k, v, small_scores):
    heads, m, dqk = q.shape
    dv = v.shape[-1]
    t = m // batch
    tq = _pick_tile(t, ATTN_Q_TILE, 2 * ATTN_CHUNK)
    assert tq % (2 * ATTN_CHUNK) == 0
    nq = t // tq
    nh = _pick_tile(heads, ATTN_HEADS_PER_STEP, 1)
    return pl.pallas_call(
        functools.partial(_attn_body, tq, tq, nh),
        grid=(batch, heads // nh, nq),
        in_specs=[
            pl.BlockSpec(memory_space=pltpu.SMEM),
            pl.BlockSpec((nh, tq, dqk), lambda b, h, i: (h, b * nq + i, 0)),
            pl.BlockSpec((nh, t, dqk), lambda b, h, i: (h, b, 0)),
            pl.BlockSpec((nh, t, dv), lambda b, h, i: (h, b, 0)),
        ],
        out_specs=pl.BlockSpec((tq, nh * dv), lambda b, h, i: (b * nq + i, h)),
        out_shape=jax.ShapeDtypeStruct((m, heads * dv), BF16),
        scratch_shapes=[pltpu.VMEM((nh, t, 2 * dv), BF16)],
        compiler_params=_params(("parallel", "parallel", "arbitrary")),
        name="mla_attention",
    )(small_scores, q, k, v)


def _mla_mix(batch, x, positions, norm_g, w_down, q_a_norm, kv_a_norm, w_uq, w_ukv,
             q_norm, k_norm, w_o):
    m, d = x.shape
    q_lora = q_a_norm.shape[0]
    kv_lora = kv_a_norm.shape[0]
    heads, qk_dim = w_uq.shape[1], w_uq.shape[2]
    rope = w_down.shape[1] - q_lora - kv_lora
    nope = qk_dim - rope
    dv = w_ukv.shape[2] - nope
    half = rope // 2
    assert nope == LANE and dv == LANE and rope <= LANE and q_lora == kv_lora and q_lora % LANE == 0
    scale = 1.0 / math.sqrt(qk_dim)

    perm = jnp.concatenate([jnp.arange(half, rope), jnp.arange(0, half)])
    lane_pad = lambda a: jnp.pad(a, [(0, 0)] * (a.ndim - 1) + [(0, LANE - a.shape[-1])])

    inv_freq = ROPE_THETA ** (-jnp.arange(0, rope, 2, dtype=F32) / rope)
    ang = positions.reshape(m, 1).astype(F32) * inv_freq
    cos, sin = jnp.cos(ang), jnp.sin(ang)
    cos_t = lane_pad(jnp.concatenate([cos, cos], axis=-1))
    sin_t = lane_pad(jnp.concatenate([-sin, sin], axis=-1))

    w_kpe = w_down[:, q_lora + kv_lora:]
    w_down_x = jnp.concatenate(
        [w_down[:, :q_lora + kv_lora], lane_pad(w_kpe), lane_pad(w_kpe[:, perm])], axis=1).astype(BF16)
    c = _simple_mm("mla_down", [_rows(x), _vec(norm_g)], _rms, w_down_x, [], lambda acc: (acc,),
                   tn_target=w_down_x.shape[1])
    qb = q_lora // q_lora
    tm = _pick_tile(m, 512, 8)

    def norm_rope(body, pe, rot, g_body, g_pe, g_rot, cos_b, sin_b, extra_scale):
        ss = jnp.sum(body * body, axis=-1, keepdims=True) + jnp.sum(pe * pe, axis=-1, keepdims=True)
        s = lax.rsqrt(ss / qk_dim + RMS_EPS) * extra_scale
        return jnp.concatenate([body * s * g_body, (pe * g_pe * cos_b + rot * g_rot * sin_b) * s], axis=-1)

    w_q_pe = w_uq[:, :, nope:]
    w_q_x = jnp.concatenate([w_uq[:, :, :nope], lane_pad(w_q_pe), lane_pad(w_q_pe[:, :, perm])],
                            axis=-1).reshape(q_lora, heads * 3 * LANE).astype(BF16)
    gq = jnp.concatenate([q_norm[:nope], lane_pad(q_norm[nope:]), lane_pad(q_norm[nope:][perm])]).reshape(1, -1)

    hb = _pick_tile(heads, MLA_HEADS_PER_STEP, 1)

    def q_epi(acc, g_t, cos_b, sin_b):
        per_head = []
        for hh in range(hb):
            a = acc[:, hh * 3 * LANE:(hh + 1) * 3 * LANE]
            per_head.append(norm_rope(a[:, :LANE], a[:, LANE:2 * LANE], a[:, 2 * LANE:], g_t[:, :LANE],
                                      g_t[:, LANE:2 * LANE], g_t[:, 2 * LANE:], cos_b, sin_b, scale))
        return (jnp.stack(per_head, axis=0),)

    q = _mm("mla_q", m,
            [(c, _row_spec(tm, q_lora, 0)), (q_a_norm.reshape(1, -1), _vec_spec(q_lora))], _rms, w_q_x,
            [(gq, _vec_spec(3 * LANE)), (cos_t, _row_spec(tm, LANE)), (sin_t, _row_spec(tm, LANE))], q_epi,
            [(jax.ShapeDtypeStruct((heads, m, 2 * LANE), BF16),
              pl.BlockSpec((hb, tm, 2 * LANE), lambda i, j: (j, i, 0)))],
            tm=tm, tn=hb * 3 * LANE)[0]

    w_kv_x = w_ukv.reshape(kv_lora, heads * (nope + dv)).astype(BF16)
    gk = jnp.concatenate([k_norm[:nope], lane_pad(k_norm[nope:]), lane_pad(k_norm[nope:][perm])]).reshape(1, -1)
    pe_block = (q_lora + kv_lora) // LANE

    def kv_epi(acc, g_t, pe, rot, cos_b, sin_b):
        k_out, v_out = [], []
        for hh in range(hb):
            a = acc[:, hh * (nope + dv):(hh + 1) * (nope + dv)]
            k_out.append(norm_rope(a[:, :nope], pe, rot, g_t[:, :LANE], g_t[:, LANE:2 * LANE],
                                   g_t[:, 2 * LANE:], cos_b, sin_b, 1.0))
            v_out.append(a[:, nope:])
        return jnp.stack(k_out, axis=0), jnp.stack(v_out, axis=0)

    k, v = _mm("mla_kv", m,
               [(c, _row_spec(tm, kv_lora, qb)), (kv_a_norm.reshape(1, -1), _vec_spec(kv_lora))], _rms, w_kv_x,
               [(gk, _vec_spec(3 * LANE)), (c, _row_spec(tm, LANE, pe_block)),
                (c, _row_spec(tm, LANE, pe_block + 1)), (cos_t, _row_spec(tm, LANE)),
                (sin_t, _row_spec(tm, LANE))], kv_epi,
               [(jax.ShapeDtypeStruct((heads, m, 2 * LANE), BF16),
                 pl.BlockSpec((hb, tm, 2 * LANE), lambda i, j: (j, i, 0))),
                (jax.ShapeDtypeStruct((heads, m, dv), BF16),
                 pl.BlockSpec((hb, tm, dv), lambda i, j: (j, i, 0)))],
               tm=tm, tn=hb * (nope + dv))

    score_bound = scale * qk_dim * jnp.max(jnp.abs(q_norm)) * jnp.max(jnp.abs(k_norm))
    small_scores = (score_bound <= MAX_PLAIN_SCORE).astype(jnp.int32).reshape(1)
    o = _attention(batch, q, k, v, small_scores)
    return _simple_mm("mla_out", [_rows(o)], lambda t: t, w_o.reshape(heads * dv, d).astype(BF16),
                      [_epi_tile(x)], lambda acc, x_t: (x_t + acc,), tn_target=d)


def kernel(x, positions, ffn_norm, ffn_w13, ffn_w2, mix_norm, rwkv_mu, rwkv_w_rkv, rwkv_w0, rwkv_w1, rwkv_w2, rwkv_a0, rwkv_a1, rwkv_a2, rwkv_g1, rwkv_g2, rwkv_k_k, rwkv_k_a, rwkv_r_k, rwkv_ln_w, rwkv_ln_b, rwkv_w_o, mla_w_down, mla_q_a_norm, mla_kv_a_norm, mla_w_uq, mla_w_ukv, mla_q_norm, mla_k_norm, mla_w_o):
    batch, seq, d = x.shape
    depth = ffn_norm.shape[0]
    n_mixers = 2
    xf = x.reshape(batch * seq, d)
    d_ff = ffn_w2.shape[2]
    w13p, w2p = _ffn_weights(ffn_w13.reshape((2 * depth,) + ffn_w13.shape[2:]),
                             ffn_w2.reshape((2 * depth,) + ffn_w2.shape[2:]))
    for i in range(depth):
        j = i // n_mixers
        xf = _ffn(xf, ffn_norm[i, 0], w13p, w2p, 2 * i, d_ff)
        if i % n_mixers == 0:
            xf = _rwkv_mix(batch, xf, mix_norm[i], rwkv_mu[j], rwkv_w_rkv[j], rwkv_w0[j], rwkv_w1[j],
                           rwkv_w2[j], rwkv_a0[j], rwkv_a1[j], rwkv_a2[j], rwkv_g1[j], rwkv_g2[j],
                           rwkv_k_k[j], rwkv_k_a[j], rwkv_r_k[j], rwkv_ln_w[j], rwkv_ln_b[j],
                           rwkv_w_o[j])
        else:
            xf = _mla_mix(batch, xf, positions, mix_norm[i], mla_w_down[j], mla_q_a_norm[j],
                          mla_kv_a_norm[j], mla_w_uq[j], mla_w_ukv[j], mla_q_norm[j], mla_k_norm[j],
                          mla_w_o[j])
        xf = _ffn(xf, ffn_norm[i, 1], w13p, w2p, 2 * i + 1, d_ff)
    return xf.reshape(batch, seq, d)
```

```python
import functools
import math

import jax
import jax.numpy as jnp
from jax import lax
from jax.experimental import pallas as pl
from jax.experimental.pallas import tpu as pltpu

F32 = jnp.float32
BF16 = jnp.bfloat16

RMS_EPS = 1e-6
GN_EPS = 64e-5
ROPE_THETA = 10000.0
ATTN_CHUNK = 64
RWKV_HEAD = 64
SCAN_CHUNK = 64
HEADS_PER_GROUP = 4
MAX_PLAIN_SCORE = 30.0
ATTN_Q_TILE = 512
ATTN_HEADS_PER_STEP = 4
FFN_ROWS = 1024
FFN_TILE = 512
MLA_HEADS_PER_STEP = 16
SCAN_CHUNKS_PER_STEP = 2
SCAN_BATCH_ROWS = 1
SCAN_STREAMS = 8
LANE = 128
SUBLANE = 8
VMEM_LIMIT_BYTES = 56 * 1024 * 1024


def _params(semantics):
    return pltpu.CompilerParams(dimension_semantics=semantics,
                                vmem_limit_bytes=VMEM_LIMIT_BYTES)


def _pick_tile(n, target, quantum):
    if n <= target:
        return n
    t = (target // quantum) * quantum
    while t > quantum and n % t:
        t -= quantum
    assert n % t == 0, (n, target, quantum)
    return t


def _round_robin(gens):
    live = list(gens)
    while live:
        live = [g for g in live if next(g, True) is None]


def _rms(x, g):
    return x * lax.rsqrt(jnp.mean(x * x, axis=-1, keepdims=True) + RMS_EPS) * g


def _ffn_body(x_ref, g_ref, wg_ref, wu_ref, w2_ref, o_ref, xn_ref):
    j = pl.program_id(1)

    @pl.when(j == 0)
    def _():
        xn_ref[...] = _rms(x_ref[...], g_ref[...]).astype(BF16)
        o_ref[...] = jnp.zeros_like(o_ref)

    xn = xn_ref[...]
    gate = jnp.dot(xn, wg_ref[...], preferred_element_type=F32)
    up = jnp.dot(xn, wu_ref[...], preferred_element_type=F32)
    act = (gate * jax.nn.sigmoid(gate) * up).astype(BF16)
    o_ref[...] += jnp.dot(act, w2_ref[...], preferred_element_type=F32)

    @pl.when(j == pl.num_programs(1) - 1)
    def _():
        o_ref[...] = x_ref[...] + 0.5 * o_ref[...]


def _ffn_tile(f):
    return min(FFN_TILE, -(-f // LANE) * LANE)


def _tile_cast_body(tiles, x_ref, o_ref):
    tw = o_ref.shape[-1]
    for t, (c0, width) in enumerate(tiles):
        o_ref[t, :, :width] = x_ref[:, c0:c0 + width].astype(BF16)
        if width < tw:
            o_ref[t, :, width:] = jnp.zeros((o_ref.shape[1], tw - width), BF16)


def _tile_cast(name, w, tiles, tw):
    n, rows, cols = w.shape
    tr = _pick_tile(rows, 128, SUBLANE)
    return pl.pallas_call(
        functools.partial(_tile_cast_body, tuple(tiles)),
        grid=(n, rows // tr),
        in_specs=[pl.BlockSpec((None, tr, cols), lambda l, i: (l, i, 0))],
        out_specs=pl.BlockSpec((None, len(tiles), tr, tw), lambda l, i: (l, 0, i, 0)),
        out_shape=jax.ShapeDtypeStruct((n, len(tiles), rows, tw), BF16),
        compiler_params=_params(("parallel", "parallel")),
        name=name,
    )(w)


def _cast_w2_body(f, x_ref, o_ref):
    tr = o_ref.shape[0]
    row = pl.program_id(1) * tr + lax.broadcasted_iota(jnp.int32, o_ref.shape, 0)
    o_ref[...] = jnp.where(row < f, x_ref[...], 0.0).astype(BF16)


def _ffn_weights(w13, w2):
    n, d, f2 = w13.shape
    f = f2 // 2
    assert f % LANE == 0
    tf = _ffn_tile(f)
    fp = -(-f // tf) * tf
    half = [(t * tf, min(tf, f - t * tf)) for t in range(fp // tf)]
    w13p = _tile_cast("cast_w13", w13, half + [(f + c0, width) for c0, width in half], tf)
    w2p = pl.pallas_call(
        functools.partial(_cast_w2_body, f),
        grid=(n, fp // tf),
        in_specs=[pl.BlockSpec((None, tf, d), lambda l, i: (l, i, 0))],
        out_specs=pl.BlockSpec((None, tf, d), lambda l, i: (l, i, 0)),
        out_shape=jax.ShapeDtypeStruct((n, fp, d), BF16),
        compiler_params=_params(("parallel", "parallel")),
        name="cast_w2",
    )(w2)
    return w13p, w2p


def _ffn(x, g, w13p, w2p, layer, f):
    m, d = x.shape
    fp = w2p.shape[1]
    tf = _ffn_tile(f)
    nj = fp // tf
    tm = _pick_tile(m, FFN_ROWS, 8)
    return pl.pallas_call(
        _ffn_body,
        grid=(m // tm, nj),
        in_specs=[
            pl.BlockSpec((tm, d), lambda i, j: (i, 0)),
            pl.BlockSpec((1, d), lambda i, j: (0, 0)),
            pl.BlockSpec((None, None, d, tf), lambda i, j: (layer, j, 0, 0)),
            pl.BlockSpec((None, None, d, tf), lambda i, j: (layer, j + nj, 0, 0)),
            pl.BlockSpec((None, tf, d), lambda i, j: (layer, j, 0)),
        ],
        out_specs=pl.BlockSpec((tm, d), lambda i, j: (i, 0)),
        out_shape=jax.ShapeDtypeStruct((m, d), F32),
        scratch_shapes=[pltpu.VMEM((tm, d), BF16)],
        compiler_params=_params(("parallel", "arbitrary")),
        name="ffn",
    )(x, g.reshape(1, d), w13p, w13p, w2p)


def _mm_body(n_lhs, n_epi, n_out, lhs_fn, epi_fn, *refs):
    lhs_refs = refs[:n_lhs]
    w_ref = refs[n_lhs]
    epi_refs = refs[n_lhs + 1:n_lhs + 1 + n_epi]
    out_refs = refs[n_lhs + 1 + n_epi:n_lhs + 1 + n_epi + n_out]
    lhs_scratch = refs[-1]

    @pl.when(pl.program_id(1) == 0)
    def _():
        lhs_scratch[...] = lhs_fn(*[r[...] for r in lhs_refs]).astype(BF16)

    acc = jnp.dot(lhs_scratch[...], w_ref[...], preferred_element_type=F32)
    outs = epi_fn(acc, *[r[...] for r in epi_refs])
    for o_ref, o in zip(out_refs, outs):
        o_ref[...] = o.astype(o_ref.dtype)


def _mm(name, m, lhs, lhs_fn, w, epi, epi_fn, outs, *, tm, tn):
    k, n = w.shape
    body = functools.partial(_mm_body, len(lhs), len(epi), len(outs), lhs_fn, epi_fn)
    return pl.pallas_call(
        body,
        grid=(m // tm, n // tn),
        in_specs=[s for _, s in lhs] + [pl.BlockSpec((k, tn), lambda i, j: (0, j))]
        + [s for _, s in epi],
        out_specs=[s for _, s in outs],
        out_shape=[s for s, _ in outs],
        scratch_shapes=[pltpu.VMEM((tm, k), BF16)],
        compiler_params=_params(("parallel", "arbitrary")),
        name=name,
    )(*[a for a, _ in lhs], w, *[a for a, _ in epi])


def _row_spec(tm, width, col_block=0):
    return pl.BlockSpec((tm, width), lambda i, j: (i, col_block))


def _tile_spec(tm, tn):
    return pl.BlockSpec((tm, tn), lambda i, j: (i, j))


def _vec_spec(width):
    return pl.BlockSpec((1, width), lambda i, j: (0, 0))


def _colvec_spec(tn):
    return pl.BlockSpec((1, tn), lambda i, j: (0, j))


def _simple_mm(name, lhs, lhs_fn, w, epi, epi_fn, out_dtype=F32, tm_target=512, tn_target=1024):
    m = lhs[0][0].shape[0]
    n = w.shape[1]
    tm = _pick_tile(m, tm_target, 8)
    tn = _pick_tile(n, tn_target, LANE)
    lhs = [(a, spec(tm)) for a, spec in lhs]
    epi = [(a, spec(tm, tn)) for a, spec in epi]
    out = (jax.ShapeDtypeStruct((m, n), out_dtype), _tile_spec(tm, tn))
    return _mm(name, m, lhs, lhs_fn, w, epi, epi_fn, [out], tm=tm, tn=tn)[0]


def _rows(a):
    return a, lambda tm: _row_spec(tm, a.shape[1])


def _vec(a):
    a = a.reshape(1, -1)
    return a, lambda tm: _vec_spec(a.shape[1])


def _epi_tile(a):
    return a, lambda tm, tn: _tile_spec(tm, tn)


def _epi_colvec(a):
    return a.reshape(1, -1), lambda tm, tn: _colvec_spec(tn)


def _rwkv_proj_body(seq_tiles, ranks, x_ref, xprev_ref, g_ref, mu_ref, wr_ref, wk_ref, wv_ref, l1_ref,
                    w2_ref, a2_ref, g2_ref, w0_ref, a0_ref, rkv_ref, ld_ref, ag_ref,
                    lerp_scr, tw_scr, ta_scr, tg_scr):
    i = pl.program_id(0)
    j = pl.program_id(1)
    rw, ra = ranks
    w_refs = (wr_ref, wk_ref, wv_ref)

    def emit_rkv(p):
        rkv_ref[p] = jnp.dot(lerp_scr[p], w_refs[p][...], preferred_element_type=F32).astype(BF16)

    def emit_lora_outputs():
        z = -(w0_ref[...] + jnp.dot(tw_scr[...], w2_ref[...], preferred_element_type=F32))
        softplus = jnp.maximum(z, 0.0) + jnp.log1p(jnp.exp(-jnp.abs(z)))
        ld_ref[...] = -jnp.exp(-softplus - 0.5)
        ag_ref[0] = jax.nn.sigmoid(
            a0_ref[...] + jnp.dot(ta_scr[...], a2_ref[...], preferred_element_type=F32)).astype(BF16)
        ag_ref[1] = jnp.dot(tg_scr[...], g2_ref[...], preferred_element_type=F32).astype(BF16)

    @pl.when(j == 0)
    def _():
        h = _rms(x_ref[...], g_ref[...])
        prev = _rms(xprev_ref[...], g_ref[...])[SUBLANE - 1:SUBLANE, :]
        prev = jnp.where(i % seq_tiles == 0, 0.0, prev)
        row = lax.broadcasted_iota(jnp.int32, h.shape, 0)
        dx = jnp.where(row == 0, prev, pltpu.roll(h, 1, 0)) - h
        def lerp(idx):
            return (h + dx * mu_ref[idx:idx + 1, :]).astype(BF16)

        for p in range(3):
            lerp_scr[p] = lerp(p)
            emit_rkv(p)
        tw_scr[...] = jnp.tanh(jnp.dot(lerp(3), l1_ref[:, :rw],
                                       preferred_element_type=F32)).astype(BF16)
        ta_scr[...] = jnp.dot(lerp(4), l1_ref[:, rw:rw + ra],
                              preferred_element_type=F32).astype(BF16)
        tg_scr[...] = jax.nn.sigmoid(jnp.dot(lerp(5), l1_ref[:, rw + ra:],
                                             preferred_element_type=F32)).astype(BF16)
        emit_lora_outputs()

    @pl.when(j > 0)
    def _():
        for p in range(3):
            emit_rkv(p)
        emit_lora_outputs()


def _pad_lora(wa, wb):
    rank = wa.shape[1]
    rp = -(-rank // LANE) * LANE
    return (jnp.pad(wa, ((0, 0), (0, rp - rank))).astype(BF16),
            jnp.pad(wb, ((0, rp - rank), (0, 0))).astype(BF16))


def _rwkv_proj(batch, x, norm_g, mu, w_rkv, w0, w1, w2, a0, a1, a2, g1, g2):
    m, d = x.shape
    tm = _pick_tile(m // batch, 512, SUBLANE)
    tn = _pick_tile(d, 512, LANE)
    w1p, w2p = _pad_lora(w1, w2)
    a1p, a2p = _pad_lora(a1, a2)
    g1p, g2p = _pad_lora(g1, g2)
    rw, ra, rg = w1p.shape[1], a1p.shape[1], g1p.shape[1]
    l1 = jnp.concatenate([w1p, a1p, g1p], axis=1)
    w_bf = _tile_cast("cast_w_rkv", w_rkv, [(t * tn, tn) for t in range(d // tn)], tn)
    mu_ord = mu[jnp.array([0, 2, 3, 1, 4, 5])]
    col = lambda i, j: (0, j)

    def w_plane(p):
        return pl.BlockSpec((None, None, d, tn), lambda i, j: (p, j, 0, 0))

    def planes(n):
        return pl.BlockSpec((n, tm, tn), lambda i, j: (0, i, j))

    body = functools.partial(_rwkv_proj_body, (m // batch) // tm, (rw, ra))
    return pl.pallas_call(
        body,
        grid=(m // tm, d // tn),
        in_specs=[
            pl.BlockSpec((tm, d), lambda i, j: (i, 0)),
            pl.BlockSpec((SUBLANE, d), lambda i, j: (jnp.maximum(i * (tm // SUBLANE) - 1, 0), 0)),
            pl.BlockSpec((1, d), lambda i, j: (0, 0)),
            pl.BlockSpec((6, d), lambda i, j: (0, 0)),
            w_plane(0), w_plane(1), w_plane(2),
            pl.BlockSpec((d, rw + ra + rg), lambda i, j: (0, 0)),
            pl.BlockSpec((rw, tn), col),
            pl.BlockSpec((ra, tn), col),
            pl.BlockSpec((rg, tn), col),
            pl.BlockSpec((1, tn), col),
            pl.BlockSpec((1, tn), col),
        ],
        out_specs=[planes(3), pl.BlockSpec((tm, tn), lambda i, j: (i, j)), planes(2)],
        out_shape=[jax.ShapeDtypeStruct((3, m, d), BF16), jax.ShapeDtypeStruct((m, d), F32),
                   jax.ShapeDtypeStruct((2, m, d), BF16)],
        scratch_shapes=[pltpu.VMEM((3, tm, d), BF16), pltpu.VMEM((tm, rw), BF16),
                        pltpu.VMEM((tm, ra), BF16), pltpu.VMEM((tm, rg), BF16)],
        compiler_params=_params(("parallel", "arbitrary")),
        name="rwkv_proj",
    )(x, x, norm_g.reshape(1, d), mu_ord, w_bf, w_bf, w_bf, l1, w2p, a2p, g2p,
      w0.reshape(1, d), a0.reshape(1, d))


def _bf(x):
    return x.astype(BF16)


def _dot(a, b):
    return jnp.dot(_bf(a), _bf(b), preferred_element_type=F32)


def _dot_nt(a, b):
    return lax.dot_general(_bf(a), _bf(b), (((1,), (1,)), ((), ())), preferred_element_type=F32)


def _dot_tn(a, b):
    return lax.dot_general(_bf(a), _bf(b), (((0,), (0,)), ((), ())), preferred_element_type=F32)


def _scan_body(n_streams, nb, cps, r_ref, ld_ref, k_ref, v_ref, al_ref, g_ref, kk_w_ref, ka_w_ref,
               rk_w_ref, lnw_ref, lnb_ref, o_ref, s_ref):
    c = SCAN_CHUNK
    w = HEADS_PER_GROUP * RWKV_HEAD
    wc = HEADS_PER_GROUP * c

    @pl.when(pl.program_id(2) == 0)
    def _():
        s_ref[...] = jnp.zeros_like(s_ref)

    tri = jnp.where(lax.broadcasted_iota(jnp.int32, (c, c), 0)
                    >= lax.broadcasted_iota(jnp.int32, (c, c), 1), 1.0, 0.0).astype(BF16)
    bd_r = lax.broadcasted_iota(jnp.int32, (w, w), 0) // RWKV_HEAD
    bd_c = lax.broadcasted_iota(jnp.int32, (w, w), 1) // RWKV_HEAD
    head_mask = bd_r == bd_c
    head_ones = jnp.where(head_mask, 1.0, 0.0).astype(BF16)
    t_idx = lax.broadcasted_iota(jnp.int32, (c, wc), 0)
    s_idx = lax.broadcasted_iota(jnp.int32, (c, wc), 1) % c
    strict = s_idx < t_idx
    incl = s_idx <= t_idx
    eye = jnp.where(s_idx == t_idx, 1.0, 0.0)
    base_level = 3
    base_blocks = (t_idx >> base_level) == (s_idx >> base_level)
    join = {lv: (((t_idx >> lv) & 1) == 1) & ((s_idx >> lv) == (t_idx >> lv) - 1)
            for lv in range(base_level, c.bit_length() - 1)}

    def head_sum(x):
        return jnp.dot(_bf(x), head_ones, preferred_element_type=F32)

    def block_diag(x):
        xb = _bf(x)
        return jnp.where(head_mask, jnp.concatenate([xb] * HEADS_PER_GROUP, axis=0),
                         jnp.zeros((), BF16))

    def stream(bi, si, cj):
        st = bi * n_streams + si
        ln = pl.ds(si * w, w)
        rows = pl.ds(cj * c, c)
        r = r_ref[bi, rows, ln].astype(F32)
        ld = ld_ref[bi, rows, ln]
        k = k_ref[bi, rows, ln].astype(F32)
        v = v_ref[bi, rows, ln].astype(F32)
        al = al_ref[bi, rows, ln].astype(F32)

        ld_hi = _bf(ld)
        rest = ld - ld_hi.astype(F32)
        ld_mid = _bf(rest)
        ld_lo = _bf(rest - ld_mid.astype(F32))
        cum = (jnp.dot(tri, ld_hi, preferred_element_type=F32)
               + jnp.dot(tri, ld_mid, preferred_element_type=F32)
               + jnp.dot(tri, ld_lo, preferred_element_type=F32))
        yield
        p_incl = jnp.exp(cum)
        p_prev = jnp.exp(cum - ld)
        p_inv = jnp.exp(-cum)
        p_last = p_incl[c - 1:c, :]

        kk = k * kk_w_ref[:, ln]
        ss = head_sum(kk * kk)
        yield
        kk = kk / jnp.maximum(jnp.sqrt(ss), 1e-12)
        kmod = k * (1.0 + (al - 1.0) * ka_w_ref[:, ln])

        a_t = -kk * p_prev
        b_t = kk * al * p_inv
        k_t = kmod * p_inv
        r_t = r * p_incl

        ar = jnp.concatenate([a_t, r_t], axis=0)
        m_b = _dot_nt(ar, block_diag(b_t))
        m_k = _dot_nt(ar, block_diag(k_t))
        yield
        a_ab = jnp.where(strict, m_b[:c], 0.0)
        a_rb = jnp.where(incl, m_b[c:], 0.0)
        a_ak = jnp.where(strict, m_k[:c], 0.0)
        a_rk = jnp.where(incl, m_k[c:], 0.0)

        pw = jnp.where(base_blocks, a_ab, 0.0)
        t_inv = eye
        for step in range(base_level):
            if step < base_level - 1:
                z = _dot(jnp.concatenate([pw, t_inv], axis=0), block_diag(pw))
                yield
                pw, t_inv = z[:c], t_inv + z[c:]
            else:
                z = _dot(t_inv, block_diag(pw))
                yield
                t_inv = t_inv + z
        n_bd = block_diag(a_ab)
        level = base_level
        while (1 << level) < c:
            tn = _dot(t_inv, n_bd)
            yield
            x = _dot(tn, block_diag(t_inv))
            yield
            t_inv = t_inv + jnp.where(join[level], x, 0.0)
            level += 1

        s0 = s_ref[st]
        w1 = _dot_nt(ar, s0)
        w2 = _dot(jnp.concatenate([a_ak, a_rk], axis=0), block_diag(v))
        yield
        u = _dot(t_inv, block_diag(w1[:c] + w2[:c]))
        yield
        y = w1[c:] + w2[c:] + _dot(a_rb, block_diag(u))

        uv = jnp.concatenate([u, v], axis=0)
        bk = jnp.concatenate([b_t, k_t], axis=0) * p_last
        s_new = _dot_tn(uv, bk)
        yield
        s_ref[st] = jnp.where(head_mask, s0 * p_last + s_new, 0.0)

        inv_n = 1.0 / RWKV_HEAD
        sums = head_sum(jnp.concatenate([y, r * kmod * rk_w_ref[:, ln]], axis=0))
        yield
        dlt = y - sums[:c] * inv_n
        bsum = sums[c:]
        var = head_sum(dlt * dlt) * inv_n
        yield
        yn = dlt * lax.rsqrt(var + GN_EPS) * lnw_ref[:, ln] + lnb_ref[:, ln]
        bonus = bsum * v
        o_ref[bi, rows, ln] = ((yn + bonus) * g_ref[bi, rows, ln]).astype(o_ref.dtype)

    def chunks(bi, si):
        for cj in range(cps):
            yield from stream(bi, si, cj)

    _round_robin([chunks(bi, si) for bi in range(nb) for si in range(n_streams)])


def _rwkv_scan(batch, rkv, ld, ag, k_k, k_a, r_k, ln_w, ln_b):
    _, m, d = rkv.shape
    t = m // batch
    c = SCAN_CHUNK
    w = HEADS_PER_GROUP * RWKV_HEAD
    assert d % w == 0 and t % c == 0 and c == RWKV_HEAD
    n_streams = _pick_tile(d // w, SCAN_STREAMS, 1)
    wb = n_streams * w
    cps = _pick_tile(t // c, SCAN_CHUNKS_PER_STEP, 1)
    nc = t // (c * cps)
    nb = _pick_tile(batch, SCAN_BATCH_ROWS, 1)
    tok = pl.BlockSpec((nb, cps * c, wb), lambda b, gi, ci: (b, ci, gi))
    par = pl.BlockSpec((1, wb), lambda b, gi, ci: (0, gi))

    def plane(p):
        return pl.BlockSpec((None, nb, cps * c, wb), lambda b, gi, ci: (p, b, ci, gi))

    rkv4 = rkv.reshape(3, batch, t, d)
    ag4 = ag.reshape(2, batch, t, d)
    return pl.pallas_call(
        functools.partial(_scan_body, n_streams, nb, cps),
        grid=(batch // nb, d // wb, nc),
        in_specs=[plane(0), tok, plane(1), plane(2), plane(0), plane(1)] + [par] * 5,
        out_specs=tok,
        out_shape=jax.ShapeDtypeStruct((batch, t, d), BF16),
        scratch_shapes=[pltpu.VMEM((nb * n_streams, w, w), F32)],
        compiler_params=_params(("parallel", "parallel", "arbitrary")),
        name="rwkv_scan",
    )(rkv4, ld.reshape(batch, t, d), rkv4, rkv4, ag4, ag4,
      *[p.reshape(1, d) for p in (k_k, k_a, r_k, ln_w, ln_b)]).reshape(m, d)


def _rwkv_mix(batch, x, norm_g, mu, w_rkv, w0, w1, w2, a0, a1, a2, g1, g2,
              k_k, k_a, r_k, ln_w, ln_b, w_o):
    rkv, ld, ag = _rwkv_proj(batch, x, norm_g, mu, w_rkv, w0, w1, w2, a0, a1, a2, g1, g2)
    yg = _rwkv_scan(batch, rkv, ld, ag, k_k, k_a, r_k.reshape(-1), ln_w, ln_b)
    return _simple_mm("rwkv_out", [_rows(yg)], lambda t: t, w_o.astype(BF16), [_epi_tile(x)],
                      lambda acc, x_t: (x_t + acc,), tn_target=x.shape[1])


def _attn_body(tq, tk, nh, small_ref, q_ref, k_ref, v_ref, o_ref, v1_ref):
    qi = pl.program_id(2)
    dv = v_ref.shape[-1]

    def scores(h, j, masked):
        start = pl.multiple_of(j * tk, tk)
        kb = k_ref[h, pl.ds(start, tk), :]
        s = lax.dot_general(q_ref[h], kb, (((1,), (1,)), ((), ())), preferred_element_type=F32)
        if masked:
            q_chunk = (qi * tq + lax.broadcasted_iota(jnp.int32, (tq, tk), 0)) // ATTN_CHUNK
            k_chunk = (j * tk + lax.broadcasted_iota(jnp.int32, (tq, tk), 1)) // ATTN_CHUNK
            s = jnp.where(k_chunk <= q_chunk, s, jnp.finfo(F32).min)
        return s

    def values(h, j):
        return v_ref[h, pl.ds(pl.multiple_of(j * tk, tk), tk), :]

    def all_heads(make_gen):
        outs = [[] for _ in range(nh)]
        _round_robin([make_gen(h, outs[h]) for h in range(nh)])
        return tuple(o[0] for o in outs)

    def sweep(step, init, diag_step=None):
        carries = lax.fori_loop(
            0, qi, lambda j, cr: all_heads(lambda h, out: step(h, j, cr[h], False, out)), init)
        if diag_step is not None:
            return all_heads(lambda h, out: diag_step(h, carries[h], out))
        return all_heads(lambda h, out: step(h, qi, carries[h], True, out))

    def online_step(h, j, carry, masked, out):
        m_i, l_i, acc = carry
        s = scores(h, j, masked)
        yield
        m_new = jnp.maximum(m_i, jnp.max(s, axis=-1, keepdims=True))
        p = jnp.exp(s - m_new)
        alpha = jnp.exp(m_i - m_new)
        l_new = alpha * l_i + jnp.sum(p, axis=-1, keepdims=True)
        pv = jnp.dot(p.astype(BF16), values(h, j), preferred_element_type=F32)
        yield
        out.append((m_new, l_new, alpha * acc + pv))

    def plain_step(h, j, acc, masked, out):
        s = scores(h, j, masked)
        yield
        p = jnp.exp(s).astype(BF16)
        v1 = v1_ref[h, pl.ds(pl.multiple_of(j * tk, tk), tk), :]
        pv = jnp.dot(p, v1, preferred_element_type=F32)
        yield
        out.append(acc + pv)

    def plain_diag(h, acc, out):
        half = tk // 2
        base = pl.multiple_of(qi * tk, tk)
        visible = (lax.broadcasted_iota(jnp.int32, (tq, half), 1) // ATTN_CHUNK
                   <= lax.broadcasted_iota(jnp.int32, (tq, half), 0) // ATTN_CHUNK)

        def weights(q_rows, key0, mask):
            kb = k_ref[h, pl.ds(key0, half), :]
            s = lax.dot_general(q_rows, kb, (((1,), (1,)), ((), ())), preferred_element_type=F32)
            return jnp.where(mask, jnp.exp(s), 0.0).astype(BF16)

        p_a = weights(q_ref[h], base, visible)
        yield
        pv_a = jnp.dot(p_a, v1_ref[h, pl.ds(base, half), :], preferred_element_type=F32)
        p_b = weights(q_ref[h, half:, :], base + half, visible[:half])
        yield
        pv_b = jnp.dot(p_b, v1_ref[h, pl.ds(base + half, half), :], preferred_element_type=F32)
        yield
        out.append(acc + jnp.concatenate([pv_a[:half], pv_a[half:] + pv_b], axis=0))

    @pl.when(small_ref[0] == 1)
    def _():
        @pl.when(qi == 0)
        def _():
            v1_ref[:, :, :dv] = v_ref[...]
            v1_ref[:, :, dv:] = jnp.ones(v_ref.shape, BF16)

        init = tuple(jnp.zeros((tq, 2 * dv), F32) for _ in range(nh))
        for h, acc in enumerate(sweep(plain_step, init, plain_diag)):
            o_ref[:, h * dv:(h + 1) * dv] = (acc[:, :dv] / acc[:, dv:]).astype(o_ref.dtype)

    @pl.when(small_ref[0] != 1)
    def _():
        init = tuple((jnp.full((tq, 1), -jnp.inf, F32), jnp.zeros((tq, 1), F32),
                      jnp.zeros((tq, dv), F32)) for _ in range(nh))
        for h, (_, l_i, acc) in enumerate(sweep(online_step, init)):
            o_ref[:, h * dv:(h + 1) * dv] = (acc / l_i).astype(o_ref.dtype)


def _attention(batch, q, k, v, small_scores):
    heads, m, dqk = q.shape
    dv = v.shape[-1]
    t = m // batch
    tq = _pick_tile(t, ATTN_Q_TILE, 2 * ATTN_CHUNK)
    assert tq % (2 * ATTN_CHUNK) == 0
    nq = t // tq
    nh = _pick_tile(heads, ATTN_HEADS_PER_STEP, 1)
    return pl.pallas_call(
        functools.partial(_attn_body, tq, tq, nh),
        grid=(batch, heads // nh, nq),
        in_specs=[
            pl.BlockSpec(memory_space=pltpu.SMEM),
            pl.BlockSpec((nh, tq, dqk), lambda b, h, i: (h, b * nq + i, 0)),
            pl.BlockSpec((nh, t, dqk), lambda b, h, i: (h, b, 0)),
            pl.BlockSpec((nh, t, dv), lambda b, h, i: (h, b, 0)),
        ],
        out_specs=pl.BlockSpec((tq, nh * dv), lambda b, h, i: (b * nq + i, h)),
        out_shape=jax.ShapeDtypeStruct((m, heads * dv), BF16),
        scratch_shapes=[pltpu.VMEM((nh, t, 2 * dv), BF16)],
        compiler_params=_params(("parallel", "parallel", "arbitrary")),
        name="mla_attention",
    )(small_scores, q, k, v)


def _mla_mix(batch, x, positions, norm_g, w_down, q_a_norm, kv_a_norm, w_uq, w_ukv,
             q_norm, k_norm, w_o):
    m, d = x.shape
    q_lora = q_a_norm.shape[0]
    kv_lora = kv_a_norm.shape[0]
    heads, qk_dim = w_uq.shape[1], w_uq.shape[2]
    rope = w_down.shape[1] - q_lora - kv_lora
    nope = qk_dim - rope
    dv = w_ukv.shape[2] - nope
    half = rope // 2
    assert nope == LANE and dv == LANE and rope <= LANE and q_lora == kv_lora and q_lora % LANE == 0
    scale = 1.0 / math.sqrt(qk_dim)

    perm = jnp.concatenate([jnp.arange(half, rope), jnp.arange(0, half)])
    lane_pad = lambda a: jnp.pad(a, [(0, 0)] * (a.ndim - 1) + [(0, LANE - a.shape[-1])])

    inv_freq = ROPE_THETA ** (-jnp.arange(0, rope, 2, dtype=F32) / rope)
    ang = positions.reshape(m, 1).astype(F32) * inv_freq
    cos, sin = jnp.cos(ang), jnp.sin(ang)
    cos_t = lane_pad(jnp.concatenate([cos, cos], axis=-1))
    sin_t = lane_pad(jnp.concatenate([-sin, sin], axis=-1))

    w_kpe = w_down[:, q_lora + kv_lora:]
    w_down_x = jnp.concatenate(
        [w_down[:, :q_lora + kv_lora], lane_pad(w_kpe), lane_pad(w_kpe[:, perm])], axis=1).astype(BF16)
    c = _simple_mm("mla_down", [_rows(x), _vec(norm_g)], _rms, w_down_x, [], lambda acc: (acc,),
                   tn_target=w_down_x.shape[1])
    qb = q_lora // q_lora
    tm = _pick_tile(m, 512, 8)

    def norm_rope(body, pe, rot, g_body, g_pe, g_rot, cos_b, sin_b, extra_scale):
        ss = jnp.sum(body * body, axis=-1, keepdims=True) + jnp.sum(pe * pe, axis=-1, keepdims=True)
        s = lax.rsqrt(ss / qk_dim + RMS_EPS) * extra_scale
        return jnp.concatenate([body * s * g_body, (pe * g_pe * cos_b + rot * g_rot * sin_b) * s], axis=-1)

    w_q_pe = w_uq[:, :, nope:]
    w_q_x = jnp.concatenate([w_uq[:, :, :nope], lane_pad(w_q_pe), lane_pad(w_q_pe[:, :, perm])],
                            axis=-1).reshape(q_lora, heads * 3 * LANE).astype(BF16)
    gq = jnp.concatenate([q_norm[:nope], lane_pad(q_norm[nope:]), lane_pad(q_norm[nope:][perm])]).reshape(1, -1)

    hb = _pick_tile(heads, MLA_HEADS_PER_STEP, 1)

    def q_epi(acc, g_t, cos_b, sin_b):
        per_head = []
        for hh in range(hb):
            a = acc[:, hh * 3 * LANE:(hh + 1) * 3 * LANE]
            per_head.append(norm_rope(a[:, :LANE], a[:, LANE:2 * LANE], a[:, 2 * LANE:], g_t[:, :LANE],
                                      g_t[:, LANE:2 * LANE], g_t[:, 2 * LANE:], cos_b, sin_b, scale))
        return (jnp.stack(per_head, axis=0),)

    q = _mm("mla_q", m,
            [(c, _row_spec(tm, q_lora, 0)), (q_a_norm.reshape(1, -1), _vec_spec(q_lora))], _rms, w_q_x,
            [(gq, _vec_spec(3 * LANE)), (cos_t, _row_spec(tm, LANE)), (sin_t, _row_spec(tm, LANE))], q_epi,
            [(jax.ShapeDtypeStruct((heads, m, 2 * LANE), BF16),
              pl.BlockSpec((hb, tm, 2 * LANE), lambda i, j: (j, i, 0)))],
            tm=tm, tn=hb * 3 * LANE)[0]

    w_kv_x = w_ukv.reshape(kv_lora, heads * (nope + dv)).astype(BF16)
    gk = jnp.concatenate([k_norm[:nope], lane_pad(k_norm[nope:]), lane_pad(k_norm[nope:][perm])]).reshape(1, -1)
    pe_block = (q_lora + kv_lora) // LANE

    def kv_epi(acc, g_t, pe, rot, cos_b, sin_b):
        k_out, v_out = [], []
        for hh in range(hb):
            a = acc[:, hh * (nope + dv):(hh + 1) * (nope + dv)]
            k_out.append(norm_rope(a[:, :nope], pe, rot, g_t[:, :LANE], g_t[:, LANE:2 * LANE],
                                   g_t[:, 2 * LANE:], cos_b, sin_b, 1.0))
            v_out.append(a[:, nope:])
        return jnp.stack(k_out, axis=0), jnp.stack(v_out, axis=0)

    k, v = _mm("mla_kv", m,
               [(c, _row_spec(tm, kv_lora, qb)), (kv_a_norm.reshape(1, -1), _vec_spec(kv_lora))], _rms, w_kv_x,
               [(gk, _vec_spec(3 * LANE)), (c, _row_spec(tm, LANE, pe_block)),
                (c, _row_spec(tm, LANE, pe_block + 1)), (cos_t, _row_spec(tm, LANE)),
                (sin_t, _row_spec(tm, LANE))], kv_epi,
               [(jax.ShapeDtypeStruct((heads, m, 2 * LANE), BF16),
                 pl.BlockSpec((hb, tm, 2 * LANE), lambda i, j: (j, i, 0))),
                (jax.ShapeDtypeStruct((heads, m, dv), BF16),
                 pl.BlockSpec((hb, tm, dv), lambda i, j: (j, i, 0)))],
               tm=tm, tn=hb * (nope + dv))

    score_bound = scale * qk_dim * jnp.max(jnp.abs(q_norm)) * jnp.max(jnp.abs(k_norm))
    small_scores = (score_bound <= MAX_PLAIN_SCORE).astype(jnp.int32).reshape(1)
    o = _attention(batch, q, k, v, small_scores)
    return _simple_mm("mla_out", [_rows(o)], lambda t: t, w_o.reshape(heads * dv, d).astype(BF16),
                      [_epi_tile(x)], lambda acc, x_t: (x_t + acc,), tn_target=d)


def kernel(x, positions, ffn_norm, ffn_w13, ffn_w2, mix_norm, rwkv_mu, rwkv_w_rkv, rwkv_w0, rwkv_w1, rwkv_w2, rwkv_a0, rwkv_a1, rwkv_a2, rwkv_g1, rwkv_g2, rwkv_k_k, rwkv_k_a, rwkv_r_k, rwkv_ln_w, rwkv_ln_b, rwkv_w_o, mla_w_down, mla_q_a_norm, mla_kv_a_norm, mla_w_uq, mla_w_ukv, mla_q_norm, mla_k_norm, mla_w_o):
    batch, seq, d = x.shape
    depth = ffn_norm.shape[0]
    n_mixers = 2
    xf = x.reshape(batch * seq, d)
    d_ff = ffn_w2.shape[2]
    w13p, w2p = _ffn_weights(ffn_w13.reshape((2 * depth,) + ffn_w13.shape[2:]),
                             ffn_w2.reshape((2 * depth,) + ffn_w2.shape[2:]))
    for i in range(depth):
        j = i // n_mixers
        xf = _ffn(xf, ffn_norm[i, 0], w13p, w2p, 2 * i, d_ff)
        if i % n_mixers == 0:
            xf = _rwkv_mix(batch, xf, mix_norm[i], rwkv_mu[j], rwkv_w_rkv[j], rwkv_w0[j], rwkv_w1[j],
                           rwkv_w2[j], rwkv_a0[j], rwkv_a1[j], rwkv_a2[j], rwkv_g1[j], rwkv_g2[j],
                           rwkv_k_k[j], rwkv_k_a[j], rwkv_r_k[j], rwkv_ln_w[j], rwkv_ln_b[j],
                           rwkv_w_o[j])
        else:
            xf = _mla_mix(batch, xf, positions, mix_norm[i], mla_w_down[j], mla_q_a_norm[j],
                          mla_kv_a_norm[j], mla_w_uq[j], mla_w_ukv[j], mla_q_norm[j], mla_k_norm[j],
                          mla_w_o[j])
        xf = _ffn(xf, ffn_norm[i, 1], w13p, w2p, 2 * i + 1, d_ff)
    return xf.reshape(batch, seq, d)
```

```python
import functools
import math

import jax
import jax.numpy as jnp
from jax import lax
from jax.experimental import pallas as pl
from jax.experimental.pallas import tpu as pltpu

F32 = jnp.float32
BF16 = jnp.bfloat16

RMS_EPS = 1e-6
GN_EPS = 64e-5
ROPE_THETA = 10000.0
ATTN_CHUNK = 64
RWKV_HEAD = 64
SCAN_CHUNK = 64
HEADS_PER_GROUP = 4
MAX_PLAIN_SCORE = 30.0
ATTN_Q_TILE = 512
ATTN_HEADS_PER_STEP = 4
FFN_ROWS = 1024
FFN_TILE = 512
MLA_HEADS_PER_STEP = 16
SCAN_CHUNKS_PER_STEP = 4
SCAN_BATCH_ROWS = 1
SCAN_STREAMS = 8
LANE = 128
SUBLANE = 8
VMEM_LIMIT_BYTES = 56 * 1024 * 1024


def _params(semantics):
    return pltpu.CompilerParams(dimension_semantics=semantics,
                                vmem_limit_bytes=VMEM_LIMIT_BYTES)


def _pick_tile(n, target, quantum):
    if n <= target:
        return n
    t = (target // quantum) * quantum
    while t > quantum and n % t:
        t -= quantum
    assert n % t == 0, (n, target, quantum)
    return t


def _round_robin(gens):
    live = list(gens)
    while live:
        live = [g for g in live if next(g, True) is None]


def _rms(x, g):
    return x * lax.rsqrt(jnp.mean(x * x, axis=-1, keepdims=True) + RMS_EPS) * g


def _ffn_body(nj, x_ref, g_ref, wg_ref, wu_ref, w2_ref, o_ref, xn_ref):
    j = pl.program_id(1)

    def tile_sum():
        xn = xn_ref[...]
        gate = jnp.dot(xn, wg_ref[...], preferred_element_type=F32)
        up = jnp.dot(xn, wu_ref[...], preferred_element_type=F32)
        act = (gate * jax.nn.sigmoid(gate) * up).astype(BF16)
        return jnp.dot(act, w2_ref[...], preferred_element_type=F32)

    def normalize():
        xn_ref[...] = _rms(x_ref[...], g_ref[...]).astype(BF16)

    if nj == 1:
        normalize()
        o_ref[...] = x_ref[...] + 0.5 * tile_sum()
    else:
        @pl.when(j == 0)
        def _():
            normalize()
            o_ref[...] = tile_sum()

        if nj > 2:
            @pl.when((j > 0) & (j < nj - 1))
            def _():
                o_ref[...] += tile_sum()

        @pl.when(j == nj - 1)
        def _():
            o_ref[...] = x_ref[...] + 0.5 * (o_ref[...] + tile_sum())


def _ffn_tile(f):
    return min(FFN_TILE, -(-f // LANE) * LANE)


def _tile_cast_body(tiles, x_ref, o_ref):
    tw = o_ref.shape[-1]
    for t, (c0, width) in enumerate(tiles):
        o_ref[t, :, :width] = x_ref[:, c0:c0 + width].astype(BF16)
        if width < tw:
            o_ref[t, :, width:] = jnp.zeros((o_ref.shape[1], tw - width), BF16)


def _tile_cast(name, w, tiles, tw):
    n, rows, cols = w.shape
    tr = _pick_tile(rows, 128, SUBLANE)
    return pl.pallas_call(
        functools.partial(_tile_cast_body, tuple(tiles)),
        grid=(n, rows // tr),
        in_specs=[pl.BlockSpec((None, tr, cols), lambda l, i: (l, i, 0))],
        out_specs=pl.BlockSpec((None, len(tiles), tr, tw), lambda l, i: (l, 0, i, 0)),
        out_shape=jax.ShapeDtypeStruct((n, len(tiles), rows, tw), BF16),
        compiler_params=_params(("parallel", "parallel")),
        name=name,
    )(w)


def _cast_w2_body(f, x_ref, o_ref):
    tr = o_ref.shape[0]
    row = pl.program_id(1) * tr + lax.broadcasted_iota(jnp.int32, o_ref.shape, 0)
    o_ref[...] = jnp.where(row < f, x_ref[...], 0.0).astype(BF16)


def _ffn_weights(w13, w2):
    n, d, f2 = w13.shape
    f = f2 // 2
    assert f % LANE == 0
    tf = _ffn_tile(f)
    fp = -(-f // tf) * tf
    half = [(t * tf, min(tf, f - t * tf)) for t in range(fp // tf)]
    w13p = _tile_cast("cast_w13", w13, half + [(f + c0, width) for c0, width in half], tf)
    w2p = pl.pallas_call(
        functools.partial(_cast_w2_body, f),
        grid=(n, fp // tf),
        in_specs=[pl.BlockSpec((None, tf, d), lambda l, i: (l, i, 0))],
        out_specs=pl.BlockSpec((None, tf, d), lambda l, i: (l, i, 0)),
        out_shape=jax.ShapeDtypeStruct((n, fp, d), BF16),
        compiler_params=_params(("parallel", "parallel")),
        name="cast_w2",
    )(w2)
    return w13p, w2p


def _ffn(x, g, w13p, w2p, layer, f):
    m, d = x.shape
    fp = w2p.shape[1]
    tf = _ffn_tile(f)
    nj = fp // tf
    tm = _pick_tile(m, FFN_ROWS, 8)
    return pl.pallas_call(
        functools.partial(_ffn_body, nj),
        grid=(m // tm, nj),
        in_specs=[
            pl.BlockSpec((tm, d), lambda i, j: (i, 0)),
            pl.BlockSpec((1, d), lambda i, j: (0, 0)),
            pl.BlockSpec((None, None, d, tf), lambda i, j: (layer, j, 0, 0)),
            pl.BlockSpec((None, None, d, tf), lambda i, j: (layer, j + nj, 0, 0)),
            pl.BlockSpec((None, tf, d), lambda i, j: (layer, j, 0)),
        ],
        out_specs=pl.BlockSpec((tm, d), lambda i, j: (i, 0)),
        out_shape=jax.ShapeDtypeStruct((m, d), F32),
        scratch_shapes=[pltpu.VMEM((tm, d), BF16)],
        compiler_params=_params(("parallel", "arbitrary")),
        name="ffn",
    )(x, g.reshape(1, d), w13p, w13p, w2p)


def _mm_body(n_lhs, n_epi, n_out, lhs_fn, epi_fn, *refs):
    lhs_refs = refs[:n_lhs]
    w_ref = refs[n_lhs]
    epi_refs = refs[n_lhs + 1:n_lhs + 1 + n_epi]
    out_refs = refs[n_lhs + 1 + n_epi:n_lhs + 1 + n_epi + n_out]
    lhs_scratch = refs[-1]

    @pl.when(pl.program_id(1) == 0)
    def _():
        lhs_scratch[...] = lhs_fn(*[r[...] for r in lhs_refs]).astype(BF16)

    acc = jnp.dot(lhs_scratch[...], w_ref[...], preferred_element_type=F32)
    outs = epi_fn(acc, *[r[...] for r in epi_refs])
    for o_ref, o in zip(out_refs, outs):
        o_ref[...] = o.astype(o_ref.dtype)


def _mm(name, m, lhs, lhs_fn, w, epi, epi_fn, outs, *, tm, tn):
    k, n = w.shape
    body = functools.partial(_mm_body, len(lhs), len(epi), len(outs), lhs_fn, epi_fn)
    return pl.pallas_call(
        body,
        grid=(m // tm, n // tn),
        in_specs=[s for _, s in lhs] + [pl.BlockSpec((k, tn), lambda i, j: (0, j))]
        + [s for _, s in epi],
        out_specs=[s for _, s in outs],
        out_shape=[s for s, _ in outs],
        scratch_shapes=[pltpu.VMEM((tm, k), BF16)],
        compiler_params=_params(("parallel", "arbitrary")),
        name=name,
    )(*[a for a, _ in lhs], w, *[a for a, _ in epi])


def _row_spec(tm, width, col_block=0):
    return pl.BlockSpec((tm, width), lambda i, j: (i, col_block))


def _tile_spec(tm, tn):
    return pl.BlockSpec((tm, tn), lambda i, j: (i, j))


def _vec_spec(width):
    return pl.BlockSpec((1, width), lambda i, j: (0, 0))


def _colvec_spec(tn):
    return pl.BlockSpec((1, tn), lambda i, j: (0, j))


def _simple_mm(name, lhs, lhs_fn, w, epi, epi_fn, out_dtype=F32, tm_target=512, tn_target=1024):
    m = lhs[0][0].shape[0]
    n = w.shape[1]
    tm = _pick_tile(m, tm_target, 8)
    tn = _pick_tile(n, tn_target, LANE)
    lhs = [(a, spec(tm)) for a, spec in lhs]
    epi = [(a, spec(tm, tn)) for a, spec in epi]
    out = (jax.ShapeDtypeStruct((m, n), out_dtype), _tile_spec(tm, tn))
    return _mm(name, m, lhs, lhs_fn, w, epi, epi_fn, [out], tm=tm, tn=tn)[0]


def _rows(a):
    return a, lambda tm: _row_spec(tm, a.shape[1])


def _vec(a):
    a = a.reshape(1, -1)
    return a, lambda tm: _vec_spec(a.shape[1])


def _epi_tile(a):
    return a, lambda tm, tn: _tile_spec(tm, tn)


def _epi_colvec(a):
    return a.reshape(1, -1), lambda tm, tn: _colvec_spec(tn)


def _rwkv_proj_body(seq_tiles, ranks, x_ref, xprev_ref, g_ref, mu_ref, wr_ref, wk_ref, wv_ref, l1_ref,
                    w2_ref, a2_ref, g2_ref, w0_ref, a0_ref, rkv_ref, ld_ref, ag_ref,
                    lerp_scr, tw_scr, ta_scr, tg_scr):
    i = pl.program_id(0)
    j = pl.program_id(1)
    rw, ra = ranks
    w_refs = (wr_ref, wk_ref, wv_ref)

    def emit_rkv(p):
        rkv_ref[p] = jnp.dot(lerp_scr[p], w_refs[p][...], preferred_element_type=F32).astype(BF16)

    def emit_lora_outputs():
        z = -(w0_ref[...] + jnp.dot(tw_scr[...], w2_ref[...], preferred_element_type=F32))
        softplus = jnp.maximum(z, 0.0) + jnp.log1p(jnp.exp(-jnp.abs(z)))
        ld_ref[...] = -jnp.exp(-softplus - 0.5)
        ag_ref[0] = jax.nn.sigmoid(
            a0_ref[...] + jnp.dot(ta_scr[...], a2_ref[...], preferred_element_type=F32)).astype(BF16)
        ag_ref[1] = jnp.dot(tg_scr[...], g2_ref[...], preferred_element_type=F32).astype(BF16)

    @pl.when(j == 0)
    def _():
        h = _rms(x_ref[...], g_ref[...])
        prev = _rms(xprev_ref[...], g_ref[...])[SUBLANE - 1:SUBLANE, :]
        prev = jnp.where(i % seq_tiles == 0, 0.0, prev)
        row = lax.broadcasted_iota(jnp.int32, h.shape, 0)
        dx = jnp.where(row == 0, prev, pltpu.roll(h, 1, 0)) - h
        def lerp(idx):
            return (h + dx * mu_ref[idx:idx + 1, :]).astype(BF16)

        for p in range(3):
            lerp_scr[p] = lerp(p)
            emit_rkv(p)
        tw_scr[...] = jnp.tanh(jnp.dot(lerp(3), l1_ref[:, :rw],
                                       preferred_element_type=F32)).astype(BF16)
        ta_scr[...] = jnp.dot(lerp(4), l1_ref[:, rw:rw + ra],
                              preferred_element_type=F32).astype(BF16)
        tg_scr[...] = jax.nn.sigmoid(jnp.dot(lerp(5), l1_ref[:, rw + ra:],
                                             preferred_element_type=F32)).astype(BF16)
        emit_lora_outputs()

    @pl.when(j > 0)
    def _():
        for p in range(3):
            emit_rkv(p)
        emit_lora_outputs()


def _pad_lora(wa, wb):
    rank = wa.shape[1]
    rp = -(-rank // LANE) * LANE
    return (jnp.pad(wa, ((0, 0), (0, rp - rank))).astype(BF16),
            jnp.pad(wb, ((0, rp - rank), (0, 0))).astype(BF16))


def _rwkv_proj(batch, x, norm_g, mu, w_rkv, w0, w1, w2, a0, a1, a2, g1, g2):
    m, d = x.shape
    tm = _pick_tile(m // batch, 512, SUBLANE)
    tn = _pick_tile(d, 512, LANE)
    w1p, w2p = _pad_lora(w1, w2)
    a1p, a2p = _pad_lora(a1, a2)
    g1p, g2p = _pad_lora(g1, g2)
    rw, ra, rg = w1p.shape[1], a1p.shape[1], g1p.shape[1]
    l1 = jnp.concatenate([w1p, a1p, g1p], axis=1)
    w_bf = _tile_cast("cast_w_rkv", w_rkv, [(t * tn, tn) for t in range(d // tn)], tn)
    mu_ord = mu[jnp.array([0, 2, 3, 1, 4, 5])]
    col = lambda i, j: (0, j)

    def w_plane(p):
        return pl.BlockSpec((None, None, d, tn), lambda i, j: (p, j, 0, 0))

    def planes(n):
        return pl.BlockSpec((n, tm, tn), lambda i, j: (0, i, j))

    body = functools.partial(_rwkv_proj_body, (m // batch) // tm, (rw, ra))
    return pl.pallas_call(
        body,
        grid=(m // tm, d // tn),
        in_specs=[
            pl.BlockSpec((tm, d), lambda i, j: (i, 0)),
            pl.BlockSpec((SUBLANE, d), lambda i, j: (jnp.maximum(i * (tm // SUBLANE) - 1, 0), 0)),
            pl.BlockSpec((1, d), lambda i, j: (0, 0)),
            pl.BlockSpec((6, d), lambda i, j: (0, 0)),
            w_plane(0), w_plane(1), w_plane(2),
            pl.BlockSpec((d, rw + ra + rg), lambda i, j: (0, 0)),
            pl.BlockSpec((rw, tn), col),
            pl.BlockSpec((ra, tn), col),
            pl.BlockSpec((rg, tn), col),
            pl.BlockSpec((1, tn), col),
            pl.BlockSpec((1, tn), col),
        ],
        out_specs=[planes(3), pl.BlockSpec((tm, tn), lambda i, j: (i, j)), planes(2)],
        out_shape=[jax.ShapeDtypeStruct((3, m, d), BF16), jax.ShapeDtypeStruct((m, d), F32),
                   jax.ShapeDtypeStruct((2, m, d), BF16)],
        scratch_shapes=[pltpu.VMEM((3, tm, d), BF16), pltpu.VMEM((tm, rw), BF16),
                        pltpu.VMEM((tm, ra), BF16), pltpu.VMEM((tm, rg), BF16)],
        compiler_params=_params(("parallel", "arbitrary")),
        name="rwkv_proj",
    )(x, x, norm_g.reshape(1, d), mu_ord, w_bf, w_bf, w_bf, l1, w2p, a2p, g2p,
      w0.reshape(1, d), a0.reshape(1, d))


def _bf(x):
    return x.astype(BF16)


def _dot(a, b):
    return jnp.dot(_bf(a), _bf(b), preferred_element_type=F32)


def _dot_nt(a, b):
    return lax.dot_general(_bf(a), _bf(b), (((1,), (1,)), ((), ())), preferred_element_type=F32)


def _dot_tn(a, b):
    return lax.dot_general(_bf(a), _bf(b), (((0,), (0,)), ((), ())), preferred_element_type=F32)


def _scan_body(n_streams, nb, cps, r_ref, ld_ref, k_ref, v_ref, al_ref, g_ref, kk_w_ref, ka_w_ref,
               rk_w_ref, lnw_ref, lnb_ref, o_ref, s_ref):
    c = SCAN_CHUNK
    w = HEADS_PER_GROUP * RWKV_HEAD
    wc = HEADS_PER_GROUP * c

    @pl.when(pl.program_id(2) == 0)
    def _():
        s_ref[...] = jnp.zeros_like(s_ref)

    tri = jnp.where(lax.broadcasted_iota(jnp.int32, (c, c), 0)
                    >= lax.broadcasted_iota(jnp.int32, (c, c), 1), 1.0, 0.0).astype(BF16)
    bd_r = lax.broadcasted_iota(jnp.int32, (w, w), 0) // RWKV_HEAD
    bd_c = lax.broadcasted_iota(jnp.int32, (w, w), 1) // RWKV_HEAD
    head_mask = bd_r == bd_c
    head_ones = jnp.where(head_mask, 1.0, 0.0).astype(BF16)
    t_idx = lax.broadcasted_iota(jnp.int32, (c, wc), 0)
    s_idx = lax.broadcasted_iota(jnp.int32, (c, wc), 1) % c
    strict = s_idx < t_idx
    incl = s_idx <= t_idx
    eye = jnp.where(s_idx == t_idx, 1.0, 0.0)
    base_level = 3
    base_blocks = (t_idx >> base_level) == (s_idx >> base_level)
    join = {lv: (((t_idx >> lv) & 1) == 1) & ((s_idx >> lv) == (t_idx >> lv) - 1)
            for lv in range(base_level, c.bit_length() - 1)}

    def head_sum(x):
        return jnp.dot(_bf(x), head_ones, preferred_element_type=F32)

    def block_diag(x):
        xb = _bf(x)
        return jnp.where(head_mask, jnp.concatenate([xb] * HEADS_PER_GROUP, axis=0),
                         jnp.zeros((), BF16))

    def stream(bi, si, cj):
        st = bi * n_streams + si
        ln = pl.ds(si * w, w)
        rows = pl.ds(cj * c, c)
        r = r_ref[bi, rows, ln].astype(F32)
        ld = ld_ref[bi, rows, ln]
        k = k_ref[bi, rows, ln].astype(F32)
        v = v_ref[bi, rows, ln].astype(F32)
        al = al_ref[bi, rows, ln].astype(F32)

        ld_hi = _bf(ld)
        rest = ld - ld_hi.astype(F32)
        ld_mid = _bf(rest)
        ld_lo = _bf(rest - ld_mid.astype(F32))
        cum = (jnp.dot(tri, ld_hi, preferred_element_type=F32)
               + jnp.dot(tri, ld_mid, preferred_element_type=F32)
               + jnp.dot(tri, ld_lo, preferred_element_type=F32))
        yield
        p_incl = jnp.exp(cum)
        p_prev = jnp.exp(cum - ld)
        p_inv = jnp.exp(-cum)
        p_last = p_incl[c - 1:c, :]

        kk = k * kk_w_ref[:, ln]
        ss = head_sum(kk * kk)
        yield
        kk = kk / jnp.maximum(jnp.sqrt(ss), 1e-12)
        kmod = k * (1.0 + (al - 1.0) * ka_w_ref[:, ln])

        a_t = -kk * p_prev
        b_t = kk * al * p_inv
        k_t = kmod * p_inv
        r_t = r * p_incl

        ar = jnp.concatenate([a_t, r_t], axis=0)
        m_b = _dot_nt(ar, block_diag(b_t))
        m_k = _dot_nt(ar, block_diag(k_t))
        yield
        a_ab = jnp.where(strict, m_b[:c], 0.0)
        a_rb = jnp.where(incl, m_b[c:], 0.0)
        a_ak = jnp.where(strict, m_k[:c], 0.0)
        a_rk = jnp.where(incl, m_k[c:], 0.0)

        pw = jnp.where(base_blocks, a_ab, 0.0)
        t_inv = eye
        for step in range(base_level):
            if step < base_level - 1:
                z = _dot(jnp.concatenate([pw, t_inv], axis=0), block_diag(pw))
                yield
                pw, t_inv = z[:c], t_inv + z[c:]
            else:
                z = _dot(t_inv, block_diag(pw))
                yield
                t_inv = t_inv + z
        n_bd = block_diag(a_ab)
        level = base_level
        while (1 << level) < c:
            tn = _dot(t_inv, n_bd)
            yield
            x = _dot(tn, block_diag(t_inv))
            yield
            t_inv = t_inv + jnp.where(join[level], x, 0.0)
            level += 1

        s0 = s_ref[st]
        w1 = _dot_nt(ar, s0)
        w2 = _dot(jnp.concatenate([a_ak, a_rk], axis=0), block_diag(v))
        yield
        u = _dot(t_inv, block_diag(w1[:c] + w2[:c]))
        yield
        y = w1[c:] + w2[c:] + _dot(a_rb, block_diag(u))

        uv = jnp.concatenate([u, v], axis=0)
        bk = jnp.concatenate([b_t, k_t], axis=0) * p_last
        s_new = _dot_tn(uv, bk)
        yield
        s_ref[st] = jnp.where(head_mask, s0 * p_last + s_new, 0.0)

        inv_n = 1.0 / RWKV_HEAD
        sums = head_sum(jnp.concatenate([y, r * kmod * rk_w_ref[:, ln]], axis=0))
        yield
        dlt = y - sums[:c] * inv_n
        bsum = sums[c:]
        var = head_sum(dlt * dlt) * inv_n
        yield
        yn = dlt * lax.rsqrt(var + GN_EPS) * lnw_ref[:, ln] + lnb_ref[:, ln]
        bonus = bsum * v
        o_ref[bi, rows, ln] = ((yn + bonus) * g_ref[bi, rows, ln]).astype(o_ref.dtype)

    def chunks(bi, si):
        for cj in range(cps):
            yield from stream(bi, si, cj)

    _round_robin([chunks(bi, si) for bi in range(nb) for si in range(n_streams)])


def _rwkv_scan(batch, rkv, ld, ag, k_k, k_a, r_k, ln_w, ln_b):
    _, m, d = rkv.shape
    t = m // batch
    c = SCAN_CHUNK
    w = HEADS_PER_GROUP * RWKV_HEAD
    assert d % w == 0 and t % c == 0 and c == RWKV_HEAD
    n_streams = _pick_tile(d // w, SCAN_STREAMS, 1)
    wb = n_streams * w
    cps = _pick_tile(t // c, SCAN_CHUNKS_PER_STEP, 1)
    nc = t // (c * cps)
    nb = _pick_tile(batch, SCAN_BATCH_ROWS, 1)
    tok = pl.BlockSpec((nb, cps * c, wb), lambda b, gi, ci: (b, ci, gi))
    par = pl.BlockSpec((1, wb), lambda b, gi, ci: (0, gi))

    def plane(p):
        return pl.BlockSpec((None, nb, cps * c, wb), lambda b, gi, ci: (p, b, ci, gi))

    rkv4 = rkv.reshape(3, batch, t, d)
    ag4 = ag.reshape(2, batch, t, d)
    return pl.pallas_call(
        functools.partial(_scan_body, n_streams, nb, cps),
        grid=(batch // nb, d // wb, nc),
        in_specs=[plane(0), tok, plane(1), plane(2), plane(0), plane(1)] + [par] * 5,
        out_specs=tok,
        out_shape=jax.ShapeDtypeStruct((batch, t, d), BF16),
        scratch_shapes=[pltpu.VMEM((nb * n_streams, w, w), F32)],
        compiler_params=_params(("parallel", "parallel", "arbitrary")),
        name="rwkv_scan",
    )(rkv4, ld.reshape(batch, t, d), rkv4, rkv4, ag4, ag4,
      *[p.reshape(1, d) for p in (k_k, k_a, r_k, ln_w, ln_b)]).reshape(m, d)


def _rwkv_mix(batch, x, norm_g, mu, w_rkv, w0, w1, w2, a0, a1, a2, g1, g2,
              k_k, k_a, r_k, ln_w, ln_b, w_o):
    rkv, ld, ag = _rwkv_proj(batch, x, norm_g, mu, w_rkv, w0, w1, w2, a0, a1, a2, g1, g2)
    yg = _rwkv_scan(batch, rkv, ld, ag, k_k, k_a, r_k.reshape(-1), ln_w, ln_b)
    return _simple_mm("rwkv_out", [_rows(yg)], lambda t: t, w_o.astype(BF16), [_epi_tile(x)],
                      lambda acc, x_t: (x_t + acc,), tn_target=x.shape[1])


def _attn_body(tq, tk, nh, small_ref, q_ref, k_ref, v_ref, o_ref, v1_ref):
    qi = pl.program_id(2)
    dv = v_ref.shape[-1]

    def scores(h, j, masked):
        start = pl.multiple_of(j * tk, tk)
        kb = k_ref[h, pl.ds(start, tk), :]
        s = lax.dot_general(q_ref[h], kb, (((1,), (1,)), ((), ())), preferred_element_type=F32)
        if masked:
            q_chunk = (qi * tq + lax.broadcasted_iota(jnp.int32, (tq, tk), 0)) // ATTN_CHUNK
            k_chunk = (j * tk + lax.broadcasted_iota(jnp.int32, (tq, tk), 1)) // ATTN_CHUNK
            s = jnp.where(k_chunk <= q_chunk, s, jnp.finfo(F32).min)
        return s

    def values(h, j):
        return v_ref[h, pl.ds(pl.multiple_of(j * tk, tk), tk), :]

    def all_heads(make_gen):
        outs = [[] for _ in range(nh)]
        _round_robin([make_gen(h, outs[h]) for h in range(nh)])
        return tuple(o[0] for o in outs)

    def sweep(step, init, diag_step=None):
        carries = lax.fori_loop(
            0, qi, lambda j, cr: all_heads(lambda h, out: step(h, j, cr[h], False, out)), init)
        if diag_step is not None:
            return all_heads(lambda h, out: diag_step(h, carries[h], out))
        return all_heads(lambda h, out: step(h, qi, carries[h], True, out))

    def online_step(h, j, carry, masked, out):
        m_i, l_i, acc = carry
        s = scores(h, j, masked)
        yield
        m_new = jnp.maximum(m_i, jnp.max(s, axis=-1, keepdims=True))
        p = jnp.exp(s - m_new)
        alpha = jnp.exp(m_i - m_new)
        l_new = alpha * l_i + jnp.sum(p, axis=-1, keepdims=True)
        pv = jnp.dot(p.astype(BF16), values(h, j), preferred_element_type=F32)
        yield
        out.append((m_new, l_new, alpha * acc + pv))

    def plain_step(h, j, acc, masked, out):
        s = scores(h, j, masked)
        yield
        p = jnp.exp(s).astype(BF16)
        v1 = v1_ref[h, pl.ds(pl.multiple_of(j * tk, tk), tk), :]
        pv = jnp.dot(p, v1, preferred_element_type=F32)
        yield
        out.append(acc + pv)

    def plain_diag(h, acc, out):
        half = tk // 2
        base = pl.multiple_of(qi * tk, tk)
        visible = (lax.broadcasted_iota(jnp.int32, (tq, half), 1) // ATTN_CHUNK
                   <= lax.broadcasted_iota(jnp.int32, (tq, half), 0) // ATTN_CHUNK)

        def weights(q_rows, key0, mask):
            kb = k_ref[h, pl.ds(key0, half), :]
            s = lax.dot_general(q_rows, kb, (((1,), (1,)), ((), ())), preferred_element_type=F32)
            return jnp.where(mask, jnp.exp(s), 0.0).astype(BF16)

        p_a = weights(q_ref[h], base, visible)
        yield
        pv_a = jnp.dot(p_a, v1_ref[h, pl.ds(base, half), :], preferred_element_type=F32)
        p_b = weights(q_ref[h, half:, :], base + half, visible[:half])
        yield
        pv_b = jnp.dot(p_b, v1_ref[h, pl.ds(base + half, half), :], preferred_element_type=F32)
        yield
        out.append(acc + jnp.concatenate([pv_a[:half], pv_a[half:] + pv_b], axis=0))

    @pl.when(small_ref[0] == 1)
    def _():
        @pl.when(qi == 0)
        def _():
            v1_ref[:, :, :dv] = v_ref[...]
            v1_ref[:, :, dv:] = jnp.ones(v_ref.shape, BF16)

        init = tuple(jnp.zeros((tq, 2 * dv), F32) for _ in range(nh))
        for h, acc in enumerate(sweep(plain_step, init, plain_diag)):
            o_ref[:, h * dv:(h + 1) * dv] = (acc[:, :dv] / acc[:, dv:]).astype(o_ref.dtype)

    @pl.when(small_ref[0] != 1)
    def _():
        init = tuple((jnp.full((tq, 1), -jnp.inf, F32), jnp.zeros((tq, 1), F32),
                      jnp.zeros((tq, dv), F32)) for _ in range(nh))
        for h, (_, l_i, acc) in enumerate(sweep(online_step, init)):
            o_ref[:, h * dv:(h + 1) * dv] = (acc / l_i).astype(o_ref.dtype)


def _attention(batch, q, k, v, small_scores):
    heads, m, dqk = q.shape
    dv = v.shape[-1]
    t = m // batch
    tq = _pick_tile(t, ATTN_Q_TILE, 2 * ATTN_CHUNK)
    assert tq % (2 * ATTN_CHUNK) == 0
    nq = t // tq
    nh = _pick_tile(heads, ATTN_HEADS_PER_STEP, 1)
    return pl.pallas_call(
        functools.partial(_attn_body, tq, tq, nh),
        grid=(batch, heads // nh, nq),
        in_specs=[
            pl.BlockSpec(memory_space=pltpu.SMEM),
            pl.BlockSpec((nh, tq, dqk), lambda b, h, i: (h, b * nq + i, 0)),
            pl.BlockSpec((nh, t, dqk), lambda b, h, i: (h, b, 0)),
            pl.BlockSpec((nh, t, dv), lambda b, h, i: (h, b, 0)),
        ],
        out_specs=pl.BlockSpec((tq, nh * dv), lambda b, h, i: (b * nq + i, h)),
        out_shape=jax.ShapeDtypeStruct((m, heads * dv), BF16),
        scratch_shapes=[pltpu.VMEM((nh, t, 2 * dv), BF16)],
        compiler_params=_params(("parallel", "parallel", "arbitrary")),
        name="mla_attention",
    )(small_scores, q, k, v)


def _mla_mix(batch, x, positions, norm_g, w_down, q_a_norm, kv_a_norm, w_uq, w_ukv,
             q_norm, k_norm, w_o):
    m, d = x.shape
    q_lora = q_a_norm.shape[0]
    kv_lora = kv_a_norm.shape[0]
    heads, qk_dim = w_uq.shape[1], w_uq.shape[2]
    rope = w_down.shape[1] - q_lora - kv_lora
    nope = qk_dim - rope
    dv = w_ukv.shape[2] - nope
    half = rope // 2
    assert nope == LANE and dv == LANE and rope <= LANE and q_lora == kv_lora and q_lora % LANE == 0
    scale = 1.0 / math.sqrt(qk_dim)

    perm = jnp.concatenate([jnp.arange(half, rope), jnp.arange(0, half)])
    lane_pad = lambda a: jnp.pad(a, [(0, 0)] * (a.ndim - 1) + [(0, LANE - a.shape[-1])])

    inv_freq = ROPE_THETA ** (-jnp.arange(0, rope, 2, dtype=F32) / rope)
    ang = positions.reshape(m, 1).astype(F32) * inv_freq
    cos, sin = jnp.cos(ang), jnp.sin(ang)
    cos_t = lane_pad(jnp.concatenate([cos, cos], axis=-1))
    sin_t = lane_pad(jnp.concatenate([-sin, sin], axis=-1))

    w_kpe = w_down[:, q_lora + kv_lora:]
    w_down_x = jnp.concatenate(
        [w_down[:, :q_lora + kv_lora], lane_pad(w_kpe), lane_pad(w_kpe[:, perm])], axis=1).astype(BF16)
    c = _simple_mm("mla_down", [_rows(x), _vec(norm_g)], _rms, w_down_x, [], lambda acc: (acc,),
                   tn_target=w_down_x.shape[1])
    qb = q_lora // q_lora
    tm = _pick_tile(m, 512, 8)

    def norm_rope(body, pe, rot, g_body, g_pe, g_rot, cos_b, sin_b, extra_scale):
        ss = jnp.sum(body * body, axis=-1, keepdims=True) + jnp.sum(pe * pe, axis=-1, keepdims=True)
        s = lax.rsqrt(ss / qk_dim + RMS_EPS) * extra_scale
        return jnp.concatenate([body * s * g_body, (pe * g_pe * cos_b + rot * g_rot * sin_b) * s], axis=-1)

    w_q_pe = w_uq[:, :, nope:]
    w_q_x = jnp.concatenate([w_uq[:, :, :nope], lane_pad(w_q_pe), lane_pad(w_q_pe[:, :, perm])],
                            axis=-1).reshape(q_lora, heads * 3 * LANE).astype(BF16)
    gq = jnp.concatenate([q_norm[:nope], lane_pad(q_norm[nope:]), lane_pad(q_norm[nope:][perm])]).reshape(1, -1)

    hb = _pick_tile(heads, MLA_HEADS_PER_STEP, 1)

    def q_epi(acc, g_t, cos_b, sin_b):
        per_head = []
        for hh in range(hb):
            a = acc[:, hh * 3 * LANE:(hh + 1) * 3 * LANE]
            per_head.append(norm_rope(a[:, :LANE], a[:, LANE:2 * LANE], a[:, 2 * LANE:], g_t[:, :LANE],
                                      g_t[:, LANE:2 * LANE], g_t[:, 2 * LANE:], cos_b, sin_b, scale))
        return (jnp.stack(per_head, axis=0),)

    q = _mm("mla_q", m,
            [(c, _row_spec(tm, q_lora, 0)), (q_a_norm.reshape(1, -1), _vec_spec(q_lora))], _rms, w_q_x,
            [(gq, _vec_spec(3 * LANE)), (cos_t, _row_spec(tm, LANE)), (sin_t, _row_spec(tm, LANE))], q_epi,
            [(jax.ShapeDtypeStruct((heads, m, 2 * LANE), BF16),
              pl.BlockSpec((hb, tm, 2 * LANE), lambda i, j: (j, i, 0)))],
            tm=tm, tn=hb * 3 * LANE)[0]

    w_kv_x = w_ukv.reshape(kv_lora, heads * (nope + dv)).astype(BF16)
    gk = jnp.concatenate([k_norm[:nope], lane_pad(k_norm[nope:]), lane_pad(k_norm[nope:][perm])]).reshape(1, -1)
    pe_block = (q_lora + kv_lora) // LANE

    def kv_epi(acc, g_t, pe, rot, cos_b, sin_b):
        k_out, v_out = [], []
        for hh in range(hb):
            a = acc[:, hh * (nope + dv):(hh + 1) * (nope + dv)]
            k_out.append(norm_rope(a[:, :nope], pe, rot, g_t[:, :LANE], g_t[:, LANE:2 * LANE],
                                   g_t[:, 2 * LANE:], cos_b, sin_b, 1.0))
            v_out.append(a[:, nope:])
        return jnp.stack(k_out, axis=0), jnp.stack(v_out, axis=0)

    k, v = _mm("mla_kv", m,
               [(c, _row_spec(tm, kv_lora, qb)), (kv_a_norm.reshape(1, -1), _vec_spec(kv_lora))], _rms, w_kv_x,
               [(gk, _vec_spec(3 * LANE)), (c, _row_spec(tm, LANE, pe_block)),
                (c, _row_spec(tm, LANE, pe_block + 1)), (cos_t, _row_spec(tm, LANE)),
                (sin_t, _row_spec(tm, LANE))], kv_epi,
               [(jax.ShapeDtypeStruct((heads, m, 2 * LANE), BF16),
                 pl.BlockSpec((hb, tm, 2 * LANE), lambda i, j: (j, i, 0))),
                (jax.ShapeDtypeStruct((heads, m, dv), BF16),
                 pl.BlockSpec((hb, tm, dv), lambda i, j: (j, i, 0)))],
               tm=tm, tn=hb * (nope + dv))

    score_bound = scale * qk_dim * jnp.max(jnp.abs(q_norm)) * jnp.max(jnp.abs(k_norm))
    small_scores = (score_bound <= MAX_PLAIN_SCORE).astype(jnp.int32).reshape(1)
    o = _attention(batch, q, k, v, small_scores)
    return _simple_mm("mla_out", [_rows(o)], lambda t: t, w_o.reshape(heads * dv, d).astype(BF16),
                      [_epi_tile(x)], lambda acc, x_t: (x_t + acc,), tn_target=d)


def kernel(x, positions, ffn_norm, ffn_w13, ffn_w2, mix_norm, rwkv_mu, rwkv_w_rkv, rwkv_w0, rwkv_w1, rwkv_w2, rwkv_a0, rwkv_a1, rwkv_a2, rwkv_g1, rwkv_g2, rwkv_k_k, rwkv_k_a, rwkv_r_k, rwkv_ln_w, rwkv_ln_b, rwkv_w_o, mla_w_down, mla_q_a_norm, mla_kv_a_norm, mla_w_uq, mla_w_ukv, mla_q_norm, mla_k_norm, mla_w_o):
    batch, seq, d = x.shape
    depth = ffn_norm.shape[0]
    n_mixers = 2
    xf = x.reshape(batch * seq, d)
    d_ff = ffn_w2.shape[2]
    w13p, w2p = _ffn_weights(ffn_w13.reshape((2 * depth,) + ffn_w13.shape[2:]),
                             ffn_w2.reshape((2 * depth,) + ffn_w2.shape[2:]))
    for i in range(depth):
        j = i // n_mixers
        xf = _ffn(xf, ffn_norm[i, 0], w13p, w2p, 2 * i, d_ff)
        if i % n_mixers == 0:
            xf = _rwkv_mix(batch, xf, mix_norm[i], rwkv_mu[j], rwkv_w_rkv[j], rwkv_w0[j], rwkv_w1[j],
                           rwkv_w2[j], rwkv_a0[j], rwkv_a1[j], rwkv_a2[j], rwkv_g1[j], rwkv_g2[j],
                           rwkv_k_k[j], rwkv_k_a[j], rwkv_r_k[j], rwkv_ln_w[j], rwkv_ln_b[j],
                           rwkv_w_o[j])
        else:
            xf = _mla_mix(batch, xf, positions, mix_norm[i], mla_w_down[j], mla_q_a_norm[j],
                          mla_kv_a_norm[j], mla_w_uq[j], mla_w_ukv[j], mla_q_norm[j], mla_k_norm[j],
                          mla_w_o[j])
        xf = _ffn(xf, ffn_norm[i, 1], w13p, w2p, 2 * i + 1, d_ff)
    return xf.reshape(batch, seq, d)
```

```python
import functools
import math

import jax
import jax.numpy as jnp
from jax import lax
from jax.experimental import pallas as pl
from jax.experimental.pallas import tpu as pltpu

F32 = jnp.float32
BF16 = jnp.bfloat16

RMS_EPS = 1e-6
GN_EPS = 64e-5
ROPE_THETA = 10000.0
ATTN_CHUNK = 64
RWKV_HEAD = 64
SCAN_CHUNK = 64
HEADS_PER_GROUP = 4
MAX_PLAIN_SCORE = 30.0
ATTN_Q_TILE = 512
ATTN_HEADS_PER_STEP = 4
FFN_ROWS = 1024
FFN_TILE = 512
MLA_HEADS_PER_STEP = 16
SCAN_CHUNKS_PER_STEP = 4
SCAN_BATCH_ROWS = 1
SCAN_STREAMS = 8
LANE = 128
SUBLANE = 8
VMEM_LIMIT_BYTES = 56 * 1024 * 1024


def _params(semantics):
    return pltpu.CompilerParams(dimension_semantics=semantics,
                                vmem_limit_bytes=VMEM_LIMIT_BYTES)


def _pick_tile(n, target, quantum):
    if n <= target:
        return n
    t = (target // quantum) * quantum
    while t > quantum and n % t:
        t -= quantum
    assert n % t == 0, (n, target, quantum)
    return t


def _round_robin(gens):
    live = list(gens)
    while live:
        live = [g for g in live if next(g, True) is None]


def _rms(x, g):
    return x * lax.rsqrt(jnp.mean(x * x, axis=-1, keepdims=True) + RMS_EPS) * g


def _ffn_body(nj, x_ref, g_ref, wg_ref, wu_ref, w2_ref, o_ref, xn_ref):
    j = pl.program_id(1)

    def tile_sum():
        xn = xn_ref[...]
        gate = jnp.dot(xn, wg_ref[...], preferred_element_type=F32)
        up = jnp.dot(xn, wu_ref[...], preferred_element_type=F32)
        act = (gate * jax.nn.sigmoid(gate) * up).astype(BF16)
        return jnp.dot(act, w2_ref[...], preferred_element_type=F32)

    def normalize():
        xn_ref[...] = _rms(x_ref[...], g_ref[...]).astype(BF16)

    if nj == 1:
        normalize()
        o_ref[...] = x_ref[...] + 0.5 * tile_sum()
    else:
        @pl.when(j == 0)
        def _():
            normalize()
            o_ref[...] = tile_sum()

        if nj > 2:
            @pl.when((j > 0) & (j < nj - 1))
            def _():
                o_ref[...] += tile_sum()

        @pl.when(j == nj - 1)
        def _():
            o_ref[...] = x_ref[...] + 0.5 * (o_ref[...] + tile_sum())


def _ffn_tile(f):
    return min(FFN_TILE, -(-f // LANE) * LANE)


def _tile_cast_body(tiles, x_ref, o_ref):
    tw = o_ref.shape[-1]
    for t, (c0, width) in enumerate(tiles):
        o_ref[t, :, :width] = x_ref[:, c0:c0 + width].astype(BF16)
        if width < tw:
            o_ref[t, :, width:] = jnp.zeros((o_ref.shape[1], tw - width), BF16)


def _tile_cast(name, w, tiles, tw):
    n, rows, cols = w.shape
    tr = _pick_tile(rows, 128, SUBLANE)
    return pl.pallas_call(
        functools.partial(_tile_cast_body, tuple(tiles)),
        grid=(n, rows // tr),
        in_specs=[pl.BlockSpec((None, tr, cols), lambda l, i: (l, i, 0))],
        out_specs=pl.BlockSpec((None, len(tiles), tr, tw), lambda l, i: (l, 0, i, 0)),
        out_shape=jax.ShapeDtypeStruct((n, len(tiles), rows, tw), BF16),
        compiler_params=_params(("parallel", "parallel")),
        name=name,
    )(w)


def _cast_w2_body(f, x_ref, o_ref):
    tr = o_ref.shape[0]
    row = pl.program_id(1) * tr + lax.broadcasted_iota(jnp.int32, o_ref.shape, 0)
    o_ref[...] = jnp.where(row < f, x_ref[...], 0.0).astype(BF16)


def _ffn_weights(w13, w2):
    n, d, f2 = w13.shape
    f = f2 // 2
    assert f % LANE == 0
    tf = _ffn_tile(f)
    fp = -(-f // tf) * tf
    half = [(t * tf, min(tf, f - t * tf)) for t in range(fp // tf)]
    w13p = _tile_cast("cast_w13", w13, half + [(f + c0, width) for c0, width in half], tf)
    w2p = pl.pallas_call(
        functools.partial(_cast_w2_body, f),
        grid=(n, fp // tf),
        in_specs=[pl.BlockSpec((None, tf, d), lambda l, i: (l, i, 0))],
        out_specs=pl.BlockSpec((None, tf, d), lambda l, i: (l, i, 0)),
        out_shape=jax.ShapeDtypeStruct((n, fp, d), BF16),
        compiler_params=_params(("parallel", "parallel")),
        name="cast_w2",
    )(w2)
    return w13p, w2p


def _ffn(x, g, w13p, w2p, layer, f):
    m, d = x.shape
    fp = w2p.shape[1]
    tf = _ffn_tile(f)
    nj = fp // tf
    tm = _pick_tile(m, FFN_ROWS, 8)
    return pl.pallas_call(
        functools.partial(_ffn_body, nj),
        grid=(m // tm, nj),
        in_specs=[
            pl.BlockSpec((tm, d), lambda i, j: (i, 0)),
            pl.BlockSpec((1, d), lambda i, j: (0, 0)),
            pl.BlockSpec((None, None, d, tf), lambda i, j: (layer, j, 0, 0)),
            pl.BlockSpec((None, None, d, tf), lambda i, j: (layer, j + nj, 0, 0)),
            pl.BlockSpec((None, tf, d), lambda i, j: (layer, j, 0)),
        ],
        out_specs=pl.BlockSpec((tm, d), lambda i, j: (i, 0)),
        out_shape=jax.ShapeDtypeStruct((m, d), F32),
        scratch_shapes=[pltpu.VMEM((tm, d), BF16)],
        compiler_params=_params(("parallel", "arbitrary")),
        name="ffn",
    )(x, g.reshape(1, d), w13p, w13p, w2p)


def _mm_body(single_col_tile, n_lhs, n_epi, n_out, lhs_fn, epi_fn, *refs):
    lhs_refs = refs[:n_lhs]
    w_ref = refs[n_lhs]
    epi_refs = refs[n_lhs + 1:n_lhs + 1 + n_epi]
    out_refs = refs[n_lhs + 1 + n_epi:n_lhs + 1 + n_epi + n_out]

    if single_col_tile:
        lhs = lhs_fn(*[r[...] for r in lhs_refs]).astype(BF16)
    else:
        lhs_scratch = refs[-1]

        @pl.when(pl.program_id(1) == 0)
        def _():
            lhs_scratch[...] = lhs_fn(*[r[...] for r in lhs_refs]).astype(BF16)

        lhs = lhs_scratch[...]
    acc = jnp.dot(lhs, w_ref[...], preferred_element_type=F32)
    outs = epi_fn(acc, *[r[...] for r in epi_refs])
    for o_ref, o in zip(out_refs, outs):
        o_ref[...] = o.astype(o_ref.dtype)


def _mm(name, m, lhs, lhs_fn, w, epi, epi_fn, outs, *, tm, tn):
    k, n = w.shape
    single = n == tn
    body = functools.partial(_mm_body, single, len(lhs), len(epi), len(outs), lhs_fn, epi_fn)
    return pl.pallas_call(
        body,
        grid=(m // tm, n // tn),
        in_specs=[s for _, s in lhs] + [pl.BlockSpec((k, tn), lambda i, j: (0, j))]
        + [s for _, s in epi],
        out_specs=[s for _, s in outs],
        out_shape=[s for s, _ in outs],
        scratch_shapes=[] if single else [pltpu.VMEM((tm, k), BF16)],
        compiler_params=_params(("parallel", "arbitrary")),
        name=name,
    )(*[a for a, _ in lhs], w, *[a for a, _ in epi])


def _row_spec(tm, width, col_block=0):
    return pl.BlockSpec((tm, width), lambda i, j: (i, col_block))


def _tile_spec(tm, tn):
    return pl.BlockSpec((tm, tn), lambda i, j: (i, j))


def _vec_spec(width):
    return pl.BlockSpec((1, width), lambda i, j: (0, 0))


def _colvec_spec(tn):
    return pl.BlockSpec((1, tn), lambda i, j: (0, j))


def _simple_mm(name, lhs, lhs_fn, w, epi, epi_fn, out_dtype=F32, tm_target=512, tn_target=1024):
    m = lhs[0][0].shape[0]
    n = w.shape[1]
    tm = _pick_tile(m, tm_target, 8)
    tn = _pick_tile(n, tn_target, LANE)
    lhs = [(a, spec(tm)) for a, spec in lhs]
    epi = [(a, spec(tm, tn)) for a, spec in epi]
    out = (jax.ShapeDtypeStruct((m, n), out_dtype), _tile_spec(tm, tn))
    return _mm(name, m, lhs, lhs_fn, w, epi, epi_fn, [out], tm=tm, tn=tn)[0]


def _rows(a):
    return a, lambda tm: _row_spec(tm, a.shape[1])


def _vec(a):
    a = a.reshape(1, -1)
    return a, lambda tm: _vec_spec(a.shape[1])


def _epi_tile(a):
    return a, lambda tm, tn: _tile_spec(tm, tn)


def _epi_colvec(a):
    return a.reshape(1, -1), lambda tm, tn: _colvec_spec(tn)


def _rwkv_proj_body(seq_tiles, ranks, x_ref, xprev_ref, g_ref, mu_ref, wr_ref, wk_ref, wv_ref, l1_ref,
                    w2_ref, a2_ref, g2_ref, w0_ref, a0_ref, rkv_ref, ld_ref, ag_ref,
                    lerp_scr, tw_scr, ta_scr, tg_scr):
    i = pl.program_id(0)
    j = pl.program_id(1)
    rw, ra = ranks
    w_refs = (wr_ref, wk_ref, wv_ref)

    def emit_rkv(p):
        rkv_ref[p] = jnp.dot(lerp_scr[p], w_refs[p][...], preferred_element_type=F32).astype(BF16)

    def emit_lora_outputs():
        z = -(w0_ref[...] + jnp.dot(tw_scr[...], w2_ref[...], preferred_element_type=F32))
        softplus = jnp.maximum(z, 0.0) + jnp.log1p(jnp.exp(-jnp.abs(z)))
        ld_ref[...] = -jnp.exp(-softplus - 0.5)
        ag_ref[0] = jax.nn.sigmoid(
            a0_ref[...] + jnp.dot(ta_scr[...], a2_ref[...], preferred_element_type=F32)).astype(BF16)
        ag_ref[1] = jnp.dot(tg_scr[...], g2_ref[...], preferred_element_type=F32).astype(BF16)

    @pl.when(j == 0)
    def _():
        h = _rms(x_ref[...], g_ref[...])
        prev = _rms(xprev_ref[...], g_ref[...])[SUBLANE - 1:SUBLANE, :]
        prev = jnp.where(i % seq_tiles == 0, 0.0, prev)
        row = lax.broadcasted_iota(jnp.int32, h.shape, 0)
        dx = jnp.where(row == 0, prev, pltpu.roll(h, 1, 0)) - h
        def lerp(idx):
            return (h + dx * mu_ref[idx:idx + 1, :]).astype(BF16)

        for p in range(3):
            lerp_scr[p] = lerp(p)
            emit_rkv(p)
        tw_scr[...] = jnp.tanh(jnp.dot(lerp(3), l1_ref[:, :rw],
                                       preferred_element_type=F32)).astype(BF16)
        ta_scr[...] = jnp.dot(lerp(4), l1_ref[:, rw:rw + ra],
                              preferred_element_type=F32).astype(BF16)
        tg_scr[...] = jax.nn.sigmoid(jnp.dot(lerp(5), l1_ref[:, rw + ra:],
                                             preferred_element_type=F32)).astype(BF16)
        emit_lora_outputs()

    @pl.when(j > 0)
    def _():
        for p in range(3):
            emit_rkv(p)
        emit_lora_outputs()


def _pad_lora(wa, wb):
    rank = wa.shape[1]
    rp = -(-rank // LANE) * LANE
    return (jnp.pad(wa, ((0, 0), (0, rp - rank))).astype(BF16),
            jnp.pad(wb, ((0, rp - rank), (0, 0))).astype(BF16))


def _rwkv_proj(batch, x, norm_g, mu, w_rkv, w0, w1, w2, a0, a1, a2, g1, g2):
    m, d = x.shape
    tm = _pick_tile(m // batch, 512, SUBLANE)
    tn = _pick_tile(d, 512, LANE)
    w1p, w2p = _pad_lora(w1, w2)
    a1p, a2p = _pad_lora(a1, a2)
    g1p, g2p = _pad_lora(g1, g2)
    rw, ra, rg = w1p.shape[1], a1p.shape[1], g1p.shape[1]
    l1 = jnp.concatenate([w1p, a1p, g1p], axis=1)
    w_bf = _tile_cast("cast_w_rkv", w_rkv, [(t * tn, tn) for t in range(d // tn)], tn)
    mu_ord = mu[jnp.array([0, 2, 3, 1, 4, 5])]
    col = lambda i, j: (0, j)

    def w_plane(p):
        return pl.BlockSpec((None, None, d, tn), lambda i, j: (p, j, 0, 0))

    def planes(n):
        return pl.BlockSpec((n, tm, tn), lambda i, j: (0, i, j))

    body = functools.partial(_rwkv_proj_body, (m // batch) // tm, (rw, ra))
    return pl.pallas_call(
        body,
        grid=(m // tm, d // tn),
        in_specs=[
            pl.BlockSpec((tm, d), lambda i, j: (i, 0)),
            pl.BlockSpec((SUBLANE, d), lambda i, j: (jnp.maximum(i * (tm // SUBLANE) - 1, 0), 0)),
            pl.BlockSpec((1, d), lambda i, j: (0, 0)),
            pl.BlockSpec((6, d), lambda i, j: (0, 0)),
            w_plane(0), w_plane(1), w_plane(2),
            pl.BlockSpec((d, rw + ra + rg), lambda i, j: (0, 0)),
            pl.BlockSpec((rw, tn), col),
            pl.BlockSpec((ra, tn), col),
            pl.BlockSpec((rg, tn), col),
            pl.BlockSpec((1, tn), col),
            pl.BlockSpec((1, tn), col),
        ],
        out_specs=[planes(3), pl.BlockSpec((tm, tn), lambda i, j: (i, j)), planes(2)],
        out_shape=[jax.ShapeDtypeStruct((3, m, d), BF16), jax.ShapeDtypeStruct((m, d), F32),
                   jax.ShapeDtypeStruct((2, m, d), BF16)],
        scratch_shapes=[pltpu.VMEM((3, tm, d), BF16), pltpu.VMEM((tm, rw), BF16),
                        pltpu.VMEM((tm, ra), BF16), pltpu.VMEM((tm, rg), BF16)],
        compiler_params=_params(("parallel", "arbitrary")),
        name="rwkv_proj",
    )(x, x, norm_g.reshape(1, d), mu_ord, w_bf, w_bf, w_bf, l1, w2p, a2p, g2p,
      w0.reshape(1, d), a0.reshape(1, d))


def _bf(x):
    return x.astype(BF16)


def _dot(a, b):
    return jnp.dot(_bf(a), _bf(b), preferred_element_type=F32)


def _dot_nt(a, b):
    return lax.dot_general(_bf(a), _bf(b), (((1,), (1,)), ((), ())), preferred_element_type=F32)


def _dot_tn(a, b):
    return lax.dot_general(_bf(a), _bf(b), (((0,), (0,)), ((), ())), preferred_element_type=F32)


def _scan_body(n_streams, nb, cps, r_ref, ld_ref, k_ref, v_ref, al_ref, g_ref, kk_w_ref, ka_w_ref,
               rk_w_ref, lnw_ref, lnb_ref, o_ref, s_ref):
    c = SCAN_CHUNK
    w = HEADS_PER_GROUP * RWKV_HEAD
    wc = HEADS_PER_GROUP * c

    @pl.when(pl.program_id(2) == 0)
    def _():
        s_ref[...] = jnp.zeros_like(s_ref)

    tri = jnp.where(lax.broadcasted_iota(jnp.int32, (c, c), 0)
                    >= lax.broadcasted_iota(jnp.int32, (c, c), 1), 1.0, 0.0).astype(BF16)
    bd_r = lax.broadcasted_iota(jnp.int32, (w, w), 0) // RWKV_HEAD
    bd_c = lax.broadcasted_iota(jnp.int32, (w, w), 1) // RWKV_HEAD
    head_mask = bd_r == bd_c
    head_ones = jnp.where(head_mask, 1.0, 0.0).astype(BF16)
    t_idx = lax.broadcasted_iota(jnp.int32, (c, wc), 0)
    s_idx = lax.broadcasted_iota(jnp.int32, (c, wc), 1) % c
    strict = s_idx < t_idx
    incl = s_idx <= t_idx
    eye = jnp.where(s_idx == t_idx, 1.0, 0.0)
    base_level = 3
    base_blocks = (t_idx >> base_level) == (s_idx >> base_level)
    join = {lv: (((t_idx >> lv) & 1) == 1) & ((s_idx >> lv) == (t_idx >> lv) - 1)
            for lv in range(base_level, c.bit_length() - 1)}

    def head_sum(x):
        return jnp.dot(_bf(x), head_ones, preferred_element_type=F32)

    def block_diag(x):
        xb = _bf(x)
        return jnp.where(head_mask, jnp.concatenate([xb] * HEADS_PER_GROUP, axis=0),
                         jnp.zeros((), BF16))

    def stream(bi, si, cj):
        st = bi * n_streams + si
        ln = pl.ds(si * w, w)
        rows = pl.ds(cj * c, c)
        r = r_ref[bi, rows, ln].astype(F32)
        ld = ld_ref[bi, rows, ln]
        k = k_ref[bi, rows, ln].astype(F32)
        v = v_ref[bi, rows, ln].astype(F32)
        al = al_ref[bi, rows, ln].astype(F32)

        ld_hi = _bf(ld)
        rest = ld - ld_hi.astype(F32)
        ld_mid = _bf(rest)
        ld_lo = _bf(rest - ld_mid.astype(F32))
        cum = (jnp.dot(tri, ld_hi, preferred_element_type=F32)
               + jnp.dot(tri, ld_mid, preferred_element_type=F32)
               + jnp.dot(tri, ld_lo, preferred_element_type=F32))
        kk = k * kk_w_ref[:, ln]
        ss = head_sum(kk * kk)
        yield
        p_incl = jnp.exp(cum)
        p_prev = jnp.exp(cum - ld)
        p_inv = jnp.exp(-cum)
        p_last = p_incl[c - 1:c, :]
        kk = kk / jnp.maximum(jnp.sqrt(ss), 1e-12)
        kmod = k * (1.0 + (al - 1.0) * ka_w_ref[:, ln])

        a_t = -kk * p_prev
        b_t = kk * al * p_inv
        k_t = kmod * p_inv
        r_t = r * p_incl

        ar = jnp.concatenate([a_t, r_t], axis=0)
        m_b = _dot_nt(ar, block_diag(b_t))
        m_k = _dot_nt(ar, block_diag(k_t))
        s0 = s_ref[st]
        w1 = _dot_nt(ar, s0)
        yield
        a_ab = jnp.where(strict, m_b[:c], 0.0)
        a_rb = jnp.where(incl, m_b[c:], 0.0)
        a_ak = jnp.where(strict, m_k[:c], 0.0)
        a_rk = jnp.where(incl, m_k[c:], 0.0)
        w2 = _dot(jnp.concatenate([a_ak, a_rk], axis=0), block_diag(v))

        pw = jnp.where(base_blocks, a_ab, 0.0)
        t_inv = eye
        for step in range(base_level):
            if step < base_level - 1:
                z = _dot(jnp.concatenate([pw, t_inv], axis=0), block_diag(pw))
                yield
                pw, t_inv = z[:c], t_inv + z[c:]
            else:
                z = _dot(t_inv, block_diag(pw))
                yield
                t_inv = t_inv + z
        n_bd = block_diag(a_ab)
        level = base_level
        while (1 << level) < c:
            tn = _dot(t_inv, n_bd)
            yield
            x = _dot(tn, block_diag(t_inv))
            yield
            t_inv = t_inv + jnp.where(join[level], x, 0.0)
            level += 1

        u = _dot(t_inv, block_diag(w1[:c] + w2[:c]))
        yield
        y = w1[c:] + w2[c:] + _dot(a_rb, block_diag(u))

        uv = jnp.concatenate([u, v], axis=0)
        bk = jnp.concatenate([b_t, k_t], axis=0) * p_last
        s_new = _dot_tn(uv, bk)
        yield
        s_ref[st] = jnp.where(head_mask, s0 * p_last + s_new, 0.0)

        inv_n = 1.0 / RWKV_HEAD
        sums = head_sum(jnp.concatenate([y, r * kmod * rk_w_ref[:, ln]], axis=0))
        yield
        dlt = y - sums[:c] * inv_n
        bsum = sums[c:]
        var = head_sum(dlt * dlt) * inv_n
        yield
        yn = dlt * lax.rsqrt(var + GN_EPS) * lnw_ref[:, ln] + lnb_ref[:, ln]
        bonus = bsum * v
        o_ref[bi, rows, ln] = ((yn + bonus) * g_ref[bi, rows, ln]).astype(o_ref.dtype)

    def chunks(bi, si):
        for cj in range(cps):
            yield from stream(bi, si, cj)

    _round_robin([chunks(bi, si) for bi in range(nb) for si in range(n_streams)])


def _rwkv_scan(batch, rkv, ld, ag, k_k, k_a, r_k, ln_w, ln_b):
    _, m, d = rkv.shape
    t = m // batch
    c = SCAN_CHUNK
    w = HEADS_PER_GROUP * RWKV_HEAD
    assert d % w == 0 and t % c == 0 and c == RWKV_HEAD
    n_streams = _pick_tile(d // w, SCAN_STREAMS, 1)
    wb = n_streams * w
    cps = _pick_tile(t // c, SCAN_CHUNKS_PER_STEP, 1)
    nc = t // (c * cps)
    nb = _pick_tile(batch, SCAN_BATCH_ROWS, 1)
    tok = pl.BlockSpec((nb, cps * c, wb), lambda b, gi, ci: (b, ci, gi))
    par = pl.BlockSpec((1, wb), lambda b, gi, ci: (0, gi))

    def plane(p):
        return pl.BlockSpec((None, nb, cps * c, wb), lambda b, gi, ci: (p, b, ci, gi))

    rkv4 = rkv.reshape(3, batch, t, d)
    ag4 = ag.reshape(2, batch, t, d)
    return pl.pallas_call(
        functools.partial(_scan_body, n_streams, nb, cps),
        grid=(batch // nb, d // wb, nc),
        in_specs=[plane(0), tok, plane(1), plane(2), plane(0), plane(1)] + [par] * 5,
        out_specs=tok,
        out_shape=jax.ShapeDtypeStruct((batch, t, d), BF16),
        scratch_shapes=[pltpu.VMEM((nb * n_streams, w, w), F32)],
        compiler_params=_params(("parallel", "parallel", "arbitrary")),
        name="rwkv_scan",
    )(rkv4, ld.reshape(batch, t, d), rkv4, rkv4, ag4, ag4,
      *[p.reshape(1, d) for p in (k_k, k_a, r_k, ln_w, ln_b)]).reshape(m, d)


def _rwkv_mix(batch, x, norm_g, mu, w_rkv, w0, w1, w2, a0, a1, a2, g1, g2,
              k_k, k_a, r_k, ln_w, ln_b, w_o):
    rkv, ld, ag = _rwkv_proj(batch, x, norm_g, mu, w_rkv, w0, w1, w2, a0, a1, a2, g1, g2)
    yg = _rwkv_scan(batch, rkv, ld, ag, k_k, k_a, r_k.reshape(-1), ln_w, ln_b)
    return _simple_mm("rwkv_out", [_rows(yg)], lambda t: t, w_o.astype(BF16), [_epi_tile(x)],
                      lambda acc, x_t: (x_t + acc,), tn_target=x.shape[1])


def _attn_body(tq, tk, nh, small_ref, q_ref, k_ref, v_ref, o_ref, v1_ref):
    qi = pl.program_id(2)
    dv = v_ref.shape[-1]

    def scores(h, j, masked):
        start = pl.multiple_of(j * tk, tk)
        kb = k_ref[h, pl.ds(start, tk), :]
        s = lax.dot_general(q_ref[h], kb, (((1,), (1,)), ((), ())), preferred_element_type=F32)
        if masked:
            q_chunk = (qi * tq + lax.broadcasted_iota(jnp.int32, (tq, tk), 0)) // ATTN_CHUNK
            k_chunk = (j * tk + lax.broadcasted_iota(jnp.int32, (tq, tk), 1)) // ATTN_CHUNK
            s = jnp.where(k_chunk <= q_chunk, s, jnp.finfo(F32).min)
        return s

    def values(h, j):
        return v_ref[h, pl.ds(pl.multiple_of(j * tk, tk), tk), :]

    def all_heads(make_gen):
        outs = [[] for _ in range(nh)]
        _round_robin([make_gen(h, outs[h]) for h in range(nh)])
        return tuple(o[0] for o in outs)

    def sweep(step, init, diag_step=None):
        carries = lax.fori_loop(
            0, qi, lambda j, cr: all_heads(lambda h, out: step(h, j, cr[h], False, out)), init)
        if diag_step is not None:
            return all_heads(lambda h, out: diag_step(h, carries[h], out))
        return all_heads(lambda h, out: step(h, qi, carries[h], True, out))

    def online_step(h, j, carry, masked, out):
        m_i, l_i, acc = carry
        s = scores(h, j, masked)
        yield
        m_new = jnp.maximum(m_i, jnp.max(s, axis=-1, keepdims=True))
        p = jnp.exp(s - m_new)
        alpha = jnp.exp(m_i - m_new)
        l_new = alpha * l_i + jnp.sum(p, axis=-1, keepdims=True)
        pv = jnp.dot(p.astype(BF16), values(h, j), preferred_element_type=F32)
        yield
        out.append((m_new, l_new, alpha * acc + pv))

    def plain_step(h, j, acc, masked, out):
        s = scores(h, j, masked)
        yield
        p = jnp.exp(s).astype(BF16)
        v1 = v1_ref[h, pl.ds(pl.multiple_of(j * tk, tk), tk), :]
        pv = jnp.dot(p, v1, preferred_element_type=F32)
        yield
        out.append(acc + pv)

    def plain_diag(h, acc, out):
        half = tk // 2
        base = pl.multiple_of(qi * tk, tk)
        visible = (lax.broadcasted_iota(jnp.int32, (tq, half), 1) // ATTN_CHUNK
                   <= lax.broadcasted_iota(jnp.int32, (tq, half), 0) // ATTN_CHUNK)

        def weights(q_rows, key0, mask):
            kb = k_ref[h, pl.ds(key0, half), :]
            s = lax.dot_general(q_rows, kb, (((1,), (1,)), ((), ())), preferred_element_type=F32)
            return jnp.where(mask, jnp.exp(s), 0.0).astype(BF16)

        p_a = weights(q_ref[h], base, visible)
        yield
        pv_a = jnp.dot(p_a, v1_ref[h, pl.ds(base, half), :], preferred_element_type=F32)
        p_b = weights(q_ref[h, half:, :], base + half, visible[:half])
        yield
        pv_b = jnp.dot(p_b, v1_ref[h, pl.ds(base + half, half), :], preferred_element_type=F32)
        yield
        out.append(acc + jnp.concatenate([pv_a[:half], pv_a[half:] + pv_b], axis=0))

    @pl.when(small_ref[0] == 1)
    def _():
        @pl.when(qi == 0)
        def _():
            v1_ref[:, :, :dv] = v_ref[...]
            v1_ref[:, :, dv:] = jnp.ones(v_ref.shape, BF16)

        init = tuple(jnp.zeros((tq, 2 * dv), F32) for _ in range(nh))
        for h, acc in enumerate(sweep(plain_step, init, plain_diag)):
            o_ref[:, h * dv:(h + 1) * dv] = (acc[:, :dv] / acc[:, dv:]).astype(o_ref.dtype)

    @pl.when(small_ref[0] != 1)
    def _():
        init = tuple((jnp.full((tq, 1), -jnp.inf, F32), jnp.zeros((tq, 1), F32),
                      jnp.zeros((tq, dv), F32)) for _ in range(nh))
        for h, (_, l_i, acc) in enumerate(sweep(online_step, init)):
            o_ref[:, h * dv:(h + 1) * dv] = (acc / l_i).astype(o_ref.dtype)


def _attention(batch, q, k, v, small_scores):
    heads, m, dqk = q.shape
    dv = v.shape[-1]
    t = m // batch
    tq = _pick_tile(t, ATTN_Q_TILE, 2 * ATTN_CHUNK)
    assert tq % (2 * ATTN_CHUNK) == 0
    nq = t // tq
    nh = _pick_tile(heads, ATTN_HEADS_PER_STEP, 1)
    return pl.pallas_call(
        functools.partial(_attn_body, tq, tq, nh),
        grid=(batch, heads // nh, nq),
        in_specs=[
            pl.BlockSpec(memory_space=pltpu.SMEM),
            pl.BlockSpec((nh, tq, dqk), lambda b, h, i: (h, b * nq + i, 0)),
            pl.BlockSpec((nh, t, dqk), lambda b, h, i: (h, b, 0)),
            pl.BlockSpec((nh, t, dv), lambda b, h, i: (h, b, 0)),
        ],
        out_specs=pl.BlockSpec((tq, nh * dv), lambda b, h, i: (b * nq + i, h)),
        out_shape=jax.ShapeDtypeStruct((m, heads * dv), BF16),
        scratch_shapes=[pltpu.VMEM((nh, t, 2 * dv), BF16)],
        compiler_params=_params(("parallel", "parallel", "arbitrary")),
        name="mla_attention",
    )(small_scores, q, k, v)


def _mla_mix(batch, x, positions, norm_g, w_down, q_a_norm, kv_a_norm, w_uq, w_ukv,
             q_norm, k_norm, w_o):
    m, d = x.shape
    q_lora = q_a_norm.shape[0]
    kv_lora = kv_a_norm.shape[0]
    heads, qk_dim = w_uq.shape[1], w_uq.shape[2]
    rope = w_down.shape[1] - q_lora - kv_lora
    nope = qk_dim - rope
    dv = w_ukv.shape[2] - nope
    half = rope // 2
    assert nope == LANE and dv == LANE and rope <= LANE and q_lora == kv_lora and q_lora % LANE == 0
    scale = 1.0 / math.sqrt(qk_dim)

    perm = jnp.concatenate([jnp.arange(half, rope), jnp.arange(0, half)])
    lane_pad = lambda a: jnp.pad(a, [(0, 0)] * (a.ndim - 1) + [(0, LANE - a.shape[-1])])

    inv_freq = ROPE_THETA ** (-jnp.arange(0, rope, 2, dtype=F32) / rope)
    ang = positions.reshape(m, 1).astype(F32) * inv_freq
    cos, sin = jnp.cos(ang), jnp.sin(ang)
    cos_t = lane_pad(jnp.concatenate([cos, cos], axis=-1))
    sin_t = lane_pad(jnp.concatenate([-sin, sin], axis=-1))

    w_kpe = w_down[:, q_lora + kv_lora:]
    w_down_x = jnp.concatenate(
        [w_down[:, :q_lora + kv_lora], lane_pad(w_kpe), lane_pad(w_kpe[:, perm])], axis=1).astype(BF16)
    c = _simple_mm("mla_down", [_rows(x), _vec(norm_g)], _rms, w_down_x, [], lambda acc: (acc,),
                   tn_target=w_down_x.shape[1])
    qb = q_lora // q_lora
    tm = _pick_tile(m, 512, 8)

    def norm_rope(body, pe, rot, g_body, g_pe, g_rot, cos_b, sin_b, extra_scale):
        ss = jnp.sum(body * body, axis=-1, keepdims=True) + jnp.sum(pe * pe, axis=-1, keepdims=True)
        s = lax.rsqrt(ss / qk_dim + RMS_EPS) * extra_scale
        return jnp.concatenate([body * s * g_body, (pe * g_pe * cos_b + rot * g_rot * sin_b) * s], axis=-1)

    w_q_pe = w_uq[:, :, nope:]
    w_q_x = jnp.concatenate([w_uq[:, :, :nope], lane_pad(w_q_pe), lane_pad(w_q_pe[:, :, perm])],
                            axis=-1).reshape(q_lora, heads * 3 * LANE).astype(BF16)
    gq = jnp.concatenate([q_norm[:nope], lane_pad(q_norm[nope:]), lane_pad(q_norm[nope:][perm])]).reshape(1, -1)

    hb = _pick_tile(heads, MLA_HEADS_PER_STEP, 1)

    def q_epi(acc, g_t, cos_b, sin_b):
        per_head = []
        for hh in range(hb):
            a = acc[:, hh * 3 * LANE:(hh + 1) * 3 * LANE]
            per_head.append(norm_rope(a[:, :LANE], a[:, LANE:2 * LANE], a[:, 2 * LANE:], g_t[:, :LANE],
                                      g_t[:, LANE:2 * LANE], g_t[:, 2 * LANE:], cos_b, sin_b, scale))
        return (jnp.stack(per_head, axis=0),)

    q = _mm("mla_q", m,
            [(c, _row_spec(tm, q_lora, 0)), (q_a_norm.reshape(1, -1), _vec_spec(q_lora))], _rms, w_q_x,
            [(gq, _vec_spec(3 * LANE)), (cos_t, _row_spec(tm, LANE)), (sin_t, _row_spec(tm, LANE))], q_epi,
            [(jax.ShapeDtypeStruct((heads, m, 2 * LANE), BF16),
              pl.BlockSpec((hb, tm, 2 * LANE), lambda i, j: (j, i, 0)))],
            tm=tm, tn=hb * 3 * LANE)[0]

    w_kv_x = w_ukv.reshape(kv_lora, heads * (nope + dv)).astype(BF16)
    gk = jnp.concatenate([k_norm[:nope], lane_pad(k_norm[nope:]), lane_pad(k_norm[nope:][perm])]).reshape(1, -1)
    pe_block = (q_lora + kv_lora) // LANE

    def kv_epi(acc, g_t, pe, rot, cos_b, sin_b):
        k_out, v_out = [], []
        for hh in range(hb):
            a = acc[:, hh * (nope + dv):(hh + 1) * (nope + dv)]
            k_out.append(norm_rope(a[:, :nope], pe, rot, g_t[:, :LANE], g_t[:, LANE:2 * LANE],
                                   g_t[:, 2 * LANE:], cos_b, sin_b, 1.0))
            v_out.append(a[:, nope:])
        return jnp.stack(k_out, axis=0), jnp.stack(v_out, axis=0)

    k, v = _mm("mla_kv", m,
               [(c, _row_spec(tm, kv_lora, qb)), (kv_a_norm.reshape(1, -1), _vec_spec(kv_lora))], _rms, w_kv_x,
               [(gk, _vec_spec(3 * LANE)), (c, _row_spec(tm, LANE, pe_block)),
                (c, _row_spec(tm, LANE, pe_block + 1)), (cos_t, _row_spec(tm, LANE)),
                (sin_t, _row_spec(tm, LANE))], kv_epi,
               [(jax.ShapeDtypeStruct((heads, m, 2 * LANE), BF16),
                 pl.BlockSpec((hb, tm, 2 * LANE), lambda i, j: (j, i, 0))),
                (jax.ShapeDtypeStruct((heads, m, dv), BF16),
                 pl.BlockSpec((hb, tm, dv), lambda i, j: (j, i, 0)))],
               tm=tm, tn=hb * (nope + dv))

    score_bound = scale * qk_dim * jnp.max(jnp.abs(q_norm)) * jnp.max(jnp.abs(k_norm))
    small_scores = (score_bound <= MAX_PLAIN_SCORE).astype(jnp.int32).reshape(1)
    o = _attention(batch, q, k, v, small_scores)
    return _simple_mm("mla_out", [_rows(o)], lambda t: t, w_o.reshape(heads * dv, d).astype(BF16),
                      [_epi_tile(x)], lambda acc, x_t: (x_t + acc,), tn_target=d)


def kernel(x, positions, ffn_norm, ffn_w13, ffn_w2, mix_norm, rwkv_mu, rwkv_w_rkv, rwkv_w0, rwkv_w1, rwkv_w2, rwkv_a0, rwkv_a1, rwkv_a2, rwkv_g1, rwkv_g2, rwkv_k_k, rwkv_k_a, rwkv_r_k, rwkv_ln_w, rwkv_ln_b, rwkv_w_o, mla_w_down, mla_q_a_norm, mla_kv_a_norm, mla_w_uq, mla_w_ukv, mla_q_norm, mla_k_norm, mla_w_o):
    batch, seq, d = x.shape
    depth = ffn_norm.shape[0]
    n_mixers = 2
    xf = x.reshape(batch * seq, d)
    d_ff = ffn_w2.shape[2]
    w13p, w2p = _ffn_weights(ffn_w13.reshape((2 * depth,) + ffn_w13.shape[2:]),
                             ffn_w2.reshape((2 * depth,) + ffn_w2.shape[2:]))
    for i in range(depth):
        j = i // n_mixers
        xf = _ffn(xf, ffn_norm[i, 0], w13p, w2p, 2 * i, d_ff)
        if i % n_mixers == 0:
            xf = _rwkv_mix(batch, xf, mix_norm[i], rwkv_mu[j], rwkv_w_rkv[j], rwkv_w0[j], rwkv_w1[j],
                           rwkv_w2[j], rwkv_a0[j], rwkv_a1[j], rwkv_a2[j], rwkv_g1[j], rwkv_g2[j],
                           rwkv_k_k[j], rwkv_k_a[j], rwkv_r_k[j], rwkv_ln_w[j], rwkv_ln_b[j],
                           rwkv_w_o[j])
        else:
            xf = _mla_mix(batch, xf, positions, mix_norm[i], mla_w_down[j], mla_q_a_norm[j],
                          mla_kv_a_norm[j], mla_w_uq[j], mla_w_ukv[j], mla_q_norm[j], mla_k_norm[j],
                          mla_w_o[j])
        xf = _ffn(xf, ffn_norm[i, 1], w13p, w2p, 2 * i + 1, d_ff)
    return xf.reshape(batch, seq, d)
```

```python
import functools
import math

import jax
import jax.numpy as jnp
from jax import lax
from jax.experimental import pallas as pl
from jax.experimental.pallas import tpu as pltpu

F32 = jnp.float32
BF16 = jnp.bfloat16

RMS_EPS = 1e-6
GN_EPS = 64e-5
ROPE_THETA = 10000.0
ATTN_CHUNK = 64
RWKV_HEAD = 64
SCAN_CHUNK = 64
HEADS_PER_GROUP = 4
MAX_PLAIN_SCORE = 30.0
ATTN_Q_TILE = 512
ATTN_HEADS_PER_STEP = 4
FFN_ROWS = 1024
FFN_TILE = 512
MLA_HEADS_PER_STEP = 16
SCAN_CHUNKS_PER_STEP = 4
SCAN_BATCH_ROWS = 1
SCAN_STREAMS = 8
LANE = 128
SUBLANE = 8
VMEM_LIMIT_BYTES = 56 * 1024 * 1024


def _params(semantics):
    return pltpu.CompilerParams(dimension_semantics=semantics,
                                vmem_limit_bytes=VMEM_LIMIT_BYTES)


def _pick_tile(n, target, quantum):
    if n <= target:
        return n
    t = (target // quantum) * quantum
    while t > quantum and n % t:
        t -= quantum
    assert n % t == 0, (n, target, quantum)
    return t


def _round_robin(gens):
    live = list(gens)
    while live:
        live = [g for g in live if next(g, True) is None]


def _rms(x, g):
    return x * lax.rsqrt(jnp.mean(x * x, axis=-1, keepdims=True) + RMS_EPS) * g


def _ffn_body(nj, x_ref, g_ref, wg_ref, wu_ref, w2_ref, o_ref, xn_ref):
    j = pl.program_id(1)

    def tile_sum():
        xn = xn_ref[...]
        gate = jnp.dot(xn, wg_ref[...], preferred_element_type=F32)
        up = jnp.dot(xn, wu_ref[...], preferred_element_type=F32)
        act = (gate * jax.nn.sigmoid(gate) * up).astype(BF16)
        return jnp.dot(act, w2_ref[...], preferred_element_type=F32)

    def normalize():
        xn_ref[...] = _rms(x_ref[...], g_ref[...]).astype(BF16)

    if nj == 1:
        normalize()
        o_ref[...] = x_ref[...] + 0.5 * tile_sum()
    else:
        @pl.when(j == 0)
        def _():
            normalize()
            o_ref[...] = tile_sum()

        if nj > 2:
            @pl.when((j > 0) & (j < nj - 1))
            def _():
                o_ref[...] += tile_sum()

        @pl.when(j == nj - 1)
        def _():
            o_ref[...] = x_ref[...] + 0.5 * (o_ref[...] + tile_sum())


def _ffn_tile(f):
    return min(FFN_TILE, -(-f // LANE) * LANE)


def _tile_cast_body(tiles, x_ref, o_ref):
    tw = o_ref.shape[-1]
    for t, (c0, width) in enumerate(tiles):
        o_ref[t, :, :width] = x_ref[:, c0:c0 + width].astype(BF16)
        if width < tw:
            o_ref[t, :, width:] = jnp.zeros((o_ref.shape[1], tw - width), BF16)


def _tile_cast(name, w, tiles, tw):
    n, rows, cols = w.shape
    tr = _pick_tile(rows, 128, SUBLANE)
    return pl.pallas_call(
        functools.partial(_tile_cast_body, tuple(tiles)),
        grid=(n, rows // tr),
        in_specs=[pl.BlockSpec((None, tr, cols), lambda l, i: (l, i, 0))],
        out_specs=pl.BlockSpec((None, len(tiles), tr, tw), lambda l, i: (l, 0, i, 0)),
        out_shape=jax.ShapeDtypeStruct((n, len(tiles), rows, tw), BF16),
        compiler_params=_params(("parallel", "parallel")),
        name=name,
    )(w)


def _cast_w2_body(f, x_ref, o_ref):
    tr = o_ref.shape[0]
    row = pl.program_id(1) * tr + lax.broadcasted_iota(jnp.int32, o_ref.shape, 0)
    o_ref[...] = jnp.where(row < f, x_ref[...], 0.0).astype(BF16)


def _ffn_weights(w13, w2):
    n, d, f2 = w13.shape
    f = f2 // 2
    assert f % LANE == 0
    tf = _ffn_tile(f)
    fp = -(-f // tf) * tf
    half = [(t * tf, min(tf, f - t * tf)) for t in range(fp // tf)]
    w13p = _tile_cast("cast_w13", w13, half + [(f + c0, width) for c0, width in half], tf)
    w2p = pl.pallas_call(
        functools.partial(_cast_w2_body, f),
        grid=(n, fp // tf),
        in_specs=[pl.BlockSpec((None, tf, d), lambda l, i: (l, i, 0))],
        out_specs=pl.BlockSpec((None, tf, d), lambda l, i: (l, i, 0)),
        out_shape=jax.ShapeDtypeStruct((n, fp, d), BF16),
        compiler_params=_params(("parallel", "parallel")),
        name="cast_w2",
    )(w2)
    return w13p, w2p


def _ffn(x, g, w13p, w2p, layer, f):
    m, d = x.shape
    fp = w2p.shape[1]
    tf = _ffn_tile(f)
    nj = fp // tf
    tm = _pick_tile(m, FFN_ROWS, 8)
    return pl.pallas_call(
        functools.partial(_ffn_body, nj),
        grid=(m // tm, nj),
        in_specs=[
            pl.BlockSpec((tm, d), lambda i, j: (i, 0)),
            pl.BlockSpec((1, d), lambda i, j: (0, 0)),
            pl.BlockSpec((None, None, d, tf), lambda i, j: (layer, j, 0, 0)),
            pl.BlockSpec((None, None, d, tf), lambda i, j: (layer, j + nj, 0, 0)),
            pl.BlockSpec((None, tf, d), lambda i, j: (layer, j, 0)),
        ],
        out_specs=pl.BlockSpec((tm, d), lambda i, j: (i, 0)),
        out_shape=jax.ShapeDtypeStruct((m, d), F32),
        scratch_shapes=[pltpu.VMEM((tm, d), BF16)],
        compiler_params=_params(("parallel", "arbitrary")),
        name="ffn",
    )(x, g.reshape(1, d), w13p, w13p, w2p)


def _mm_body(single_col_tile, n_lhs, n_epi, n_out, lhs_fn, epi_fn, *refs):
    lhs_refs = refs[:n_lhs]
    w_ref = refs[n_lhs]
    epi_refs = refs[n_lhs + 1:n_lhs + 1 + n_epi]
    out_refs = refs[n_lhs + 1 + n_epi:n_lhs + 1 + n_epi + n_out]

    if single_col_tile:
        lhs = lhs_fn(*[r[...] for r in lhs_refs]).astype(BF16)
    else:
        lhs_scratch = refs[-1]

        @pl.when(pl.program_id(1) == 0)
        def _():
            lhs_scratch[...] = lhs_fn(*[r[...] for r in lhs_refs]).astype(BF16)

        lhs = lhs_scratch[...]
    acc = jnp.dot(lhs, w_ref[...], preferred_element_type=F32)
    outs = epi_fn(acc, *[r[...] for r in epi_refs])
    for o_ref, o in zip(out_refs, outs):
        o_ref[...] = o.astype(o_ref.dtype)


def _mm(name, m, lhs, lhs_fn, w, epi, epi_fn, outs, *, tm, tn):
    k, n = w.shape
    single = n == tn
    body = functools.partial(_mm_body, single, len(lhs), len(epi), len(outs), lhs_fn, epi_fn)
    return pl.pallas_call(
        body,
        grid=(m // tm, n // tn),
        in_specs=[s for _, s in lhs] + [pl.BlockSpec((k, tn), lambda i, j: (0, j))]
        + [s for _, s in epi],
        out_specs=[s for _, s in outs],
        out_shape=[s for s, _ in outs],
        scratch_shapes=[] if single else [pltpu.VMEM((tm, k), BF16)],
        compiler_params=_params(("parallel", "arbitrary")),
        name=name,
    )(*[a for a, _ in lhs], w, *[a for a, _ in epi])


def _row_spec(tm, width, col_block=0):
    return pl.BlockSpec((tm, width), lambda i, j: (i, col_block))


def _tile_spec(tm, tn):
    return pl.BlockSpec((tm, tn), lambda i, j: (i, j))


def _vec_spec(width):
    return pl.BlockSpec((1, width), lambda i, j: (0, 0))


def _colvec_spec(tn):
    return pl.BlockSpec((1, tn), lambda i, j: (0, j))


def _simple_mm(name, lhs, lhs_fn, w, epi, epi_fn, out_dtype=F32, tm_target=512, tn_target=1024):
    m = lhs[0][0].shape[0]
    n = w.shape[1]
    tm = _pick_tile(m, tm_target, 8)
    tn = _pick_tile(n, tn_target, LANE)
    lhs = [(a, spec(tm)) for a, spec in lhs]
    epi = [(a, spec(tm, tn)) for a, spec in epi]
    out = (jax.ShapeDtypeStruct((m, n), out_dtype), _tile_spec(tm, tn))
    return _mm(name, m, lhs, lhs_fn, w, epi, epi_fn, [out], tm=tm, tn=tn)[0]


def _rows(a):
    return a, lambda tm: _row_spec(tm, a.shape[1])


def _vec(a):
    a = a.reshape(1, -1)
    return a, lambda tm: _vec_spec(a.shape[1])


def _epi_tile(a):
    return a, lambda tm, tn: _tile_spec(tm, tn)


def _epi_colvec(a):
    return a.reshape(1, -1), lambda tm, tn: _colvec_spec(tn)


def _rwkv_proj_body(seq_tiles, ranks, x_ref, xprev_ref, g_ref, mu_ref, wr_ref, wk_ref, wv_ref, l1_ref,
                    w2_ref, a2_ref, g2_ref, w0_ref, a0_ref, rkv_ref, ld_ref, ag_ref,
                    lerp_scr, tw_scr, ta_scr, tg_scr):
    i = pl.program_id(0)
    j = pl.program_id(1)
    rw, ra = ranks
    w_refs = (wr_ref, wk_ref, wv_ref)

    def emit_rkv(p):
        rkv_ref[p] = jnp.dot(lerp_scr[p], w_refs[p][...], preferred_element_type=F32).astype(BF16)

    def emit_lora_outputs():
        z = -(w0_ref[...] + jnp.dot(tw_scr[...], w2_ref[...], preferred_element_type=F32))
        softplus = jnp.maximum(z, 0.0) + jnp.log1p(jnp.exp(-jnp.abs(z)))
        ld_ref[...] = -jnp.exp(-softplus - 0.5)
        ag_ref[0] = jax.nn.sigmoid(
            a0_ref[...] + jnp.dot(ta_scr[...], a2_ref[...], preferred_element_type=F32)).astype(BF16)
        ag_ref[1] = jnp.dot(tg_scr[...], g2_ref[...], preferred_element_type=F32).astype(BF16)

    @pl.when(j == 0)
    def _():
        h = _rms(x_ref[...], g_ref[...])
        prev = _rms(xprev_ref[...], g_ref[...])[SUBLANE - 1:SUBLANE, :]
        prev = jnp.where(i % seq_tiles == 0, 0.0, prev)
        row = lax.broadcasted_iota(jnp.int32, h.shape, 0)
        dx = jnp.where(row == 0, prev, pltpu.roll(h, 1, 0)) - h
        def lerp(idx):
            return (h + dx * mu_ref[idx:idx + 1, :]).astype(BF16)

        for p in range(3):
            lerp_scr[p] = lerp(p)
            emit_rkv(p)
        tw_scr[...] = jnp.tanh(jnp.dot(lerp(3), l1_ref[:, :rw],
                                       preferred_element_type=F32)).astype(BF16)
        ta_scr[...] = jnp.dot(lerp(4), l1_ref[:, rw:rw + ra],
                              preferred_element_type=F32).astype(BF16)
        tg_scr[...] = jax.nn.sigmoid(jnp.dot(lerp(5), l1_ref[:, rw + ra:],
                                             preferred_element_type=F32)).astype(BF16)
        emit_lora_outputs()

    @pl.when(j > 0)
    def _():
        for p in range(3):
            emit_rkv(p)
        emit_lora_outputs()


def _pad_lora(wa, wb):
    rank = wa.shape[1]
    rp = -(-rank // LANE) * LANE
    return (jnp.pad(wa, ((0, 0), (0, rp - rank))).astype(BF16),
            jnp.pad(wb, ((0, rp - rank), (0, 0))).astype(BF16))


def _rwkv_proj(batch, x, norm_g, mu, w_rkv, w0, w1, w2, a0, a1, a2, g1, g2):
    m, d = x.shape
    tm = _pick_tile(m // batch, 512, SUBLANE)
    tn = _pick_tile(d, 512, LANE)
    w1p, w2p = _pad_lora(w1, w2)
    a1p, a2p = _pad_lora(a1, a2)
    g1p, g2p = _pad_lora(g1, g2)
    rw, ra, rg = w1p.shape[1], a1p.shape[1], g1p.shape[1]
    l1 = jnp.concatenate([w1p, a1p, g1p], axis=1)
    w_bf = _tile_cast("cast_w_rkv", w_rkv, [(t * tn, tn) for t in range(d // tn)], tn)
    mu_ord = mu[jnp.array([0, 2, 3, 1, 4, 5])]
    col = lambda i, j: (0, j)

    def w_plane(p):
        return pl.BlockSpec((None, None, d, tn), lambda i, j: (p, j, 0, 0))

    def planes(n):
        return pl.BlockSpec((n, tm, tn), lambda i, j: (0, i, j))

    body = functools.partial(_rwkv_proj_body, (m // batch) // tm, (rw, ra))
    return pl.pallas_call(
        body,
        grid=(m // tm, d // tn),
        in_specs=[
            pl.BlockSpec((tm, d), lambda i, j: (i, 0)),
            pl.BlockSpec((SUBLANE, d), lambda i, j: (jnp.maximum(i * (tm // SUBLANE) - 1, 0), 0)),
            pl.BlockSpec((1, d), lambda i, j: (0, 0)),
            pl.BlockSpec((6, d), lambda i, j: (0, 0)),
            w_plane(0), w_plane(1), w_plane(2),
            pl.BlockSpec((d, rw + ra + rg), lambda i, j: (0, 0)),
            pl.BlockSpec((rw, tn), col),
            pl.BlockSpec((ra, tn), col),
            pl.BlockSpec((rg, tn), col),
            pl.BlockSpec((1, tn), col),
            pl.BlockSpec((1, tn), col),
        ],
        out_specs=[planes(3), pl.BlockSpec((tm, tn), lambda i, j: (i, j)), planes(2)],
        out_shape=[jax.ShapeDtypeStruct((3, m, d), BF16), jax.ShapeDtypeStruct((m, d), F32),
                   jax.ShapeDtypeStruct((2, m, d), BF16)],
        scratch_shapes=[pltpu.VMEM((3, tm, d), BF16), pltpu.VMEM((tm, rw), BF16),
                        pltpu.VMEM((tm, ra), BF16), pltpu.VMEM((tm, rg), BF16)],
        compiler_params=_params(("parallel", "arbitrary")),
        name="rwkv_proj",
    )(x, x, norm_g.reshape(1, d), mu_ord, w_bf, w_bf, w_bf, l1, w2p, a2p, g2p,
      w0.reshape(1, d), a0.reshape(1, d))


def _bf(x):
    return x.astype(BF16)


def _dot(a, b):
    return jnp.dot(_bf(a), _bf(b), preferred_element_type=F32)


def _dot_nt(a, b):
    return lax.dot_general(_bf(a), _bf(b), (((1,), (1,)), ((), ())), preferred_element_type=F32)


def _dot_tn(a, b):
    return lax.dot_general(_bf(a), _bf(b), (((0,), (0,)), ((), ())), preferred_element_type=F32)


def _scan_body(n_streams, nb, cps, r_ref, ld_ref, k_ref, v_ref, al_ref, g_ref, kk_w_ref, ka_w_ref,
               rk_w_ref, lnw_ref, lnb_ref, o_ref, s_ref):
    c = SCAN_CHUNK
    w = HEADS_PER_GROUP * RWKV_HEAD
    wc = HEADS_PER_GROUP * c

    @pl.when(pl.program_id(2) == 0)
    def _():
        s_ref[...] = jnp.zeros_like(s_ref)

    tri = jnp.where(lax.broadcasted_iota(jnp.int32, (c, c), 0)
                    >= lax.broadcasted_iota(jnp.int32, (c, c), 1), 1.0, 0.0).astype(BF16)
    bd_r = lax.broadcasted_iota(jnp.int32, (w, w), 0) // RWKV_HEAD
    bd_c = lax.broadcasted_iota(jnp.int32, (w, w), 1) // RWKV_HEAD
    head_mask = bd_r == bd_c
    head_ones = jnp.where(head_mask, 1.0, 0.0).astype(BF16)
    t_idx = lax.broadcasted_iota(jnp.int32, (c, wc), 0)
    s_idx = lax.broadcasted_iota(jnp.int32, (c, wc), 1) % c
    strict = s_idx < t_idx
    incl = s_idx <= t_idx
    eye = jnp.where(s_idx == t_idx, 1.0, 0.0)
    base_level = 3
    base_blocks = (t_idx >> base_level) == (s_idx >> base_level)
    join = {lv: (((t_idx >> lv) & 1) == 1) & ((s_idx >> lv) == (t_idx >> lv) - 1)
            for lv in range(base_level, c.bit_length() - 1)}

    def head_sum(x):
        return jnp.dot(_bf(x), head_ones, preferred_element_type=F32)

    def block_diag(x):
        xb = _bf(x)
        return jnp.where(head_mask, jnp.concatenate([xb] * HEADS_PER_GROUP, axis=0),
                         jnp.zeros((), BF16))

    def stream(bi, si, cj):
        st = bi * n_streams + si
        ln = pl.ds(si * w, w)
        rows = pl.ds(cj * c, c)
        r = r_ref[bi, rows, ln].astype(F32)
        ld = ld_ref[bi, rows, ln]
        k = k_ref[bi, rows, ln].astype(F32)
        v = v_ref[bi, rows, ln].astype(F32)
        al = al_ref[bi, rows, ln].astype(F32)

        ld_hi = _bf(ld)
        rest = ld - ld_hi.astype(F32)
        ld_mid = _bf(rest)
        ld_lo = _bf(rest - ld_mid.astype(F32))
        cum = (jnp.dot(tri, ld_hi, preferred_element_type=F32)
               + jnp.dot(tri, ld_mid, preferred_element_type=F32)
               + jnp.dot(tri, ld_lo, preferred_element_type=F32))
        kk = k * kk_w_ref[:, ln]
        ss = head_sum(kk * kk)
        yield
        p_incl = jnp.exp(cum)
        p_prev = jnp.exp(cum - ld)
        p_inv = jnp.exp(-cum)
        p_last = p_incl[c - 1:c, :]
        kk = kk / jnp.maximum(jnp.sqrt(ss), 1e-12)
        kmod = k * (1.0 + (al - 1.0) * ka_w_ref[:, ln])

        a_t = -kk * p_prev
        b_t = kk * al * p_inv
        k_t = kmod * p_inv
        r_t = r * p_incl

        ar = jnp.concatenate([a_t, r_t], axis=0)
        m_b = _dot_nt(ar, block_diag(b_t))
        m_k = _dot_nt(ar, block_diag(k_t))
        s0 = s_ref[st]
        w1 = _dot_nt(ar, s0)
        yield
        a_ab = jnp.where(strict, m_b[:c], 0.0)
        a_rb = jnp.where(incl, m_b[c:], 0.0)
        a_ak = jnp.where(strict, m_k[:c], 0.0)
        a_rk = jnp.where(incl, m_k[c:], 0.0)
        w2 = _dot(jnp.concatenate([a_ak, a_rk], axis=0), block_diag(v))

        pw = jnp.where(base_blocks, a_ab, 0.0)
        t_inv = eye
        for step in range(base_level):
            if step < base_level - 1:
                z = _dot(jnp.concatenate([pw, t_inv], axis=0), block_diag(pw))
                yield
                pw, t_inv = z[:c], t_inv + z[c:]
            else:
                z = _dot(t_inv, block_diag(pw))
                yield
                t_inv = t_inv + z
        n_bd = block_diag(a_ab)
        level = base_level
        while (1 << level) < c:
            tn = _dot(t_inv, n_bd)
            yield
            x = _dot(tn, block_diag(t_inv))
            yield
            t_inv = t_inv + jnp.where(join[level], x, 0.0)
            level += 1

        u = _dot(t_inv, block_diag(w1[:c] + w2[:c]))
        yield
        y = w1[c:] + w2[c:] + _dot(a_rb, block_diag(u))

        uv = jnp.concatenate([u, v], axis=0)
        bk = jnp.concatenate([b_t, k_t], axis=0) * p_last
        s_new = _dot_tn(uv, bk)
        yield
        s_ref[st] = jnp.where(head_mask, s0 * p_last + s_new, 0.0)

        inv_n = 1.0 / RWKV_HEAD
        sums = head_sum(jnp.concatenate([y, r * kmod * rk_w_ref[:, ln]], axis=0))
        yield
        dlt = y - sums[:c] * inv_n
        bsum = sums[c:]
        var = head_sum(dlt * dlt) * inv_n
        yield
        yn = dlt * lax.rsqrt(var + GN_EPS) * lnw_ref[:, ln] + lnb_ref[:, ln]
        bonus = bsum * v
        o_ref[bi, rows, ln] = ((yn + bonus) * g_ref[bi, rows, ln]).astype(o_ref.dtype)

    def chunks(bi, si):
        for cj in range(cps):
            yield from stream(bi, si, cj)

    _round_robin([chunks(bi, si) for bi in range(nb) for si in range(n_streams)])


def _rwkv_scan(batch, rkv, ld, ag, k_k, k_a, r_k, ln_w, ln_b):
    _, m, d = rkv.shape
    t = m // batch
    c = SCAN_CHUNK
    w = HEADS_PER_GROUP * RWKV_HEAD
    assert d % w == 0 and t % c == 0 and c == RWKV_HEAD
    n_streams = _pick_tile(d // w, SCAN_STREAMS, 1)
    wb = n_streams * w
    cps = _pick_tile(t // c, SCAN_CHUNKS_PER_STEP, 1)
    nc = t // (c * cps)
    nb = _pick_tile(batch, SCAN_BATCH_ROWS, 1)
    tok = pl.BlockSpec((nb, cps * c, wb), lambda b, gi, ci: (b, ci, gi))
    par = pl.BlockSpec((1, wb), lambda b, gi, ci: (0, gi))

    def plane(p):
        return pl.BlockSpec((None, nb, cps * c, wb), lambda b, gi, ci: (p, b, ci, gi))

    rkv4 = rkv.reshape(3, batch, t, d)
    ag4 = ag.reshape(2, batch, t, d)
    return pl.pallas_call(
        functools.partial(_scan_body, n_streams, nb, cps),
        grid=(batch // nb, d // wb, nc),
        in_specs=[plane(0), tok, plane(1), plane(2), plane(0), plane(1)] + [par] * 5,
        out_specs=tok,
        out_shape=jax.ShapeDtypeStruct((batch, t, d), BF16),
        scratch_shapes=[pltpu.VMEM((nb * n_streams, w, w), F32)],
        compiler_params=_params(("parallel", "parallel", "arbitrary")),
        name="rwkv_scan",
    )(rkv4, ld.reshape(batch, t, d), rkv4, rkv4, ag4, ag4,
      *[p.reshape(1, d) for p in (k_k, k_a, r_k, ln_w, ln_b)]).reshape(m, d)


def _rwkv_mix(batch, x, norm_g, mu, w_rkv, w0, w1, w2, a0, a1, a2, g1, g2,
              k_k, k_a, r_k, ln_w, ln_b, w_o):
    rkv, ld, ag = _rwkv_proj(batch, x, norm_g, mu, w_rkv, w0, w1, w2, a0, a1, a2, g1, g2)
    yg = _rwkv_scan(batch, rkv, ld, ag, k_k, k_a, r_k.reshape(-1), ln_w, ln_b)
    return _simple_mm("rwkv_out", [_rows(yg)], lambda t: t, w_o.astype(BF16), [_epi_tile(x)],
                      lambda acc, x_t: (x_t + acc,), tn_target=x.shape[1])


def _attn_body(tq, tk, nh, nq, small_ref, q_ref, k_ref, v_ref, o_ref, v1_ref):
    qi = pl.program_id(2)
    dv = v_ref.shape[-1]

    def key_rows(j):
        start = j * tk if isinstance(j, int) else pl.multiple_of(j * tk, tk)
        return pl.ds(start, tk)

    def scores(h, j, masked):
        kb = k_ref[h, key_rows(j), :]
        s = lax.dot_general(q_ref[h], kb, (((1,), (1,)), ((), ())), preferred_element_type=F32)
        if masked:
            q_chunk = (qi * tq + lax.broadcasted_iota(jnp.int32, (tq, tk), 0)) // ATTN_CHUNK
            k_chunk = (j * tk + lax.broadcasted_iota(jnp.int32, (tq, tk), 1)) // ATTN_CHUNK
            s = jnp.where(k_chunk <= q_chunk, s, jnp.finfo(F32).min)
        return s

    def values(h, j):
        return v_ref[h, key_rows(j), :]

    def all_heads(make_gen):
        outs = [[] for _ in range(nh)]
        _round_robin([make_gen(h, outs[h]) for h in range(nh)])
        return tuple(o[0] for o in outs)

    def sweep(step, init, diag_step=None):
        carries = lax.fori_loop(
            0, qi, lambda j, cr: all_heads(lambda h, out: step(h, j, cr[h], False, out)), init)
        if diag_step is not None:
            return all_heads(lambda h, out: diag_step(h, carries[h], out))
        return all_heads(lambda h, out: step(h, qi, carries[h], True, out))

    def online_step(h, j, carry, masked, out):
        m_i, l_i, acc = carry
        s = scores(h, j, masked)
        yield
        m_new = jnp.maximum(m_i, jnp.max(s, axis=-1, keepdims=True))
        p = jnp.exp(s - m_new)
        alpha = jnp.exp(m_i - m_new)
        l_new = alpha * l_i + jnp.sum(p, axis=-1, keepdims=True)
        pv = jnp.dot(p.astype(BF16), values(h, j), preferred_element_type=F32)
        yield
        out.append((m_new, l_new, alpha * acc + pv))

    def plain_step(h, j, acc, masked, out):
        s = scores(h, j, masked)
        yield
        p = jnp.exp(s).astype(BF16)
        v1 = v1_ref[h, key_rows(j), :]
        pv = jnp.dot(p, v1, preferred_element_type=F32)
        yield
        out.append(acc + pv)

    def plain_diag(h, acc, out):
        half = tk // 2
        base = pl.multiple_of(qi * tk, tk)
        visible = (lax.broadcasted_iota(jnp.int32, (tq, half), 1) // ATTN_CHUNK
                   <= lax.broadcasted_iota(jnp.int32, (tq, half), 0) // ATTN_CHUNK)

        def weights(q_rows, key0, mask):
            kb = k_ref[h, pl.ds(key0, half), :]
            s = lax.dot_general(q_rows, kb, (((1,), (1,)), ((), ())), preferred_element_type=F32)
            return jnp.where(mask, jnp.exp(s), 0.0).astype(BF16)

        p_a = weights(q_ref[h], base, visible)
        yield
        pv_a = jnp.dot(p_a, v1_ref[h, pl.ds(base, half), :], preferred_element_type=F32)
        p_b = weights(q_ref[h, half:, :], base + half, visible[:half])
        yield
        pv_b = jnp.dot(p_b, v1_ref[h, pl.ds(base + half, half), :], preferred_element_type=F32)
        yield
        out.append(acc + jnp.concatenate([pv_a[:half], pv_a[half:] + pv_b], axis=0))

    @pl.when(small_ref[0] == 1)
    def _():
        @pl.when(qi == 0)
        def _():
            v1_ref[:, :, :dv] = v_ref[...]
            v1_ref[:, :, dv:] = jnp.ones(v_ref.shape, BF16)

        def unrolled(n_full):
            accs = tuple(jnp.zeros((tq, 2 * dv), F32) for _ in range(nh))
            for j in range(n_full):
                accs = all_heads(lambda h, out, j=j: plain_step(h, j, accs[h], False, out))
            accs = all_heads(lambda h, out: plain_diag(h, accs[h], out))
            for h, acc in enumerate(accs):
                o_ref[:, h * dv:(h + 1) * dv] = (acc[:, :dv] / acc[:, dv:]).astype(o_ref.dtype)

        for n_full in range(nq):
            pl.when(qi == n_full)(functools.partial(unrolled, n_full))

    @pl.when(small_ref[0] != 1)
    def _():
        init = tuple((jnp.full((tq, 1), -jnp.inf, F32), jnp.zeros((tq, 1), F32),
                      jnp.zeros((tq, dv), F32)) for _ in range(nh))
        for h, (_, l_i, acc) in enumerate(sweep(online_step, init)):
            o_ref[:, h * dv:(h + 1) * dv] = (acc / l_i).astype(o_ref.dtype)


def _attention(batch, q, k, v, small_scores):
    heads, m, dqk = q.shape
    dv = v.shape[-1]
    t = m // batch
    tq = _pick_tile(t, ATTN_Q_TILE, 2 * ATTN_CHUNK)
    assert tq % (2 * ATTN_CHUNK) == 0
    nq = t // tq
    nh = _pick_tile(heads, ATTN_HEADS_PER_STEP, 1)
    return pl.pallas_call(
        functools.partial(_attn_body, tq, tq, nh, nq),
        grid=(batch, heads // nh, nq),
        in_specs=[
            pl.BlockSpec(memory_space=pltpu.SMEM),
            pl.BlockSpec((nh, tq, dqk), lambda b, h, i: (h, b * nq + i, 0)),
            pl.BlockSpec((nh, t, dqk), lambda b, h, i: (h, b, 0)),
            pl.BlockSpec((nh, t, dv), lambda b, h, i: (h, b, 0)),
        ],
        out_specs=pl.BlockSpec((tq, nh * dv), lambda b, h, i: (b * nq + i, h)),
        out_shape=jax.ShapeDtypeStruct((m, heads * dv), BF16),
        scratch_shapes=[pltpu.VMEM((nh, t, 2 * dv), BF16)],
        compiler_params=_params(("parallel", "parallel", "arbitrary")),
        name="mla_attention",
    )(small_scores, q, k, v)


def _mla_mix(batch, x, positions, norm_g, w_down, q_a_norm, kv_a_norm, w_uq, w_ukv,
             q_norm, k_norm, w_o):
    m, d = x.shape
    q_lora = q_a_norm.shape[0]
    kv_lora = kv_a_norm.shape[0]
    heads, qk_dim = w_uq.shape[1], w_uq.shape[2]
    rope = w_down.shape[1] - q_lora - kv_lora
    nope = qk_dim - rope
    dv = w_ukv.shape[2] - nope
    half = rope // 2
    assert nope == LANE and dv == LANE and rope <= LANE and q_lora == kv_lora and q_lora % LANE == 0
    scale = 1.0 / math.sqrt(qk_dim)

    perm = jnp.concatenate([jnp.arange(half, rope), jnp.arange(0, half)])
    lane_pad = lambda a: jnp.pad(a, [(0, 0)] * (a.ndim - 1) + [(0, LANE - a.shape[-1])])

    inv_freq = ROPE_THETA ** (-jnp.arange(0, rope, 2, dtype=F32) / rope)
    ang = positions.reshape(m, 1).astype(F32) * inv_freq
    cos, sin = jnp.cos(ang), jnp.sin(ang)
    cos_t = lane_pad(jnp.concatenate([cos, cos], axis=-1))
    sin_t = lane_pad(jnp.concatenate([-sin, sin], axis=-1))

    w_kpe = w_down[:, q_lora + kv_lora:]
    w_down_x = jnp.concatenate(
        [w_down[:, :q_lora + kv_lora], lane_pad(w_kpe), lane_pad(w_kpe[:, perm])], axis=1).astype(BF16)
    c = _simple_mm("mla_down", [_rows(x), _vec(norm_g)], _rms, w_down_x, [], lambda acc: (acc,),
                   tn_target=w_down_x.shape[1])
    qb = q_lora // q_lora
    tm = _pick_tile(m, 512, 8)

    def norm_rope(body, pe, rot, g_body, g_pe, g_rot, cos_b, sin_b, extra_scale):
        ss = jnp.sum(body * body, axis=-1, keepdims=True) + jnp.sum(pe * pe, axis=-1, keepdims=True)
        s = lax.rsqrt(ss / qk_dim + RMS_EPS) * extra_scale
        return jnp.concatenate([body * s * g_body, (pe * g_pe * cos_b + rot * g_rot * sin_b) * s], axis=-1)

    w_q_pe = w_uq[:, :, nope:]
    w_q_x = jnp.concatenate([w_uq[:, :, :nope], lane_pad(w_q_pe), lane_pad(w_q_pe[:, :, perm])],
                            axis=-1).reshape(q_lora, heads * 3 * LANE).astype(BF16)
    gq = jnp.concatenate([q_norm[:nope], lane_pad(q_norm[nope:]), lane_pad(q_norm[nope:][perm])]).reshape(1, -1)

    hb = _pick_tile(heads, MLA_HEADS_PER_STEP, 1)

    def q_epi(acc, g_t, cos_b, sin_b):
        per_head = []
        for hh in range(hb):
            a = acc[:, hh * 3 * LANE:(hh + 1) * 3 * LANE]
            per_head.append(norm_rope(a[:, :LANE], a[:, LANE:2 * LANE], a[:, 2 * LANE:], g_t[:, :LANE],
                                      g_t[:, LANE:2 * LANE], g_t[:, 2 * LANE:], cos_b, sin_b, scale))
        return (jnp.stack(per_head, axis=0),)

    q = _mm("mla_q", m,
            [(c, _row_spec(tm, q_lora, 0)), (q_a_norm.reshape(1, -1), _vec_spec(q_lora))], _rms, w_q_x,
            [(gq, _vec_spec(3 * LANE)), (cos_t, _row_spec(tm, LANE)), (sin_t, _row_spec(tm, LANE))], q_epi,
            [(jax.ShapeDtypeStruct((heads, m, 2 * LANE), BF16),
              pl.BlockSpec((hb, tm, 2 * LANE), lambda i, j: (j, i, 0)))],
            tm=tm, tn=hb * 3 * LANE)[0]

    w_kv_x = w_ukv.reshape(kv_lora, heads * (nope + dv)).astype(BF16)
    gk = jnp.concatenate([k_norm[:nope], lane_pad(k_norm[nope:]), lane_pad(k_norm[nope:][perm])]).reshape(1, -1)
    pe_block = (q_lora + kv_lora) // LANE

    def kv_epi(acc, g_t, pe, rot, cos_b, sin_b):
        k_out, v_out = [], []
        for hh in range(hb):
            a = acc[:, hh * (nope + dv):(hh + 1) * (nope + dv)]
            k_out.append(norm_rope(a[:, :nope], pe, rot, g_t[:, :LANE], g_t[:, LANE:2 * LANE],
                                   g_t[:, 2 * LANE:], cos_b, sin_b, 1.0))
            v_out.append(a[:, nope:])
        return jnp.stack(k_out, axis=0), jnp.stack(v_out, axis=0)

    k, v = _mm("mla_kv", m,
               [(c, _row_spec(tm, kv_lora, qb)), (kv_a_norm.reshape(1, -1), _vec_spec(kv_lora))], _rms, w_kv_x,
               [(gk, _vec_spec(3 * LANE)), (c, _row_spec(tm, LANE, pe_block)),
                (c, _row_spec(tm, LANE, pe_block + 1)), (cos_t, _row_spec(tm, LANE)),
                (sin_t, _row_spec(tm, LANE))], kv_epi,
               [(jax.ShapeDtypeStruct((heads, m, 2 * LANE), BF16),
                 pl.BlockSpec((hb, tm, 2 * LANE), lambda i, j: (j, i, 0))),
                (jax.ShapeDtypeStruct((heads, m, dv), BF16),
                 pl.BlockSpec((hb, tm, dv), lambda i, j: (j, i, 0)))],
               tm=tm, tn=hb * (nope + dv))

    score_bound = scale * qk_dim * jnp.max(jnp.abs(q_norm)) * jnp.max(jnp.abs(k_norm))
    small_scores = (score_bound <= MAX_PLAIN_SCORE).astype(jnp.int32).reshape(1)
    o = _attention(batch, q, k, v, small_scores)
    return _simple_mm("mla_out", [_rows(o)], lambda t: t, w_o.reshape(heads * dv, d).astype(BF16),
                      [_epi_tile(x)], lambda acc, x_t: (x_t + acc,), tn_target=d)


def kernel(x, positions, ffn_norm, ffn_w13, ffn_w2, mix_norm, rwkv_mu, rwkv_w_rkv, rwkv_w0, rwkv_w1, rwkv_w2, rwkv_a0, rwkv_a1, rwkv_a2, rwkv_g1, rwkv_g2, rwkv_k_k, rwkv_k_a, rwkv_r_k, rwkv_ln_w, rwkv_ln_b, rwkv_w_o, mla_w_down, mla_q_a_norm, mla_kv_a_norm, mla_w_uq, mla_w_ukv, mla_q_norm, mla_k_norm, mla_w_o):
    batch, seq, d = x.shape
    depth = ffn_norm.shape[0]
    n_mixers = 2
    xf = x.reshape(batch * seq, d)
    d_ff = ffn_w2.shape[2]
    w13p, w2p = _ffn_weights(ffn_w13.reshape((2 * depth,) + ffn_w13.shape[2:]),
                             ffn_w2.reshape((2 * depth,) + ffn_w2.shape[2:]))
    for i in range(depth):
        j = i // n_mixers
        xf = _ffn(xf, ffn_norm[i, 0], w13p, w2p, 2 * i, d_ff)
        if i % n_mixers == 0:
            xf = _rwkv_mix(batch, xf, mix_norm[i], rwkv_mu[j], rwkv_w_rkv[j], rwkv_w0[j], rwkv_w1[j],
                           rwkv_w2[j], rwkv_a0[j], rwkv_a1[j], rwkv_a2[j], rwkv_g1[j], rwkv_g2[j],
                           rwkv_k_k[j], rwkv_k_a[j], rwkv_r_k[j], rwkv_ln_w[j], rwkv_ln_b[j],
                           rwkv_w_o[j])
        else:
            xf = _mla_mix(batch, xf, positions, mix_norm[i], mla_w_down[j], mla_q_a_norm[j],
                          mla_kv_a_norm[j], mla_w_uq[j], mla_w_ukv[j], mla_q_norm[j], mla_k_norm[j],
                          mla_w_o[j])
        xf = _ffn(xf, ffn_norm[i, 1], w13p, w2p, 2 * i + 1, d_ff)
    return xf.reshape(batch, seq, d)
```

```python
import functools
import math

import jax
import jax.numpy as jnp
from jax import lax
from jax.experimental import pallas as pl
from jax.experimental.pallas import tpu as pltpu

F32 = jnp.float32
BF16 = jnp.bfloat16

RMS_EPS = 1e-6
GN_EPS = 64e-5
ROPE_THETA = 10000.0
ATTN_CHUNK = 64
RWKV_HEAD = 64

LANE = 128
SUBLANE = 8
VMEM_LIMIT_BYTES = 56 * 1024 * 1024

SCAN_CHUNK = 64
HEADS_PER_GROUP = 4
SCAN_STREAMS = 8
SCAN_BATCH_ROWS = 1
SCAN_CHUNKS_PER_STEP = 4
ATTN_Q_TILE = 512
ATTN_HEADS_PER_STEP = 4
MAX_PLAIN_SCORE = 30.0
FFN_ROWS = 1024
FFN_TILE = 512
MLA_HEADS_PER_STEP = 16
PROJ_ROWS = 512
CAST_ROWS = 128


def _params(semantics):
    return pltpu.CompilerParams(dimension_semantics=semantics,
                                vmem_limit_bytes=VMEM_LIMIT_BYTES)


def _pick_tile(n, target, quantum):
    if n <= target:
        return n
    t = (target // quantum) * quantum
    while t > quantum and n % t:
        t -= quantum
    assert n % t == 0, (n, target, quantum)
    return t


def _round_robin(gens):
    live = list(gens)
    while live:
        live = [g for g in live if next(g, True) is None]


def _rms(x, g):
    return x * lax.rsqrt(jnp.mean(x * x, axis=-1, keepdims=True) + RMS_EPS) * g


def _ffn_body(nj, x_ref, g_ref, wg_ref, wu_ref, w2_ref, o_ref, xn_ref):
    j = pl.program_id(1)

    def tile_sum():
        xn = xn_ref[...]
        gate = jnp.dot(xn, wg_ref[...], preferred_element_type=F32)
        up = jnp.dot(xn, wu_ref[...], preferred_element_type=F32)
        act = (gate * jax.nn.sigmoid(gate) * up).astype(BF16)
        return jnp.dot(act, w2_ref[...], preferred_element_type=F32)

    def normalize():
        xn_ref[...] = _rms(x_ref[...], g_ref[...]).astype(BF16)

    if nj == 1:
        normalize()
        o_ref[...] = x_ref[...] + 0.5 * tile_sum()
    else:
        @pl.when(j == 0)
        def _():
            normalize()
            o_ref[...] = tile_sum()

        if nj > 2:
            @pl.when((j > 0) & (j < nj - 1))
            def _():
                o_ref[...] += tile_sum()

        @pl.when(j == nj - 1)
        def _():
            o_ref[...] = x_ref[...] + 0.5 * (o_ref[...] + tile_sum())


def _ffn_tile(f):
    return min(FFN_TILE, -(-f // LANE) * LANE)


def _tile_cast_body(tiles, x_ref, o_ref):
    tw = o_ref.shape[-1]
    for t, (c0, width) in enumerate(tiles):
        o_ref[t, :, :width] = x_ref[:, c0:c0 + width].astype(BF16)
        if width < tw:
            o_ref[t, :, width:] = jnp.zeros((o_ref.shape[1], tw - width), BF16)


def _tile_cast(name, w, tiles, tw):
    n, rows, cols = w.shape
    tr = _pick_tile(rows, CAST_ROWS, SUBLANE)
    return pl.pallas_call(
        functools.partial(_tile_cast_body, tuple(tiles)),
        grid=(n, rows // tr),
        in_specs=[pl.BlockSpec((None, tr, cols), lambda l, i: (l, i, 0))],
        out_specs=pl.BlockSpec((None, len(tiles), tr, tw), lambda l, i: (l, 0, i, 0)),
        out_shape=jax.ShapeDtypeStruct((n, len(tiles), rows, tw), BF16),
        compiler_params=_params(("parallel", "parallel")),
        name=name,
    )(w)


def _cast_w2_body(f, x_ref, o_ref):
    tr = o_ref.shape[0]
    row = pl.program_id(1) * tr + lax.broadcasted_iota(jnp.int32, o_ref.shape, 0)
    o_ref[...] = jnp.where(row < f, x_ref[...], 0.0).astype(BF16)


def _ffn_weights(w13, w2):
    n, d, f2 = w13.shape
    f = f2 // 2
    assert f % LANE == 0
    tf = _ffn_tile(f)
    fp = -(-f // tf) * tf
    half = [(t * tf, min(tf, f - t * tf)) for t in range(fp // tf)]
    w13p = _tile_cast("cast_w13", w13, half + [(f + c0, width) for c0, width in half], tf)
    w2p = pl.pallas_call(
        functools.partial(_cast_w2_body, f),
        grid=(n, fp // tf),
        in_specs=[pl.BlockSpec((None, tf, d), lambda l, i: (l, i, 0))],
        out_specs=pl.BlockSpec((None, tf, d), lambda l, i: (l, i, 0)),
        out_shape=jax.ShapeDtypeStruct((n, fp, d), BF16),
        compiler_params=_params(("parallel", "parallel")),
        name="cast_w2",
    )(w2)
    return w13p, w2p


def _ffn(x, g, w13p, w2p, layer, f):
    m, d = x.shape
    fp = w2p.shape[1]
    tf = _ffn_tile(f)
    nj = fp // tf
    tm = _pick_tile(m, FFN_ROWS, SUBLANE)
    return pl.pallas_call(
        functools.partial(_ffn_body, nj),
        grid=(m // tm, nj),
        in_specs=[
            pl.BlockSpec((tm, d), lambda i, j: (i, 0)),
            pl.BlockSpec((1, d), lambda i, j: (0, 0)),
            pl.BlockSpec((None, None, d, tf), lambda i, j: (layer, j, 0, 0)),
            pl.BlockSpec((None, None, d, tf), lambda i, j: (layer, j + nj, 0, 0)),
            pl.BlockSpec((None, tf, d), lambda i, j: (layer, j, 0)),
        ],
        out_specs=pl.BlockSpec((tm, d), lambda i, j: (i, 0)),
        out_shape=jax.ShapeDtypeStruct((m, d), F32),
        scratch_shapes=[pltpu.VMEM((tm, d), BF16)],
        compiler_params=_params(("parallel", "arbitrary")),
        name="ffn",
    )(x, g.reshape(1, d), w13p, w13p, w2p)


def _mm_body(single_col_tile, n_lhs, n_epi, n_out, lhs_fn, epi_fn, *refs):
    lhs_refs = refs[:n_lhs]
    w_ref = refs[n_lhs]
    epi_refs = refs[n_lhs + 1:n_lhs + 1 + n_epi]
    out_refs = refs[n_lhs + 1 + n_epi:n_lhs + 1 + n_epi + n_out]

    if single_col_tile:
        lhs = lhs_fn(*[r[...] for r in lhs_refs]).astype(BF16)
    else:
        lhs_scratch = refs[-1]

        @pl.when(pl.program_id(1) == 0)
        def _():
            lhs_scratch[...] = lhs_fn(*[r[...] for r in lhs_refs]).astype(BF16)

        lhs = lhs_scratch[...]
    acc = jnp.dot(lhs, w_ref[...], preferred_element_type=F32)
    outs = epi_fn(acc, *[r[...] for r in epi_refs])
    for o_ref, o in zip(out_refs, outs):
        o_ref[...] = o.astype(o_ref.dtype)


def _mm(name, m, lhs, lhs_fn, w, epi, epi_fn, outs, *, tm, tn):
    k, n = w.shape
    single = n == tn
    body = functools.partial(_mm_body, single, len(lhs), len(epi), len(outs), lhs_fn, epi_fn)
    return pl.pallas_call(
        body,
        grid=(m // tm, n // tn),
        in_specs=[s for _, s in lhs] + [pl.BlockSpec((k, tn), lambda i, j: (0, j))]
        + [s for _, s in epi],
        out_specs=[s for _, s in outs],
        out_shape=[s for s, _ in outs],
        scratch_shapes=[] if single else [pltpu.VMEM((tm, k), BF16)],
        compiler_params=_params(("parallel", "arbitrary")),
        name=name,
    )(*[a for a, _ in lhs], w, *[a for a, _ in epi])


def _row_spec(tm, width, col_block=0):
    return pl.BlockSpec((tm, width), lambda i, j: (i, col_block))


def _tile_spec(tm, tn):
    return pl.BlockSpec((tm, tn), lambda i, j: (i, j))


def _vec_spec(width):
    return pl.BlockSpec((1, width), lambda i, j: (0, 0))


def _colvec_spec(tn):
    return pl.BlockSpec((1, tn), lambda i, j: (0, j))


def _simple_mm(name, lhs, lhs_fn, w, epi, epi_fn, out_dtype=F32, tm_target=PROJ_ROWS, tn_target=1024):
    m = lhs[0][0].shape[0]
    n = w.shape[1]
    tm = _pick_tile(m, tm_target, 8)
    tn = _pick_tile(n, tn_target, LANE)
    lhs = [(a, spec(tm)) for a, spec in lhs]
    epi = [(a, spec(tm, tn)) for a, spec in epi]
    out = (jax.ShapeDtypeStruct((m, n), out_dtype), _tile_spec(tm, tn))
    return _mm(name, m, lhs, lhs_fn, w, epi, epi_fn, [out], tm=tm, tn=tn)[0]


def _rows(a):
    return a, lambda tm: _row_spec(tm, a.shape[1])


def _vec(a):
    a = a.reshape(1, -1)
    return a, lambda tm: _vec_spec(a.shape[1])


def _epi_tile(a):
    return a, lambda tm, tn: _tile_spec(tm, tn)


def _epi_colvec(a):
    return a.reshape(1, -1), lambda tm, tn: _colvec_spec(tn)


def _rwkv_proj_body(seq_tiles, ranks, x_ref, xprev_ref, g_ref, mu_ref, wr_ref, wk_ref, wv_ref, l1_ref,
                    w2_ref, a2_ref, g2_ref, w0_ref, a0_ref, rkv_ref, ld_ref, ag_ref,
                    lerp_scr, tw_scr, ta_scr, tg_scr):
    i = pl.program_id(0)
    j = pl.program_id(1)
    rw, ra = ranks
    w_refs = (wr_ref, wk_ref, wv_ref)

    def emit_rkv(p):
        rkv_ref[p] = jnp.dot(lerp_scr[p], w_refs[p][...], preferred_element_type=F32).astype(BF16)

    def emit_lora_outputs():
        z = -(w0_ref[...] + jnp.dot(tw_scr[...], w2_ref[...], preferred_element_type=F32))
        softplus = jnp.maximum(z, 0.0) + jnp.log1p(jnp.exp(-jnp.abs(z)))
        ld_ref[...] = -jnp.exp(-softplus - 0.5)
        ag_ref[0] = jax.nn.sigmoid(
            a0_ref[...] + jnp.dot(ta_scr[...], a2_ref[...], preferred_element_type=F32)).astype(BF16)
        ag_ref[1] = jnp.dot(tg_scr[...], g2_ref[...], preferred_element_type=F32).astype(BF16)

    @pl.when(j == 0)
    def _():
        h = _rms(x_ref[...], g_ref[...])
        prev = _rms(xprev_ref[...], g_ref[...])[SUBLANE - 1:SUBLANE, :]
        prev = jnp.where(i % seq_tiles == 0, 0.0, prev)
        row = lax.broadcasted_iota(jnp.int32, h.shape, 0)
        dx = jnp.where(row == 0, prev, pltpu.roll(h, 1, 0)) - h
        def lerp(idx):
            return (h + dx * mu_ref[idx:idx + 1, :]).astype(BF16)

        for p in range(3):
            lerp_scr[p] = lerp(p)
            emit_rkv(p)
        tw_scr[...] = jnp.tanh(jnp.dot(lerp(3), l1_ref[:, :rw],
                                       preferred_element_type=F32)).astype(BF16)
        ta_scr[...] = jnp.dot(lerp(4), l1_ref[:, rw:rw + ra],
                              preferred_element_type=F32).astype(BF16)
        tg_scr[...] = jax.nn.sigmoid(jnp.dot(lerp(5), l1_ref[:, rw + ra:],
                                             preferred_element_type=F32)).astype(BF16)
        emit_lora_outputs()

    @pl.when(j > 0)
    def _():
        for p in range(3):
            emit_rkv(p)
        emit_lora_outputs()


def _pad_lora(wa, wb):
    rank = wa.shape[1]
    rp = -(-rank // LANE) * LANE
    return (jnp.pad(wa, ((0, 0), (0, rp - rank))).astype(BF16),
            jnp.pad(wb, ((0, rp - rank), (0, 0))).astype(BF16))


def _rwkv_proj(batch, x, norm_g, mu, w_rkv, w0, w1, w2, a0, a1, a2, g1, g2):
    m, d = x.shape
    tm = _pick_tile(m // batch, PROJ_ROWS, SUBLANE)
    tn = _pick_tile(d, 512, LANE)
    w1p, w2p = _pad_lora(w1, w2)
    a1p, a2p = _pad_lora(a1, a2)
    g1p, g2p = _pad_lora(g1, g2)
    rw, ra, rg = w1p.shape[1], a1p.shape[1], g1p.shape[1]
    l1 = jnp.concatenate([w1p, a1p, g1p], axis=1)
    w_bf = _tile_cast("cast_w_rkv", w_rkv, [(t * tn, tn) for t in range(d // tn)], tn)
    mu_ord = mu[jnp.array([0, 2, 3, 1, 4, 5])]
    col = lambda i, j: (0, j)

    def w_plane(p):
        return pl.BlockSpec((None, None, d, tn), lambda i, j: (p, j, 0, 0))

    def planes(n):
        return pl.BlockSpec((n, tm, tn), lambda i, j: (0, i, j))

    body = functools.partial(_rwkv_proj_body, (m // batch) // tm, (rw, ra))
    return pl.pallas_call(
        body,
        grid=(m // tm, d // tn),
        in_specs=[
            pl.BlockSpec((tm, d), lambda i, j: (i, 0)),
            pl.BlockSpec((SUBLANE, d), lambda i, j: (jnp.maximum(i * (tm // SUBLANE) - 1, 0), 0)),
            pl.BlockSpec((1, d), lambda i, j: (0, 0)),
            pl.BlockSpec((6, d), lambda i, j: (0, 0)),
            w_plane(0), w_plane(1), w_plane(2),
            pl.BlockSpec((d, rw + ra + rg), lambda i, j: (0, 0)),
            pl.BlockSpec((rw, tn), col),
            pl.BlockSpec((ra, tn), col),
            pl.BlockSpec((rg, tn), col),
            pl.BlockSpec((1, tn), col),
            pl.BlockSpec((1, tn), col),
        ],
        out_specs=[planes(3), pl.BlockSpec((tm, tn), lambda i, j: (i, j)), planes(2)],
        out_shape=[jax.ShapeDtypeStruct((3, m, d), BF16), jax.ShapeDtypeStruct((m, d), F32),
                   jax.ShapeDtypeStruct((2, m, d), BF16)],
        scratch_shapes=[pltpu.VMEM((3, tm, d), BF16), pltpu.VMEM((tm, rw), BF16),
                        pltpu.VMEM((tm, ra), BF16), pltpu.VMEM((tm, rg), BF16)],
        compiler_params=_params(("parallel", "arbitrary")),
        name="rwkv_proj",
    )(x, x, norm_g.reshape(1, d), mu_ord, w_bf, w_bf, w_bf, l1, w2p, a2p, g2p,
      w0.reshape(1, d), a0.reshape(1, d))


def _bf(x):
    return x.astype(BF16)


def _dot(a, b):
    return jnp.dot(_bf(a), _bf(b), preferred_element_type=F32)


def _dot_nt(a, b):
    return lax.dot_general(_bf(a), _bf(b), (((1,), (1,)), ((), ())), preferred_element_type=F32)


def _dot_tn(a, b):
    return lax.dot_general(_bf(a), _bf(b), (((0,), (0,)), ((), ())), preferred_element_type=F32)


def _scan_body(n_streams, nb, cps, r_ref, ld_ref, k_ref, v_ref, al_ref, g_ref, kk_w_ref, ka_w_ref,
               rk_w_ref, lnw_ref, lnb_ref, o_ref, s_ref):
    c = SCAN_CHUNK
    w = HEADS_PER_GROUP * RWKV_HEAD
    wc = HEADS_PER_GROUP * c

    @pl.when(pl.program_id(2) == 0)
    def _():
        s_ref[...] = jnp.zeros_like(s_ref)

    tri = jnp.where(lax.broadcasted_iota(jnp.int32, (c, c), 0)
                    >= lax.broadcasted_iota(jnp.int32, (c, c), 1), 1.0, 0.0).astype(BF16)
    bd_r = lax.broadcasted_iota(jnp.int32, (w, w), 0) // RWKV_HEAD
    bd_c = lax.broadcasted_iota(jnp.int32, (w, w), 1) // RWKV_HEAD
    head_mask = bd_r == bd_c
    head_ones = jnp.where(head_mask, 1.0, 0.0).astype(BF16)
    t_idx = lax.broadcasted_iota(jnp.int32, (c, wc), 0)
    s_idx = lax.broadcasted_iota(jnp.int32, (c, wc), 1) % c
    strict = s_idx < t_idx
    incl = s_idx <= t_idx
    eye = jnp.where(s_idx == t_idx, 1.0, 0.0)
    base_level = 3
    base_blocks = (t_idx >> base_level) == (s_idx >> base_level)
    join = {lv: (((t_idx >> lv) & 1) == 1) & ((s_idx >> lv) == (t_idx >> lv) - 1)
            for lv in range(base_level, c.bit_length() - 1)}

    def head_sum(x):
        return jnp.dot(_bf(x), head_ones, preferred_element_type=F32)

    def block_diag(x):
        xb = _bf(x)
        return jnp.where(head_mask, jnp.concatenate([xb] * HEADS_PER_GROUP, axis=0),
                         jnp.zeros((), BF16))

    def stream(bi, si, cj):
        st = bi * n_streams + si
        ln = pl.ds(si * w, w)
        rows = pl.ds(cj * c, c)
        r = r_ref[bi, rows, ln].astype(F32)
        ld = ld_ref[bi, rows, ln]
        k = k_ref[bi, rows, ln].astype(F32)
        v = v_ref[bi, rows, ln].astype(F32)
        al = al_ref[bi, rows, ln].astype(F32)

        ld_hi = _bf(ld)
        rest = ld - ld_hi.astype(F32)
        ld_mid = _bf(rest)
        ld_lo = _bf(rest - ld_mid.astype(F32))
        cum = (jnp.dot(tri, ld_hi, preferred_element_type=F32)
               + jnp.dot(tri, ld_mid, preferred_element_type=F32)
               + jnp.dot(tri, ld_lo, preferred_element_type=F32))
        kk = k * kk_w_ref[:, ln]
        ss = head_sum(kk * kk)
        yield
        p_incl = jnp.exp(cum)
        p_prev = jnp.exp(cum - ld)
        p_inv = jnp.exp(-cum)
        p_last = p_incl[c - 1:c, :]
        kk = kk / jnp.maximum(jnp.sqrt(ss), 1e-12)
        kmod = k * (1.0 + (al - 1.0) * ka_w_ref[:, ln])

        a_t = -kk * p_prev
        b_t = kk * al * p_inv
        k_t = kmod * p_inv
        r_t = r * p_incl

        ar = _bf(jnp.concatenate([a_t, r_t], axis=0))
        m_b = _dot_nt(ar, block_diag(b_t))
        m_k = _dot_nt(ar, block_diag(k_t))
        s0 = s_ref[st]
        w1 = _dot_nt(ar, s0)
        yield
        a_ab = jnp.where(strict, m_b[:c], 0.0)
        a_rb = jnp.where(incl, m_b[c:], 0.0)
        a_ak = jnp.where(strict, m_k[:c], 0.0)
        a_rk = jnp.where(incl, m_k[c:], 0.0)
        w2 = _dot(jnp.concatenate([a_ak, a_rk], axis=0), block_diag(v))

        pw = jnp.where(base_blocks, a_ab, 0.0)
        t_inv = eye
        for step in range(base_level):
            if step < base_level - 1:
                pw_b = _bf(pw)
                z = _dot(jnp.concatenate([pw_b, _bf(t_inv)], axis=0), block_diag(pw_b))
                yield
                pw, t_inv = z[:c], t_inv + z[c:]
            else:
                z = _dot(t_inv, block_diag(pw))
                yield
                t_inv = t_inv + z
        n_bd = block_diag(a_ab)
        level = base_level
        while (1 << level) < c:
            t_b = _bf(t_inv)
            tn = _dot(t_b, n_bd)
            yield
            x = _dot(tn, block_diag(t_b))
            yield
            t_inv = t_inv + jnp.where(join[level], x, 0.0)
            level += 1

        u = _dot(t_inv, block_diag(w1[:c] + w2[:c]))
        yield
        y = w1[c:] + w2[c:] + _dot(a_rb, block_diag(u))

        uv = jnp.concatenate([u, v], axis=0)
        bk = jnp.concatenate([b_t, k_t], axis=0) * p_last
        s_new = _dot_tn(uv, bk)
        yield
        s_ref[st] = jnp.where(head_mask, s0 * p_last + s_new, 0.0)

        inv_n = 1.0 / RWKV_HEAD
        sums = head_sum(jnp.concatenate([y, r * kmod * rk_w_ref[:, ln]], axis=0))
        yield
        dlt = y - sums[:c] * inv_n
        bsum = sums[c:]
        var = head_sum(dlt * dlt) * inv_n
        yield
        yn = dlt * lax.rsqrt(var + GN_EPS) * lnw_ref[:, ln] + lnb_ref[:, ln]
        bonus = bsum * v
        o_ref[bi, rows, ln] = ((yn + bonus) * g_ref[bi, rows, ln]).astype(o_ref.dtype)

    def chunks(bi, si):
        for cj in range(cps):
            yield from stream(bi, si, cj)

    _round_robin([chunks(bi, si) for bi in range(nb) for si in range(n_streams)])


def _rwkv_scan(batch, rkv, ld, ag, k_k, k_a, r_k, ln_w, ln_b):
    _, m, d = rkv.shape
    t = m // batch
    c = SCAN_CHUNK
    w = HEADS_PER_GROUP * RWKV_HEAD
    assert d % w == 0 and t % c == 0 and c == RWKV_HEAD
    n_streams = _pick_tile(d // w, SCAN_STREAMS, 1)
    wb = n_streams * w
    cps = _pick_tile(t // c, SCAN_CHUNKS_PER_STEP, 1)
    nc = t // (c * cps)
    nb = _pick_tile(batch, SCAN_BATCH_ROWS, 1)
    tok = pl.BlockSpec((nb, cps * c, wb), lambda b, gi, ci: (b, ci, gi))
    par = pl.BlockSpec((1, wb), lambda b, gi, ci: (0, gi))

    def plane(p):
        return pl.BlockSpec((None, nb, cps * c, wb), lambda b, gi, ci: (p, b, ci, gi))

    rkv4 = rkv.reshape(3, batch, t, d)
    ag4 = ag.reshape(2, batch, t, d)
    return pl.pallas_call(
        functools.partial(_scan_body, n_streams, nb, cps),
        grid=(batch // nb, d // wb, nc),
        in_specs=[plane(0), tok, plane(1), plane(2), plane(0), plane(1)] + [par] * 5,
        out_specs=tok,
        out_shape=jax.ShapeDtypeStruct((batch, t, d), BF16),
        scratch_shapes=[pltpu.VMEM((nb * n_streams, w, w), F32)],
        compiler_params=_params(("parallel", "parallel", "arbitrary")),
        name="rwkv_scan",
    )(rkv4, ld.reshape(batch, t, d), rkv4, rkv4, ag4, ag4,
      *[p.reshape(1, d) for p in (k_k, k_a, r_k, ln_w, ln_b)]).reshape(m, d)


def _rwkv_mix(batch, x, norm_g, mu, w_rkv, w0, w1, w2, a0, a1, a2, g1, g2,
              k_k, k_a, r_k, ln_w, ln_b, w_o):
    rkv, ld, ag = _rwkv_proj(batch, x, norm_g, mu, w_rkv, w0, w1, w2, a0, a1, a2, g1, g2)
    yg = _rwkv_scan(batch, rkv, ld, ag, k_k, k_a, r_k.reshape(-1), ln_w, ln_b)
    return _simple_mm("rwkv_out", [_rows(yg)], lambda t: t, w_o.astype(BF16), [_epi_tile(x)],
                      lambda acc, x_t: (x_t + acc,), tn_target=x.shape[1])


def _attn_body(tq, tk, nh, nq, small_ref, q_ref, k_ref, v_ref, o_ref, v1_ref):
    qi = pl.program_id(2)
    dv = v_ref.shape[-1]

    def key_rows(j):
        start = j * tk if isinstance(j, int) else pl.multiple_of(j * tk, tk)
        return pl.ds(start, tk)

    def scores(h, j, masked):
        kb = k_ref[h, key_rows(j), :]
        s = lax.dot_general(q_ref[h], kb, (((1,), (1,)), ((), ())), preferred_element_type=F32)
        if masked:
            q_chunk = (qi * tq + lax.broadcasted_iota(jnp.int32, (tq, tk), 0)) // ATTN_CHUNK
            k_chunk = (j * tk + lax.broadcasted_iota(jnp.int32, (tq, tk), 1)) // ATTN_CHUNK
            s = jnp.where(k_chunk <= q_chunk, s, jnp.finfo(F32).min)
        return s

    def values(h, j):
        return v_ref[h, key_rows(j), :]

    def all_heads(make_gen):
        outs = [[] for _ in range(nh)]
        _round_robin([make_gen(h, outs[h]) for h in range(nh)])
        return tuple(o[0] for o in outs)

    def sweep(step, init, diag_step=None):
        carries = lax.fori_loop(
            0, qi, lambda j, cr: all_heads(lambda h, out: step(h, j, cr[h], False, out)), init)
        if diag_step is not None:
            return all_heads(lambda h, out: diag_step(h, carries[h], out))
        return all_heads(lambda h, out: step(h, qi, carries[h], True, out))

    def online_step(h, j, carry, masked, out):
        m_i, l_i, acc = carry
        s = scores(h, j, masked)
        yield
        m_new = jnp.maximum(m_i, jnp.max(s, axis=-1, keepdims=True))
        p = jnp.exp(s - m_new)
        alpha = jnp.exp(m_i - m_new)
        l_new = alpha * l_i + jnp.sum(p, axis=-1, keepdims=True)
        pv = jnp.dot(p.astype(BF16), values(h, j), preferred_element_type=F32)
        yield
        out.append((m_new, l_new, alpha * acc + pv))

    def plain_step(h, j, acc, masked, out):
        s = scores(h, j, masked)
        yield
        p = jnp.exp(s).astype(BF16)
        v1 = v1_ref[h, key_rows(j), :]
        pv = jnp.dot(p, v1, preferred_element_type=F32)
        yield
        out.append(acc + pv)

    def plain_diag(h, acc, out):
        half = tk // 2
        base = pl.multiple_of(qi * tk, tk)
        visible = (lax.broadcasted_iota(jnp.int32, (tq, half), 1) // ATTN_CHUNK
                   <= lax.broadcasted_iota(jnp.int32, (tq, half), 0) // ATTN_CHUNK)

        def weights(q_rows, key0, mask):
            kb = k_ref[h, pl.ds(key0, half), :]
            s = lax.dot_general(q_rows, kb, (((1,), (1,)), ((), ())), preferred_element_type=F32)
            return jnp.where(mask, jnp.exp(s), 0.0).astype(BF16)

        p_a = weights(q_ref[h], base, visible)
        yield
        pv_a = jnp.dot(p_a, v1_ref[h, pl.ds(base, half), :], preferred_element_type=F32)
        p_b = weights(q_ref[h, half:, :], base + half, visible[:half])
        yield
        pv_b = jnp.dot(p_b, v1_ref[h, pl.ds(base + half, half), :], preferred_element_type=F32)
        yield
        out.append(acc + jnp.concatenate([pv_a[:half], pv_a[half:] + pv_b], axis=0))

    @pl.when(small_ref[0] == 1)
    def _():
        @pl.when(qi == 0)
        def _():
            v1_ref[:, :, :dv] = v_ref[...]
            v1_ref[:, :, dv:] = jnp.ones(v_ref.shape, BF16)

        def unrolled(n_full):
            accs = tuple(jnp.zeros((tq, 2 * dv), F32) for _ in range(nh))
            for j in range(n_full):
                accs = all_heads(lambda h, out, j=j: plain_step(h, j, accs[h], False, out))
            accs = all_heads(lambda h, out: plain_diag(h, accs[h], out))
            for h, acc in enumerate(accs):
                o_ref[:, h * dv:(h + 1) * dv] = (acc[:, :dv] / acc[:, dv:]).astype(o_ref.dtype)

        for n_full in range(nq):
            pl.when(qi == n_full)(functools.partial(unrolled, n_full))

    @pl.when(small_ref[0] != 1)
    def _():
        init = tuple((jnp.full((tq, 1), -jnp.inf, F32), jnp.zeros((tq, 1), F32),
                      jnp.zeros((tq, dv), F32)) for _ in range(nh))
        for h, (_, l_i, acc) in enumerate(sweep(online_step, init)):
            o_ref[:, h * dv:(h + 1) * dv] = (acc / l_i).astype(o_ref.dtype)


def _attention(batch, q, k, v, small_scores):
    heads, m, dqk = q.shape
    dv = v.shape[-1]
    t = m // batch
    tq = _pick_tile(t, ATTN_Q_TILE, 2 * ATTN_CHUNK)
    assert tq % (2 * ATTN_CHUNK) == 0
    nq = t // tq
    nh = _pick_tile(heads, ATTN_HEADS_PER_STEP, 1)
    return pl.pallas_call(
        functools.partial(_attn_body, tq, tq, nh, nq),
        grid=(batch, heads // nh, nq),
        in_specs=[
            pl.BlockSpec(memory_space=pltpu.SMEM),
            pl.BlockSpec((nh, tq, dqk), lambda b, h, i: (h, b * nq + i, 0)),
            pl.BlockSpec((nh, t, dqk), lambda b, h, i: (h, b, 0)),
            pl.BlockSpec((nh, t, dv), lambda b, h, i: (h, b, 0)),
        ],
        out_specs=pl.BlockSpec((tq, nh * dv), lambda b, h, i: (b * nq + i, h)),
        out_shape=jax.ShapeDtypeStruct((m, heads * dv), BF16),
        scratch_shapes=[pltpu.VMEM((nh, t, 2 * dv), BF16)],
        compiler_params=_params(("parallel", "parallel", "arbitrary")),
        name="mla_attention",
    )(small_scores, q, k, v)


def _mla_mix(batch, x, positions, norm_g, w_down, q_a_norm, kv_a_norm, w_uq, w_ukv,
             q_norm, k_norm, w_o):
    m, d = x.shape
    q_lora = q_a_norm.shape[0]
    kv_lora = kv_a_norm.shape[0]
    heads, qk_dim = w_uq.shape[1], w_uq.shape[2]
    rope = w_down.shape[1] - q_lora - kv_lora
    nope = qk_dim - rope
    dv = w_ukv.shape[2] - nope
    half = rope // 2
    assert nope == LANE and dv == LANE and rope <= LANE and q_lora == kv_lora and q_lora % LANE == 0
    scale = 1.0 / math.sqrt(qk_dim)

    perm = jnp.concatenate([jnp.arange(half, rope), jnp.arange(0, half)])
    lane_pad = lambda a: jnp.pad(a, [(0, 0)] * (a.ndim - 1) + [(0, LANE - a.shape[-1])])

    inv_freq = ROPE_THETA ** (-jnp.arange(0, rope, 2, dtype=F32) / rope)
    ang = positions.reshape(m, 1).astype(F32) * inv_freq
    cos, sin = jnp.cos(ang), jnp.sin(ang)
    cos_t = lane_pad(jnp.concatenate([cos, cos], axis=-1))
    sin_t = lane_pad(jnp.concatenate([-sin, sin], axis=-1))

    w_kpe = w_down[:, q_lora + kv_lora:]
    w_down_x = jnp.concatenate(
        [w_down[:, :q_lora + kv_lora], lane_pad(w_kpe), lane_pad(w_kpe[:, perm])], axis=1).astype(BF16)
    c = _simple_mm("mla_down", [_rows(x), _vec(norm_g)], _rms, w_down_x, [], lambda acc: (acc,),
                   tn_target=w_down_x.shape[1])
    qb = q_lora // q_lora
    tm = _pick_tile(m, PROJ_ROWS, SUBLANE)

    def norm_rope(body, pe, rot, g_body, g_pe, g_rot, cos_b, sin_b, extra_scale):
        ss = jnp.sum(body * body, axis=-1, keepdims=True) + jnp.sum(pe * pe, axis=-1, keepdims=True)
        s = lax.rsqrt(ss / qk_dim + RMS_EPS) * extra_scale
        return jnp.concatenate([body * s * g_body, (pe * g_pe * cos_b + rot * g_rot * sin_b) * s], axis=-1)

    w_q_pe = w_uq[:, :, nope:]
    w_q_x = jnp.concatenate([w_uq[:, :, :nope], lane_pad(w_q_pe), lane_pad(w_q_pe[:, :, perm])],
                            axis=-1).reshape(q_lora, heads * 3 * LANE).astype(BF16)
    gq = jnp.concatenate([q_norm[:nope], lane_pad(q_norm[nope:]), lane_pad(q_norm[nope:][perm])]).reshape(1, -1)

    hb = _pick_tile(heads, MLA_HEADS_PER_STEP, 1)

    def q_epi(acc, g_t, cos_b, sin_b):
        per_head = []
        for hh in range(hb):
            a = acc[:, hh * 3 * LANE:(hh + 1) * 3 * LANE]
            per_head.append(norm_rope(a[:, :LANE], a[:, LANE:2 * LANE], a[:, 2 * LANE:], g_t[:, :LANE],
                                      g_t[:, LANE:2 * LANE], g_t[:, 2 * LANE:], cos_b, sin_b, scale))
        return (jnp.stack(per_head, axis=0),)

    q = _mm("mla_q", m,
            [(c, _row_spec(tm, q_lora, 0)), (q_a_norm.reshape(1, -1), _vec_spec(q_lora))], _rms, w_q_x,
            [(gq, _vec_spec(3 * LANE)), (cos_t, _row_spec(tm, LANE)), (sin_t, _row_spec(tm, LANE))], q_epi,
            [(jax.ShapeDtypeStruct((heads, m, 2 * LANE), BF16),
              pl.BlockSpec((hb, tm, 2 * LANE), lambda i, j: (j, i, 0)))],
            tm=tm, tn=hb * 3 * LANE)[0]

    w_kv_x = w_ukv.reshape(kv_lora, heads * (nope + dv)).astype(BF16)
    gk = jnp.concatenate([k_norm[:nope], lane_pad(k_norm[nope:]), lane_pad(k_norm[nope:][perm])]).reshape(1, -1)
    pe_block = (q_lora + kv_lora) // LANE

    def kv_epi(acc, g_t, pe, rot, cos_b, sin_b):
        k_out, v_out = [], []
        for hh in range(hb):
            a = acc[:, hh * (nope + dv):(hh + 1) * (nope + dv)]
            k_out.append(norm_rope(a[:, :nope], pe, rot, g_t[:, :LANE], g_t[:, LANE:2 * LANE],
                                   g_t[:, 2 * LANE:], cos_b, sin_b, 1.0))
            v_out.append(a[:, nope:])
        return jnp.stack(k_out, axis=0), jnp.stack(v_out, axis=0)

    k, v = _mm("mla_kv", m,
               [(c, _row_spec(tm, kv_lora, qb)), (kv_a_norm.reshape(1, -1), _vec_spec(kv_lora))], _rms, w_kv_x,
               [(gk, _vec_spec(3 * LANE)), (c, _row_spec(tm, LANE, pe_block)),
                (c, _row_spec(tm, LANE, pe_block + 1)), (cos_t, _row_spec(tm, LANE)),
                (sin_t, _row_spec(tm, LANE))], kv_epi,
               [(jax.ShapeDtypeStruct((heads, m, 2 * LANE), BF16),
                 pl.BlockSpec((hb, tm, 2 * LANE), lambda i, j: (j, i, 0))),
                (jax.ShapeDtypeStruct((heads, m, dv), BF16),
                 pl.BlockSpec((hb, tm, dv), lambda i, j: (j, i, 0)))],
               tm=tm, tn=hb * (nope + dv))

    score_bound = scale * qk_dim * jnp.max(jnp.abs(q_norm)) * jnp.max(jnp.abs(k_norm))
    small_scores = (score_bound <= MAX_PLAIN_SCORE).astype(jnp.int32).reshape(1)
    o = _attention(batch, q, k, v, small_scores)
    return _simple_mm("mla_out", [_rows(o)], lambda t: t, w_o.reshape(heads * dv, d).astype(BF16),
                      [_epi_tile(x)], lambda acc, x_t: (x_t + acc,), tn_target=d)


def kernel(x, positions, ffn_norm, ffn_w13, ffn_w2, mix_norm, rwkv_mu, rwkv_w_rkv, rwkv_w0, rwkv_w1, rwkv_w2, rwkv_a0, rwkv_a1, rwkv_a2, rwkv_g1, rwkv_g2, rwkv_k_k, rwkv_k_a, rwkv_r_k, rwkv_ln_w, rwkv_ln_b, rwkv_w_o, mla_w_down, mla_q_a_norm, mla_kv_a_norm, mla_w_uq, mla_w_ukv, mla_q_norm, mla_k_norm, mla_w_o):
    batch, seq, d = x.shape
    depth = ffn_norm.shape[0]
    n_mixers = 2
    xf = x.reshape(batch * seq, d)
    d_ff = ffn_w2.shape[2]
    w13p, w2p = _ffn_weights(ffn_w13.reshape((2 * depth,) + ffn_w13.shape[2:]),
                             ffn_w2.reshape((2 * depth,) + ffn_w2.shape[2:]))
    for i in range(depth):
        j = i // n_mixers
        xf = _ffn(xf, ffn_norm[i, 0], w13p, w2p, 2 * i, d_ff)
        if i % n_mixers == 0:
            xf = _rwkv_mix(batch, xf, mix_norm[i], rwkv_mu[j], rwkv_w_rkv[j], rwkv_w0[j], rwkv_w1[j],
                           rwkv_w2[j], rwkv_a0[j], rwkv_a1[j], rwkv_a2[j], rwkv_g1[j], rwkv_g2[j],
                           rwkv_k_k[j], rwkv_k_a[j], rwkv_r_k[j], rwkv_ln_w[j], rwkv_ln_b[j],
                           rwkv_w_o[j])
        else:
            xf = _mla_mix(batch, xf, positions, mix_norm[i], mla_w_down[j], mla_q_a_norm[j],
                          mla_kv_a_norm[j], mla_w_uq[j], mla_w_ukv[j], mla_q_norm[j], mla_k_norm[j],
                          mla_w_o[j])
        xf = _ffn(xf, ffn_norm[i, 1], w13p, w2p, 2 * i + 1, d_ff)
    return xf.reshape(batch, seq, d)
```

```python
import functools
import math

import jax
import jax.numpy as jnp
from jax import lax
from jax.experimental import pallas as pl
from jax.experimental.pallas import tpu as pltpu

F32 = jnp.float32
BF16 = jnp.bfloat16

RMS_EPS = 1e-6
GN_EPS = 64e-5
ROPE_THETA = 10000.0
ATTN_CHUNK = 64
RWKV_HEAD = 64

LANE = 128
SUBLANE = 8
VMEM_LIMIT_BYTES = 56 * 1024 * 1024

SCAN_CHUNK = 64
HEADS_PER_GROUP = 4
SCAN_STREAMS = 8
SCAN_BATCH_ROWS = 1
SCAN_CHUNKS_PER_STEP = 4
ATTN_Q_TILE = 512
ATTN_HEADS_PER_STEP = 4
MAX_PLAIN_SCORE = 30.0
FFN_ROWS = 1024
FFN_TILE = 512
MLA_HEADS_PER_STEP = 16
PROJ_ROWS = 512
CAST_ROWS = 128


def _params(semantics):
    return pltpu.CompilerParams(dimension_semantics=semantics,
                                vmem_limit_bytes=VMEM_LIMIT_BYTES)


def _pick_tile(n, target, quantum):
    if n <= target:
        return n
    t = (target // quantum) * quantum
    while t > quantum and n % t:
        t -= quantum
    assert n % t == 0, (n, target, quantum)
    return t


def _round_robin(gens):
    live = list(gens)
    while live:
        live = [g for g in live if next(g, True) is None]


def _rms(x, g):
    return x * lax.rsqrt(jnp.mean(x * x, axis=-1, keepdims=True) + RMS_EPS) * g


def _ffn_body(nj, x_ref, g_ref, wg_ref, wu_ref, w2_ref, o_ref, xn_ref):
    j = pl.program_id(1)

    def tile_sum():
        xn = xn_ref[...]
        gate = jnp.dot(xn, wg_ref[...], preferred_element_type=F32)
        up = jnp.dot(xn, wu_ref[...], preferred_element_type=F32)
        act = (gate * jax.nn.sigmoid(gate) * up).astype(BF16)
        return jnp.dot(act, w2_ref[...], preferred_element_type=F32)

    def normalize():
        xn_ref[...] = _rms(x_ref[...], g_ref[...]).astype(BF16)

    if nj == 1:
        normalize()
        o_ref[...] = x_ref[...] + 0.5 * tile_sum()
    else:
        @pl.when(j == 0)
        def _():
            normalize()
            o_ref[...] = tile_sum()

        if nj > 2:
            @pl.when((j > 0) & (j < nj - 1))
            def _():
                o_ref[...] += tile_sum()

        @pl.when(j == nj - 1)
        def _():
            o_ref[...] = x_ref[...] + 0.5 * (o_ref[...] + tile_sum())


def _ffn_tile(f):
    return min(FFN_TILE, -(-f // LANE) * LANE)


def _tile_cast_body(tiles, x_ref, o_ref):
    tw = o_ref.shape[-1]
    for t, (c0, width) in enumerate(tiles):
        o_ref[t, :, :width] = x_ref[:, c0:c0 + width].astype(BF16)
        if width < tw:
            o_ref[t, :, width:] = jnp.zeros((o_ref.shape[1], tw - width), BF16)


def _tile_cast(name, w, tiles, tw):
    n, rows, cols = w.shape
    tr = _pick_tile(rows, CAST_ROWS, SUBLANE)
    return pl.pallas_call(
        functools.partial(_tile_cast_body, tuple(tiles)),
        grid=(n, rows // tr),
        in_specs=[pl.BlockSpec((None, tr, cols), lambda l, i: (l, i, 0))],
        out_specs=pl.BlockSpec((None, len(tiles), tr, tw), lambda l, i: (l, 0, i, 0)),
        out_shape=jax.ShapeDtypeStruct((n, len(tiles), rows, tw), BF16),
        compiler_params=_params(("parallel", "parallel")),
        name=name,
    )(w)


def _cast_w2_body(f, x_ref, o_ref):
    tr = o_ref.shape[0]
    row = pl.program_id(1) * tr + lax.broadcasted_iota(jnp.int32, o_ref.shape, 0)
    o_ref[...] = jnp.where(row < f, x_ref[...], 0.0).astype(BF16)


def _ffn_weights(w13, w2):
    n, d, f2 = w13.shape
    f = f2 // 2
    assert f % LANE == 0
    tf = _ffn_tile(f)
    fp = -(-f // tf) * tf
    half = [(t * tf, min(tf, f - t * tf)) for t in range(fp // tf)]
    w13p = _tile_cast("cast_w13", w13, half + [(f + c0, width) for c0, width in half], tf)
    w2p = pl.pallas_call(
        functools.partial(_cast_w2_body, f),
        grid=(n, fp // tf),
        in_specs=[pl.BlockSpec((None, tf, d), lambda l, i: (l, i, 0))],
        out_specs=pl.BlockSpec((None, tf, d), lambda l, i: (l, i, 0)),
        out_shape=jax.ShapeDtypeStruct((n, fp, d), BF16),
        compiler_params=_params(("parallel", "parallel")),
        name="cast_w2",
    )(w2)
    return w13p, w2p


def _ffn(x, g, w13p, w2p, layer, f):
    m, d = x.shape
    fp = w2p.shape[1]
    tf = _ffn_tile(f)
    nj = fp // tf
    tm = _pick_tile(m, FFN_ROWS, SUBLANE)
    return pl.pallas_call(
        functools.partial(_ffn_body, nj),
        grid=(m // tm, nj),
        in_specs=[
            pl.BlockSpec((tm, d), lambda i, j: (i, 0)),
            pl.BlockSpec((1, d), lambda i, j: (0, 0)),
            pl.BlockSpec((None, None, d, tf), lambda i, j: (layer, j, 0, 0)),
            pl.BlockSpec((None, None, d, tf), lambda i, j: (layer, j + nj, 0, 0)),
            pl.BlockSpec((None, tf, d), lambda i, j: (layer, j, 0)),
        ],
        out_specs=pl.BlockSpec((tm, d), lambda i, j: (i, 0)),
        out_shape=jax.ShapeDtypeStruct((m, d), F32),
        scratch_shapes=[pltpu.VMEM((tm, d), BF16)],
        compiler_params=_params(("parallel", "arbitrary")),
        name="ffn",
    )(x, g.reshape(1, d), w13p, w13p, w2p)


def _mm_body(single_col_tile, n_lhs, n_epi, n_out, lhs_fn, epi_fn, *refs):
    lhs_refs = refs[:n_lhs]
    w_ref = refs[n_lhs]
    epi_refs = refs[n_lhs + 1:n_lhs + 1 + n_epi]
    out_refs = refs[n_lhs + 1 + n_epi:n_lhs + 1 + n_epi + n_out]

    if single_col_tile:
        lhs = lhs_fn(*[r[...] for r in lhs_refs]).astype(BF16)
    else:
        lhs_scratch = refs[-1]

        @pl.when(pl.program_id(1) == 0)
        def _():
            lhs_scratch[...] = lhs_fn(*[r[...] for r in lhs_refs]).astype(BF16)

        lhs = lhs_scratch[...]
    acc = jnp.dot(lhs, w_ref[...], preferred_element_type=F32)
    outs = epi_fn(acc, *[r[...] for r in epi_refs])
    for o_ref, o in zip(out_refs, outs):
        o_ref[...] = o.astype(o_ref.dtype)


def _mm(name, m, lhs, lhs_fn, w, epi, epi_fn, outs, *, tm, tn):
    k, n = w.shape
    single = n == tn
    body = functools.partial(_mm_body, single, len(lhs), len(epi), len(outs), lhs_fn, epi_fn)
    return pl.pallas_call(
        body,
        grid=(m // tm, n // tn),
        in_specs=[s for _, s in lhs] + [pl.BlockSpec((k, tn), lambda i, j: (0, j))]
        + [s for _, s in epi],
        out_specs=[s for _, s in outs],
        out_shape=[s for s, _ in outs],
        scratch_shapes=[] if single else [pltpu.VMEM((tm, k), BF16)],
        compiler_params=_params(("parallel", "arbitrary")),
        name=name,
    )(*[a for a, _ in lhs], w, *[a for a, _ in epi])


def _row_spec(tm, width, col_block=0):
    return pl.BlockSpec((tm, width), lambda i, j: (i, col_block))


def _tile_spec(tm, tn):
    return pl.BlockSpec((tm, tn), lambda i, j: (i, j))


def _vec_spec(width):
    return pl.BlockSpec((1, width), lambda i, j: (0, 0))


def _colvec_spec(tn):
    return pl.BlockSpec((1, tn), lambda i, j: (0, j))


def _simple_mm(name, lhs, lhs_fn, w, epi, epi_fn, out_dtype=F32, tm_target=PROJ_ROWS, tn_target=1024):
    m = lhs[0][0].shape[0]
    n = w.shape[1]
    tm = _pick_tile(m, tm_target, 8)
    tn = _pick_tile(n, tn_target, LANE)
    lhs = [(a, spec(tm)) for a, spec in lhs]
    epi = [(a, spec(tm, tn)) for a, spec in epi]
    out = (jax.ShapeDtypeStruct((m, n), out_dtype), _tile_spec(tm, tn))
    return _mm(name, m, lhs, lhs_fn, w, epi, epi_fn, [out], tm=tm, tn=tn)[0]


def _rows(a):
    return a, lambda tm: _row_spec(tm, a.shape[1])


def _vec(a):
    a = a.reshape(1, -1)
    return a, lambda tm: _vec_spec(a.shape[1])


def _epi_tile(a):
    return a, lambda tm, tn: _tile_spec(tm, tn)


def _epi_colvec(a):
    return a.reshape(1, -1), lambda tm, tn: _colvec_spec(tn)


def _rwkv_proj_body(seq_tiles, ranks, x_ref, xprev_ref, g_ref, mu_ref, wr_ref, wk_ref, wv_ref, l1_ref,
                    w2_ref, a2_ref, g2_ref, w0_ref, a0_ref, rkv_ref, ld_ref, ag_ref,
                    lerp_scr, tw_scr, ta_scr, tg_scr):
    i = pl.program_id(0)
    j = pl.program_id(1)
    rw, ra = ranks
    w_refs = (wr_ref, wk_ref, wv_ref)

    def emit_rkv(p):
        rkv_ref[p] = jnp.dot(lerp_scr[p], w_refs[p][...], preferred_element_type=F32).astype(BF16)

    def emit_lora_outputs():
        z = -(w0_ref[...] + jnp.dot(tw_scr[...], w2_ref[...], preferred_element_type=F32))
        softplus = jnp.maximum(z, 0.0) + jnp.log1p(jnp.exp(-jnp.abs(z)))
        ld_ref[...] = -jnp.exp(-softplus - 0.5)
        ag_ref[0] = jax.nn.sigmoid(
            a0_ref[...] + jnp.dot(ta_scr[...], a2_ref[...], preferred_element_type=F32)).astype(BF16)
        ag_ref[1] = jnp.dot(tg_scr[...], g2_ref[...], preferred_element_type=F32).astype(BF16)

    @pl.when(j == 0)
    def _():
        h = _rms(x_ref[...], g_ref[...])
        prev = _rms(xprev_ref[...], g_ref[...])[SUBLANE - 1:SUBLANE, :]
        prev = jnp.where(i % seq_tiles == 0, 0.0, prev)
        row = lax.broadcasted_iota(jnp.int32, h.shape, 0)
        dx = jnp.where(row == 0, prev, pltpu.roll(h, 1, 0)) - h
        def lerp(idx):
            return (h + dx * mu_ref[idx:idx + 1, :]).astype(BF16)

        for p in range(3):
            lerp_scr[p] = lerp(p)
            emit_rkv(p)
        tw_scr[...] = jnp.tanh(jnp.dot(lerp(3), l1_ref[:, :rw],
                                       preferred_element_type=F32)).astype(BF16)
        ta_scr[...] = jnp.dot(lerp(4), l1_ref[:, rw:rw + ra],
                              preferred_element_type=F32).astype(BF16)
        tg_scr[...] = jax.nn.sigmoid(jnp.dot(lerp(5), l1_ref[:, rw + ra:],
                                             preferred_element_type=F32)).astype(BF16)
        emit_lora_outputs()

    @pl.when(j > 0)
    def _():
        for p in range(3):
            emit_rkv(p)
        emit_lora_outputs()


def _pad_lora(wa, wb):
    rank = wa.shape[1]
    rp = -(-rank // LANE) * LANE
    return (jnp.pad(wa, ((0, 0), (0, rp - rank))).astype(BF16),
            jnp.pad(wb, ((0, rp - rank), (0, 0))).astype(BF16))


def _rwkv_proj(batch, x, norm_g, mu, w_rkv, w0, w1, w2, a0, a1, a2, g1, g2):
    m, d = x.shape
    tm = _pick_tile(m // batch, PROJ_ROWS, SUBLANE)
    tn = _pick_tile(d, 512, LANE)
    w1p, w2p = _pad_lora(w1, w2)
    a1p, a2p = _pad_lora(a1, a2)
    g1p, g2p = _pad_lora(g1, g2)
    rw, ra, rg = w1p.shape[1], a1p.shape[1], g1p.shape[1]
    l1 = jnp.concatenate([w1p, a1p, g1p], axis=1)
    w_bf = _tile_cast("cast_w_rkv", w_rkv, [(t * tn, tn) for t in range(d // tn)], tn)
    mu_ord = mu[jnp.array([0, 2, 3, 1, 4, 5])]
    col = lambda i, j: (0, j)

    def w_plane(p):
        return pl.BlockSpec((None, None, d, tn), lambda i, j: (p, j, 0, 0))

    def planes(n):
        return pl.BlockSpec((n, tm, tn), lambda i, j: (0, i, j))

    body = functools.partial(_rwkv_proj_body, (m // batch) // tm, (rw, ra))
    return pl.pallas_call(
        body,
        grid=(m // tm, d // tn),
        in_specs=[
            pl.BlockSpec((tm, d), lambda i, j: (i, 0)),
            pl.BlockSpec((SUBLANE, d), lambda i, j: (jnp.maximum(i * (tm // SUBLANE) - 1, 0), 0)),
            pl.BlockSpec((1, d), lambda i, j: (0, 0)),
            pl.BlockSpec((6, d), lambda i, j: (0, 0)),
            w_plane(0), w_plane(1), w_plane(2),
            pl.BlockSpec((d, rw + ra + rg), lambda i, j: (0, 0)),
            pl.BlockSpec((rw, tn), col),
            pl.BlockSpec((ra, tn), col),
            pl.BlockSpec((rg, tn), col),
            pl.BlockSpec((1, tn), col),
            pl.BlockSpec((1, tn), col),
        ],
        out_specs=[planes(3), pl.BlockSpec((tm, tn), lambda i, j: (i, j)), planes(2)],
        out_shape=[jax.ShapeDtypeStruct((3, m, d), BF16), jax.ShapeDtypeStruct((m, d), F32),
                   jax.ShapeDtypeStruct((2, m, d), BF16)],
        scratch_shapes=[pltpu.VMEM((3, tm, d), BF16), pltpu.VMEM((tm, rw), BF16),
                        pltpu.VMEM((tm, ra), BF16), pltpu.VMEM((tm, rg), BF16)],
        compiler_params=_params(("parallel", "arbitrary")),
        name="rwkv_proj",
    )(x, x, norm_g.reshape(1, d), mu_ord, w_bf, w_bf, w_bf, l1, w2p, a2p, g2p,
      w0.reshape(1, d), a0.reshape(1, d))


def _bf(x):
    return x.astype(BF16)


def _dot(a, b):
    return jnp.dot(_bf(a), _bf(b), preferred_element_type=F32)


def _dot_nt(a, b):
    return lax.dot_general(_bf(a), _bf(b), (((1,), (1,)), ((), ())), preferred_element_type=F32)


def _dot_tn(a, b):
    return lax.dot_general(_bf(a), _bf(b), (((0,), (0,)), ((), ())), preferred_element_type=F32)


def _scan_body(n_streams, nb, cps, r_ref, ld_ref, k_ref, v_ref, al_ref, g_ref, kk_w_ref, ka_w_ref,
               rk_w_ref, lnw_ref, lnb_ref, o_ref, s_ref):
    c = SCAN_CHUNK
    w = HEADS_PER_GROUP * RWKV_HEAD
    wc = HEADS_PER_GROUP * c

    @pl.when(pl.program_id(2) == 0)
    def _():
        s_ref[...] = jnp.zeros_like(s_ref)

    tri3 = jnp.where(lax.broadcasted_iota(jnp.int32, (c, 3 * c), 0)
                     >= lax.broadcasted_iota(jnp.int32, (c, 3 * c), 1) % c, 1.0, 0.0).astype(BF16)
    bd_r = lax.broadcasted_iota(jnp.int32, (w, w), 0) // RWKV_HEAD
    bd_c = lax.broadcasted_iota(jnp.int32, (w, w), 1) // RWKV_HEAD
    head_mask = bd_r == bd_c
    head_ones = jnp.where(head_mask, 1.0, 0.0).astype(BF16)
    t_idx = lax.broadcasted_iota(jnp.int32, (c, wc), 0)
    s_idx = lax.broadcasted_iota(jnp.int32, (c, wc), 1) % c
    strict = s_idx < t_idx
    incl = s_idx <= t_idx
    eye = jnp.where(s_idx == t_idx, 1.0, 0.0)
    base_level = 3
    base_blocks = (t_idx >> base_level) == (s_idx >> base_level)
    join = {lv: (((t_idx >> lv) & 1) == 1) & ((s_idx >> lv) == (t_idx >> lv) - 1)
            for lv in range(base_level, c.bit_length() - 1)}

    def head_sum(x):
        return jnp.dot(_bf(x), head_ones, preferred_element_type=F32)

    def block_diag(x):
        xb = _bf(x)
        return jnp.where(head_mask, jnp.concatenate([xb] * HEADS_PER_GROUP, axis=0),
                         jnp.zeros((), BF16))

    def stream(bi, si, cj):
        st = bi * n_streams + si
        ln = pl.ds(si * w, w)
        rows = pl.ds(cj * c, c)
        r = r_ref[bi, rows, ln].astype(F32)
        ld = ld_ref[bi, rows, ln]
        k = k_ref[bi, rows, ln].astype(F32)
        v = v_ref[bi, rows, ln].astype(F32)
        al = al_ref[bi, rows, ln].astype(F32)

        ld_hi = _bf(ld)
        rest = ld - ld_hi.astype(F32)
        ld_mid = _bf(rest)
        ld_lo = _bf(rest - ld_mid.astype(F32))
        cum = jnp.dot(tri3, jnp.concatenate([ld_hi, ld_mid, ld_lo], axis=0),
                      preferred_element_type=F32)
        kk = k * kk_w_ref[:, ln]
        ss = head_sum(kk * kk)
        yield
        p_incl = jnp.exp(cum)
        p_prev = jnp.exp(cum - ld)
        p_inv = jnp.exp(-cum)
        p_last = p_incl[c - 1:c, :]
        kk = kk / jnp.maximum(jnp.sqrt(ss), 1e-12)
        kmod = k * (1.0 + (al - 1.0) * ka_w_ref[:, ln])

        a_t = -kk * p_prev
        b_t = kk * al * p_inv
        k_t = kmod * p_inv
        r_t = r * p_incl

        ar = _bf(jnp.concatenate([a_t, r_t], axis=0))
        m_b = _dot_nt(ar, block_diag(b_t))
        m_k = _dot_nt(ar, block_diag(k_t))
        s0 = s_ref[st]
        w1 = _dot_nt(ar, s0)
        yield
        a_ab = jnp.where(strict, m_b[:c], 0.0)
        a_rb = jnp.where(incl, m_b[c:], 0.0)
        a_ak = jnp.where(strict, m_k[:c], 0.0)
        a_rk = jnp.where(incl, m_k[c:], 0.0)
        w2 = _dot(jnp.concatenate([a_ak, a_rk], axis=0), block_diag(v))

        pw = jnp.where(base_blocks, a_ab, 0.0)
        t_inv = eye
        for step in range(base_level):
            if step < base_level - 1:
                pw_b = _bf(pw)
                z = _dot(jnp.concatenate([pw_b, _bf(t_inv)], axis=0), block_diag(pw_b))
                yield
                pw, t_inv = z[:c], t_inv + z[c:]
            else:
                z = _dot(t_inv, block_diag(pw))
                yield
                t_inv = t_inv + z
        n_bd = block_diag(a_ab)
        level = base_level
        while (1 << level) < c:
            t_b = _bf(t_inv)
            tn = _dot(t_b, n_bd)
            yield
            x = _dot(tn, block_diag(t_b))
            yield
            t_inv = t_inv + jnp.where(join[level], x, 0.0)
            level += 1

        u = _dot(t_inv, block_diag(w1[:c] + w2[:c]))
        yield
        y = w1[c:] + w2[c:] + _dot(a_rb, block_diag(u))

        uv = jnp.concatenate([u, v], axis=0)
        bk = jnp.concatenate([b_t, k_t], axis=0) * p_last
        s_new = _dot_tn(uv, bk)
        yield
        s_ref[st] = jnp.where(head_mask, s0 * p_last + s_new, 0.0)

        inv_n = 1.0 / RWKV_HEAD
        sums = head_sum(jnp.concatenate([y, r * kmod * rk_w_ref[:, ln]], axis=0))
        yield
        dlt = y - sums[:c] * inv_n
        bsum = sums[c:]
        var = head_sum(dlt * dlt) * inv_n
        yield
        yn = dlt * lax.rsqrt(var + GN_EPS) * lnw_ref[:, ln] + lnb_ref[:, ln]
        bonus = bsum * v
        o_ref[bi, rows, ln] = ((yn + bonus) * g_ref[bi, rows, ln]).astype(o_ref.dtype)

    def chunks(bi, si):
        for cj in range(cps):
            yield from stream(bi, si, cj)

    _round_robin([chunks(bi, si) for bi in range(nb) for si in range(n_streams)])


def _rwkv_scan(batch, rkv, ld, ag, k_k, k_a, r_k, ln_w, ln_b):
    _, m, d = rkv.shape
    t = m // batch
    c = SCAN_CHUNK
    w = HEADS_PER_GROUP * RWKV_HEAD
    assert d % w == 0 and t % c == 0 and c == RWKV_HEAD
    n_streams = _pick_tile(d // w, SCAN_STREAMS, 1)
    wb = n_streams * w
    cps = _pick_tile(t // c, SCAN_CHUNKS_PER_STEP, 1)
    nc = t // (c * cps)
    nb = _pick_tile(batch, SCAN_BATCH_ROWS, 1)
    tok = pl.BlockSpec((nb, cps * c, wb), lambda b, gi, ci: (b, ci, gi))
    par = pl.BlockSpec((1, wb), lambda b, gi, ci: (0, gi))

    def plane(p):
        return pl.BlockSpec((None, nb, cps * c, wb), lambda b, gi, ci: (p, b, ci, gi))

    rkv4 = rkv.reshape(3, batch, t, d)
    ag4 = ag.reshape(2, batch, t, d)
    return pl.pallas_call(
        functools.partial(_scan_body, n_streams, nb, cps),
        grid=(batch // nb, d // wb, nc),
        in_specs=[plane(0), tok, plane(1), plane(2), plane(0), plane(1)] + [par] * 5,
        out_specs=tok,
        out_shape=jax.ShapeDtypeStruct((batch, t, d), BF16),
        scratch_shapes=[pltpu.VMEM((nb * n_streams, w, w), F32)],
        compiler_params=_params(("parallel", "parallel", "arbitrary")),
        name="rwkv_scan",
    )(rkv4, ld.reshape(batch, t, d), rkv4, rkv4, ag4, ag4,
      *[p.reshape(1, d) for p in (k_k, k_a, r_k, ln_w, ln_b)]).reshape(m, d)


def _rwkv_mix(batch, x, norm_g, mu, w_rkv, w0, w1, w2, a0, a1, a2, g1, g2,
              k_k, k_a, r_k, ln_w, ln_b, w_o):
    rkv, ld, ag = _rwkv_proj(batch, x, norm_g, mu, w_rkv, w0, w1, w2, a0, a1, a2, g1, g2)
    yg = _rwkv_scan(batch, rkv, ld, ag, k_k, k_a, r_k.reshape(-1), ln_w, ln_b)
    return _simple_mm("rwkv_out", [_rows(yg)], lambda t: t, w_o.astype(BF16), [_epi_tile(x)],
                      lambda acc, x_t: (x_t + acc,), tn_target=x.shape[1])


def _attn_body(tq, tk, nh, nq, small_ref, q_ref, k_ref, v_ref, o_ref, v1_ref):
    qi = pl.program_id(2)
    dv = v_ref.shape[-1]

    def key_rows(j):
        start = j * tk if isinstance(j, int) else pl.multiple_of(j * tk, tk)
        return pl.ds(start, tk)

    def scores(h, j, masked):
        kb = k_ref[h, key_rows(j), :]
        s = lax.dot_general(q_ref[h], kb, (((1,), (1,)), ((), ())), preferred_element_type=F32)
        if masked:
            q_chunk = (qi * tq + lax.broadcasted_iota(jnp.int32, (tq, tk), 0)) // ATTN_CHUNK
            k_chunk = (j * tk + lax.broadcasted_iota(jnp.int32, (tq, tk), 1)) // ATTN_CHUNK
            s = jnp.where(k_chunk <= q_chunk, s, jnp.finfo(F32).min)
        return s

    def values(h, j):
        return v_ref[h, key_rows(j), :]

    def all_heads(make_gen):
        outs = [[] for _ in range(nh)]
        _round_robin([make_gen(h, outs[h]) for h in range(nh)])
        return tuple(o[0] for o in outs)

    def sweep(step, init, diag_step=None):
        carries = lax.fori_loop(
            0, qi, lambda j, cr: all_heads(lambda h, out: step(h, j, cr[h], False, out)), init)
        if diag_step is not None:
            return all_heads(lambda h, out: diag_step(h, carries[h], out))
        return all_heads(lambda h, out: step(h, qi, carries[h], True, out))

    def online_step(h, j, carry, masked, out):
        m_i, l_i, acc = carry
        s = scores(h, j, masked)
        yield
        m_new = jnp.maximum(m_i, jnp.max(s, axis=-1, keepdims=True))
        p = jnp.exp(s - m_new)
        alpha = jnp.exp(m_i - m_new)
        l_new = alpha * l_i + jnp.sum(p, axis=-1, keepdims=True)
        pv = jnp.dot(p.astype(BF16), values(h, j), preferred_element_type=F32)
        yield
        out.append((m_new, l_new, alpha * acc + pv))

    def plain_step(h, j, acc, masked, out):
        s = scores(h, j, masked)
        yield
        p = jnp.exp(s).astype(BF16)
        v1 = v1_ref[h, key_rows(j), :]
        pv = jnp.dot(p, v1, preferred_element_type=F32)
        yield
        out.append(acc + pv)

    def plain_diag(h, acc, out):
        half = tk // 2
        base = pl.multiple_of(qi * tk, tk)
        visible = (lax.broadcasted_iota(jnp.int32, (tq, half), 1) // ATTN_CHUNK
                   <= lax.broadcasted_iota(jnp.int32, (tq, half), 0) // ATTN_CHUNK)

        def weights(q_rows, key0, mask):
            kb = k_ref[h, pl.ds(key0, half), :]
            s = lax.dot_general(q_rows, kb, (((1,), (1,)), ((), ())), preferred_element_type=F32)
            return jnp.where(mask, jnp.exp(s), 0.0).astype(BF16)

        p_a = weights(q_ref[h], base, visible)
        yield
        pv_a = jnp.dot(p_a, v1_ref[h, pl.ds(base, half), :], preferred_element_type=F32)
        p_b = weights(q_ref[h, half:, :], base + half, visible[:half])
        yield
        pv_b = jnp.dot(p_b, v1_ref[h, pl.ds(base + half, half), :], preferred_element_type=F32)
        yield
        out.append(acc + jnp.concatenate([pv_a[:half], pv_a[half:] + pv_b], axis=0))

    @pl.when(small_ref[0] == 1)
    def _():
        @pl.when(qi == 0)
        def _():
            v1_ref[:, :, :dv] = v_ref[...]
            v1_ref[:, :, dv:] = jnp.ones(v_ref.shape, BF16)

        def unrolled(n_full):
            accs = tuple(jnp.zeros((tq, 2 * dv), F32) for _ in range(nh))
            for j in range(n_full):
                accs = all_heads(lambda h, out, j=j: plain_step(h, j, accs[h], False, out))
            accs = all_heads(lambda h, out: plain_diag(h, accs[h], out))
            for h, acc in enumerate(accs):
                o_ref[:, h * dv:(h + 1) * dv] = (acc[:, :dv] / acc[:, dv:]).astype(o_ref.dtype)

        for n_full in range(nq):
            pl.when(qi == n_full)(functools.partial(unrolled, n_full))

    @pl.when(small_ref[0] != 1)
    def _():
        init = tuple((jnp.full((tq, 1), -jnp.inf, F32), jnp.zeros((tq, 1), F32),
                      jnp.zeros((tq, dv), F32)) for _ in range(nh))
        for h, (_, l_i, acc) in enumerate(sweep(online_step, init)):
            o_ref[:, h * dv:(h + 1) * dv] = (acc / l_i).astype(o_ref.dtype)


def _attention(batch, q, k, v, small_scores):
    heads, m, dqk = q.shape
    dv = v.shape[-1]
    t = m // batch
    tq = _pick_tile(t, ATTN_Q_TILE, 2 * ATTN_CHUNK)
    assert tq % (2 * ATTN_CHUNK) == 0
    nq = t // tq
    nh = _pick_tile(heads, ATTN_HEADS_PER_STEP, 1)
    return pl.pallas_call(
        functools.partial(_attn_body, tq, tq, nh, nq),
        grid=(batch, heads // nh, nq),
        in_specs=[
            pl.BlockSpec(memory_space=pltpu.SMEM),
            pl.BlockSpec((nh, tq, dqk), lambda b, h, i: (h, b * nq + i, 0)),
            pl.BlockSpec((nh, t, dqk), lambda b, h, i: (h, b, 0)),
            pl.BlockSpec((nh, t, dv), lambda b, h, i: (h, b, 0)),
        ],
        out_specs=pl.BlockSpec((tq, nh * dv), lambda b, h, i: (b * nq + i, h)),
        out_shape=jax.ShapeDtypeStruct((m, heads * dv), BF16),
        scratch_shapes=[pltpu.VMEM((nh, t, 2 * dv), BF16)],
        compiler_params=_params(("parallel", "parallel", "arbitrary")),
        name="mla_attention",
    )(small_scores, q, k, v)


def _mla_mix(batch, x, positions, norm_g, w_down, q_a_norm, kv_a_norm, w_uq, w_ukv,
             q_norm, k_norm, w_o):
    m, d = x.shape
    q_lora = q_a_norm.shape[0]
    kv_lora = kv_a_norm.shape[0]
    heads, qk_dim = w_uq.shape[1], w_uq.shape[2]
    rope = w_down.shape[1] - q_lora - kv_lora
    nope = qk_dim - rope
    dv = w_ukv.shape[2] - nope
    half = rope // 2
    assert nope == LANE and dv == LANE and rope <= LANE and q_lora == kv_lora and q_lora % LANE == 0
    scale = 1.0 / math.sqrt(qk_dim)

    perm = jnp.concatenate([jnp.arange(half, rope), jnp.arange(0, half)])
    lane_pad = lambda a: jnp.pad(a, [(0, 0)] * (a.ndim - 1) + [(0, LANE - a.shape[-1])])

    inv_freq = ROPE_THETA ** (-jnp.arange(0, rope, 2, dtype=F32) / rope)
    freq_t = lane_pad(jnp.concatenate([inv_freq, inv_freq]))
    keep_t = lane_pad(jnp.ones((rope,), F32))
    sign_t = lane_pad(jnp.concatenate([-jnp.ones((half,), F32), jnp.ones((half,), F32)]))
    ang = positions.reshape(m, 1).astype(F32) * freq_t
    cos_t = jnp.cos(ang) * keep_t
    sin_t = jnp.sin(ang) * sign_t

    w_kpe = w_down[:, q_lora + kv_lora:]
    w_down_x = jnp.concatenate(
        [w_down[:, :q_lora + kv_lora], lane_pad(w_kpe), lane_pad(w_kpe[:, perm])], axis=1).astype(BF16)
    c = _simple_mm("mla_down", [_rows(x), _vec(norm_g)], _rms, w_down_x, [], lambda acc: (acc,),
                   tn_target=w_down_x.shape[1])
    qb = q_lora // q_lora
    tm = _pick_tile(m, PROJ_ROWS, SUBLANE)

    def norm_rope(body, pe, rot, g_body, g_pe, g_rot, cos_b, sin_b, extra_scale):
        ss = jnp.sum(body * body, axis=-1, keepdims=True) + jnp.sum(pe * pe, axis=-1, keepdims=True)
        s = lax.rsqrt(ss / qk_dim + RMS_EPS) * extra_scale
        return jnp.concatenate([body * s * g_body, (pe * g_pe * cos_b + rot * g_rot * sin_b) * s], axis=-1)

    w_q_pe = w_uq[:, :, nope:]
    w_q_x = jnp.concatenate([w_uq[:, :, :nope], lane_pad(w_q_pe), lane_pad(w_q_pe[:, :, perm])],
                            axis=-1).reshape(q_lora, heads * 3 * LANE).astype(BF16)
    gq = jnp.concatenate([q_norm[:nope], lane_pad(q_norm[nope:]), lane_pad(q_norm[nope:][perm])]).reshape(1, -1)

    hb = _pick_tile(heads, MLA_HEADS_PER_STEP, 1)

    def q_epi(acc, g_t, cos_b, sin_b):
        per_head = []
        for hh in range(hb):
            a = acc[:, hh * 3 * LANE:(hh + 1) * 3 * LANE]
            per_head.append(norm_rope(a[:, :LANE], a[:, LANE:2 * LANE], a[:, 2 * LANE:], g_t[:, :LANE],
                                      g_t[:, LANE:2 * LANE], g_t[:, 2 * LANE:], cos_b, sin_b, scale))
        return (jnp.stack(per_head, axis=0),)

    q = _mm("mla_q", m,
            [(c, _row_spec(tm, q_lora, 0)), (q_a_norm.reshape(1, -1), _vec_spec(q_lora))], _rms, w_q_x,
            [(gq, _vec_spec(3 * LANE)), (cos_t, _row_spec(tm, LANE)), (sin_t, _row_spec(tm, LANE))], q_epi,
            [(jax.ShapeDtypeStruct((heads, m, 2 * LANE), BF16),
              pl.BlockSpec((hb, tm, 2 * LANE), lambda i, j: (j, i, 0)))],
            tm=tm, tn=hb * 3 * LANE)[0]

    w_kv_x = w_ukv.reshape(kv_lora, heads * (nope + dv)).astype(BF16)
    gk = jnp.concatenate([k_norm[:nope], lane_pad(k_norm[nope:]), lane_pad(k_norm[nope:][perm])]).reshape(1, -1)
    pe_block = (q_lora + kv_lora) // LANE

    def kv_epi(acc, g_t, pe, rot, cos_b, sin_b):
        k_out, v_out = [], []
        for hh in range(hb):
            a = acc[:, hh * (nope + dv):(hh + 1) * (nope + dv)]
            k_out.append(norm_rope(a[:, :nope], pe, rot, g_t[:, :LANE], g_t[:, LANE:2 * LANE],
                                   g_t[:, 2 * LANE:], cos_b, sin_b, 1.0))
            v_out.append(a[:, nope:])
        return jnp.stack(k_out, axis=0), jnp.stack(v_out, axis=0)

    k, v = _mm("mla_kv", m,
               [(c, _row_spec(tm, kv_lora, qb)), (kv_a_norm.reshape(1, -1), _vec_spec(kv_lora))], _rms, w_kv_x,
               [(gk, _vec_spec(3 * LANE)), (c, _row_spec(tm, LANE, pe_block)),
                (c, _row_spec(tm, LANE, pe_block + 1)), (cos_t, _row_spec(tm, LANE)),
                (sin_t, _row_spec(tm, LANE))], kv_epi,
               [(jax.ShapeDtypeStruct((heads, m, 2 * LANE), BF16),
                 pl.BlockSpec((hb, tm, 2 * LANE), lambda i, j: (j, i, 0))),
                (jax.ShapeDtypeStruct((heads, m, dv), BF16),
                 pl.BlockSpec((hb, tm, dv), lambda i, j: (j, i, 0)))],
               tm=tm, tn=hb * (nope + dv))

    score_bound = scale * qk_dim * jnp.max(jnp.abs(q_norm)) * jnp.max(jnp.abs(k_norm))
    small_scores = (score_bound <= MAX_PLAIN_SCORE).astype(jnp.int32).reshape(1)
    o = _attention(batch, q, k, v, small_scores)
    return _simple_mm("mla_out", [_rows(o)], lambda t: t, w_o.reshape(heads * dv, d).astype(BF16),
                      [_epi_tile(x)], lambda acc, x_t: (x_t + acc,), tn_target=d)


def kernel(x, positions, ffn_norm, ffn_w13, ffn_w2, mix_norm, rwkv_mu, rwkv_w_rkv, rwkv_w0, rwkv_w1, rwkv_w2, rwkv_a0, rwkv_a1, rwkv_a2, rwkv_g1, rwkv_g2, rwkv_k_k, rwkv_k_a, rwkv_r_k, rwkv_ln_w, rwkv_ln_b, rwkv_w_o, mla_w_down, mla_q_a_norm, mla_kv_a_norm, mla_w_uq, mla_w_ukv, mla_q_norm, mla_k_norm, mla_w_o):
    batch, seq, d = x.shape
    depth = ffn_norm.shape[0]
    n_mixers = 2
    xf = x.reshape(batch * seq, d)
    d_ff = ffn_w2.shape[2]
    w13p, w2p = _ffn_weights(ffn_w13.reshape((2 * depth,) + ffn_w13.shape[2:]),
                             ffn_w2.reshape((2 * depth,) + ffn_w2.shape[2:]))
    for i in range(depth):
        j = i // n_mixers
        xf = _ffn(xf, ffn_norm[i, 0], w13p, w2p, 2 * i, d_ff)
        if i % n_mixers == 0:
            xf = _rwkv_mix(batch, xf, mix_norm[i], rwkv_mu[j], rwkv_w_rkv[j], rwkv_w0[j], rwkv_w1[j],
                           rwkv_w2[j], rwkv_a0[j], rwkv_a1[j], rwkv_a2[j], rwkv_g1[j], rwkv_g2[j],
                           rwkv_k_k[j], rwkv_k_a[j], rwkv_r_k[j], rwkv_ln_w[j], rwkv_ln_b[j],
                           rwkv_w_o[j])
        else:
            xf = _mla_mix(batch, xf, positions, mix_norm[i], mla_w_down[j], mla_q_a_norm[j],
                          mla_kv_a_norm[j], mla_w_uq[j], mla_w_ukv[j], mla_q_norm[j], mla_k_norm[j],
                          mla_w_o[j])
        xf = _ffn(xf, ffn_norm[i, 1], w13p, w2p, 2 * i + 1, d_ff)
    return xf.reshape(batch, seq, d)
```

```python
import functools
import math

import jax
import jax.numpy as jnp
from jax import lax
from jax.experimental import pallas as pl
from jax.experimental.pallas import tpu as pltpu

F32 = jnp.float32
BF16 = jnp.bfloat16

RMS_EPS = 1e-6
GN_EPS = 64e-5
ROPE_THETA = 10000.0
ATTN_CHUNK = 64
RWKV_HEAD = 64

LANE = 128
SUBLANE = 8
VMEM_LIMIT_BYTES = 56 * 1024 * 1024

SCAN_CHUNK = 64
HEADS_PER_GROUP = 4
SCAN_STREAMS = 8
SCAN_BATCH_ROWS = 1
SCAN_CHUNKS_PER_STEP = 4
ATTN_Q_TILE = 512
ATTN_HEADS_PER_STEP = 4
MAX_PLAIN_SCORE = 30.0
FFN_ROWS = 1024
FFN_TILE = 512
MLA_HEADS_PER_STEP = 16
PROJ_ROWS = 512
CAST_ROWS = 128


def _params(semantics):
    return pltpu.CompilerParams(dimension_semantics=semantics,
                                vmem_limit_bytes=VMEM_LIMIT_BYTES)


def _pick_tile(n, target, quantum):
    if n <= target:
        return n
    t = (target // quantum) * quantum
    while t > quantum and n % t:
        t -= quantum
    assert n % t == 0, (n, target, quantum)
    return t


def _round_robin(gens):
    live = list(gens)
    while live:
        live = [g for g in live if next(g, True) is None]


def _rms(x, g):
    return x * lax.rsqrt(jnp.mean(x * x, axis=-1, keepdims=True) + RMS_EPS) * g


def _ffn_body(nj, x_ref, g_ref, w13_ref, w2_ref, o_ref, xn_ref):
    j = pl.program_id(1)
    tf = w2_ref.shape[0]

    def tile_sum():
        gu = jnp.dot(xn_ref[...], w13_ref[...], preferred_element_type=F32)
        gate, up = gu[:, :tf], gu[:, tf:]
        act = (gate * jax.nn.sigmoid(gate) * up).astype(BF16)
        return jnp.dot(act, w2_ref[...], preferred_element_type=F32)

    def normalize():
        xn_ref[...] = _rms(x_ref[...], g_ref[...]).astype(BF16)

    if nj == 1:
        normalize()
        o_ref[...] = x_ref[...] + 0.5 * tile_sum()
    else:
        @pl.when(j == 0)
        def _():
            normalize()
            o_ref[...] = tile_sum()

        if nj > 2:
            @pl.when((j > 0) & (j < nj - 1))
            def _():
                o_ref[...] += tile_sum()

        @pl.when(j == nj - 1)
        def _():
            o_ref[...] = x_ref[...] + 0.5 * (o_ref[...] + tile_sum())


def _ffn_tile(f):
    return min(FFN_TILE, -(-f // LANE) * LANE)


def _tile_cast_body(tiles, x_ref, o_ref):
    tw = o_ref.shape[-1]
    for t, segments in enumerate(tiles):
        filled = 0
        for src, width, dst in segments:
            if dst > filled:
                o_ref[t, :, filled:dst] = jnp.zeros((o_ref.shape[1], dst - filled), BF16)
            o_ref[t, :, dst:dst + width] = x_ref[:, src:src + width].astype(BF16)
            filled = dst + width
        if filled < tw:
            o_ref[t, :, filled:] = jnp.zeros((o_ref.shape[1], tw - filled), BF16)


def _tile_cast(name, w, tiles, tw):
    n, rows, cols = w.shape
    tr = _pick_tile(rows, CAST_ROWS, SUBLANE)
    return pl.pallas_call(
        functools.partial(_tile_cast_body, tuple(tuple(t) for t in tiles)),
        grid=(n, rows // tr),
        in_specs=[pl.BlockSpec((None, tr, cols), lambda l, i: (l, i, 0))],
        out_specs=pl.BlockSpec((None, len(tiles), tr, tw), lambda l, i: (l, 0, i, 0)),
        out_shape=jax.ShapeDtypeStruct((n, len(tiles), rows, tw), BF16),
        compiler_params=_params(("parallel", "parallel")),
        name=name,
    )(w)


def _cast_w2_body(f, x_ref, o_ref):
    tr = o_ref.shape[0]
    row = pl.program_id(1) * tr + lax.broadcasted_iota(jnp.int32, o_ref.shape, 0)
    o_ref[...] = jnp.where(row < f, x_ref[...], 0.0).astype(BF16)


def _ffn_weights(w13, w2):
    n, d, f2 = w13.shape
    f = f2 // 2
    assert f % LANE == 0
    tf = _ffn_tile(f)
    fp = -(-f // tf) * tf
    tiles = []
    for t in range(fp // tf):
        width = min(tf, f - t * tf)
        tiles.append([(t * tf, width, 0), (f + t * tf, width, tf)])
    w13p = _tile_cast("cast_w13", w13, tiles, 2 * tf)
    w2p = pl.pallas_call(
        functools.partial(_cast_w2_body, f),
        grid=(n, fp // tf),
        in_specs=[pl.BlockSpec((None, tf, d), lambda l, i: (l, i, 0))],
        out_specs=pl.BlockSpec((None, tf, d), lambda l, i: (l, i, 0)),
        out_shape=jax.ShapeDtypeStruct((n, fp, d), BF16),
        compiler_params=_params(("parallel", "parallel")),
        name="cast_w2",
    )(w2)
    return w13p, w2p


def _ffn(x, g, w13p, w2p, layer, f):
    m, d = x.shape
    fp = w2p.shape[1]
    tf = _ffn_tile(f)
    nj = fp // tf
    tm = _pick_tile(m, FFN_ROWS, SUBLANE)
    return pl.pallas_call(
        functools.partial(_ffn_body, nj),
        grid=(m // tm, nj),
        in_specs=[
            pl.BlockSpec((tm, d), lambda i, j: (i, 0)),
            pl.BlockSpec((1, d), lambda i, j: (0, 0)),
            pl.BlockSpec((None, None, d, 2 * tf), lambda i, j: (layer, j, 0, 0)),
            pl.BlockSpec((None, tf, d), lambda i, j: (layer, j, 0)),
        ],
        out_specs=pl.BlockSpec((tm, d), lambda i, j: (i, 0)),
        out_shape=jax.ShapeDtypeStruct((m, d), F32),
        scratch_shapes=[pltpu.VMEM((tm, d), BF16)],
        compiler_params=_params(("parallel", "arbitrary")),
        name="ffn",
    )(x, g.reshape(1, d), w13p, w2p)


def _mm_body(single_col_tile, n_lhs, n_epi, n_out, lhs_fn, epi_fn, *refs):
    lhs_refs = refs[:n_lhs]
    w_ref = refs[n_lhs]
    epi_refs = refs[n_lhs + 1:n_lhs + 1 + n_epi]
    out_refs = refs[n_lhs + 1 + n_epi:n_lhs + 1 + n_epi + n_out]

    if single_col_tile:
        lhs = lhs_fn(*[r[...] for r in lhs_refs]).astype(BF16)
    else:
        lhs_scratch = refs[-1]

        @pl.when(pl.program_id(1) == 0)
        def _():
            lhs_scratch[...] = lhs_fn(*[r[...] for r in lhs_refs]).astype(BF16)

        lhs = lhs_scratch[...]
    acc = jnp.dot(lhs, w_ref[...], preferred_element_type=F32)
    outs = epi_fn(acc, *[r[...] for r in epi_refs])
    for o_ref, o in zip(out_refs, outs):
        o_ref[...] = o.astype(o_ref.dtype)


def _mm(name, m, lhs, lhs_fn, w, epi, epi_fn, outs, *, tm, tn):
    k, n = w.shape
    single = n == tn
    body = functools.partial(_mm_body, single, len(lhs), len(epi), len(outs), lhs_fn, epi_fn)
    return pl.pallas_call(
        body,
        grid=(m // tm, n // tn),
        in_specs=[s for _, s in lhs] + [pl.BlockSpec((k, tn), lambda i, j: (0, j))]
        + [s for _, s in epi],
        out_specs=[s for _, s in outs],
        out_shape=[s for s, _ in outs],
        scratch_shapes=[] if single else [pltpu.VMEM((tm, k), BF16)],
        compiler_params=_params(("parallel", "arbitrary")),
        name=name,
    )(*[a for a, _ in lhs], w, *[a for a, _ in epi])


def _row_spec(tm, width, col_block=0):
    return pl.BlockSpec((tm, width), lambda i, j: (i, col_block))


def _tile_spec(tm, tn):
    return pl.BlockSpec((tm, tn), lambda i, j: (i, j))


def _vec_spec(width):
    return pl.BlockSpec((1, width), lambda i, j: (0, 0))


def _colvec_spec(tn):
    return pl.BlockSpec((1, tn), lambda i, j: (0, j))


def _simple_mm(name, lhs, lhs_fn, w, epi, epi_fn, out_dtype=F32, tm_target=PROJ_ROWS, tn_target=1024):
    m = lhs[0][0].shape[0]
    n = w.shape[1]
    tm = _pick_tile(m, tm_target, 8)
    tn = _pick_tile(n, tn_target, LANE)
    lhs = [(a, spec(tm)) for a, spec in lhs]
    epi = [(a, spec(tm, tn)) for a, spec in epi]
    out = (jax.ShapeDtypeStruct((m, n), out_dtype), _tile_spec(tm, tn))
    return _mm(name, m, lhs, lhs_fn, w, epi, epi_fn, [out], tm=tm, tn=tn)[0]


def _rows(a):
    return a, lambda tm: _row_spec(tm, a.shape[1])


def _vec(a):
    a = a.reshape(1, -1)
    return a, lambda tm: _vec_spec(a.shape[1])


def _epi_tile(a):
    return a, lambda tm, tn: _tile_spec(tm, tn)


def _epi_colvec(a):
    return a.reshape(1, -1), lambda tm, tn: _colvec_spec(tn)


def _rwkv_proj_body(seq_tiles, ranks, x_ref, xprev_ref, g_ref, mu_ref, wr_ref, wk_ref, wv_ref, l1_ref,
                    w2_ref, a2_ref, g2_ref, w0_ref, a0_ref, rkv_ref, ld_ref, ag_ref,
                    lerp_scr, tw_scr, ta_scr, tg_scr):
    i = pl.program_id(0)
    j = pl.program_id(1)
    rw, ra = ranks
    w_refs = (wr_ref, wk_ref, wv_ref)

    def emit_rkv(p):
        rkv_ref[p] = jnp.dot(lerp_scr[p], w_refs[p][...], preferred_element_type=F32).astype(BF16)

    def emit_lora_outputs():
        z = -(w0_ref[...] + jnp.dot(tw_scr[...], w2_ref[...], preferred_element_type=F32))
        softplus = jnp.maximum(z, 0.0) + jnp.log1p(jnp.exp(-jnp.abs(z)))
        ld_ref[...] = -jnp.exp(-softplus - 0.5)
        ag_ref[0] = jax.nn.sigmoid(
            a0_ref[...] + jnp.dot(ta_scr[...], a2_ref[...], preferred_element_type=F32)).astype(BF16)
        ag_ref[1] = jnp.dot(tg_scr[...], g2_ref[...], preferred_element_type=F32).astype(BF16)

    @pl.when(j == 0)
    def _():
        h = _rms(x_ref[...], g_ref[...])
        prev = _rms(xprev_ref[...], g_ref[...])[SUBLANE - 1:SUBLANE, :]
        prev = jnp.where(i % seq_tiles == 0, 0.0, prev)
        row = lax.broadcasted_iota(jnp.int32, h.shape, 0)
        dx = jnp.where(row == 0, prev, pltpu.roll(h, 1, 0)) - h
        def lerp(idx):
            return (h + dx * mu_ref[idx:idx + 1, :]).astype(BF16)

        for p in range(3):
            lerp_scr[p] = lerp(p)
            emit_rkv(p)
        tw_scr[...] = jnp.tanh(jnp.dot(lerp(3), l1_ref[:, :rw],
                                       preferred_element_type=F32)).astype(BF16)
        ta_scr[...] = jnp.dot(lerp(4), l1_ref[:, rw:rw + ra],
                              preferred_element_type=F32).astype(BF16)
        tg_scr[...] = jax.nn.sigmoid(jnp.dot(lerp(5), l1_ref[:, rw + ra:],
                                             preferred_element_type=F32)).astype(BF16)
        emit_lora_outputs()

    @pl.when(j > 0)
    def _():
        for p in range(3):
            emit_rkv(p)
        emit_lora_outputs()


def _pad_lora(wa, wb):
    rank = wa.shape[1]
    rp = -(-rank // LANE) * LANE
    return (jnp.pad(wa, ((0, 0), (0, rp - rank))).astype(BF16),
            jnp.pad(wb, ((0, rp - rank), (0, 0))).astype(BF16))


def _rwkv_proj(batch, x, norm_g, mu, w_rkv, w0, w1, w2, a0, a1, a2, g1, g2):
    m, d = x.shape
    tm = _pick_tile(m // batch, PROJ_ROWS, SUBLANE)
    tn = _pick_tile(d, 512, LANE)
    w1p, w2p = _pad_lora(w1, w2)
    a1p, a2p = _pad_lora(a1, a2)
    g1p, g2p = _pad_lora(g1, g2)
    rw, ra, rg = w1p.shape[1], a1p.shape[1], g1p.shape[1]
    l1 = jnp.concatenate([w1p, a1p, g1p], axis=1)
    w_bf = _tile_cast("cast_w_rkv", w_rkv, [[(t * tn, tn, 0)] for t in range(d // tn)], tn)
    mu_ord = mu[jnp.array([0, 2, 3, 1, 4, 5])]
    col = lambda i, j: (0, j)

    def w_plane(p):
        return pl.BlockSpec((None, None, d, tn), lambda i, j: (p, j, 0, 0))

    def planes(n):
        return pl.BlockSpec((n, tm, tn), lambda i, j: (0, i, j))

    body = functools.partial(_rwkv_proj_body, (m // batch) // tm, (rw, ra))
    return pl.pallas_call(
        body,
        grid=(m // tm, d // tn),
        in_specs=[
            pl.BlockSpec((tm, d), lambda i, j: (i, 0)),
            pl.BlockSpec((SUBLANE, d), lambda i, j: (jnp.maximum(i * (tm // SUBLANE) - 1, 0), 0)),
            pl.BlockSpec((1, d), lambda i, j: (0, 0)),
            pl.BlockSpec((6, d), lambda i, j: (0, 0)),
            w_plane(0), w_plane(1), w_plane(2),
            pl.BlockSpec((d, rw + ra + rg), lambda i, j: (0, 0)),
            pl.BlockSpec((rw, tn), col),
            pl.BlockSpec((ra, tn), col),
            pl.BlockSpec((rg, tn), col),
            pl.BlockSpec((1, tn), col),
            pl.BlockSpec((1, tn), col),
        ],
        out_specs=[planes(3), pl.BlockSpec((tm, tn), lambda i, j: (i, j)), planes(2)],
        out_shape=[jax.ShapeDtypeStruct((3, m, d), BF16), jax.ShapeDtypeStruct((m, d), F32),
                   jax.ShapeDtypeStruct((2, m, d), BF16)],
        scratch_shapes=[pltpu.VMEM((3, tm, d), BF16), pltpu.VMEM((tm, rw), BF16),
                        pltpu.VMEM((tm, ra), BF16), pltpu.VMEM((tm, rg), BF16)],
        compiler_params=_params(("parallel", "arbitrary")),
        name="rwkv_proj",
    )(x, x, norm_g.reshape(1, d), mu_ord, w_bf, w_bf, w_bf, l1, w2p, a2p, g2p,
      w0.reshape(1, d), a0.reshape(1, d))


def _bf(x):
    return x.astype(BF16)


def _dot(a, b):
    return jnp.dot(_bf(a), _bf(b), preferred_element_type=F32)


def _dot_nt(a, b):
    return lax.dot_general(_bf(a), _bf(b), (((1,), (1,)), ((), ())), preferred_element_type=F32)


def _dot_tn(a, b):
    return lax.dot_general(_bf(a), _bf(b), (((0,), (0,)), ((), ())), preferred_element_type=F32)


def _scan_body(n_streams, nb, cps, r_ref, ld_ref, k_ref, v_ref, al_ref, g_ref, kk_w_ref, ka_w_ref,
               rk_w_ref, lnw_ref, lnb_ref, o_ref, s_ref):
    c = SCAN_CHUNK
    w = HEADS_PER_GROUP * RWKV_HEAD
    wc = HEADS_PER_GROUP * c

    @pl.when(pl.program_id(2) == 0)
    def _():
        s_ref[...] = jnp.zeros_like(s_ref)

    tri3 = jnp.where(lax.broadcasted_iota(jnp.int32, (c, 3 * c), 0)
                     >= lax.broadcasted_iota(jnp.int32, (c, 3 * c), 1) % c, 1.0, 0.0).astype(BF16)
    bd_r = lax.broadcasted_iota(jnp.int32, (w, w), 0) // RWKV_HEAD
    bd_c = lax.broadcasted_iota(jnp.int32, (w, w), 1) // RWKV_HEAD
    head_mask = bd_r == bd_c
    head_ones = jnp.where(head_mask, 1.0, 0.0).astype(BF16)
    t_idx = lax.broadcasted_iota(jnp.int32, (c, wc), 0)
    s_idx = lax.broadcasted_iota(jnp.int32, (c, wc), 1) % c
    strict = s_idx < t_idx
    incl = s_idx <= t_idx
    eye = jnp.where(s_idx == t_idx, 1.0, 0.0)
    base_level = 3
    base_blocks = (t_idx >> base_level) == (s_idx >> base_level)
    join = {lv: (((t_idx >> lv) & 1) == 1) & ((s_idx >> lv) == (t_idx >> lv) - 1)
            for lv in range(base_level, c.bit_length() - 1)}

    def head_sum(x):
        return jnp.dot(_bf(x), head_ones, preferred_element_type=F32)

    def block_diag(x):
        xb = _bf(x)
        return jnp.where(head_mask, jnp.concatenate([xb] * HEADS_PER_GROUP, axis=0),
                         jnp.zeros((), BF16))

    def stream(bi, si, cj):
        st = bi * n_streams + si
        ln = pl.ds(si * w, w)
        rows = pl.ds(cj * c, c)
        r = r_ref[bi, rows, ln].astype(F32)
        ld = ld_ref[bi, rows, ln]
        k = k_ref[bi, rows, ln].astype(F32)
        v = v_ref[bi, rows, ln].astype(F32)
        al = al_ref[bi, rows, ln].astype(F32)

        ld_hi = _bf(ld)
        rest = ld - ld_hi.astype(F32)
        ld_mid = _bf(rest)
        ld_lo = _bf(rest - ld_mid.astype(F32))
        cum = jnp.dot(tri3, jnp.concatenate([ld_hi, ld_mid, ld_lo], axis=0),
                      preferred_element_type=F32)
        kk = k * kk_w_ref[:, ln]
        ss = head_sum(kk * kk)
        yield
        p_incl = jnp.exp(cum)
        p_prev = jnp.exp(cum - ld)
        p_inv = jnp.exp(-cum)
        p_last = p_incl[c - 1:c, :]
        kk = kk / jnp.maximum(jnp.sqrt(ss), 1e-12)
        kmod = k * (1.0 + (al - 1.0) * ka_w_ref[:, ln])

        a_t = -kk * p_prev
        b_t = kk * al * p_inv
        k_t = kmod * p_inv
        r_t = r * p_incl

        ar = _bf(jnp.concatenate([a_t, r_t], axis=0))
        m_b = _dot_nt(ar, block_diag(b_t))
        m_k = _dot_nt(ar, block_diag(k_t))
        s0 = s_ref[st]
        w1 = _dot_nt(ar, s0)
        yield
        a_ab = jnp.where(strict, m_b[:c], 0.0)
        a_rb = jnp.where(incl, m_b[c:], 0.0)
        a_ak = jnp.where(strict, m_k[:c], 0.0)
        a_rk = jnp.where(incl, m_k[c:], 0.0)
        w2 = _dot(jnp.concatenate([a_ak, a_rk], axis=0), block_diag(v))

        pw = jnp.where(base_blocks, a_ab, 0.0)
        t_inv = eye
        for step in range(base_level):
            if step < base_level - 1:
                pw_b = _bf(pw)
                z = _dot(jnp.concatenate([pw_b, _bf(t_inv)], axis=0), block_diag(pw_b))
                yield
                pw, t_inv = z[:c], t_inv + z[c:]
            else:
                z = _dot(t_inv, block_diag(pw))
                yield
                t_inv = t_inv + z
        n_bd = block_diag(a_ab)
        level = base_level
        while (1 << level) < c:
            t_b = _bf(t_inv)
            tn = _dot(t_b, n_bd)
            yield
            x = _dot(tn, block_diag(t_b))
            yield
            t_inv = t_inv + jnp.where(join[level], x, 0.0)
            level += 1

        u = _dot(t_inv, block_diag(w1[:c] + w2[:c]))
        yield
        y = w1[c:] + w2[c:] + _dot(a_rb, block_diag(u))

        uv = jnp.concatenate([u, v], axis=0)
        bk = jnp.concatenate([b_t, k_t], axis=0) * p_last
        s_new = _dot_tn(uv, bk)
        yield
        s_ref[st] = jnp.where(head_mask, s0 * p_last + s_new, 0.0)

        inv_n = 1.0 / RWKV_HEAD
        sums = head_sum(jnp.concatenate([y, r * kmod * rk_w_ref[:, ln]], axis=0))
        yield
        dlt = y - sums[:c] * inv_n
        bsum = sums[c:]
        var = head_sum(dlt * dlt) * inv_n
        yield
        yn = dlt * lax.rsqrt(var + GN_EPS) * lnw_ref[:, ln] + lnb_ref[:, ln]
        bonus = bsum * v
        o_ref[bi, rows, ln] = ((yn + bonus) * g_ref[bi, rows, ln]).astype(o_ref.dtype)

    def chunks(bi, si):
        for cj in range(cps):
            yield from stream(bi, si, cj)

    _round_robin([chunks(bi, si) for bi in range(nb) for si in range(n_streams)])


def _rwkv_scan(batch, rkv, ld, ag, k_k, k_a, r_k, ln_w, ln_b):
    _, m, d = rkv.shape
    t = m // batch
    c = SCAN_CHUNK
    w = HEADS_PER_GROUP * RWKV_HEAD
    assert d % w == 0 and t % c == 0 and c == RWKV_HEAD
    n_streams = _pick_tile(d // w, SCAN_STREAMS, 1)
    wb = n_streams * w
    cps = _pick_tile(t // c, SCAN_CHUNKS_PER_STEP, 1)
    nc = t // (c * cps)
    nb = _pick_tile(batch, SCAN_BATCH_ROWS, 1)
    tok = pl.BlockSpec((nb, cps * c, wb), lambda b, gi, ci: (b, ci, gi))
    par = pl.BlockSpec((1, wb), lambda b, gi, ci: (0, gi))

    def plane(p):
        return pl.BlockSpec((None, nb, cps * c, wb), lambda b, gi, ci: (p, b, ci, gi))

    rkv4 = rkv.reshape(3, batch, t, d)
    ag4 = ag.reshape(2, batch, t, d)
    return pl.pallas_call(
        functools.partial(_scan_body, n_streams, nb, cps),
        grid=(batch // nb, d // wb, nc),
        in_specs=[plane(0), tok, plane(1), plane(2), plane(0), plane(1)] + [par] * 5,
        out_specs=tok,
        out_shape=jax.ShapeDtypeStruct((batch, t, d), BF16),
        scratch_shapes=[pltpu.VMEM((nb * n_streams, w, w), F32)],
        compiler_params=_params(("parallel", "parallel", "arbitrary")),
        name="rwkv_scan",
    )(rkv4, ld.reshape(batch, t, d), rkv4, rkv4, ag4, ag4,
      *[p.reshape(1, d) for p in (k_k, k_a, r_k, ln_w, ln_b)]).reshape(m, d)


def _rwkv_mix(batch, x, norm_g, mu, w_rkv, w0, w1, w2, a0, a1, a2, g1, g2,
              k_k, k_a, r_k, ln_w, ln_b, w_o):
    rkv, ld, ag = _rwkv_proj(batch, x, norm_g, mu, w_rkv, w0, w1, w2, a0, a1, a2, g1, g2)
    yg = _rwkv_scan(batch, rkv, ld, ag, k_k, k_a, r_k.reshape(-1), ln_w, ln_b)
    return _simple_mm("rwkv_out", [_rows(yg)], lambda t: t, w_o.astype(BF16), [_epi_tile(x)],
                      lambda acc, x_t: (x_t + acc,), tn_target=x.shape[1])


def _attn_body(tq, tk, nh, nq, small_ref, q_ref, k_ref, v_ref, o_ref, v1_ref):
    qi = pl.program_id(2)
    dv = v_ref.shape[-1]

    def key_rows(j):
        start = j * tk if isinstance(j, int) else pl.multiple_of(j * tk, tk)
        return pl.ds(start, tk)

    def scores(h, j, masked):
        kb = k_ref[h, key_rows(j), :]
        s = lax.dot_general(q_ref[h], kb, (((1,), (1,)), ((), ())), preferred_element_type=F32)
        if masked:
            q_chunk = (qi * tq + lax.broadcasted_iota(jnp.int32, (tq, tk), 0)) // ATTN_CHUNK
            k_chunk = (j * tk + lax.broadcasted_iota(jnp.int32, (tq, tk), 1)) // ATTN_CHUNK
            s = jnp.where(k_chunk <= q_chunk, s, jnp.finfo(F32).min)
        return s

    def values(h, j):
        return v_ref[h, key_rows(j), :]

    def all_heads(make_gen):
        outs = [[] for _ in range(nh)]
        _round_robin([make_gen(h, outs[h]) for h in range(nh)])
        return tuple(o[0] for o in outs)

    def sweep(step, init, diag_step=None):
        carries = lax.fori_loop(
            0, qi, lambda j, cr: all_heads(lambda h, out: step(h, j, cr[h], False, out)), init)
        if diag_step is not None:
            return all_heads(lambda h, out: diag_step(h, carries[h], out))
        return all_heads(lambda h, out: step(h, qi, carries[h], True, out))

    def online_step(h, j, carry, masked, out):
        m_i, l_i, acc = carry
        s = scores(h, j, masked)
        yield
        m_new = jnp.maximum(m_i, jnp.max(s, axis=-1, keepdims=True))
        p = jnp.exp(s - m_new)
        alpha = jnp.exp(m_i - m_new)
        l_new = alpha * l_i + jnp.sum(p, axis=-1, keepdims=True)
        pv = jnp.dot(p.astype(BF16), values(h, j), preferred_element_type=F32)
        yield
        out.append((m_new, l_new, alpha * acc + pv))

    def plain_step(h, j, acc, masked, out):
        s = scores(h, j, masked)
        yield
        p = jnp.exp(s).astype(BF16)
        v1 = v1_ref[h, key_rows(j), :]
        pv = jnp.dot(p, v1, preferred_element_type=F32)
        yield
        out.append(acc + pv)

    def plain_diag(h, acc, out):
        half = tk // 2
        base = pl.multiple_of(qi * tk, tk)
        visible = (lax.broadcasted_iota(jnp.int32, (tq, half), 1) // ATTN_CHUNK
                   <= lax.broadcasted_iota(jnp.int32, (tq, half), 0) // ATTN_CHUNK)

        def weights(q_rows, key0, mask):
            kb = k_ref[h, pl.ds(key0, half), :]
            s = lax.dot_general(q_rows, kb, (((1,), (1,)), ((), ())), preferred_element_type=F32)
            return jnp.where(mask, jnp.exp(s), 0.0).astype(BF16)

        p_a = weights(q_ref[h], base, visible)
        yield
        pv_a = jnp.dot(p_a, v1_ref[h, pl.ds(base, half), :], preferred_element_type=F32)
        p_b = weights(q_ref[h, half:, :], base + half, visible[:half])
        yield
        pv_b = jnp.dot(p_b, v1_ref[h, pl.ds(base + half, half), :], preferred_element_type=F32)
        yield
        out.append(acc + jnp.concatenate([pv_a[:half], pv_a[half:] + pv_b], axis=0))

    @pl.when(small_ref[0] == 1)
    def _():
        @pl.when(qi == 0)
        def _():
            v1_ref[:, :, :dv] = v_ref[...]
            v1_ref[:, :, dv:] = jnp.ones(v_ref.shape, BF16)

        def unrolled(n_full):
            accs = tuple(jnp.zeros((tq, 2 * dv), F32) for _ in range(nh))
            for j in range(n_full):
                accs = all_heads(lambda h, out, j=j: plain_step(h, j, accs[h], False, out))
            accs = all_heads(lambda h, out: plain_diag(h, accs[h], out))
            for h, acc in enumerate(accs):
                o_ref[:, h * dv:(h + 1) * dv] = (acc[:, :dv] / acc[:, dv:]).astype(o_ref.dtype)

        for n_full in range(nq):
            pl.when(qi == n_full)(functools.partial(unrolled, n_full))

    @pl.when(small_ref[0] != 1)
    def _():
        init = tuple((jnp.full((tq, 1), -jnp.inf, F32), jnp.zeros((tq, 1), F32),
                      jnp.zeros((tq, dv), F32)) for _ in range(nh))
        for h, (_, l_i, acc) in enumerate(sweep(online_step, init)):
            o_ref[:, h * dv:(h + 1) * dv] = (acc / l_i).astype(o_ref.dtype)


def _attention(batch, q, k, v, small_scores):
    heads, m, dqk = q.shape
    dv = v.shape[-1]
    t = m // batch
    tq = _pick_tile(t, ATTN_Q_TILE, 2 * ATTN_CHUNK)
    assert tq % (2 * ATTN_CHUNK) == 0
    nq = t // tq
    nh = _pick_tile(heads, ATTN_HEADS_PER_STEP, 1)
    return pl.pallas_call(
        functools.partial(_attn_body, tq, tq, nh, nq),
        grid=(batch, heads // nh, nq),
        in_specs=[
            pl.BlockSpec(memory_space=pltpu.SMEM),
            pl.BlockSpec((nh, tq, dqk), lambda b, h, i: (h, b * nq + i, 0)),
            pl.BlockSpec((nh, t, dqk), lambda b, h, i: (h, b, 0)),
            pl.BlockSpec((nh, t, dv), lambda b, h, i: (h, b, 0)),
        ],
        out_specs=pl.BlockSpec((tq, nh * dv), lambda b, h, i: (b * nq + i, h)),
        out_shape=jax.ShapeDtypeStruct((m, heads * dv), BF16),
        scratch_shapes=[pltpu.VMEM((nh, t, 2 * dv), BF16)],
        compiler_params=_params(("parallel", "parallel", "arbitrary")),
        name="mla_attention",
    )(small_scores, q, k, v)


def _mla_mix(batch, x, positions, norm_g, w_down, q_a_norm, kv_a_norm, w_uq, w_ukv,
             q_norm, k_norm, w_o):
    m, d = x.shape
    q_lora = q_a_norm.shape[0]
    kv_lora = kv_a_norm.shape[0]
    heads, qk_dim = w_uq.shape[1], w_uq.shape[2]
    rope = w_down.shape[1] - q_lora - kv_lora
    nope = qk_dim - rope
    dv = w_ukv.shape[2] - nope
    half = rope // 2
    assert nope == LANE and dv == LANE and rope <= LANE and q_lora == kv_lora and q_lora % LANE == 0
    scale = 1.0 / math.sqrt(qk_dim)

    perm = jnp.concatenate([jnp.arange(half, rope), jnp.arange(0, half)])
    lane_pad = lambda a: jnp.pad(a, [(0, 0)] * (a.ndim - 1) + [(0, LANE - a.shape[-1])])

    inv_freq = ROPE_THETA ** (-jnp.arange(0, rope, 2, dtype=F32) / rope)
    freq_t = lane_pad(jnp.concatenate([inv_freq, inv_freq]))
    keep_t = lane_pad(jnp.ones((rope,), F32))
    sign_t = lane_pad(jnp.concatenate([-jnp.ones((half,), F32), jnp.ones((half,), F32)]))
    ang = positions.reshape(m, 1).astype(F32) * freq_t
    cos_t = jnp.cos(ang) * keep_t
    sin_t = jnp.sin(ang) * sign_t

    w_kpe = w_down[:, q_lora + kv_lora:]
    w_down_x = jnp.concatenate(
        [w_down[:, :q_lora + kv_lora], lane_pad(w_kpe), lane_pad(w_kpe[:, perm])], axis=1).astype(BF16)
    c = _simple_mm("mla_down", [_rows(x), _vec(norm_g)], _rms, w_down_x, [], lambda acc: (acc,),
                   tn_target=w_down_x.shape[1])
    qb = q_lora // q_lora
    tm = _pick_tile(m, PROJ_ROWS, SUBLANE)

    def norm_rope(body, pe, rot, g_body, g_pe, g_rot, cos_b, sin_b, extra_scale):
        ss = jnp.sum(body * body, axis=-1, keepdims=True) + jnp.sum(pe * pe, axis=-1, keepdims=True)
        s = lax.rsqrt(ss / qk_dim + RMS_EPS) * extra_scale
        return jnp.concatenate([body * s * g_body, (pe * g_pe * cos_b + rot * g_rot * sin_b) * s], axis=-1)

    w_q_pe = w_uq[:, :, nope:]
    w_q_x = jnp.concatenate([w_uq[:, :, :nope], lane_pad(w_q_pe), lane_pad(w_q_pe[:, :, perm])],
                            axis=-1).reshape(q_lora, heads * 3 * LANE).astype(BF16)
    gq = jnp.concatenate([q_norm[:nope], lane_pad(q_norm[nope:]), lane_pad(q_norm[nope:][perm])]).reshape(1, -1)

    hb = _pick_tile(heads, MLA_HEADS_PER_STEP, 1)

    def q_epi(acc, g_t, cos_b, sin_b):
        per_head = []
        for hh in range(hb):
            a = acc[:, hh * 3 * LANE:(hh + 1) * 3 * LANE]
            per_head.append(norm_rope(a[:, :LANE], a[:, LANE:2 * LANE], a[:, 2 * LANE:], g_t[:, :LANE],
                                      g_t[:, LANE:2 * LANE], g_t[:, 2 * LANE:], cos_b, sin_b, scale))
        return (jnp.stack(per_head, axis=0),)

    q = _mm("mla_q", m,
            [(c, _row_spec(tm, q_lora, 0)), (q_a_norm.reshape(1, -1), _vec_spec(q_lora))], _rms, w_q_x,
            [(gq, _vec_spec(3 * LANE)), (cos_t, _row_spec(tm, LANE)), (sin_t, _row_spec(tm, LANE))], q_epi,
            [(jax.ShapeDtypeStruct((heads, m, 2 * LANE), BF16),
              pl.BlockSpec((hb, tm, 2 * LANE), lambda i, j: (j, i, 0)))],
            tm=tm, tn=hb * 3 * LANE)[0]

    w_kv_x = w_ukv.reshape(kv_lora, heads * (nope + dv)).astype(BF16)
    gk = jnp.concatenate([k_norm[:nope], lane_pad(k_norm[nope:]), lane_pad(k_norm[nope:][perm])]).reshape(1, -1)
    pe_block = (q_lora + kv_lora) // LANE

    def kv_epi(acc, g_t, pe, rot, cos_b, sin_b):
        k_out, v_out = [], []
        for hh in range(hb):
            a = acc[:, hh * (nope + dv):(hh + 1) * (nope + dv)]
            k_out.append(norm_rope(a[:, :nope], pe, rot, g_t[:, :LANE], g_t[:, LANE:2 * LANE],
                                   g_t[:, 2 * LANE:], cos_b, sin_b, 1.0))
            v_out.append(a[:, nope:])
        return jnp.stack(k_out, axis=0), jnp.stack(v_out, axis=0)

    k, v = _mm("mla_kv", m,
               [(c, _row_spec(tm, kv_lora, qb)), (kv_a_norm.reshape(1, -1), _vec_spec(kv_lora))], _rms, w_kv_x,
               [(gk, _vec_spec(3 * LANE)), (c, _row_spec(tm, LANE, pe_block)),
                (c, _row_spec(tm, LANE, pe_block + 1)), (cos_t, _row_spec(tm, LANE)),
                (sin_t, _row_spec(tm, LANE))], kv_epi,
               [(jax.ShapeDtypeStruct((heads, m, 2 * LANE), BF16),
                 pl.BlockSpec((hb, tm, 2 * LANE), lambda i, j: (j, i, 0))),
                (jax.ShapeDtypeStruct((heads, m, dv), BF16),
                 pl.BlockSpec((hb, tm, dv), lambda i, j: (j, i, 0)))],
               tm=tm, tn=hb * (nope + dv))

    score_bound = scale * qk_dim * jnp.max(jnp.abs(q_norm)) * jnp.max(jnp.abs(k_norm))
    small_scores = (score_bound <= MAX_PLAIN_SCORE).astype(jnp.int32).reshape(1)
    o = _attention(batch, q, k, v, small_scores)
    return _simple_mm("mla_out", [_rows(o)], lambda t: t, w_o.reshape(heads * dv, d).astype(BF16),
                      [_epi_tile(x)], lambda acc, x_t: (x_t + acc,), tn_target=d)


def kernel(x, positions, ffn_norm, ffn_w13, ffn_w2, mix_norm, rwkv_mu, rwkv_w_rkv, rwkv_w0, rwkv_w1, rwkv_w2, rwkv_a0, rwkv_a1, rwkv_a2, rwkv_g1, rwkv_g2, rwkv_k_k, rwkv_k_a, rwkv_r_k, rwkv_ln_w, rwkv_ln_b, rwkv_w_o, mla_w_down, mla_q_a_norm, mla_kv_a_norm, mla_w_uq, mla_w_ukv, mla_q_norm, mla_k_norm, mla_w_o):
    batch, seq, d = x.shape
    depth = ffn_norm.shape[0]
    n_mixers = 2
    xf = x.reshape(batch * seq, d)
    d_ff = ffn_w2.shape[2]
    w13p, w2p = _ffn_weights(ffn_w13.reshape((2 * depth,) + ffn_w13.shape[2:]),
                             ffn_w2.reshape((2 * depth,) + ffn_w2.shape[2:]))
    for i in range(depth):
        j = i // n_mixers
        xf = _ffn(xf, ffn_norm[i, 0], w13p, w2p, 2 * i, d_ff)
        if i % n_mixers == 0:
            xf = _rwkv_mix(batch, xf, mix_norm[i], rwkv_mu[j], rwkv_w_rkv[j], rwkv_w0[j], rwkv_w1[j],
                           rwkv_w2[j], rwkv_a0[j], rwkv_a1[j], rwkv_a2[j], rwkv_g1[j], rwkv_g2[j],
                           rwkv_k_k[j], rwkv_k_a[j], rwkv_r_k[j], rwkv_ln_w[j], rwkv_ln_b[j],
                           rwkv_w_o[j])
        else:
            xf = _mla_mix(batch, xf, positions, mix_norm[i], mla_w_down[j], mla_q_a_norm[j],
                          mla_kv_a_norm[j], mla_w_uq[j], mla_w_ukv[j], mla_q_norm[j], mla_k_norm[j],
                          mla_w_o[j])
        xf = _ffn(xf, ffn_norm[i, 1], w13p, w2p, 2 * i + 1, d_ff)
    return xf.reshape(batch, seq, d)
```
